```python
import jax, jax.numpy as jnp
from jax import lax
import numpy as np

D_MODEL = 2048
BATCH = 8
SEQ = 4096
DEPTH = 4

CHUNK = 64
N_META = 16
N_MIXERS = 2
N_MLA = (DEPTH + N_MIXERS - 1) // N_MIXERS
N_LRU = DEPTH // N_MIXERS

MLA_HEADS = 16
Q_LORA = 512
KV_LORA = 512
QK_NOPE = 128
QK_ROPE = 64
V_HEAD = 128
ROPE_THETA = 10000.0
Q_BLOCK = 128

D_RNN = D_MODEL
RNN_BLOCKS = 16
RNN_BW = D_RNN // RNN_BLOCKS
CONV_W = 4
LRU_C = 8.0

D_FF = -(-8 * D_MODEL // (3 * 256)) * 256

RMS_EPS = 1e-6
NEG_BIG = -1e30

kernel_name = 'hybrid_mla_rglru_streaming_trunk'


def rms_norm(x, g):
    xf = x.astype(jnp.float32)
    y = xf * lax.rsqrt(jnp.mean(xf * xf, axis=-1, keepdims=True) + RMS_EPS)
    return (y * g.astype(jnp.float32)).astype(x.dtype)


def chunk_ids(n):
    pos = jnp.arange(n)
    return jnp.where(pos < N_META, 0, 1 + (pos - N_META) // CHUNK)


def apply_rope(x, cos, sin):
    xf = x.astype(jnp.float32)
    x1, x2 = jnp.split(xf, 2, axis=-1)
    out = jnp.concatenate([x1 * cos - x2 * sin, x2 * cos + x1 * sin], axis=-1)
    return out.astype(x.dtype)


def mla_mixer(h, w_in, q_norm, kv_norm, w_uq, w_ukv, w_o):
    B, T, _ = h.shape
    proj = h @ w_in
    c_q, c_kv, k_rope = jnp.split(proj, [Q_LORA, Q_LORA + KV_LORA], axis=-1)
    c_q = rms_norm(c_q, q_norm)
    c_kv = rms_norm(c_kv, kv_norm)
    q = (c_q @ w_uq).reshape(B, T, MLA_HEADS, QK_NOPE + QK_ROPE)
    q_nope, q_rope = jnp.split(q, [QK_NOPE], axis=-1)
    kv = (c_kv @ w_ukv).reshape(B, T, MLA_HEADS, QK_NOPE + V_HEAD)
    k_nope, v = jnp.split(kv, [QK_NOPE], axis=-1)

    pos = jnp.arange(T, dtype=jnp.float32)
    inv_freq = ROPE_THETA ** (-jnp.arange(0, QK_ROPE, 2, dtype=jnp.float32) / QK_ROPE)
    ang = pos[:, None] * inv_freq[None, :]
    cos, sin = jnp.cos(ang), jnp.sin(ang)
    q_rope = apply_rope(q_rope, cos[:, None, :], sin[:, None, :])
    k_rope = apply_rope(k_rope, cos, sin)

    scale = (QK_NOPE + QK_ROPE) ** -0.5
    n_blk = -(-T // Q_BLOCK)
    t_pad = n_blk * Q_BLOCK
    pad = ((0, 0), (0, t_pad - T), (0, 0), (0, 0))
    q_nope = jnp.pad(q_nope, pad)
    q_rope = jnp.pad(q_rope, pad)
    k_chunk = chunk_ids(T)
    q_chunk = chunk_ids(t_pad)

    def attend_block(i):
        s = i * Q_BLOCK
        qn = lax.dynamic_slice_in_dim(q_nope, s, Q_BLOCK, axis=1)
        qr = lax.dynamic_slice_in_dim(q_rope, s, Q_BLOCK, axis=1)
        qc = lax.dynamic_slice_in_dim(q_chunk, s, Q_BLOCK)
        scores = (jnp.einsum('bqhd,bkhd->bhqk', qn, k_nope)
                  + jnp.einsum('bqhd,bkd->bhqk', qr, k_rope)).astype(jnp.float32) * scale
        mask = k_chunk[None, :] <= qc[:, None]
        scores = jnp.where(mask[None, None], scores, NEG_BIG)
        p = jax.nn.softmax(scores, axis=-1).astype(v.dtype)
        return jnp.einsum('bhqk,bkhd->bqhd', p, v)

    out = lax.map(attend_block, jnp.arange(n_blk))
    out = jnp.moveaxis(out, 0, 1).reshape(B, t_pad, MLA_HEADS * V_HEAD)[:, :T]
    return out @ w_o


def rglru_mixer(h, w_in, conv_w, conv_b, w_ga, b_ga, w_gx, b_gx, lam, w_o):
    B, T, _ = h.shape
    xb, yb = jnp.split(h @ w_in, 2, axis=-1)
    yb = jax.nn.gelu(yb, approximate=True)
    xb = lax.conv_general_dilated(
        xb, conv_w[:, None, :], window_strides=(1,), padding=[(CONV_W - 1, 0)],
        dimension_numbers=('NWC', 'WIO', 'NWC'), feature_group_count=D_RNN) + conv_b
    xg = xb.reshape(B, T, RNN_BLOCKS, RNN_BW)
    r = jax.nn.sigmoid(jnp.einsum('btnc,ncd->btnd', xg, w_ga) + b_ga).reshape(B, T, D_RNN)
    i = jax.nn.sigmoid(jnp.einsum('btnc,ncd->btnd', xg, w_gx) + b_gx).reshape(B, T, D_RNN)
    log_a = -LRU_C * r.astype(jnp.float32) * jax.nn.softplus(-lam.astype(jnp.float32))
    a = jnp.exp(log_a)
    b = jnp.sqrt(-jnp.expm1(2.0 * log_a)) * (i * xb).astype(jnp.float32)

    def combine(lhs, rhs):
        a1, b1 = lhs
        a2, b2 = rhs
        return a1 * a2, a2 * b1 + b2

    _, hs = lax.associative_scan(combine, (a, b), axis=1)
    return (hs.astype(h.dtype) * yb) @ w_o


def swiglu(h, w_gu, w_down):
    g, u = jnp.split(h @ w_gu, 2, axis=-1)
    return (jax.nn.silu(g) * u) @ w_down


def _fwd_setup_inputs(seed: int = 0) -> dict:
    key = jax.random.key(seed)
    ks = jax.random.split(key, 24)
    f32 = jnp.float32

    def dense(k, shape, fan_in):
        return jax.random.normal(k, shape, f32) * (fan_in ** -0.5)

    def gain(k, shape):
        return 1.0 + 0.02 * jax.random.normal(k, shape, f32)

    x = jax.random.normal(ks[0], (BATCH, SEQ, D_MODEL), f32)
    meta_tokens = jax.random.normal(ks[1], (N_META, D_MODEL), f32)
    norm_mix = gain(ks[2], (DEPTH, D_MODEL))
    norm_ffn = gain(ks[3], (DEPTH, D_MODEL))
    norm_final = gain(ks[4], (D_MODEL,))

    mla_w_in = dense(ks[5], (N_MLA, D_MODEL, Q_LORA + KV_LORA + QK_ROPE), D_MODEL)
    mla_q_norm = gain(ks[6], (N_MLA, Q_LORA))
    mla_kv_norm = gain(ks[7], (N_MLA, KV_LORA))
    mla_w_uq = dense(ks[8], (N_MLA, Q_LORA, MLA_HEADS * (QK_NOPE + QK_ROPE)), Q_LORA)
    mla_w_ukv = dense(ks[9], (N_MLA, KV_LORA, MLA_HEADS * (QK_NOPE + V_HEAD)), KV_LORA)
    mla_w_o = dense(ks[10], (N_MLA, MLA_HEADS * V_HEAD, D_MODEL), MLA_HEADS * V_HEAD)

    lru_w_in = dense(ks[11], (N_LRU, D_MODEL, 2 * D_RNN), D_MODEL)
    lru_conv_w = dense(ks[12], (N_LRU, CONV_W, D_RNN), CONV_W)
    lru_conv_b = 0.01 * jax.random.normal(ks[13], (N_LRU, D_RNN), f32)
    lru_w_gate_a = dense(ks[14], (N_LRU, RNN_BLOCKS, RNN_BW, RNN_BW), RNN_BW)
    lru_b_gate_a = 0.01 * jax.random.normal(ks[15], (N_LRU, RNN_BLOCKS, RNN_BW), f32)
    lru_w_gate_x = dense(ks[16], (N_LRU, RNN_BLOCKS, RNN_BW, RNN_BW), RNN_BW)
    lru_b_gate_x = 0.01 * jax.random.normal(ks[17], (N_LRU, RNN_BLOCKS, RNN_BW), f32)
    a_c = jax.random.uniform(ks[18], (N_LRU, D_RNN), f32, 0.9, 0.999)
    a0 = a_c ** (1.0 / LRU_C)
    lru_lambda = jnp.log(a0) - jnp.log1p(-a0)
    lru_w_o = dense(ks[19], (N_LRU, D_RNN, D_MODEL), D_RNN)

    ffn_w_gu = dense(ks[20], (DEPTH, D_MODEL, 2 * D_FF), D_MODEL)
    ffn_w_down = dense(ks[21], (DEPTH, D_FF, D_MODEL), D_FF)

    return {
        'x': x, 'meta_tokens': meta_tokens,
        'norm_mix': norm_mix, 'norm_ffn': norm_ffn, 'norm_final': norm_final,
        'mla_w_in': mla_w_in, 'mla_q_norm': mla_q_norm, 'mla_kv_norm': mla_kv_norm,
        'mla_w_uq': mla_w_uq, 'mla_w_ukv': mla_w_ukv, 'mla_w_o': mla_w_o,
        'lru_w_in': lru_w_in, 'lru_conv_w': lru_conv_w, 'lru_conv_b': lru_conv_b,
        'lru_w_gate_a': lru_w_gate_a, 'lru_b_gate_a': lru_b_gate_a,
        'lru_w_gate_x': lru_w_gate_x, 'lru_b_gate_x': lru_b_gate_x,
        'lru_lambda': lru_lambda, 'lru_w_o': lru_w_o,
        'ffn_w_gu': ffn_w_gu, 'ffn_w_down': ffn_w_down,
    }


def _fwd_reference(x, meta_tokens, norm_mix, norm_ffn, norm_final,
              mla_w_in, mla_q_norm, mla_kv_norm, mla_w_uq, mla_w_ukv, mla_w_o,
              lru_w_in, lru_conv_w, lru_conv_b, lru_w_gate_a, lru_b_gate_a,
              lru_w_gate_x, lru_b_gate_x, lru_lambda, lru_w_o,
              ffn_w_gu, ffn_w_down):
    B = x.shape[0]
    meta = jnp.broadcast_to(meta_tokens.astype(x.dtype)[None], (B, N_META, D_MODEL))
    h = jnp.concatenate([meta, x], axis=1)
    for layer in range(DEPTH):
        j = layer // N_MIXERS
        hn = rms_norm(h, norm_mix[layer])
        if layer % N_MIXERS == 0:
            mix = mla_mixer(hn, mla_w_in[j], mla_q_norm[j], mla_kv_norm[j],
                            mla_w_uq[j], mla_w_ukv[j], mla_w_o[j])
        else:
            mix = rglru_mixer(hn, lru_w_in[j], lru_conv_w[j], lru_conv_b[j],
                              lru_w_gate_a[j], lru_b_gate_a[j],
                              lru_w_gate_x[j], lru_b_gate_x[j],
                              lru_lambda[j], lru_w_o[j])
        h = h + mix
        h = h + swiglu(rms_norm(h, norm_ffn[layer]), ffn_w_gu[layer], ffn_w_down[layer])
    h = rms_norm(h, norm_final)
    return h[:, N_META:]


import jax as _jax
import jax.numpy as _jnp

TWIN_FORMAT = 'train_step'
FWD_PARAMS = ['x', 'meta_tokens', 'norm_mix', 'norm_ffn', 'norm_final', 'mla_w_in', 'mla_q_norm', 'mla_kv_norm', 'mla_w_uq', 'mla_w_ukv', 'mla_w_o', 'lru_w_in', 'lru_conv_w', 'lru_conv_b', 'lru_w_gate_a', 'lru_b_gate_a', 'lru_w_gate_x', 'lru_b_gate_x', 'lru_lambda', 'lru_w_o', 'ffn_w_gu', 'ffn_w_down']
TWIN_WEIGHTS = ['meta_tokens', 'norm_mix', 'norm_ffn', 'norm_final', 'mla_w_in', 'mla_q_norm', 'mla_kv_norm', 'mla_w_uq', 'mla_w_ukv', 'mla_w_o', 'lru_w_in', 'lru_conv_w', 'lru_conv_b', 'lru_w_gate_a', 'lru_b_gate_a', 'lru_w_gate_x', 'lru_b_gate_x', 'lru_lambda', 'lru_w_o', 'ffn_w_gu', 'ffn_w_down']
TWIN_DIFF_INPUT = 'x'
TWIN_INPUTS = ['x', 'meta_tokens', 'norm_mix', 'norm_ffn', 'norm_final', 'mla_w_in', 'mla_q_norm', 'mla_kv_norm', 'mla_w_uq', 'mla_w_ukv', 'mla_w_o', 'lru_w_in', 'lru_conv_w', 'lru_conv_b', 'lru_w_gate_a', 'lru_b_gate_a', 'lru_w_gate_x', 'lru_b_gate_x', 'lru_lambda', 'lru_w_o', 'ffn_w_gu', 'ffn_w_down', 'loss_target', 'm_meta_tokens', 'm_norm_mix', 'm_norm_ffn', 'm_norm_final', 'm_mla_w_in', 'm_mla_q_norm', 'm_mla_kv_norm', 'm_mla_w_uq', 'm_mla_w_ukv', 'm_mla_w_o', 'm_lru_w_in', 'm_lru_conv_w', 'm_lru_conv_b', 'm_lru_w_gate_a', 'm_lru_b_gate_a', 'm_lru_w_gate_x', 'm_lru_b_gate_x', 'm_lru_lambda', 'm_lru_w_o', 'm_ffn_w_gu', 'm_ffn_w_down', 'v_meta_tokens', 'v_norm_mix', 'v_norm_ffn', 'v_norm_final', 'v_mla_w_in', 'v_mla_q_norm', 'v_mla_kv_norm', 'v_mla_w_uq', 'v_mla_w_ukv', 'v_mla_w_o', 'v_lru_w_in', 'v_lru_conv_w', 'v_lru_conv_b', 'v_lru_w_gate_a', 'v_lru_b_gate_a', 'v_lru_w_gate_x', 'v_lru_b_gate_x', 'v_lru_lambda', 'v_lru_w_o', 'v_ffn_w_gu', 'v_ffn_w_down']
TWIN_OUTPUTS = ['loss', 'grad_x', 'grad_meta_tokens', 'grad_norm_mix', 'grad_norm_ffn', 'grad_norm_final', 'grad_mla_w_in', 'grad_mla_q_norm', 'grad_mla_kv_norm', 'grad_mla_w_uq', 'grad_mla_w_ukv', 'grad_mla_w_o', 'grad_lru_w_in', 'grad_lru_conv_w', 'grad_lru_conv_b', 'grad_lru_w_gate_a', 'grad_lru_b_gate_a', 'grad_lru_w_gate_x', 'grad_lru_b_gate_x', 'grad_lru_lambda', 'grad_lru_w_o', 'grad_ffn_w_gu', 'grad_ffn_w_down', 'delta_meta_tokens', 'delta_norm_mix', 'delta_norm_ffn', 'delta_norm_final', 'delta_mla_w_in', 'delta_mla_q_norm', 'delta_mla_kv_norm', 'delta_mla_w_uq', 'delta_mla_w_ukv', 'delta_mla_w_o', 'delta_lru_w_in', 'delta_lru_conv_w', 'delta_lru_conv_b', 'delta_lru_w_gate_a', 'delta_lru_b_gate_a', 'delta_lru_w_gate_x', 'delta_lru_b_gate_x', 'delta_lru_lambda', 'delta_lru_w_o', 'delta_ffn_w_gu', 'delta_ffn_w_down', 'new_m_meta_tokens', 'new_m_norm_mix', 'new_m_norm_ffn', 'new_m_norm_final', 'new_m_mla_w_in', 'new_m_mla_q_norm', 'new_m_mla_kv_norm', 'new_m_mla_w_uq', 'new_m_mla_w_ukv', 'new_m_mla_w_o', 'new_m_lru_w_in', 'new_m_lru_conv_w', 'new_m_lru_conv_b', 'new_m_lru_w_gate_a', 'new_m_lru_b_gate_a', 'new_m_lru_w_gate_x', 'new_m_lru_b_gate_x', 'new_m_lru_lambda', 'new_m_lru_w_o', 'new_m_ffn_w_gu', 'new_m_ffn_w_down', 'new_v_meta_tokens', 'new_v_norm_mix', 'new_v_norm_ffn', 'new_v_norm_final', 'new_v_mla_w_in', 'new_v_mla_q_norm', 'new_v_mla_kv_norm', 'new_v_mla_w_uq', 'new_v_mla_w_ukv', 'new_v_mla_w_o', 'new_v_lru_w_in', 'new_v_lru_conv_w', 'new_v_lru_conv_b', 'new_v_lru_w_gate_a', 'new_v_lru_b_gate_a', 'new_v_lru_w_gate_x', 'new_v_lru_b_gate_x', 'new_v_lru_lambda', 'new_v_lru_w_o', 'new_v_ffn_w_gu', 'new_v_ffn_w_down']
TWIN_LEAF_KINDS = {'loss': 'loss', 'grad_x': 'grad_x', 'grad_meta_tokens': 'grad_w', 'grad_norm_mix': 'grad_w', 'grad_norm_ffn': 'grad_w', 'grad_norm_final': 'grad_w', 'grad_mla_w_in': 'grad_w', 'grad_mla_q_norm': 'grad_w', 'grad_mla_kv_norm': 'grad_w', 'grad_mla_w_uq': 'grad_w', 'grad_mla_w_ukv': 'grad_w', 'grad_mla_w_o': 'grad_w', 'grad_lru_w_in': 'grad_w', 'grad_lru_conv_w': 'grad_w', 'grad_lru_conv_b': 'grad_w', 'grad_lru_w_gate_a': 'grad_w', 'grad_lru_b_gate_a': 'grad_w', 'grad_lru_w_gate_x': 'grad_w', 'grad_lru_b_gate_x': 'grad_w', 'grad_lru_lambda': 'grad_w', 'grad_lru_w_o': 'grad_w', 'grad_ffn_w_gu': 'grad_w', 'grad_ffn_w_down': 'grad_w', 'delta_meta_tokens': 'delta_w', 'delta_norm_mix': 'delta_w', 'delta_norm_ffn': 'delta_w', 'delta_norm_final': 'delta_w', 'delta_mla_w_in': 'delta_w', 'delta_mla_q_norm': 'delta_w', 'delta_mla_kv_norm': 'delta_w', 'delta_mla_w_uq': 'delta_w', 'delta_mla_w_ukv': 'delta_w', 'delta_mla_w_o': 'delta_w', 'delta_lru_w_in': 'delta_w', 'delta_lru_conv_w': 'delta_w', 'delta_lru_conv_b': 'delta_w', 'delta_lru_w_gate_a': 'delta_w', 'delta_lru_b_gate_a': 'delta_w', 'delta_lru_w_gate_x': 'delta_w', 'delta_lru_b_gate_x': 'delta_w', 'delta_lru_lambda': 'delta_w', 'delta_lru_w_o': 'delta_w', 'delta_ffn_w_gu': 'delta_w', 'delta_ffn_w_down': 'delta_w', 'new_m_meta_tokens': 'new_m', 'new_m_norm_mix': 'new_m', 'new_m_norm_ffn': 'new_m', 'new_m_norm_final': 'new_m', 'new_m_mla_w_in': 'new_m', 'new_m_mla_q_norm': 'new_m', 'new_m_mla_kv_norm': 'new_m', 'new_m_mla_w_uq': 'new_m', 'new_m_mla_w_ukv': 'new_m', 'new_m_mla_w_o': 'new_m', 'new_m_lru_w_in': 'new_m', 'new_m_lru_conv_w': 'new_m', 'new_m_lru_conv_b': 'new_m', 'new_m_lru_w_gate_a': 'new_m', 'new_m_lru_b_gate_a': 'new_m', 'new_m_lru_w_gate_x': 'new_m', 'new_m_lru_b_gate_x': 'new_m', 'new_m_lru_lambda': 'new_m', 'new_m_lru_w_o': 'new_m', 'new_m_ffn_w_gu': 'new_m', 'new_m_ffn_w_down': 'new_m', 'new_v_meta_tokens': 'new_v', 'new_v_norm_mix': 'new_v', 'new_v_norm_ffn': 'new_v', 'new_v_norm_final': 'new_v', 'new_v_mla_w_in': 'new_v', 'new_v_mla_q_norm': 'new_v', 'new_v_mla_kv_norm': 'new_v', 'new_v_mla_w_uq': 'new_v', 'new_v_mla_w_ukv': 'new_v', 'new_v_mla_w_o': 'new_v', 'new_v_lru_w_in': 'new_v', 'new_v_lru_conv_w': 'new_v', 'new_v_lru_conv_b': 'new_v', 'new_v_lru_w_gate_a': 'new_v', 'new_v_lru_b_gate_a': 'new_v', 'new_v_lru_w_gate_x': 'new_v', 'new_v_lru_b_gate_x': 'new_v', 'new_v_lru_lambda': 'new_v', 'new_v_lru_w_o': 'new_v', 'new_v_ffn_w_gu': 'new_v', 'new_v_ffn_w_down': 'new_v'}


def _forward(args):
    return _fwd_reference(*[args[k] for k in FWD_PARAMS])


def _output_shape():
    def fwd():
        inp = _fwd_setup_inputs(0)
        return _fwd_reference(*[inp[k] for k in FWD_PARAMS])
    out = _jax.eval_shape(fwd)
    return out.shape, out.dtype

N_MICROBATCH = 1
ADAM_LR = 0.001
ADAM_B1 = 0.9
ADAM_B2 = 0.999
ADAM_EPS = 1e-08
ADAM_WD = 0.01
ADAM_STEP = 10
PER_EXAMPLE_BATCH_AXIS = {'x': 0, 'loss_target': 0}
SHARED_INPUTS = []
_WEIGHT_DTYPES = {'meta_tokens': _jnp.float32, 'norm_mix': _jnp.float32, 'norm_ffn': _jnp.float32, 'norm_final': _jnp.float32, 'mla_w_in': _jnp.float32, 'mla_q_norm': _jnp.float32, 'mla_kv_norm': _jnp.float32, 'mla_w_uq': _jnp.float32, 'mla_w_ukv': _jnp.float32, 'mla_w_o': _jnp.float32, 'lru_w_in': _jnp.float32, 'lru_conv_w': _jnp.float32, 'lru_conv_b': _jnp.float32, 'lru_w_gate_a': _jnp.float32, 'lru_b_gate_a': _jnp.float32, 'lru_w_gate_x': _jnp.float32, 'lru_b_gate_x': _jnp.float32, 'lru_lambda': _jnp.float32, 'lru_w_o': _jnp.float32, 'ffn_w_gu': _jnp.float32, 'ffn_w_down': _jnp.float32}
MOMENT_SCALE = {'meta_tokens': 6.163892e-03, 'norm_mix': 4.347834e-02, 'norm_ffn': 6.522877e-02, 'norm_final': 1.600641e+01, 'mla_w_in': 4.666571e-02, 'mla_q_norm': 3.029621e-02, 'mla_kv_norm': 6.014902e-02, 'mla_w_uq': 1.235767e-02, 'mla_w_ukv': 2.075204e-02, 'mla_w_o': 2.652047e-02, 'lru_w_in': 4.013388e-02, 'lru_conv_w': 4.445179e-02, 'lru_conv_b': 4.705751e-01, 'lru_w_gate_a': 1.282581e-02, 'lru_b_gate_a': 1.131160e-02, 'lru_w_gate_x': 2.322197e-02, 'lru_b_gate_x': 1.692582e-02, 'lru_lambda': 2.288154e-02, 'lru_w_o': 4.356401e-02, 'ffn_w_gu': 2.821535e-02, 'ffn_w_down': 4.609322e-02}


def _to_microbatches(a, axis):
    t = _jnp.moveaxis(a, axis, 0)
    t = t.reshape((N_MICROBATCH, t.shape[0] // N_MICROBATCH) + t.shape[1:])
    return _jnp.moveaxis(t, 1, axis + 1)


def setup_inputs(seed: int = 0) -> dict:
    inp = _fwd_setup_inputs(seed)
    key = _jax.random.fold_in(_jax.random.key(seed), 7919)
    shape, _ = _output_shape()
    out = dict(inp)
    out["loss_target"] = _jax.random.normal(_jax.random.fold_in(key, 0), shape, _jnp.float32)
    for i, name in enumerate(TWIN_WEIGHTS):
        w = inp[name].astype(_jnp.float32)
        if MOMENT_SCALE is None:
            s = _jnp.sqrt(_jnp.mean(_jnp.square(w)) + 1e-30)
        else:
            s = MOMENT_SCALE[name]
        km, kv = _jax.random.split(_jax.random.fold_in(key, i + 1))
        out[name] = w
        out["m_" + name] = s * _jax.random.normal(km, w.shape, _jnp.float32)
        out["v_" + name] = (s * s) * _jax.random.uniform(kv, w.shape, _jnp.float32, 0.5, 1.5)
    if N_MICROBATCH > 1:
        for name, axis in PER_EXAMPLE_BATCH_AXIS.items():
            out[name] = _to_microbatches(out[name], axis)
    return {'x': out['x'], 'meta_tokens': out['meta_tokens'], 'norm_mix': out['norm_mix'], 'norm_ffn': out['norm_ffn'], 'norm_final': out['norm_final'], 'mla_w_in': out['mla_w_in'], 'mla_q_norm': out['mla_q_norm'], 'mla_kv_norm': out['mla_kv_norm'], 'mla_w_uq': out['mla_w_uq'], 'mla_w_ukv': out['mla_w_ukv'], 'mla_w_o': out['mla_w_o'], 'lru_w_in': out['lru_w_in'], 'lru_conv_w': out['lru_conv_w'], 'lru_conv_b': out['lru_conv_b'], 'lru_w_gate_a': out['lru_w_gate_a'], 'lru_b_gate_a': out['lru_b_gate_a'], 'lru_w_gate_x': out['lru_w_gate_x'], 'lru_b_gate_x': out['lru_b_gate_x'], 'lru_lambda': out['lru_lambda'], 'lru_w_o': out['lru_w_o'], 'ffn_w_gu': out['ffn_w_gu'], 'ffn_w_down': out['ffn_w_down'], 'loss_target': out['loss_target'], 'm_meta_tokens': out['m_meta_tokens'], 'm_norm_mix': out['m_norm_mix'], 'm_norm_ffn': out['m_norm_ffn'], 'm_norm_final': out['m_norm_final'], 'm_mla_w_in': out['m_mla_w_in'], 'm_mla_q_norm': out['m_mla_q_norm'], 'm_mla_kv_norm': out['m_mla_kv_norm'], 'm_mla_w_uq': out['m_mla_w_uq'], 'm_mla_w_ukv': out['m_mla_w_ukv'], 'm_mla_w_o': out['m_mla_w_o'], 'm_lru_w_in': out['m_lru_w_in'], 'm_lru_conv_w': out['m_lru_conv_w'], 'm_lru_conv_b': out['m_lru_conv_b'], 'm_lru_w_gate_a': out['m_lru_w_gate_a'], 'm_lru_b_gate_a': out['m_lru_b_gate_a'], 'm_lru_w_gate_x': out['m_lru_w_gate_x'], 'm_lru_b_gate_x': out['m_lru_b_gate_x'], 'm_lru_lambda': out['m_lru_lambda'], 'm_lru_w_o': out['m_lru_w_o'], 'm_ffn_w_gu': out['m_ffn_w_gu'], 'm_ffn_w_down': out['m_ffn_w_down'], 'v_meta_tokens': out['v_meta_tokens'], 'v_norm_mix': out['v_norm_mix'], 'v_norm_ffn': out['v_norm_ffn'], 'v_norm_final': out['v_norm_final'], 'v_mla_w_in': out['v_mla_w_in'], 'v_mla_q_norm': out['v_mla_q_norm'], 'v_mla_kv_norm': out['v_mla_kv_norm'], 'v_mla_w_uq': out['v_mla_w_uq'], 'v_mla_w_ukv': out['v_mla_w_ukv'], 'v_mla_w_o': out['v_mla_w_o'], 'v_lru_w_in': out['v_lru_w_in'], 'v_lru_conv_w': out['v_lru_conv_w'], 'v_lru_conv_b': out['v_lru_conv_b'], 'v_lru_w_gate_a': out['v_lru_w_gate_a'], 'v_lru_b_gate_a': out['v_lru_b_gate_a'], 'v_lru_w_gate_x': out['v_lru_w_gate_x'], 'v_lru_b_gate_x': out['v_lru_b_gate_x'], 'v_lru_lambda': out['v_lru_lambda'], 'v_lru_w_o': out['v_lru_w_o'], 'v_ffn_w_gu': out['v_ffn_w_gu'], 'v_ffn_w_down': out['v_ffn_w_down']}


def _loss(weights, diff, rest, loss_target):
    with _jax.named_scope("forward"):
        args = {**rest, TWIN_DIFF_INPUT: diff, **{k: w.astype(_WEIGHT_DTYPES[k]) for k, w in weights.items()}}
        y = _forward(args)
    with _jax.named_scope("loss_head"):
        err = _jnp.square(y.astype(_jnp.float32) - loss_target)
        return 0.5 * _jnp.sum(_jnp.mean(err, axis=-1)) if err.ndim else 0.5 * err


def _adamw(w, g, m, v):
    m = ADAM_B1 * m + (1.0 - ADAM_B1) * g
    v = ADAM_B2 * v + (1.0 - ADAM_B2) * _jnp.square(g)
    m_hat = m / (1.0 - ADAM_B1 ** ADAM_STEP)
    v_hat = v / (1.0 - ADAM_B2 ** ADAM_STEP)
    delta = -ADAM_LR * (m_hat / (_jnp.sqrt(v_hat) + ADAM_EPS) + ADAM_WD * w)
    return delta, m, v


def reference(x, meta_tokens, norm_mix, norm_ffn, norm_final, mla_w_in, mla_q_norm, mla_kv_norm, mla_w_uq, mla_w_ukv, mla_w_o, lru_w_in, lru_conv_w, lru_conv_b, lru_w_gate_a, lru_b_gate_a, lru_w_gate_x, lru_b_gate_x, lru_lambda, lru_w_o, ffn_w_gu, ffn_w_down, loss_target, m_meta_tokens, m_norm_mix, m_norm_ffn, m_norm_final, m_mla_w_in, m_mla_q_norm, m_mla_kv_norm, m_mla_w_uq, m_mla_w_ukv, m_mla_w_o, m_lru_w_in, m_lru_conv_w, m_lru_conv_b, m_lru_w_gate_a, m_lru_b_gate_a, m_lru_w_gate_x, m_lru_b_gate_x, m_lru_lambda, m_lru_w_o, m_ffn_w_gu, m_ffn_w_down, v_meta_tokens, v_norm_mix, v_norm_ffn, v_norm_final, v_mla_w_in, v_mla_q_norm, v_mla_kv_norm, v_mla_w_uq, v_mla_w_ukv, v_mla_w_o, v_lru_w_in, v_lru_conv_w, v_lru_conv_b, v_lru_w_gate_a, v_lru_b_gate_a, v_lru_w_gate_x, v_lru_b_gate_x, v_lru_lambda, v_lru_w_o, v_ffn_w_gu, v_ffn_w_down):
    given = dict(x=x, meta_tokens=meta_tokens, norm_mix=norm_mix, norm_ffn=norm_ffn, norm_final=norm_final, mla_w_in=mla_w_in, mla_q_norm=mla_q_norm, mla_kv_norm=mla_kv_norm, mla_w_uq=mla_w_uq, mla_w_ukv=mla_w_ukv, mla_w_o=mla_w_o, lru_w_in=lru_w_in, lru_conv_w=lru_conv_w, lru_conv_b=lru_conv_b, lru_w_gate_a=lru_w_gate_a, lru_b_gate_a=lru_b_gate_a, lru_w_gate_x=lru_w_gate_x, lru_b_gate_x=lru_b_gate_x, lru_lambda=lru_lambda, lru_w_o=lru_w_o, ffn_w_gu=ffn_w_gu, ffn_w_down=ffn_w_down, loss_target=loss_target, m_meta_tokens=m_meta_tokens, m_norm_mix=m_norm_mix, m_norm_ffn=m_norm_ffn, m_norm_final=m_norm_final, m_mla_w_in=m_mla_w_in, m_mla_q_norm=m_mla_q_norm, m_mla_kv_norm=m_mla_kv_norm, m_mla_w_uq=m_mla_w_uq, m_mla_w_ukv=m_mla_w_ukv, m_mla_w_o=m_mla_w_o, m_lru_w_in=m_lru_w_in, m_lru_conv_w=m_lru_conv_w, m_lru_conv_b=m_lru_conv_b, m_lru_w_gate_a=m_lru_w_gate_a, m_lru_b_gate_a=m_lru_b_gate_a, m_lru_w_gate_x=m_lru_w_gate_x, m_lru_b_gate_x=m_lru_b_gate_x, m_lru_lambda=m_lru_lambda, m_lru_w_o=m_lru_w_o, m_ffn_w_gu=m_ffn_w_gu, m_ffn_w_down=m_ffn_w_down, v_meta_tokens=v_meta_tokens, v_norm_mix=v_norm_mix, v_norm_ffn=v_norm_ffn, v_norm_final=v_norm_final, v_mla_w_in=v_mla_w_in, v_mla_q_norm=v_mla_q_norm, v_mla_kv_norm=v_mla_kv_norm, v_mla_w_uq=v_mla_w_uq, v_mla_w_ukv=v_mla_w_ukv, v_mla_w_o=v_mla_w_o, v_lru_w_in=v_lru_w_in, v_lru_conv_w=v_lru_conv_w, v_lru_conv_b=v_lru_conv_b, v_lru_w_gate_a=v_lru_w_gate_a, v_lru_b_gate_a=v_lru_b_gate_a, v_lru_w_gate_x=v_lru_w_gate_x, v_lru_b_gate_x=v_lru_b_gate_x, v_lru_lambda=v_lru_lambda, v_lru_w_o=v_lru_w_o, v_ffn_w_gu=v_ffn_w_gu, v_ffn_w_down=v_ffn_w_down)
    weights = {n: given[n] for n in TWIN_WEIGHTS}
    shared = {n: given[n] for n in SHARED_INPUTS}
    per_example = {n: given[n] for n in ['x']}
    grad_fn = _jax.value_and_grad(_loss, argnums=(0, 1))

    def one_microbatch(ex, loss_target):
        ex = dict(ex)
        diff = ex.pop(TWIN_DIFF_INPUT)
        return grad_fn(weights, diff, {**shared, **ex}, loss_target)

    if N_MICROBATCH == 1:
        loss, (grad_w, grad_x) = one_microbatch(per_example, given["loss_target"])
    else:
        def body(carry, xs):
            loss_sum, grad_sum = carry
            l_k, (gw_k, gx_k) = one_microbatch(xs[0], xs[1])
            with _jax.named_scope("update"):
                return (loss_sum + l_k, _jax.tree.map(_jnp.add, grad_sum, gw_k)), gx_k

        init = (_jnp.zeros((), _jnp.float32), _jax.tree.map(_jnp.zeros_like, weights))
        (loss, grad_w), grad_x = _jax.lax.scan(body, init, (per_example, given["loss_target"]))
    with _jax.named_scope("update"):
        delta_w, new_m, new_v = {}, {}, {}
        for n in TWIN_WEIGHTS:
            delta_w[n], new_m[n], new_v[n] = _adamw(weights[n], grad_w[n], given["m_" + n], given["v_" + n])
    return (loss, grad_x, *[grad_w[n] for n in TWIN_WEIGHTS], *[delta_w[n] for n in TWIN_WEIGHTS],
            *[new_m[n] for n in TWIN_WEIGHTS], *[new_v[n] for n in TWIN_WEIGHTS])
```

```python
import math

import jax
import jax.numpy as jnp
from jax import lax
from jax.experimental import pallas as pl
from jax.experimental.pallas import tpu as pltpu

F32 = jnp.float32
CDT = jnp.bfloat16
MESH = pl.DeviceIdType.MESH

D_MODEL = 2048
SEQ = 4096
DEPTH = 4
CHUNK = 64
N_META = 16
MLA_HEADS = 16
Q_LORA = 512
KV_LORA = 512
QK_NOPE = 128
QK_ROPE = 64
V_HEAD = 128
ROPE_THETA = 10000.0
RNN_BLOCKS = 16
CONV_W = 4
LRU_C = 8.0
D_FF = 5632
RMS_EPS = 1e-6
NEG_BIG = -1e30
ADAM_LR = 0.001
ADAM_B1 = 0.9
ADAM_B2 = 0.999
ADAM_EPS = 1e-08
ADAM_WD = 0.01
ADAM_STEP = 10

N_CHIPS = 4
N_DEV = 8
LANES = 128
VMEM_LIMIT = 52 * 1024 * 1024
ROW_TILE = 384
MM_ROW_TILE = 704
ATT_TILE = 384
SCAN_COLS = 128


def _div_tile(n, pref, mult):
    if n <= pref:
        return n
    d = (pref // mult) * mult
    while d >= mult:
        if n % d == 0:
            return d
        d -= mult
    raise ValueError(f"no tile for {n} <= {pref} (multiple of {mult})")


def _t_pad():
    t = N_META + SEQ
    step = math.lcm(_row_tile_unit(), 8)
    return -(-t // step) * step


def _row_tile_unit():
    return math.lcm(math.lcm(ROW_TILE, MM_ROW_TILE), ATT_TILE)


def _params(*sem):
    return pltpu.CompilerParams(dimension_semantics=sem, vmem_limit_bytes=VMEM_LIMIT)


def _b_spec(form, b, kind, layer, t_out, t_con):
    if kind == "plain":
        if form == "nn":
            return pl.BlockSpec((t_con, t_out), lambda i, j, k: (k, j))
        return pl.BlockSpec((t_out, t_con), lambda i, j, k: (j, k))
    rows, cols = b.shape[2], b.shape[3]
    if form == "nn":
        blk = (None, None, t_con, t_out)
        if kind == "row":
            per = rows // t_con
            return pl.BlockSpec(blk, lambda i, j, k: (k // per, layer, k % per, j))
        per = cols // t_out
        return pl.BlockSpec(blk, lambda i, j, k: (j // per, layer, k, j % per))
    blk = (None, None, t_out, t_con)
    if kind == "row":
        per = rows // t_out
        return pl.BlockSpec(blk, lambda i, j, k: (j // per, layer, j % per, k))
    per = cols // t_con
    return pl.BlockSpec(blk, lambda i, j, k: (k // per, layer, j, k % per))


def _mm_body(nk, dims, has_resid):
    def body(*refs):
        if has_resid:
            a_ref, b_ref, r_ref, o_ref = refs[:4]
        else:
            a_ref, b_ref, o_ref = refs[:3]
        prod = lax.dot_general(a_ref[...].astype(CDT), b_ref[...].astype(CDT), (dims, ((), ())),
                               preferred_element_type=F32)

        def finish(acc):
            if has_resid:
                acc = acc + r_ref[...]
            o_ref[...] = acc.astype(o_ref.dtype)

        if nk == 1:
            finish(prod)
            return
        acc_ref = refs[-1]
        k = pl.program_id(2)

        @pl.when(k == 0)
        def _():
            acc_ref[...] = prod

        @pl.when(k > 0)
        def _():
            acc_ref[...] += prod

        @pl.when(k == nk - 1)
        def _():
            finish(acc_ref[...])

    return body


def _mm(form, a, b, *, kind="plain", layer=0, out_dtype=F32, resid=None, tm, tn, tk, name):
    m, con = a.shape
    if kind == "plain":
        w_rows, w_cols = b.shape
    elif kind == "row":
        w_rows, w_cols = b.shape[0] * b.shape[2], b.shape[3]
    else:
        w_rows, w_cols = b.shape[2], b.shape[0] * b.shape[3]
    n_out = w_cols if form == "nn" else w_rows
    assert con == (w_rows if form == "nn" else w_cols), (name, a.shape, b.shape)
    nk = con // tk
    assert m % tm == 0 and n_out % tn == 0 and con % tk == 0, (name, m, n_out, con, tm, tn, tk)
    dims = ((1,), (0,)) if form == "nn" else ((1,), (1,))
    in_specs = [pl.BlockSpec((tm, tk), lambda i, j, k: (i, k)), _b_spec(form, b, kind, layer, tn, tk)]
    args = [a, b]
    if resid is not None:
        in_specs.append(pl.BlockSpec((tm, tn), lambda i, j, k: (i, j)))
        args.append(resid)
    return pl.pallas_call(
        _mm_body(nk, dims, resid is not None),
        grid=(m // tm, n_out // tn, nk),
        in_specs=in_specs,
        out_specs=pl.BlockSpec((tm, tn), lambda i, j, k: (i, j)),
        out_shape=jax.ShapeDtypeStruct((m, n_out), out_dtype),
        scratch_shapes=[pltpu.VMEM((tm, tn), F32)] if nk > 1 else [],
        compiler_params=_params("parallel", "parallel", "arbitrary"),
        name=name,
    )(*args)


def _mm_tn(a, b, *, into=None, kind="plain", layer=0, n_layers=1, tm, tn, tk, name):
    t, m = a.shape
    n = b.shape[1]
    nk = t // tk
    assert t % tk == 0 and m % tm == 0 and n % tn == 0, (name, t, m, n, tm, tn, tk)
    in_specs = [pl.BlockSpec((tk, tm), lambda i, j, k: (k, i)), pl.BlockSpec((tk, tn), lambda i, j, k: (k, j))]
    args = [a, b]
    aliases = {}
    if kind == "plain":
        out_shape = jax.ShapeDtypeStruct((m, n), F32)
        out_spec = pl.BlockSpec((tm, tn), lambda i, j, k: (i, j))
    else:
        blk = (None, None, tm, tn)
        if kind == "row":
            per = (m // N_CHIPS) // tm
            out_shape = jax.ShapeDtypeStruct((N_CHIPS, n_layers, m // N_CHIPS, n), F32)
            out_spec = pl.BlockSpec(blk, lambda i, j, k: (i // per, layer, i % per, j))
        else:
            per = (n // N_CHIPS) // tn
            out_shape = jax.ShapeDtypeStruct((N_CHIPS, n_layers, m, n // N_CHIPS), F32)
            out_spec = pl.BlockSpec(blk, lambda i, j, k: (j // per, layer, i, j % per))
        assert per >= 1, name
        if into is not None:
            in_specs.append(pl.BlockSpec(memory_space=pl.ANY))
            args.append(into)
            aliases = {2: 0}
    body = _mm_body(nk, ((0,), (0,)), False)

    def kernel_fn(*refs):
        if into is not None and kind != "plain":
            refs = refs[:2] + refs[3:]
        body(*refs)

    return pl.pallas_call(
        kernel_fn,
        grid=(m // tm, n // tn, nk),
        in_specs=in_specs,
        out_specs=out_spec,
        out_shape=out_shape,
        scratch_shapes=[pltpu.VMEM((tm, tn), F32)] if nk > 1 else [],
        input_output_aliases=aliases,
        compiler_params=_params("parallel", "parallel", "arbitrary"),
        name=name,
    )(*args)


def _rms_fwd(x, g, *, width, col_block, name):
    tp = x.shape[0]
    tr = _div_tile(tp, ROW_TILE, 8)

    def body(x_ref, g_ref, o_ref):
        xf = x_ref[...]
        r = lax.rsqrt(jnp.mean(xf * xf, axis=-1, keepdims=True) + RMS_EPS)
        o_ref[...] = ((xf * r) * g_ref[...]).astype(o_ref.dtype)

    return pl.pallas_call(
        body,
        grid=(tp // tr,),
        in_specs=[pl.BlockSpec((tr, width), lambda i: (i, col_block)), pl.BlockSpec((1, width), lambda i: (0, 0))],
        out_specs=pl.BlockSpec((tr, width), lambda i: (i, 0)),
        out_shape=jax.ShapeDtypeStruct((tp, width), CDT),
        compiler_params=_params("parallel"),
        name=name,
    )(x, g.reshape(1, width))


def _rms_bwd(x, g, dy, resid, *, width, col_block, name):
    tp = x.shape[0]
    tr = _div_tile(tp, ROW_TILE, 8)
    has_resid = resid is not None

    def body(*refs):
        if has_resid:
            x_ref, g_ref, dy_ref, res_ref, dx_ref, dxb_ref, dg_ref = refs
        else:
            x_ref, g_ref, dy_ref, dx_ref, dxb_ref, dg_ref = refs
        i = pl.program_id(0)
        xf = x_ref[...]
        r = lax.rsqrt(jnp.mean(xf * xf, axis=-1, keepdims=True) + RMS_EPS)
        xh = xf * r
        dy = dy_ref[...].astype(F32)
        dg = jnp.sum(dy * xh, axis=0, keepdims=True)
        dxh = dy * g_ref[...]
        dx = r * (dxh - xh * jnp.mean(dxh * xh, axis=-1, keepdims=True))
        if has_resid:
            dx = dx + res_ref[...]
        dx_ref[...] = dx
        dxb_ref[...] = dx.astype(CDT)

        @pl.when(i == 0)
        def _():
            dg_ref[...] = dg

        @pl.when(i > 0)
        def _():
            dg_ref[...] += dg

    row = pl.BlockSpec((tr, width), lambda i: (i, 0))
    in_specs = [pl.BlockSpec((tr, width), lambda i: (i, col_block)), pl.BlockSpec((1, width), lambda i: (0, 0)), row]
    args = [x, g.reshape(1, width), dy]
    if has_resid:
        in_specs.append(row)
        args.append(resid)
    return pl.pallas_call(
        body,
        grid=(tp // tr,),
        in_specs=in_specs,
        out_specs=[row, row, pl.BlockSpec((1, width), lambda i: (0, 0))],
        out_shape=[jax.ShapeDtypeStruct((tp, width), F32), jax.ShapeDtypeStruct((tp, width), CDT),
                   jax.ShapeDtypeStruct((1, width), F32)],
        compiler_params=_params("arbitrary"),
        name=name,
    )(*args)


def _final_loss(h, g, target):
    tp, d = h.shape
    tr = _div_tile(tp, ROW_TILE, 8)

    def body(h_ref, g_ref, t_ref, dh_ref, dhb_ref, dg_ref, loss_ref):
        i = pl.program_id(0)
        xf = h_ref[...]
        r = lax.rsqrt(jnp.mean(xf * xf, axis=-1, keepdims=True) + RMS_EPS)
        xh = xf * r
        gain = g_ref[...]
        y = xh * gain
        rows = i * tr + lax.broadcasted_iota(jnp.int32, (tr, 1), 0)
        valid = jnp.logical_and(rows >= N_META, rows < N_META + SEQ)
        err = jnp.where(valid, y - t_ref[...], 0.0)
        part = 0.5 * jnp.sum(jnp.mean(err * err, axis=-1, keepdims=True), axis=0, keepdims=True)
        dy = err * (1.0 / d)
        dg = jnp.sum(dy * xh, axis=0, keepdims=True)
        dxh = dy * gain
        dx = r * (dxh - xh * jnp.mean(dxh * xh, axis=-1, keepdims=True))
        dh_ref[...] = dx
        dhb_ref[...] = dx.astype(CDT)

        @pl.when(i == 0)
        def _():
            dg_ref[...] = dg
            loss_ref[...] = part

        @pl.when(i > 0)
        def _():
            dg_ref[...] += dg
            loss_ref[...] += part

    row = pl.BlockSpec((tr, d), lambda i: (i, 0))
    vec = pl.BlockSpec((1, d), lambda i: (0, 0))
    return pl.pallas_call(
        body,
        grid=(tp // tr,),
        in_specs=[row, vec, row],
        out_specs=[row, row, vec, pl.BlockSpec((1, 1), lambda i: (0, 0))],
        out_shape=[jax.ShapeDtypeStruct((tp, d), F32), jax.ShapeDtypeStruct((tp, d), CDT),
                   jax.ShapeDtypeStruct((1, d), F32), jax.ShapeDtypeStruct((1, 1), F32)],
        compiler_params=_params("arbitrary"),
        name="final_loss",
    )(h, g.reshape(1, d), target)


def _sigmoid(x):
    return 1.0 / (1.0 + jnp.exp(-x))


def _swiglu_fwd(gu):
    tp, f2 = gu.shape
    f = f2 // 2
    tr = _div_tile(tp, ROW_TILE, 8)
    tf = _div_tile(f, 1408, LANES)
    nf = f // tf

    def body(g_ref, u_ref, o_ref):
        g = g_ref[...]
        o_ref[...] = ((g * _sigmoid(g)) * u_ref[...]).astype(o_ref.dtype)

    return pl.pallas_call(
        body,
        grid=(tp // tr, nf),
        in_specs=[pl.BlockSpec((tr, tf), lambda i, j: (i, j)), pl.BlockSpec((tr, tf), lambda i, j: (i, j + nf))],
        out_specs=pl.BlockSpec((tr, tf), lambda i, j: (i, j)),
        out_shape=jax.ShapeDtypeStruct((tp, f), CDT),
        compiler_params=_params("parallel", "parallel"),
        name="swiglu_fwd",
    )(gu, gu)


def _swiglu_bwd(gu, da):
    tp, f2 = gu.shape
    f = f2 // 2
    tr = _div_tile(tp, ROW_TILE, 8)
    tf = _div_tile(f, 1408, LANES)
    nf = f // tf

    def body(g_ref, u_ref, da_ref, o_ref):
        j = pl.program_id(1)
        g = g_ref[...]
        da = da_ref[...]
        sg = _sigmoid(g)

        @pl.when(j < nf)
        def _():
            o_ref[...] = (da * u_ref[...] * (sg * (1.0 + g * (1.0 - sg)))).astype(o_ref.dtype)

        @pl.when(j >= nf)
        def _():
            o_ref[...] = (da * (g * sg)).astype(o_ref.dtype)

    return pl.pallas_call(
        body,
        grid=(tp // tr, 2 * nf),
        in_specs=[pl.BlockSpec((tr, tf), lambda i, j: (i, j % nf)),
                  pl.BlockSpec((tr, tf), lambda i, j: (i, nf + j % nf)),
                  pl.BlockSpec((tr, tf), lambda i, j: (i, j % nf))],
        out_specs=pl.BlockSpec((tr, tf), lambda i, j: (i, j)),
        out_shape=jax.ShapeDtypeStruct((tp, f2), CDT),
        compiler_params=_params("parallel", "parallel"),
        name="swiglu_bwd",
    )(gu, gu, da)


def _swap_halves(x):
    lane = lax.broadcasted_iota(jnp.int32, x.shape, x.ndim - 1)
    first = (lane % QK_ROPE) < (QK_ROPE // 2)
    return jnp.where(first, pltpu.roll(x, LANES - QK_ROPE // 2, x.ndim - 1), pltpu.roll(x, QK_ROPE // 2, x.ndim - 1))


def _rope_tables(tp):
    pos = jnp.arange(tp, dtype=F32)
    inv_freq = ROPE_THETA ** (-jnp.arange(0, QK_ROPE, 2, dtype=F32) / QK_ROPE)
    ang = pos[:, None] * inv_freq[None, :]
    cos, sin = jnp.cos(ang), jnp.sin(ang)
    reps = LANES // QK_ROPE
    return jnp.tile(jnp.concatenate([cos, cos], -1), (1, reps)), jnp.tile(jnp.concatenate([-sin, sin], -1), (1, reps))


def _chunk_of(pos):
    shift = CHUNK.bit_length() - 1
    assert CHUNK == 1 << shift
    return jnp.where(pos < N_META, 0, 1 + lax.shift_right_arithmetic(pos - N_META, shift))


def _mla_prep(q, proj, cos2, sin2):
    tp = q.shape[0]
    tr = _div_tile(tp, ROW_TILE, 8)
    wr = MLA_HEADS * QK_ROPE
    nope_blocks = (MLA_HEADS * QK_NOPE) // wr
    kr_block = (Q_LORA + KV_LORA) // LANES

    def body(q_ref, kr_ref, c_ref, s_ref, qr_out, kr_out):
        c = c_ref[...]
        s = s_ref[...]
        for p in range(wr // LANES):
            x = q_ref[:, p * LANES:(p + 1) * LANES]
            qr_out[:, p * LANES:(p + 1) * LANES] = (x * c + _swap_halves(x) * s).astype(CDT)
        k = kr_ref[...]
        k = k + pltpu.roll(k, QK_ROPE, 1)
        kr_out[...] = (k * c + _swap_halves(k) * s).astype(CDT)

    tab = pl.BlockSpec((tr, LANES), lambda i: (i, 0))
    return pl.pallas_call(
        body,
        grid=(tp // tr,),
        in_specs=[pl.BlockSpec((tr, wr), lambda i: (i, nope_blocks)), pl.BlockSpec((tr, LANES), lambda i: (i, kr_block)),
                  tab, tab],
        out_specs=[pl.BlockSpec((tr, wr), lambda i: (i, 0)), tab],
        out_shape=[jax.ShapeDtypeStruct((tp, wr), CDT), jax.ShapeDtypeStruct((tp, LANES), CDT)],
        compiler_params=_params("parallel"),
        name="mla_prep",
    )(q, proj, cos2, sin2)


def _head_half(x, h):
    lane = lax.broadcasted_iota(jnp.int32, x.shape, x.ndim - 1)
    return jnp.where((lane // QK_ROPE) == (h % 2), x, jnp.zeros_like(x))


def _dot_nt(a, b):
    return lax.dot_general(a, b, (((1,), (1,)), ((), ())), preferred_element_type=F32)


def _dot(a, b):
    return jnp.dot(a, b, preferred_element_type=F32)


def _last_key_block(i, bq, bk, nk):
    cq = 1 + (i * bq + bq - 1 - N_META) // CHUNK
    return jnp.minimum((N_META + CHUNK * cq - 1) // bk, nk - 1)


def _first_query_block(j, bk, bq):
    p0 = N_META + CHUNK * (jnp.maximum(j * bk - N_META, 0) // CHUNK)
    return p0 // bq


def _attn_specs(bq, bk, qi, ki):
    return [
        pl.BlockSpec((bq, QK_NOPE), lambda h, a, b: (qi(a, b), h)),
        pl.BlockSpec((bq, LANES), lambda h, a, b: (qi(a, b), h // 2)),
        pl.BlockSpec((bk, QK_NOPE), lambda h, a, b: (ki(a, b), 2 * h)),
        pl.BlockSpec((bk, LANES), lambda h, a, b: (ki(a, b), 0)),
        pl.BlockSpec((bk, V_HEAD), lambda h, a, b: (ki(a, b), 2 * h + 1)),
    ]


def _masked_scores(qn, qrm, kn, kr2, q0, k0, scale, transposed):
    if transposed:
        s = (_dot_nt(kn, qn) + _dot_nt(kr2, qrm)) * scale
        kc = _chunk_of(k0 + lax.broadcasted_iota(jnp.int32, (s.shape[0], 1), 0))
        qc = _chunk_of(q0 + lax.broadcasted_iota(jnp.int32, (1, s.shape[1]), 1))
    else:
        s = (_dot_nt(qn, kn) + _dot_nt(qrm, kr2)) * scale
        qc = _chunk_of(q0 + lax.broadcasted_iota(jnp.int32, (s.shape[0], 1), 0))
        kc = _chunk_of(k0 + lax.broadcasted_iota(jnp.int32, (1, s.shape[1]), 1))
    return jnp.where(kc <= qc, s, NEG_BIG)


def _attn_fwd(q, qr, kv, kr2):
    tp = q.shape[0]
    bq = bk = _div_tile(tp, ATT_TILE, LANES)
    nq, nk = tp // bq, tp // bk
    scale = (QK_NOPE + QK_ROPE) ** -0.5

    def body(qn_ref, qr_ref, kn_ref, kr_ref, v_ref, o_ref, lse_ref, m_ref, l_ref, acc_ref):
        h, i, j = pl.program_id(0), pl.program_id(1), pl.program_id(2)

        @pl.when(j == 0)
        def _():
            m_ref[...] = jnp.full(m_ref.shape, NEG_BIG, F32)
            l_ref[...] = jnp.zeros(l_ref.shape, F32)
            acc_ref[...] = jnp.zeros(acc_ref.shape, F32)

        @pl.when(j <= _last_key_block(i, bq, bk, nk))
        def _():
            s = _masked_scores(qn_ref[...].astype(CDT), _head_half(qr_ref[...], h), kn_ref[...], kr_ref[...],
                               i * bq, j * bk, scale, False)
            m_old = m_ref[...]
            m_new = jnp.maximum(m_old, jnp.max(s, axis=-1, keepdims=True))
            alpha = jnp.exp(m_old - m_new)
            p = jnp.exp(s - m_new)
            l_ref[...] = alpha * l_ref[...] + jnp.sum(p, axis=-1, keepdims=True)
            acc_ref[...] = alpha * acc_ref[...] + _dot(p.astype(CDT), v_ref[...])
            m_ref[...] = m_new

        @pl.when(j == nk - 1)
        def _():
            o_ref[...] = acc_ref[...] / l_ref[...]
            lse_ref[...] = m_ref[...] + jnp.log(l_ref[...])

    return pl.pallas_call(
        body,
        grid=(MLA_HEADS, nq, nk),
        in_specs=_attn_specs(bq, bk, lambda i, j: i, lambda i, j: jnp.minimum(j, _last_key_block(i, bq, bk, nk))),
        out_specs=[pl.BlockSpec((bq, V_HEAD), lambda h, i, j: (i, h)), pl.BlockSpec((None, bq, 1), lambda h, i, j: (h, i, 0))],
        out_shape=[jax.ShapeDtypeStruct((tp, MLA_HEADS * V_HEAD), F32), jax.ShapeDtypeStruct((MLA_HEADS, tp, 1), F32)],
        scratch_shapes=[pltpu.VMEM((bq, 1), F32), pltpu.VMEM((bq, 1), F32), pltpu.VMEM((bq, V_HEAD), F32)],
        compiler_params=_params("parallel", "parallel", "arbitrary"),
        name="attn_fwd",
    )(q, qr, kv, kr2, kv)


def _attn_delta(d_out, out):
    tp = out.shape[0]
    tr = _div_tile(tp, ROW_TILE, 8)

    def body(do_ref, o_ref, d_ref):
        for h in range(MLA_HEADS):
            cols = slice(h * V_HEAD, (h + 1) * V_HEAD)
            d_ref[h] = jnp.sum(do_ref[:, cols] * o_ref[:, cols], axis=-1, keepdims=True)

    row = pl.BlockSpec((tr, MLA_HEADS * V_HEAD), lambda i: (i, 0))
    return pl.pallas_call(
        body,
        grid=(tp // tr,),
        in_specs=[row, row],
        out_specs=pl.BlockSpec((MLA_HEADS, tr, 1), lambda i: (0, i, 0)),
        out_shape=jax.ShapeDtypeStruct((MLA_HEADS, tp, 1), F32),
        compiler_params=_params("parallel"),
        name="attn_delta",
    )(d_out, out)


def _attn_bwd_kv(q, qr, kv, kr2, d_out, lse_row, delta_row):
    tp = q.shape[0]
    bq = bk = _div_tile(tp, ATT_TILE, LANES)
    nq, nk = tp // bq, tp // bk
    scale = (QK_NOPE + QK_ROPE) ** -0.5

    def body(qn_ref, qr_ref, kn_ref, kr_ref, v_ref, do_ref, lse_ref, dl_ref, dkv_ref, dkr_ref, dkn_acc, dv_acc, dkr_acc):
        h, j, i = pl.program_id(0), pl.program_id(1), pl.program_id(2)

        @pl.when(i == 0)
        def _():
            dkn_acc[...] = jnp.zeros(dkn_acc.shape, F32)
            dv_acc[...] = jnp.zeros(dv_acc.shape, F32)
            dkr_acc[...] = jnp.zeros(dkr_acc.shape, F32)

        @pl.when(i >= _first_query_block(j, bk, bq))
        def _():
            qn = qn_ref[...].astype(CDT)
            qrm = _head_half(qr_ref[...], h)
            do = do_ref[...].astype(CDT)
            s_t = _masked_scores(qn, qrm, kn_ref[...], kr_ref[...], i * bq, j * bk, scale, True)
            p_t = jnp.exp(s_t - lse_ref[...])
            dv_acc[...] += _dot(p_t.astype(CDT), do)
            dp_t = _dot_nt(v_ref[...], do)
            ds_t = (p_t * (dp_t - dl_ref[...]) * scale).astype(CDT)
            dkn_acc[...] += _dot(ds_t, qn)
            dkr_acc[...] += _dot(ds_t, qrm)

        @pl.when(i == nq - 1)
        def _():
            dkv_ref[:, :QK_NOPE] = dkn_acc[...].astype(CDT)
            dkv_ref[:, QK_NOPE:] = dv_acc[...].astype(CDT)
            dkr_ref[...] = dkr_acc[...]

    qi = lambda j, i: jnp.maximum(i, _first_query_block(j, bk, bq))
    ki = lambda j, i: j
    stat = pl.BlockSpec((None, 1, bq), lambda h, j, i: (h, 0, qi(j, i)))
    return pl.pallas_call(
        body,
        grid=(MLA_HEADS, nk, nq),
        in_specs=_attn_specs(bq, bk, qi, ki) + [pl.BlockSpec((bq, V_HEAD), lambda h, j, i: (qi(j, i), h)), stat, stat],
        out_specs=[pl.BlockSpec((bk, QK_NOPE + V_HEAD), lambda h, j, i: (j, h)),
                   pl.BlockSpec((None, bk, LANES), lambda h, j, i: (h, j, 0))],
        out_shape=[jax.ShapeDtypeStruct((tp, MLA_HEADS * (QK_NOPE + V_HEAD)), CDT),
                   jax.ShapeDtypeStruct((MLA_HEADS, tp, LANES), F32)],
        scratch_shapes=[pltpu.VMEM((bk, QK_NOPE), F32), pltpu.VMEM((bk, V_HEAD), F32), pltpu.VMEM((bk, LANES), F32)],
        compiler_params=_params("parallel", "parallel", "arbitrary"),
        name="attn_bwd_kv",
    )(q, qr, kv, kr2, kv, d_out, lse_row, delta_row)


def _attn_bwd_q(q, qr, kv, kr2, d_out, lse, delta):
    tp = q.shape[0]
    bq = bk = _div_tile(tp, ATT_TILE, LANES)
    nq, nk = tp // bq, tp // bk
    scale = (QK_NOPE + QK_ROPE) ** -0.5

    def body(qn_ref, qr_ref, kn_ref, kr_ref, v_ref, do_ref, lse_ref, dl_ref, dqn_ref, dqr_ref, dqn_acc, dqr_acc):
        h, i, j = pl.program_id(0), pl.program_id(1), pl.program_id(2)

        @pl.when(j == 0)
        def _():
            dqn_acc[...] = jnp.zeros(dqn_acc.shape, F32)
            dqr_acc[...] = jnp.zeros(dqr_acc.shape, F32)

        @pl.when(j <= _last_key_block(i, bq, bk, nk))
        def _():
            kn = kn_ref[...]
            kr = kr_ref[...]
            s = _masked_scores(qn_ref[...].astype(CDT), _head_half(qr_ref[...], h), kn, kr, i * bq, j * bk, scale, False)
            p = jnp.exp(s - lse_ref[...])
            dp = _dot_nt(do_ref[...].astype(CDT), v_ref[...])
            ds = (p * (dp - dl_ref[...]) * scale).astype(CDT)
            dqn_acc[...] += _dot(ds, kn)
            dqr_acc[...] += _dot(ds, kr)

        @pl.when(j == nk - 1)
        def _():
            dqn_ref[...] = dqn_acc[...].astype(CDT)
            dqr_ref[...] = _head_half(dqr_acc[...], h)

    qi = lambda i, j: i
    ki = lambda i, j: jnp.minimum(j, _last_key_block(i, bq, bk, nk))
    stat = pl.BlockSpec((None, bq, 1), lambda h, i, j: (h, i, 0))
    return pl.pallas_call(
        body,
        grid=(MLA_HEADS, nq, nk),
        in_specs=_attn_specs(bq, bk, qi, ki) + [pl.BlockSpec((bq, V_HEAD), lambda h, i, j: (i, h)), stat, stat],
        out_specs=[pl.BlockSpec((bq, QK_NOPE), lambda h, i, j: (i, h)), pl.BlockSpec((None, bq, LANES), lambda h, i, j: (h, i, 0))],
        out_shape=[jax.ShapeDtypeStruct((tp, MLA_HEADS * QK_NOPE), CDT), jax.ShapeDtypeStruct((MLA_HEADS, tp, LANES), F32)],
        scratch_shapes=[pltpu.VMEM((bq, QK_NOPE), F32), pltpu.VMEM((bq, LANES), F32)],
        compiler_params=_params("parallel", "parallel", "arbitrary"),
        name="attn_bwd_q",
    )(q, qr, kv, kr2, kv, d_out, lse, delta)


def _mla_unprep(dqr_h, dkr_h, cos2, sin2):
    tp = dqr_h.shape[1]
    tr = _div_tile(tp, ROW_TILE, 8)
    wr = MLA_HEADS * QK_ROPE

    def body(dq_ref, dk_ref, c_ref, s_ref, dqr_out, dkr_out):
        c = c_ref[...]
        s = s_ref[...]
        for p in range(MLA_HEADS // 2):
            x = dq_ref[2 * p] + dq_ref[2 * p + 1]
            dqr_out[:, p * LANES:(p + 1) * LANES] = (x * c - _swap_halves(x) * s).astype(CDT)
        t = dk_ref[0]
        for h in range(1, MLA_HEADS):
            t = t + dk_ref[h]
        t = t * c - _swap_halves(t) * s
        t = t + pltpu.roll(t, QK_ROPE, 1)
        lane = lax.broadcasted_iota(jnp.int32, t.shape, 1)
        dkr_out[...] = jnp.where(lane < QK_ROPE, t, 0.0)

    per_head = pl.BlockSpec((MLA_HEADS, tr, LANES), lambda i: (0, i, 0))
    tab = pl.BlockSpec((tr, LANES), lambda i: (i, 0))
    return pl.pallas_call(
        body,
        grid=(tp // tr,),
        in_specs=[per_head, per_head, tab, tab],
        out_specs=[pl.BlockSpec((tr, wr), lambda i: (i, 0)), tab],
        out_shape=[jax.ShapeDtypeStruct((tp, wr), CDT), jax.ShapeDtypeStruct((tp, LANES), F32)],
        compiler_params=_params("parallel"),
        name="mla_unprep",
    )(dqr_h, dkr_h, cos2, sin2)


HALO = 8


def _softplus(x):
    return jnp.maximum(x, 0.0) + jnp.log1p(jnp.exp(-jnp.abs(x)))


def _one_minus_sq(log_a, a):
    return -jnp.tanh(log_a) * (a * a + 1.0)


def _gelu(y):
    k = math.sqrt(2.0 / math.pi)
    return 0.5 * y * (1.0 + jnp.tanh(k * (y + 0.044715 * (y * y * y))))


def _gelu_grad(y):
    k = math.sqrt(2.0 / math.pi)
    th = jnp.tanh(k * (y + 0.044715 * (y * y * y)))
    return 0.5 * (1.0 + th) + 0.5 * y * (1.0 - th * th) * (k * (1.0 + 3.0 * 0.044715 * (y * y)))


def _lru_gates_fwd(xy, conv_w, conv_b, w_ga, b_ga, w_gx, b_gx, lam):
    tp = xy.shape[0]
    dr = xy.shape[1] // 2
    bw = dr // RNN_BLOCKS
    tr = _div_tile(tp, ROW_TILE, 8)

    def body(x_ref, halo_ref, cw_ref, cb_ref, wa_ref, ba_ref, wx_ref, bx_ref, lam_ref,
             xc_ref, r_ref, i_ref, a_ref, b_ref, xs):
        i = pl.program_id(0)
        xs[0:HALO, :] = jnp.where(i == 0, 0.0, halo_ref[...])
        xs[HALO:, :] = x_ref[...]
        xc = cb_ref[...] + cw_ref[0:1, :] * xs[pl.ds(HALO - CONV_W + 1, tr), :]
        for j in range(1, CONV_W):
            xc = xc + cw_ref[j:j + 1, :] * xs[pl.ds(HALO - CONV_W + 1 + j, tr), :]
        xcb = xc.astype(CDT)
        r = _sigmoid(_dot(xcb, wa_ref[...]) + ba_ref[...])
        ig = _sigmoid(_dot(xcb, wx_ref[...]) + bx_ref[...])
        log_a = (-LRU_C * r) * _softplus(-lam_ref[...])
        a = jnp.exp(log_a)
        xc_ref[...] = xc
        r_ref[...] = r
        i_ref[...] = ig
        a_ref[...] = a
        b_ref[...] = jnp.sqrt(_one_minus_sq(log_a, a)) * (ig * xc)

    blk = pl.BlockSpec((tr, bw), lambda i, n: (i, n))
    vec = pl.BlockSpec((1, bw), lambda i, n: (0, n))
    mat = pl.BlockSpec((None, bw, bw), lambda i, n: (n, 0, 0))
    bias = pl.BlockSpec((None, 1, bw), lambda i, n: (n, 0, 0))
    out = jax.ShapeDtypeStruct((tp, dr), F32)
    return pl.pallas_call(
        body,
        grid=(tp // tr, RNN_BLOCKS),
        in_specs=[blk, pl.BlockSpec((HALO, bw), lambda i, n: (jnp.maximum(i * (tr // HALO) - 1, 0), n)),
                  pl.BlockSpec((CONV_W, bw), lambda i, n: (0, n)), vec, mat, bias, mat, bias, vec],
        out_specs=[blk] * 5,
        out_shape=[out] * 5,
        scratch_shapes=[pltpu.VMEM((tr + HALO, bw), F32)],
        compiler_params=_params("parallel", "parallel"),
        name="lru_gates_fwd",
    )(xy, xy, conv_w, conv_b.reshape(1, dr), w_ga.astype(CDT), b_ga.reshape(RNN_BLOCKS, 1, bw),
      w_gx.astype(CDT), b_gx.reshape(RNN_BLOCKS, 1, bw), lam.reshape(1, dr))


def _stack_rows(rows):
    idx = lax.broadcasted_iota(jnp.int32, (len(rows), rows[0].shape[1]), 0)
    out = jnp.broadcast_to(rows[0], idx.shape)
    for j in range(1, len(rows)):
        out = jnp.where(idx == j, jnp.broadcast_to(rows[j], idx.shape), out)
    return out


def _lru_scan_fwd(a, b, xy):
    tp, dr = a.shape
    cw = SCAN_COLS
    ycol0 = dr // cw

    def body(a_ref, b_ref, y_ref, hs_ref, m_ref):
        def group(g, h):
            base = pl.multiple_of(g * 8, 8)
            at = a_ref[pl.ds(base, 8), :]
            bt = b_ref[pl.ds(base, 8), :]
            rows = []
            for j in range(8):
                h = at[j:j + 1, :] * h + bt[j:j + 1, :]
                rows.append(h)
            hs_ref[pl.ds(base, 8), :] = _stack_rows(rows)
            return h

        lax.fori_loop(0, tp // 8, group, jnp.zeros((1, cw), F32))
        m_ref[...] = (hs_ref[...] * _gelu(y_ref[...])).astype(CDT)

    col = pl.BlockSpec((tp, cw), lambda n: (0, n))
    return pl.pallas_call(
        body,
        grid=(dr // cw,),
        in_specs=[col, col, pl.BlockSpec((tp, cw), lambda n: (0, ycol0 + n))],
        out_specs=[col, col],
        out_shape=[jax.ShapeDtypeStruct((tp, dr), F32), jax.ShapeDtypeStruct((tp, dr), CDT)],
        compiler_params=_params("parallel"),
        name="lru_scan_fwd",
    )(a, b, xy)


def _lru_scan_bwd(a, hs, dm, xy):
    tp, dr = a.shape
    cw = SCAN_COLS
    ycol0 = dr // cw
    ng = tp // 8

    def body(a_ref, hs_ref, dm_ref, y_ref, db_ref, da_ref, dy_ref):
        y = y_ref[...]
        dm = dm_ref[...]
        db_ref[...] = dm * _gelu(y)
        dy_ref[...] = (dm * hs_ref[...] * _gelu_grad(y)).astype(CDT)

        def group(k, carry):
            g_next, a_next = carry
            g = ng - 1 - k
            base = pl.multiple_of(g * 8, 8)
            prev = pl.multiple_of(jnp.maximum(g - 1, 0) * 8, 8)
            dt = db_ref[pl.ds(base, 8), :]
            at = a_ref[pl.ds(base, 8), :]
            ht = hs_ref[pl.ds(base, 8), :]
            h_before = jnp.where(g == 0, 0.0, hs_ref[pl.ds(prev, 8), :][7:8, :])
            g_rows = [None] * 8
            da_rows = [None] * 8
            for j in range(7, -1, -1):
                g_cur = dt[j:j + 1, :] + a_next * g_next
                g_rows[j] = g_cur
                da_rows[j] = g_cur * (ht[j - 1:j, :] if j > 0 else h_before)
                g_next = g_cur
                a_next = at[j:j + 1, :]
            db_ref[pl.ds(base, 8), :] = _stack_rows(g_rows)
            da_ref[pl.ds(base, 8), :] = _stack_rows(da_rows)
            return g_next, a_next

        zero = jnp.zeros((1, cw), F32)
        lax.fori_loop(0, ng, group, (zero, zero))

    col = pl.BlockSpec((tp, cw), lambda n: (0, n))
    return pl.pallas_call(
        body,
        grid=(dr // cw,),
        in_specs=[col, col, col, pl.BlockSpec((tp, cw), lambda n: (0, ycol0 + n))],
        out_specs=[col, col, col],
        out_shape=[jax.ShapeDtypeStruct((tp, dr), F32), jax.ShapeDtypeStruct((tp, dr), F32),
                   jax.ShapeDtypeStruct((tp, dr), CDT)],
        compiler_params=_params("parallel"),
        name="lru_scan_bwd",
    )(a, hs, dm, xy)


def _lru_gates_bwd(db, da, xc, r, ig, a, lam, w_ga, w_gx):
    tp, dr = xc.shape
    bw = dr // RNN_BLOCKS
    tr = _div_tile(tp, ROW_TILE, 8)
    nr = tp // tr

    def body(db_ref, da_ref, xc_ref, r_ref, i_ref, a_ref, lam_ref, wa_ref, wx_ref,
             dxc_ref, dwa_ref, dba_ref, dwx_ref, dbx_ref, dlam_ref):
        i = pl.program_id(1)
        xc = xc_ref[...]
        r = r_ref[...]
        ig = i_ref[...]
        a = a_ref[...]
        dbv = db_ref[...]
        sp = _softplus(-lam_ref[...])
        log_a = (-LRU_C * r) * sp
        s = jnp.sqrt(_one_minus_sq(log_a, a))
        d_ix = dbv * s
        d_s = dbv * (ig * xc)
        d_log_a = da_ref[...] * a - d_s * (a * a) / s
        d_r = d_log_a * (-LRU_C * sp)
        d_sp = jnp.sum(d_log_a * (-LRU_C * r), axis=0, keepdims=True)
        dzr = d_r * r * (1.0 - r)
        dzi = (d_ix * xc) * ig * (1.0 - ig)
        dzr_b = dzr.astype(CDT)
        dzi_b = dzi.astype(CDT)
        xcb = xc.astype(CDT)
        dxc_ref[...] = d_ix * ig + _dot_nt(dzr_b, wa_ref[...]) + _dot_nt(dzi_b, wx_ref[...])
        tn_dims = (((0,), (0,)), ((), ()))
        dwa = lax.dot_general(xcb, dzr_b, tn_dims, preferred_element_type=F32)
        dwx = lax.dot_general(xcb, dzi_b, tn_dims, preferred_element_type=F32)
        dba = jnp.sum(dzr, axis=0, keepdims=True)
        dbx = jnp.sum(dzi, axis=0, keepdims=True)

        @pl.when(i == 0)
        def _():
            dwa_ref[...] = dwa
            dwx_ref[...] = dwx
            dba_ref[...] = dba
            dbx_ref[...] = dbx
            dlam_ref[...] = d_sp

        @pl.when(i > 0)
        def _():
            dwa_ref[...] += dwa
            dwx_ref[...] += dwx
            dba_ref[...] += dba
            dbx_ref[...] += dbx
            dlam_ref[...] += d_sp

        @pl.when(i == nr - 1)
        def _():
            dlam_ref[...] = dlam_ref[...] * (-_sigmoid(-lam_ref[...]))

    blk = pl.BlockSpec((tr, bw), lambda n, i: (i, n))
    vec = pl.BlockSpec((1, bw), lambda n, i: (0, n))
    mat = pl.BlockSpec((None, bw, bw), lambda n, i: (n, 0, 0))
    bias = pl.BlockSpec((None, 1, bw), lambda n, i: (n, 0, 0))
    return pl.pallas_call(
        body,
        grid=(RNN_BLOCKS, nr),
        in_specs=[blk] * 6 + [vec, mat, mat],
        out_specs=[blk, mat, bias, mat, bias, vec],
        out_shape=[jax.ShapeDtypeStruct((tp, dr), F32),
                   jax.ShapeDtypeStruct((RNN_BLOCKS, bw, bw), F32), jax.ShapeDtypeStruct((RNN_BLOCKS, 1, bw), F32),
                   jax.ShapeDtypeStruct((RNN_BLOCKS, bw, bw), F32), jax.ShapeDtypeStruct((RNN_BLOCKS, 1, bw), F32),
                   jax.ShapeDtypeStruct((1, dr), F32)],
        compiler_params=_params("parallel", "arbitrary"),
        name="lru_gates_bwd",
    )(db, da, xc, r, ig, a, lam.reshape(1, dr), w_ga.astype(CDT), w_gx.astype(CDT))


def _lru_conv_bwd(dxc, xy, conv_w):
    tp, dr = dxc.shape
    bw = dr // RNN_BLOCKS
    tr = _div_tile(tp, ROW_TILE, 8)
    nr = tp // tr
    per = tr // HALO

    def body(d_ref, dnext_ref, x_ref, xprev_ref, cw_ref, dxb_ref, dcw_ref, dcb_ref, ds, xs):
        i = pl.program_id(1)
        d = d_ref[...]
        ds[0:tr, :] = d
        ds[tr:, :] = jnp.where(i == nr - 1, 0.0, dnext_ref[...])
        xs[0:HALO, :] = jnp.where(i == 0, 0.0, xprev_ref[...])
        xs[HALO:, :] = x_ref[...]
        dxb = cw_ref[0:1, :] * ds[pl.ds(CONV_W - 1, tr), :]
        for j in range(1, CONV_W):
            dxb = dxb + cw_ref[j:j + 1, :] * ds[pl.ds(CONV_W - 1 - j, tr), :]
        dxb_ref[...] = dxb.astype(CDT)
        dcb = jnp.sum(d, axis=0, keepdims=True)
        dcw = [jnp.sum(d * xs[pl.ds(HALO - CONV_W + 1 + j, tr), :], axis=0, keepdims=True) for j in range(CONV_W)]

        @pl.when(i == 0)
        def _():
            dcb_ref[...] = dcb
            for j in range(CONV_W):
                dcw_ref[j] = dcw[j]

        @pl.when(i > 0)
        def _():
            dcb_ref[...] += dcb
            for j in range(CONV_W):
                dcw_ref[j] += dcw[j]

    blk = pl.BlockSpec((tr, bw), lambda n, i: (i, n))
    return pl.pallas_call(
        body,
        grid=(RNN_BLOCKS, nr),
        in_specs=[blk, pl.BlockSpec((HALO, bw), lambda n, i: (jnp.minimum((i + 1) * per, tp // HALO - 1), n)),
                  blk, pl.BlockSpec((HALO, bw), lambda n, i: (jnp.maximum(i * per - 1, 0), n)),
                  pl.BlockSpec((CONV_W, bw), lambda n, i: (0, n))],
        out_specs=[blk, pl.BlockSpec((CONV_W, 1, bw), lambda n, i: (0, 0, n)), pl.BlockSpec((1, bw), lambda n, i: (0, n))],
        out_shape=[jax.ShapeDtypeStruct((tp, dr), CDT), jax.ShapeDtypeStruct((CONV_W, 1, dr), F32),
                   jax.ShapeDtypeStruct((1, dr), F32)],
        scratch_shapes=[pltpu.VMEM((tr + HALO, bw), F32), pltpu.VMEM((tr + HALO, bw), F32)],
        compiler_params=_params("parallel", "arbitrary"),
        name="lru_conv_bwd",
    )(dxc, dxc, xy, xy, conv_w)


def _me():
    return lax.axis_index("x"), lax.axis_index("y"), lax.axis_index("c")


def _peer(rel):
    x, y, c = _me()
    return (1 - x if rel & 4 else x, 1 - y if rel & 2 else y, 1 - c if rel & 1 else c)


def _chip_of(dev):
    return 2 * dev[0] + dev[1]


def _linear(dev):
    return 4 * dev[0] + 2 * dev[1] + dev[2]


CHIP_RELS = (4, 2, 6)
ALL_RELS = (1, 2, 3, 4, 5, 6, 7)
PAIR_RELS = (1,)


def _scatter_send(pieces, rels, piece_of, name):
    n = len(rels)

    def body(src_ref, recv_ref, send_sems, recv_sems):
        copies = []
        for k, rel in enumerate(rels):
            peer = _peer(rel)
            cp = pltpu.make_async_remote_copy(
                src_ref=src_ref.at[piece_of(peer)], dst_ref=recv_ref.at[k],
                send_sem=send_sems.at[k], recv_sem=recv_sems.at[k], device_id=peer, device_id_type=MESH)
            cp.start()
            copies.append(cp)
        for cp in copies:
            cp.wait()

    return pl.pallas_call(
        body,
        in_specs=[pl.BlockSpec(memory_space=pl.ANY)],
        out_specs=pl.BlockSpec(memory_space=pl.ANY),
        out_shape=jax.ShapeDtypeStruct((n,) + pieces.shape[1:], pieces.dtype),
        scratch_shapes=[pltpu.SemaphoreType.DMA((n,)), pltpu.SemaphoreType.DMA((n,))],
        name=name,
    )(pieces)


def _gather_send(piece, rels, n_slots, slot_of, name):
    n = len(rels)

    def body(src_ref, out_ref, send_sems, recv_sems, local_sem):
        me = _me()
        mine = pltpu.make_async_copy(src_ref, out_ref.at[slot_of(me)], local_sem)
        mine.start()
        sends = []
        for k, rel in enumerate(rels):
            cp = pltpu.make_async_remote_copy(
                src_ref=src_ref, dst_ref=out_ref.at[slot_of(me)],
                send_sem=send_sems.at[k], recv_sem=recv_sems.at[k], device_id=_peer(rel), device_id_type=MESH)
            cp.start()
            sends.append(cp)
        for k, rel in enumerate(rels):
            peer = _peer(rel)
            pltpu.make_async_remote_copy(
                src_ref=src_ref, dst_ref=out_ref.at[slot_of(peer)],
                send_sem=send_sems.at[k], recv_sem=recv_sems.at[k], device_id=peer, device_id_type=MESH).wait_recv()
        for cp in sends:
            cp.wait_send()
        mine.wait()

    return pl.pallas_call(
        body,
        in_specs=[pl.BlockSpec(memory_space=pl.ANY)],
        out_specs=pl.BlockSpec(memory_space=pl.ANY),
        out_shape=jax.ShapeDtypeStruct((n_slots,) + piece.shape, piece.dtype),
        scratch_shapes=[pltpu.SemaphoreType.DMA((n,)), pltpu.SemaphoreType.DMA((n,)), pltpu.SemaphoreType.DMA],
        name=name,
    )(piece)


def _gather_chips(shard, name):
    return _gather_send(shard, CHIP_RELS, N_CHIPS, _chip_of, name)


def _sum_pieces(own, recv, name):
    rr, cc = own.shape
    n = recv.shape[0]
    tr = _div_tile(rr, max(8, (1 << 17) // cc // 8 * 8), 8)

    def body(own_ref, recv_ref, o_ref):
        acc = own_ref[...]
        for k in range(n):
            acc = acc + recv_ref[k].astype(F32)
        o_ref[...] = acc

    return pl.pallas_call(
        body,
        grid=(rr // tr,),
        in_specs=[pl.BlockSpec((tr, cc), lambda i: (i, 0)), pl.BlockSpec((n, tr, cc), lambda i: (0, i, 0))],
        out_specs=pl.BlockSpec((tr, cc), lambda i: (i, 0)),
        out_shape=jax.ShapeDtypeStruct((rr, cc), F32),
        compiler_params=_params("parallel"),
        name=name,
    )(own, recv)


def _reduce_to_owner(g8, payload_dtype, name):
    x, y, c = _me()
    own = lax.dynamic_index_in_dim(g8, _linear((x, y, c)), 0, keepdims=False)
    recv = _scatter_send(g8.astype(payload_dtype), ALL_RELS, _linear, name + "_scatter")
    return _sum_pieces(own, recv, name + "_sum")


def _reduce_sharded(g, name):
    n_chips, nl, rr, cc = g.shape
    half = (nl // 2) * rr
    red = _reduce_to_owner(g.reshape(N_DEV, half, cc), CDT, name)
    both = _gather_send(red, PAIR_RELS, 2, lambda dev: dev[2], name + "_pair")
    return both.reshape(nl, rr, cc)


def _adamw(w, g, m, v, name):
    rr, cc = w.shape
    tr = _div_tile(rr, max(8, (1 << 17) // cc // 8 * 8), 8)
    c1 = 1.0 - ADAM_B1 ** ADAM_STEP
    c2 = 1.0 - ADAM_B2 ** ADAM_STEP

    def body(w_ref, g_ref, m_ref, v_ref, d_ref, mo_ref, vo_ref):
        g_ = g_ref[...]
        m_ = ADAM_B1 * m_ref[...] + (1.0 - ADAM_B1) * g_
        v_ = ADAM_B2 * v_ref[...] + (1.0 - ADAM_B2) * (g_ * g_)
        d_ref[...] = -ADAM_LR * ((m_ / c1) / (jnp.sqrt(v_ / c2) + ADAM_EPS) + ADAM_WD * w_ref[...])
        mo_ref[...] = m_
        vo_ref[...] = v_

    blk = pl.BlockSpec((tr, cc), lambda i: (i, 0))
    out = jax.ShapeDtypeStruct((rr, cc), F32)
    return pl.pallas_call(
        body,
        grid=(rr // tr,),
        in_specs=[blk] * 4,
        out_specs=[blk] * 3,
        out_shape=[out] * 3,
        compiler_params=_params("parallel"),
        name=name,
    )(w, g, m, v)


def _pack(arrays, cols, row_mult):
    flat = jnp.concatenate([a.reshape(-1) for a in arrays])
    rows = -(-flat.shape[0] // cols)
    rows = -(-rows // row_mult) * row_mult
    return jnp.pad(flat, (0, rows * cols - flat.shape[0])).reshape(rows, cols)


def _unpack(buf, shapes):
    flat = buf.reshape(-1)
    out, off = [], 0
    for s in shapes:
        n = math.prod(s)
        out.append(flat[off:off + n].reshape(s))
        off += n
    return out


def kernel(x, meta_tokens, norm_mix, norm_ffn, norm_final, mla_w_in, mla_q_norm, mla_kv_norm, mla_w_uq, mla_w_ukv, mla_w_o, lru_w_in, lru_conv_w, lru_conv_b, lru_w_gate_a, lru_b_gate_a, lru_w_gate_x, lru_b_gate_x, lru_lambda, lru_w_o, ffn_w_gu, ffn_w_down, loss_target, m_meta_tokens, m_norm_mix, m_norm_ffn, m_norm_final, m_mla_w_in, m_mla_q_norm, m_mla_kv_norm, m_mla_w_uq, m_mla_w_ukv, m_mla_w_o, m_lru_w_in, m_lru_conv_w, m_lru_conv_b, m_lru_w_gate_a, m_lru_b_gate_a, m_lru_w_gate_x, m_lru_b_gate_x, m_lru_lambda, m_lru_w_o, m_ffn_w_gu, m_ffn_w_down, v_meta_tokens, v_norm_mix, v_norm_ffn, v_norm_final, v_mla_w_in, v_mla_q_norm, v_mla_kv_norm, v_mla_w_uq, v_mla_w_ukv, v_mla_w_o, v_lru_w_in, v_lru_conv_w, v_lru_conv_b, v_lru_w_gate_a, v_lru_b_gate_a, v_lru_w_gate_x, v_lru_b_gate_x, v_lru_lambda, v_lru_w_o, v_ffn_w_gu, v_ffn_w_down):
    d = D_MODEL
    t_real = N_META + SEQ
    tp = _t_pad()
    n_mla = mla_w_in.shape[0]
    n_lru = lru_w_in.shape[0]
    h_dim = MLA_HEADS * V_HEAD
    w_in_cols = Q_LORA + KV_LORA + QK_ROPE
    w_in_pad = Q_LORA + KV_LORA + LANES
    q_cols = MLA_HEADS * (QK_NOPE + QK_ROPE)
    tmm = _div_tile(tp, MM_ROW_TILE, 16)
    tkt = _div_tile(tp, 1408, 16)

    def tile(n, pref):
        return _div_tile(n, pref, LANES)

    w_in4 = _gather_chips(jnp.pad(mla_w_in, ((0, 0), (0, 0), (0, w_in_pad - w_in_cols))).astype(CDT), "gather_mla_w_in")
    w_uq4 = _gather_chips(mla_w_uq.astype(CDT), "gather_mla_w_uq")
    w_ukv4 = _gather_chips(mla_w_ukv.astype(CDT), "gather_mla_w_ukv")
    w_o4 = _gather_chips(mla_w_o.astype(CDT), "gather_mla_w_o")
    lw_in4 = _gather_chips(lru_w_in.astype(CDT), "gather_lru_w_in")
    lw_o4 = _gather_chips(lru_w_o.astype(CDT), "gather_lru_w_o")
    w_gu4 = _gather_chips(ffn_w_gu.astype(CDT), "gather_ffn_w_gu")
    w_down4 = _gather_chips(ffn_w_down.astype(CDT), "gather_ffn_w_down")
    small_shapes = [meta_tokens.shape, lru_conv_w.shape, lru_conv_b.shape, lru_lambda.shape]
    csh = meta_tokens.shape[1]
    small4 = _gather_chips(_pack([meta_tokens, lru_conv_w, lru_conv_b, lru_lambda], csh, 16), "gather_small")
    small_full = [jnp.concatenate(parts, axis=-1) for parts in zip(*[_unpack(small4[k], small_shapes) for k in range(N_CHIPS)])]
    meta_full, conv_w_full, conv_b_full, lam_full = small_full

    w_uq_full = jnp.moveaxis(w_uq4, 0, 2).reshape(n_mla, Q_LORA, MLA_HEADS, QK_NOPE + QK_ROPE)
    w_uq_perm = jnp.concatenate([w_uq_full[..., :QK_NOPE].reshape(n_mla, Q_LORA, -1),
                                 w_uq_full[..., QK_NOPE:].reshape(n_mla, Q_LORA, -1)], axis=-1)
    cos2, sin2 = _rope_tables(tp)

    h = jnp.concatenate([meta_full, x[0], jnp.zeros((tp - t_real, d), F32)], axis=0)
    saved = []
    for layer in range(DEPTH):
        j = layer // 2
        s = {"h_in": h}
        hn = _rms_fwd(h, norm_mix[layer], width=d, col_block=0, name="norm_mix_fwd")
        s["hn"] = hn
        if layer % 2 == 0:
            proj = _mm("nn", hn, w_in4, kind="row", layer=j, tm=tmm, tn=tile(w_in_pad, 1152), tk=tile(d // N_CHIPS, 512), name="mla_in")
            c_q = _rms_fwd(proj, mla_q_norm[j], width=Q_LORA, col_block=0, name="q_norm_fwd")
            c_kv = _rms_fwd(proj, mla_kv_norm[j], width=KV_LORA, col_block=Q_LORA // KV_LORA, name="kv_norm_fwd")
            q = _mm("nn", c_q, w_uq_perm[j], tm=tmm, tn=tile(q_cols, 1024), tk=Q_LORA, name="mla_uq")
            kv = _mm("nn", c_kv, w_ukv4, kind="col", layer=j, out_dtype=CDT, tm=tmm, tn=tile(w_ukv4.shape[3], 1024), tk=KV_LORA, name="mla_ukv")
            qr, kr2 = _mla_prep(q, proj, cos2, sin2)
            att, lse = _attn_fwd(q, qr, kv, kr2)
            h = _mm("nn", att, w_o4, kind="row", layer=j, resid=h, tm=tmm, tn=tile(d, 1024), tk=tile(h_dim // N_CHIPS, 512), name="mla_out")
            s.update(proj=proj, c_q=c_q, c_kv=c_kv, q=q, kv=kv, qr=qr, kr2=kr2, att=att, lse=lse)
        else:
            xy = _mm("nn", hn, lw_in4, kind="col", layer=j, tm=tmm, tn=tile(lw_in4.shape[3], 1024), tk=d, name="lru_in")
            xc, r, ig, a, b = _lru_gates_fwd(xy, conv_w_full[j], conv_b_full[j], lru_w_gate_a[j], lru_b_gate_a[j],
                                             lru_w_gate_x[j], lru_b_gate_x[j], lam_full[j])
            hs, mixed = _lru_scan_fwd(a, b, xy)
            h = _mm("nn", mixed, lw_o4, kind="row", layer=j, resid=h, tm=tmm, tn=tile(d, 1024), tk=tile(d // N_CHIPS, 512), name="lru_out")
            s.update(xy=xy, xc=xc, r=r, ig=ig, a=a, hs=hs, mixed=mixed)
        s["h_mid"] = h
        hn2 = _rms_fwd(h, norm_ffn[layer], width=d, col_block=0, name="norm_ffn_fwd")
        gu = _mm("nn", hn2, w_gu4, kind="col", layer=layer, tm=tmm, tn=tile(w_gu4.shape[3], 1408), tk=d, name="ffn_gu")
        act = _swiglu_fwd(gu)
        h = _mm("nn", act, w_down4, kind="row", layer=layer, resid=h, tm=tmm, tn=tile(d, 1024), tk=tile(D_FF // N_CHIPS, 1408), name="ffn_down")
        s.update(hn2=hn2, gu=gu, act=act)
        saved.append(s)

    target = jnp.concatenate([jnp.zeros((N_META, d), F32), loss_target[0], jnp.zeros((tp - t_real, d), F32)], axis=0)
    dh, dhb, g_norm_final, loss_part = _final_loss(h, norm_final, target)
    loss = lax.psum(loss_part[0, 0], ("x", "y", "c"))

    g_norm_mix, g_norm_ffn = [None] * DEPTH, [None] * DEPTH
    g_q_norm, g_kv_norm = [None] * n_mla, [None] * n_mla
    g_w_uq = [None] * n_mla
    g_gate = {k: [None] * n_lru for k in ("wa", "ba", "wx", "bx", "lam", "cw", "cb")}
    G = {k: None for k in ("w_in", "w_ukv", "w_o", "lw_in", "lw_o", "w_gu", "w_down")}

    def grad_w(key, a_op, b_op, kind, lyr, n_layers, tm, tn):
        G[key] = _mm_tn(a_op, b_op, into=G[key], kind=kind, layer=lyr, n_layers=n_layers, tm=tm, tn=tn, tk=tkt, name="grad_" + key)

    for layer in reversed(range(DEPTH)):
        j = layer // 2
        s = saved[layer]
        grad_w("w_down", s["act"], dhb, "row", layer, DEPTH, tile(D_FF // N_CHIPS, 1408), tile(d, 1024))
        d_act = _mm("nt", dhb, w_down4, kind="row", layer=layer, tm=tmm, tn=tile(D_FF // N_CHIPS, 1408), tk=d, name="ffn_down_bwd")
        dgu = _swiglu_bwd(s["gu"], d_act)
        grad_w("w_gu", s["hn2"], dgu, "col", layer, DEPTH, tile(d, 1024), tile(w_gu4.shape[3], 1408))
        dhn2 = _mm("nt", dgu, w_gu4, kind="col", layer=layer, tm=tmm, tn=tile(d, 1024), tk=tile(w_gu4.shape[3], 1408), name="ffn_gu_bwd")
        dh, dhb, g_norm_ffn[layer] = _rms_bwd(s["h_mid"], norm_ffn[layer], dhn2, dh, width=d, col_block=0, name="norm_ffn_bwd")
        if layer % 2 == 0:
            grad_w("w_o", s["att"], dhb, "row", j, n_mla, tile(h_dim // N_CHIPS, 512), tile(d, 1024))
            d_att = _mm("nt", dhb, w_o4, kind="row", layer=j, tm=tmm, tn=tile(h_dim // N_CHIPS, 512), tk=d, name="mla_out_bwd")
            delta = _attn_delta(d_att, s["att"])
            lse = s["lse"]
            dkv, dkr_h = _attn_bwd_kv(s["q"], s["qr"], s["kv"], s["kr2"], d_att, lse.reshape(MLA_HEADS, 1, tp), delta.reshape(MLA_HEADS, 1, tp))
            dqn, dqr_h = _attn_bwd_q(s["q"], s["qr"], s["kv"], s["kr2"], d_att, lse, delta)
            dqr, dkr = _mla_unprep(dqr_h, dkr_h, cos2, sin2)
            dq = jnp.concatenate([dqn, dqr], axis=-1)
            g_w_uq[j] = _mm_tn(s["c_q"], dq, tm=Q_LORA, tn=tile(q_cols, 1024), tk=tkt, name="grad_w_uq")
            dc_q = _mm("nt", dq, w_uq_perm[j], tm=tmm, tn=Q_LORA, tk=tile(q_cols, 1024), name="mla_uq_bwd")
            grad_w("w_ukv", s["c_kv"], dkv, "col", j, n_mla, KV_LORA, tile(w_ukv4.shape[3], 1024))
            dc_kv = _mm("nt", dkv, w_ukv4, kind="col", layer=j, tm=tmm, tn=KV_LORA, tk=tile(w_ukv4.shape[3], 1024), name="mla_ukv_bwd")
            dpq, _, g_q_norm[j] = _rms_bwd(s["proj"], mla_q_norm[j], dc_q, None, width=Q_LORA, col_block=0, name="q_norm_bwd")
            dpkv, _, g_kv_norm[j] = _rms_bwd(s["proj"], mla_kv_norm[j], dc_kv, None, width=KV_LORA, col_block=Q_LORA // KV_LORA, name="kv_norm_bwd")
            dproj = jnp.concatenate([dpq, dpkv, dkr], axis=-1).astype(CDT)
            grad_w("w_in", s["hn"], dproj, "row", j, n_mla, tile(d // N_CHIPS, 512), tile(w_in_pad, 1152))
            dhn = _mm("nt", dproj, w_in4, kind="row", layer=j, tm=tmm, tn=tile(d // N_CHIPS, 512), tk=tile(w_in_pad, 1152), name="mla_in_bwd")
        else:
            grad_w("lw_o", s["mixed"], dhb, "row", j, n_lru, tile(d // N_CHIPS, 512), tile(d, 1024))
            dm = _mm("nt", dhb, lw_o4, kind="row", layer=j, tm=tmm, tn=tile(d // N_CHIPS, 512), tk=d, name="lru_out_bwd")
            db, da, dy = _lru_scan_bwd(s["a"], s["hs"], dm, s["xy"])
            dxc, g_gate["wa"][j], g_gate["ba"][j], g_gate["wx"][j], g_gate["bx"][j], g_gate["lam"][j] = _lru_gates_bwd(
                db, da, s["xc"], s["r"], s["ig"], s["a"], lam_full[j], lru_w_gate_a[j], lru_w_gate_x[j])
            dxb, g_gate["cw"][j], g_gate["cb"][j] = _lru_conv_bwd(dxc, s["xy"], conv_w_full[j])
            dxy = jnp.concatenate([dxb, dy], axis=-1)
            grad_w("lw_in", s["hn"], dxy, "col", j, n_lru, tile(d, 1024), tile(lw_in4.shape[3], 1024))
            dhn = _mm("nt", dxy, lw_in4, kind="col", layer=j, tm=tmm, tn=tile(d, 1024), tk=tile(lw_in4.shape[3], 1024), name="lru_in_bwd")
        dh, dhb, g_norm_mix[layer] = _rms_bwd(s["h_in"], norm_mix[layer], dhn, dh, width=d, col_block=0, name="norm_mix_bwd")

    grad_x = dh[N_META:t_real][None]
    g_meta_full = dh[:N_META]

    nope_w = MLA_HEADS * QK_NOPE
    g_uq = jnp.stack(g_w_uq)
    g_uq = jnp.concatenate([g_uq[..., :nope_w].reshape(n_mla, Q_LORA, MLA_HEADS, QK_NOPE),
                            g_uq[..., nope_w:].reshape(n_mla, Q_LORA, MLA_HEADS, QK_ROPE)], axis=-1)
    G["w_uq"] = jnp.moveaxis(g_uq.reshape(n_mla, Q_LORA, N_CHIPS, q_cols // N_CHIPS), 2, 0)

    big = [("w_in", mla_w_in, m_mla_w_in, v_mla_w_in), ("w_uq", mla_w_uq, m_mla_w_uq, v_mla_w_uq),
           ("w_ukv", mla_w_ukv, m_mla_w_ukv, v_mla_w_ukv), ("w_o", mla_w_o, m_mla_w_o, v_mla_w_o),
           ("lw_in", lru_w_in, m_lru_w_in, v_lru_w_in), ("lw_o", lru_w_o, m_lru_w_o, v_lru_w_o),
           ("w_gu", ffn_w_gu, m_ffn_w_gu, v_ffn_w_gu), ("w_down", ffn_w_down, m_ffn_w_down, v_ffn_w_down)]
    res = {}
    for key, w, m, v in big:
        g = _reduce_sharded(G[key], "reduce_" + key)
        if key == "w_in":
            g = g[:, :, :w_in_cols]
        shp = w.shape
        two_d = (shp[0] * shp[1], shp[2])
        dlt, nm, nv = _adamw(w.reshape(two_d), g.reshape(two_d), m.reshape(two_d), v.reshape(two_d), "adamw_" + key)
        res[key] = (g, dlt.reshape(shp), nm.reshape(shp), nv.reshape(shp))

    g_small_full = [g_meta_full, jnp.stack(g_gate["cw"]).reshape(n_lru, CONV_W, d), jnp.stack(g_gate["cb"]).reshape(n_lru, d),
                    jnp.stack(g_gate["lam"]).reshape(n_lru, d)]
    g_small4 = jnp.stack([_pack([a[..., k * csh:(k + 1) * csh] for a in g_small_full], csh, 16) for k in range(N_CHIPS)])
    rows_s = g_small4.shape[1]
    red = _reduce_to_owner(g_small4.reshape(N_DEV, rows_s // 2, csh), F32, "reduce_small")
    g_small = _gather_send(red, PAIR_RELS, 2, lambda dev: dev[2], "reduce_small_pair").reshape(rows_s, csh)
    small_w = [meta_tokens, lru_conv_w, lru_conv_b, lru_lambda]
    small_m = [m_meta_tokens, m_lru_conv_w, m_lru_conv_b, m_lru_lambda]
    small_v = [v_meta_tokens, v_lru_conv_w, v_lru_conv_b, v_lru_lambda]
    sd, sm, sv = _adamw(_pack(small_w, csh, 16), g_small, _pack(small_m, csh, 16), _pack(small_v, csh, 16), "adamw_small")
    small_out = [_unpack(buf, small_shapes) for buf in (g_small, sd, sm, sv)]

    rep_w = [norm_mix, norm_ffn, norm_final, mla_q_norm, mla_kv_norm, lru_w_gate_a, lru_b_gate_a, lru_w_gate_x, lru_b_gate_x]
    rep_m = [m_norm_mix, m_norm_ffn, m_norm_final, m_mla_q_norm, m_mla_kv_norm, m_lru_w_gate_a, m_lru_b_gate_a, m_lru_w_gate_x, m_lru_b_gate_x]
    rep_v = [v_norm_mix, v_norm_ffn, v_norm_final, v_mla_q_norm, v_mla_kv_norm, v_lru_w_gate_a, v_lru_b_gate_a, v_lru_w_gate_x, v_lru_b_gate_x]
    rep_g = [jnp.stack(g_norm_mix), jnp.stack(g_norm_ffn), g_norm_final, jnp.stack(g_q_norm), jnp.stack(g_kv_norm),
             jnp.stack(g_gate["wa"]), jnp.stack(g_gate["ba"]), jnp.stack(g_gate["wx"]), jnp.stack(g_gate["bx"])]
    rep_shapes = [w.shape for w in rep_w]
    g_rep = _pack(rep_g, LANES, 8 * N_DEV)
    rows_r = g_rep.shape[0]
    red = _reduce_to_owner(g_rep.reshape(N_DEV, rows_r // N_DEV, LANES), F32, "reduce_rep")
    g_rep = _gather_send(red, ALL_RELS, N_DEV, _linear, "reduce_rep_all").reshape(rows_r, LANES)
    rd, rm, rv = _adamw(_pack(rep_w, LANES, 8 * N_DEV), g_rep, _pack(rep_m, LANES, 8 * N_DEV), _pack(rep_v, LANES, 8 * N_DEV), "adamw_rep")
    rep_out = [_unpack(buf, rep_shapes) for buf in (g_rep, rd, rm, rv)]

    def leaf(kind):
        s_, r_ = small_out[kind], rep_out[kind]
        return [s_[0], r_[0], r_[1], r_[2], res["w_in"][kind], r_[3], r_[4], res["w_uq"][kind], res["w_ukv"][kind],
                res["w_o"][kind], res["lw_in"][kind], s_[1], s_[2], r_[5], r_[6], r_[7], r_[8], s_[3],
                res["lw_o"][kind], res["w_gu"][kind], res["w_down"][kind]]

    return (loss, grad_x, *leaf(0), *leaf(1), *leaf(2), *leaf(3))
```

```python
import math

import jax
import jax.numpy as jnp
from jax import lax
from jax.experimental import pallas as pl
from jax.experimental.pallas import tpu as pltpu

F32 = jnp.float32
CDT = jnp.bfloat16
MESH = pl.DeviceIdType.MESH

D_MODEL = 2048
SEQ = 4096
DEPTH = 4
CHUNK = 64
N_META = 16
MLA_HEADS = 16
Q_LORA = 512
KV_LORA = 512
QK_NOPE = 128
QK_ROPE = 64
V_HEAD = 128
ROPE_THETA = 10000.0
RNN_BLOCKS = 16
CONV_W = 4
LRU_C = 8.0
D_FF = 5632
RMS_EPS = 1e-6
NEG_BIG = -1e30
ADAM_LR = 0.001
ADAM_B1 = 0.9
ADAM_B2 = 0.999
ADAM_EPS = 1e-08
ADAM_WD = 0.01
ADAM_STEP = 10

N_CHIPS = 4
N_DEV = 8
LANES = 128
VMEM_LIMIT = 52 * 1024 * 1024
ROW_TILE = 384
MM_ROW_TILE = 704
ATT_TILE = 384
SCAN_COLS = 128
PAIR_CHUNKS = 8


def _div_tile(n, pref, mult):
    if n <= pref:
        return n
    d = (pref // mult) * mult
    while d >= mult:
        if n % d == 0:
            return d
        d -= mult
    raise ValueError(f"no tile for {n} <= {pref} (multiple of {mult})")


def _t_pad():
    t = N_META + SEQ
    step = math.lcm(_row_tile_unit(), 8)
    return -(-t // step) * step


def _row_tile_unit():
    return math.lcm(math.lcm(ROW_TILE, MM_ROW_TILE), ATT_TILE)


def _params(*sem):
    return pltpu.CompilerParams(dimension_semantics=sem, vmem_limit_bytes=VMEM_LIMIT)


def _b_spec(form, b, kind, layer, t_out, t_con):
    if kind == "plain":
        if form == "nn":
            return pl.BlockSpec((t_con, t_out), lambda i, j, k: (k, j))
        return pl.BlockSpec((t_out, t_con), lambda i, j, k: (j, k))
    rows, cols = b.shape[2], b.shape[3]
    if form == "nn":
        blk = (None, None, t_con, t_out)
        if kind == "row":
            per = rows // t_con
            return pl.BlockSpec(blk, lambda i, j, k: (k // per, layer, k % per, j))
        per = cols // t_out
        return pl.BlockSpec(blk, lambda i, j, k: (j // per, layer, k, j % per))
    blk = (None, None, t_out, t_con)
    if kind == "row":
        per = rows // t_out
        return pl.BlockSpec(blk, lambda i, j, k: (j // per, layer, j % per, k))
    per = cols // t_con
    return pl.BlockSpec(blk, lambda i, j, k: (k // per, layer, j, k % per))


def _mm_body(nk, dims, has_resid):
    def body(*refs):
        if has_resid:
            a_ref, b_ref, r_ref, o_ref = refs[:4]
        else:
            a_ref, b_ref, o_ref = refs[:3]
        prod = lax.dot_general(a_ref[...].astype(CDT), b_ref[...].astype(CDT), (dims, ((), ())),
                               preferred_element_type=F32)

        def finish(acc):
            if has_resid:
                acc = acc + r_ref[...]
            o_ref[...] = acc.astype(o_ref.dtype)

        if nk == 1:
            finish(prod)
            return
        acc_ref = refs[-1]
        k = pl.program_id(2)

        @pl.when(k == 0)
        def _():
            acc_ref[...] = prod

        @pl.when(k > 0)
        def _():
            acc_ref[...] += prod

        @pl.when(k == nk - 1)
        def _():
            finish(acc_ref[...])

    return body


def _mm(form, a, b, *, kind="plain", layer=0, out_dtype=F32, resid=None, tm, tn, tk, name):
    m, con = a.shape
    if kind == "plain":
        w_rows, w_cols = b.shape
    elif kind == "row":
        w_rows, w_cols = b.shape[0] * b.shape[2], b.shape[3]
    else:
        w_rows, w_cols = b.shape[2], b.shape[0] * b.shape[3]
    n_out = w_cols if form == "nn" else w_rows
    assert con == (w_rows if form == "nn" else w_cols), (name, a.shape, b.shape)
    nk = con // tk
    assert m % tm == 0 and n_out % tn == 0 and con % tk == 0, (name, m, n_out, con, tm, tn, tk)
    dims = ((1,), (0,)) if form == "nn" else ((1,), (1,))
    in_specs = [pl.BlockSpec((tm, tk), lambda i, j, k: (i, k)), _b_spec(form, b, kind, layer, tn, tk)]
    args = [a, b]
    if resid is not None:
        in_specs.append(pl.BlockSpec((tm, tn), lambda i, j, k: (i, j)))
        args.append(resid)
    return pl.pallas_call(
        _mm_body(nk, dims, resid is not None),
        grid=(m // tm, n_out // tn, nk),
        in_specs=in_specs,
        out_specs=pl.BlockSpec((tm, tn), lambda i, j, k: (i, j)),
        out_shape=jax.ShapeDtypeStruct((m, n_out), out_dtype),
        scratch_shapes=[pltpu.VMEM((tm, tn), F32)] if nk > 1 else [],
        compiler_params=_params("parallel", "parallel", "arbitrary"),
        name=name,
    )(*args)


def _mm_tn(a, b, *, into=None, kind="plain", layer=0, n_layers=1, tm, tn, tk, name):
    t, m = a.shape
    n = b.shape[1]
    nk = t // tk
    assert t % tk == 0 and m % tm == 0 and n % tn == 0, (name, t, m, n, tm, tn, tk)
    in_specs = [pl.BlockSpec((tk, tm), lambda i, j, k: (k, i)), pl.BlockSpec((tk, tn), lambda i, j, k: (k, j))]
    args = [a, b]
    aliases = {}
    if kind == "plain":
        out_shape = jax.ShapeDtypeStruct((m, n), F32)
        out_spec = pl.BlockSpec((tm, tn), lambda i, j, k: (i, j))
    else:
        blk = (None, None, tm, tn)
        if kind == "row":
            per = (m // N_CHIPS) // tm
            out_shape = jax.ShapeDtypeStruct((N_CHIPS, n_layers, m // N_CHIPS, n), F32)
            out_spec = pl.BlockSpec(blk, lambda i, j, k: (i // per, layer, i % per, j))
        else:
            per = (n // N_CHIPS) // tn
            out_shape = jax.ShapeDtypeStruct((N_CHIPS, n_layers, m, n // N_CHIPS), F32)
            out_spec = pl.BlockSpec(blk, lambda i, j, k: (j // per, layer, i, j % per))
        assert per >= 1, name
        if into is not None:
            in_specs.append(pl.BlockSpec(memory_space=pl.ANY))
            args.append(into)
            aliases = {2: 0}
    body = _mm_body(nk, ((0,), (0,)), False)

    def kernel_fn(*refs):
        if into is not None and kind != "plain":
            refs = refs[:2] + refs[3:]
        body(*refs)

    return pl.pallas_call(
        kernel_fn,
        grid=(m // tm, n // tn, nk),
        in_specs=in_specs,
        out_specs=out_spec,
        out_shape=out_shape,
        scratch_shapes=[pltpu.VMEM((tm, tn), F32)] if nk > 1 else [],
        input_output_aliases=aliases,
        compiler_params=_params("parallel", "parallel", "arbitrary"),
        name=name,
    )(*args)


def _rms_fwd(x, g, *, width, col_block, name):
    tp = x.shape[0]
    tr = _div_tile(tp, ROW_TILE, 8)

    def body(x_ref, g_ref, o_ref):
        xf = x_ref[...]
        r = lax.rsqrt(jnp.mean(xf * xf, axis=-1, keepdims=True) + RMS_EPS)
        o_ref[...] = ((xf * r) * g_ref[...]).astype(o_ref.dtype)

    return pl.pallas_call(
        body,
        grid=(tp // tr,),
        in_specs=[pl.BlockSpec((tr, width), lambda i: (i, col_block)), pl.BlockSpec((1, width), lambda i: (0, 0))],
        out_specs=pl.BlockSpec((tr, width), lambda i: (i, 0)),
        out_shape=jax.ShapeDtypeStruct((tp, width), CDT),
        compiler_params=_params("parallel"),
        name=name,
    )(x, g.reshape(1, width))


def _rms_bwd(x, g, dy, resid, *, width, col_block, name):
    tp = x.shape[0]
    tr = _div_tile(tp, ROW_TILE, 8)
    has_resid = resid is not None

    def body(*refs):
        if has_resid:
            x_ref, g_ref, dy_ref, res_ref, dx_ref, dxb_ref, dg_ref = refs
        else:
            x_ref, g_ref, dy_ref, dx_ref, dxb_ref, dg_ref = refs
        i = pl.program_id(0)
        xf = x_ref[...]
        r = lax.rsqrt(jnp.mean(xf * xf, axis=-1, keepdims=True) + RMS_EPS)
        xh = xf * r
        dy = dy_ref[...].astype(F32)
        dg = jnp.sum(dy * xh, axis=0, keepdims=True)
        dxh = dy * g_ref[...]
        dx = r * (dxh - xh * jnp.mean(dxh * xh, axis=-1, keepdims=True))
        if has_resid:
            dx = dx + res_ref[...]
        dx_ref[...] = dx
        dxb_ref[...] = dx.astype(CDT)

        @pl.when(i == 0)
        def _():
            dg_ref[...] = dg

        @pl.when(i > 0)
        def _():
            dg_ref[...] += dg

    row = pl.BlockSpec((tr, width), lambda i: (i, 0))
    in_specs = [pl.BlockSpec((tr, width), lambda i: (i, col_block)), pl.BlockSpec((1, width), lambda i: (0, 0)), row]
    args = [x, g.reshape(1, width), dy]
    if has_resid:
        in_specs.append(row)
        args.append(resid)
    return pl.pallas_call(
        body,
        grid=(tp // tr,),
        in_specs=in_specs,
        out_specs=[row, row, pl.BlockSpec((1, width), lambda i: (0, 0))],
        out_shape=[jax.ShapeDtypeStruct((tp, width), F32), jax.ShapeDtypeStruct((tp, width), CDT),
                   jax.ShapeDtypeStruct((1, width), F32)],
        compiler_params=_params("arbitrary"),
        name=name,
    )(*args)


def _final_loss(h, g, target):
    tp, d = h.shape
    tr = _div_tile(tp, ROW_TILE, 8)

    def body(h_ref, g_ref, t_ref, dh_ref, dhb_ref, dg_ref, loss_ref):
        i = pl.program_id(0)
        xf = h_ref[...]
        r = lax.rsqrt(jnp.mean(xf * xf, axis=-1, keepdims=True) + RMS_EPS)
        xh = xf * r
        gain = g_ref[...]
        y = xh * gain
        rows = i * tr + lax.broadcasted_iota(jnp.int32, (tr, 1), 0)
        valid = jnp.logical_and(rows >= N_META, rows < N_META + SEQ)
        err = jnp.where(valid, y - t_ref[...], 0.0)
        part = 0.5 * jnp.sum(jnp.mean(err * err, axis=-1, keepdims=True), axis=0, keepdims=True)
        dy = err * (1.0 / d)
        dg = jnp.sum(dy * xh, axis=0, keepdims=True)
        dxh = dy * gain
        dx = r * (dxh - xh * jnp.mean(dxh * xh, axis=-1, keepdims=True))
        dh_ref[...] = dx
        dhb_ref[...] = dx.astype(CDT)

        @pl.when(i == 0)
        def _():
            dg_ref[...] = dg
            loss_ref[...] = part

        @pl.when(i > 0)
        def _():
            dg_ref[...] += dg
            loss_ref[...] += part

    row = pl.BlockSpec((tr, d), lambda i: (i, 0))
    vec = pl.BlockSpec((1, d), lambda i: (0, 0))
    return pl.pallas_call(
        body,
        grid=(tp // tr,),
        in_specs=[row, vec, row],
        out_specs=[row, row, vec, pl.BlockSpec((1, 1), lambda i: (0, 0))],
        out_shape=[jax.ShapeDtypeStruct((tp, d), F32), jax.ShapeDtypeStruct((tp, d), CDT),
                   jax.ShapeDtypeStruct((1, d), F32), jax.ShapeDtypeStruct((1, 1), F32)],
        compiler_params=_params("arbitrary"),
        name="final_loss",
    )(h, g.reshape(1, d), target)


def _sigmoid(x):
    return 1.0 / (1.0 + jnp.exp(-x))


def _swiglu_fwd(gu):
    tp, f2 = gu.shape
    f = f2 // 2
    tr = _div_tile(tp, ROW_TILE, 8)
    tf = _div_tile(f, 1408, LANES)
    nf = f // tf

    def body(g_ref, u_ref, o_ref):
        g = g_ref[...]
        o_ref[...] = ((g * _sigmoid(g)) * u_ref[...]).astype(o_ref.dtype)

    return pl.pallas_call(
        body,
        grid=(tp // tr, nf),
        in_specs=[pl.BlockSpec((tr, tf), lambda i, j: (i, j)), pl.BlockSpec((tr, tf), lambda i, j: (i, j + nf))],
        out_specs=pl.BlockSpec((tr, tf), lambda i, j: (i, j)),
        out_shape=jax.ShapeDtypeStruct((tp, f), CDT),
        compiler_params=_params("parallel", "parallel"),
        name="swiglu_fwd",
    )(gu, gu)


def _swiglu_bwd(gu, da):
    tp, f2 = gu.shape
    f = f2 // 2
    tr = _div_tile(tp, ROW_TILE, 8)
    tf = _div_tile(f, 1408, LANES)
    nf = f // tf

    def body(g_ref, u_ref, da_ref, o_ref):
        j = pl.program_id(1)
        g = g_ref[...]
        da = da_ref[...]
        sg = _sigmoid(g)

        @pl.when(j < nf)
        def _():
            o_ref[...] = (da * u_ref[...] * (sg * (1.0 + g * (1.0 - sg)))).astype(o_ref.dtype)

        @pl.when(j >= nf)
        def _():
            o_ref[...] = (da * (g * sg)).astype(o_ref.dtype)

    return pl.pallas_call(
        body,
        grid=(tp // tr, 2 * nf),
        in_specs=[pl.BlockSpec((tr, tf), lambda i, j: (i, j % nf)),
                  pl.BlockSpec((tr, tf), lambda i, j: (i, nf + j % nf)),
                  pl.BlockSpec((tr, tf), lambda i, j: (i, j % nf))],
        out_specs=pl.BlockSpec((tr, tf), lambda i, j: (i, j)),
        out_shape=jax.ShapeDtypeStruct((tp, f2), CDT),
        compiler_params=_params("parallel", "parallel"),
        name="swiglu_bwd",
    )(gu, gu, da)


def _swap_halves(x):
    lane = lax.broadcasted_iota(jnp.int32, x.shape, x.ndim - 1)
    first = (lane % QK_ROPE) < (QK_ROPE // 2)
    return jnp.where(first, pltpu.roll(x, LANES - QK_ROPE // 2, x.ndim - 1), pltpu.roll(x, QK_ROPE // 2, x.ndim - 1))


def _rope_tables(tp):
    pos = jnp.arange(tp, dtype=F32)
    inv_freq = ROPE_THETA ** (-jnp.arange(0, QK_ROPE, 2, dtype=F32) / QK_ROPE)
    ang = pos[:, None] * inv_freq[None, :]
    cos, sin = jnp.cos(ang), jnp.sin(ang)
    reps = LANES // QK_ROPE
    return jnp.tile(jnp.concatenate([cos, cos], -1), (1, reps)), jnp.tile(jnp.concatenate([-sin, sin], -1), (1, reps))


def _chunk_of(pos):
    shift = CHUNK.bit_length() - 1
    assert CHUNK == 1 << shift
    return jnp.where(pos < N_META, 0, 1 + lax.shift_right_arithmetic(pos - N_META, shift))


def _head_half(x, h):
    lane = lax.broadcasted_iota(jnp.int32, x.shape, x.ndim - 1)
    return jnp.where((lane // QK_ROPE) == (h % 2), x, jnp.zeros_like(x))


def _mla_prep(q, kv, proj, cos2, sin2):
    tp = q.shape[0]
    tr = _div_tile(tp, ROW_TILE, 8)
    nope_w = MLA_HEADS * QK_NOPE
    kr_block = (Q_LORA + KV_LORA) // LANES
    depth = QK_NOPE + LANES

    def body(q_ref, kv_ref, kr_ref, c_ref, s_ref, qp_out, kp_out):
        c = c_ref[...]
        s = s_ref[...]
        k = kr_ref[...]
        k = k + pltpu.roll(k, QK_ROPE, 1)
        k = (k * c + _swap_halves(k) * s).astype(CDT)
        for p in range(MLA_HEADS // 2):
            x = q_ref[:, nope_w + p * LANES:nope_w + (p + 1) * LANES]
            pair = (x * c + _swap_halves(x) * s).astype(CDT)
            for h in (2 * p, 2 * p + 1):
                qp_out[h, :, :QK_NOPE] = q_ref[:, h * QK_NOPE:(h + 1) * QK_NOPE].astype(CDT)
                qp_out[h, :, QK_NOPE:] = _head_half(pair, h)
                kp_out[h, :, :QK_NOPE] = kv_ref[:, 2 * h * QK_NOPE:(2 * h + 1) * QK_NOPE]
                kp_out[h, :, QK_NOPE:] = k

    tab = pl.BlockSpec((tr, LANES), lambda i: (i, 0))
    per_head = pl.BlockSpec((MLA_HEADS, tr, depth), lambda i: (0, i, 0))
    out = jax.ShapeDtypeStruct((MLA_HEADS, tp, depth), CDT)
    return pl.pallas_call(
        body,
        grid=(tp // tr,),
        in_specs=[pl.BlockSpec((tr, q.shape[1]), lambda i: (i, 0)), pl.BlockSpec((tr, kv.shape[1]), lambda i: (i, 0)),
                  pl.BlockSpec((tr, LANES), lambda i: (i, kr_block)), tab, tab],
        out_specs=[per_head, per_head],
        out_shape=[out, out],
        compiler_params=_params("parallel"),
        name="mla_prep",
    )(q, kv, proj, cos2, sin2)


def _dot_nt(a, b):
    return lax.dot_general(a, b, (((1,), (1,)), ((), ())), preferred_element_type=F32)


def _dot_tn(a, b):
    return lax.dot_general(a, b, (((0,), (0,)), ((), ())), preferred_element_type=F32)


def _dot(a, b):
    return jnp.dot(a, b, preferred_element_type=F32)


def _chunk_scalar(p):
    return jnp.where(p < N_META, 0, 1 + jnp.maximum(p - N_META, 0) // CHUNK)


def _last_key_block(i, bq, bk, nk):
    cq = _chunk_scalar(i * bq + bq - 1)
    return jnp.minimum((N_META + CHUNK * cq - 1) // bk, nk - 1)


def _full_key_blocks(i, bq, bk):
    return (N_META + CHUNK * _chunk_scalar(i * bq)) // bk


def _first_query_block(j, bk, bq):
    p0 = N_META + CHUNK * (jnp.maximum(j * bk - N_META, 0) // CHUNK)
    return p0 // bq


def _first_full_query_block(j, bk, bq, nq):
    ck = _chunk_scalar(j * bk + bk - 1)
    p0 = jnp.where(ck == 0, 0, N_META + CHUNK * (ck - 1))
    return jnp.minimum((p0 + bq - 1) // bq, nq)


def _chunk_mask(q0, k0, shape, keys_on_rows):
    if keys_on_rows:
        kc = _chunk_of(k0 + lax.broadcasted_iota(jnp.int32, (shape[0], 1), 0))
        qc = _chunk_of(q0 + lax.broadcasted_iota(jnp.int32, (1, shape[1]), 1))
    else:
        qc = _chunk_of(q0 + lax.broadcasted_iota(jnp.int32, (shape[0], 1), 0))
        kc = _chunk_of(k0 + lax.broadcasted_iota(jnp.int32, (1, shape[1]), 1))
    return kc <= qc


def _attn_fwd(qp, kp, kv):
    tp = qp.shape[1]
    depth = qp.shape[2]
    bq = bk = _div_tile(tp, ATT_TILE, LANES)
    nq, nk = tp // bq, tp // bk
    scale = (QK_NOPE + QK_ROPE) ** -0.5

    def body(q_ref, k_ref, v_ref, o_ref, lse_ref):
        def q_block(i, _):
            q0 = pl.multiple_of(i * bq, bq)
            qb = q_ref[pl.ds(q0, bq), :]

            def k_step(masked, j, carry):
                m_old, l_old, acc = carry
                k0 = pl.multiple_of(j * bk, bk)
                s = _dot_nt(qb, k_ref[pl.ds(k0, bk), :]) * scale
                if masked:
                    s = jnp.where(_chunk_mask(q0, k0, s.shape, False), s, NEG_BIG)
                m_new = jnp.maximum(m_old, jnp.max(s, axis=-1, keepdims=True))
                alpha = jnp.exp(m_old - m_new)
                p = jnp.exp(s - m_new)
                l_new = alpha * l_old + jnp.sum(p, axis=-1, keepdims=True)
                acc = alpha * acc + _dot(p.astype(CDT), v_ref[pl.ds(k0, bk), :])
                return m_new, l_new, acc

            n_full = _full_key_blocks(i, bq, bk)
            carry = (jnp.full((bq, 1), NEG_BIG, F32), jnp.zeros((bq, 1), F32), jnp.zeros((bq, V_HEAD), F32))
            carry = lax.fori_loop(0, n_full, lambda j, c: k_step(False, j, c), carry)
            m_fin, l_fin, acc = lax.fori_loop(n_full, _last_key_block(i, bq, bk, nk) + 1,
                                              lambda j, c: k_step(True, j, c), carry)
            o_ref[pl.ds(q0, bq), :] = acc / l_fin
            lse_ref[pl.ds(q0, bq), :] = m_fin + jnp.log(l_fin)
            return 0

        lax.fori_loop(0, nq, q_block, 0)

    per_head = pl.BlockSpec((None, tp, depth), lambda h: (h, 0, 0))
    return pl.pallas_call(
        body,
        grid=(MLA_HEADS,),
        in_specs=[per_head, per_head, pl.BlockSpec((tp, V_HEAD), lambda h: (0, 2 * h + 1))],
        out_specs=[pl.BlockSpec((tp, V_HEAD), lambda h: (0, h)), pl.BlockSpec((None, tp, 1), lambda h: (h, 0, 0))],
        out_shape=[jax.ShapeDtypeStruct((tp, MLA_HEADS * V_HEAD), F32), jax.ShapeDtypeStruct((MLA_HEADS, tp, 1), F32)],
        compiler_params=_params("parallel"),
        name="attn_fwd",
    )(qp, kp, kv)


def _attn_delta(d_out, out):
    tp = out.shape[0]
    tr = _div_tile(tp, ROW_TILE, 8)

    def body(do_ref, o_ref, d_ref):
        for h in range(MLA_HEADS):
            cols = slice(h * V_HEAD, (h + 1) * V_HEAD)
            d_ref[h] = jnp.sum(do_ref[:, cols] * o_ref[:, cols], axis=-1, keepdims=True)

    row = pl.BlockSpec((tr, MLA_HEADS * V_HEAD), lambda i: (i, 0))
    return pl.pallas_call(
        body,
        grid=(tp // tr,),
        in_specs=[row, row],
        out_specs=pl.BlockSpec((MLA_HEADS, tr, 1), lambda i: (0, i, 0)),
        out_shape=jax.ShapeDtypeStruct((MLA_HEADS, tp, 1), F32),
        compiler_params=_params("parallel"),
        name="attn_delta",
    )(d_out, out)


def _attn_bwd(qp, kp, kv, d_out, lse, delta):
    tp = qp.shape[1]
    depth = qp.shape[2]
    bq = bk = _div_tile(tp, ATT_TILE, LANES)
    nq, nk = tp // bq, tp // bk
    scale = (QK_NOPE + QK_ROPE) ** -0.5
    lse_rows = lse.reshape(MLA_HEADS, nq, 1, bq)
    delta_rows = delta.reshape(MLA_HEADS, nq, 1, bq)

    def body(q_ref, k_ref, v_ref, do_ref, lse_ref, dl_ref, dqn_ref, dqr_ref, dkv_ref, dkr_ref, dq_acc):
        h = pl.program_id(0)
        dq_acc[...] = jnp.zeros(dq_acc.shape, F32)

        def k_block(j, _):
            k0 = pl.multiple_of(j * bk, bk)
            kb = k_ref[pl.ds(k0, bk), :]
            vb = v_ref[pl.ds(k0, bk), :]

            def q_step(masked, i, carry):
                dk, dv = carry
                q0 = pl.multiple_of(i * bq, bq)
                qb = q_ref[pl.ds(q0, bq), :]
                dob = do_ref[pl.ds(q0, bq), :].astype(CDT)
                s_t = _dot_nt(kb, qb) * scale
                if masked:
                    s_t = jnp.where(_chunk_mask(q0, k0, s_t.shape, True), s_t, NEG_BIG)
                p_t = jnp.exp(s_t - lse_ref[i])
                dv = dv + _dot(p_t.astype(CDT), dob)
                dp_t = _dot_nt(vb, dob)
                ds_t = (p_t * (dp_t - dl_ref[i]) * scale).astype(CDT)
                dk = dk + _dot(ds_t, qb)
                dq_acc[pl.ds(q0, bq), :] += _dot_tn(ds_t, kb)
                return dk, dv

            i_full = _first_full_query_block(j, bk, bq, nq)
            carry = (jnp.zeros((bk, depth), F32), jnp.zeros((bk, V_HEAD), F32))
            carry = lax.fori_loop(_first_query_block(j, bk, bq), i_full, lambda i, c: q_step(True, i, c), carry)
            dk, dv = lax.fori_loop(i_full, nq, lambda i, c: q_step(False, i, c), carry)
            dkv_ref[pl.ds(k0, bk), :QK_NOPE] = dk[:, :QK_NOPE].astype(CDT)
            dkv_ref[pl.ds(k0, bk), QK_NOPE:] = dv.astype(CDT)
            dkr_ref[pl.ds(k0, bk), :] = dk[:, QK_NOPE:]
            return 0

        lax.fori_loop(0, nk, k_block, 0)
        dqn_ref[...] = dq_acc[:, :QK_NOPE].astype(CDT)
        dqr_ref[...] = _head_half(dq_acc[:, QK_NOPE:], h)

    per_head = pl.BlockSpec((None, tp, depth), lambda h: (h, 0, 0))
    stat = pl.BlockSpec((None, nq, 1, bq), lambda h: (h, 0, 0, 0))
    lanes_out = pl.BlockSpec((None, tp, LANES), lambda h: (h, 0, 0))
    return pl.pallas_call(
        body,
        grid=(MLA_HEADS,),
        in_specs=[per_head, per_head, pl.BlockSpec((tp, V_HEAD), lambda h: (0, 2 * h + 1)),
                  pl.BlockSpec((tp, V_HEAD), lambda h: (0, h)), stat, stat],
        out_specs=[pl.BlockSpec((tp, QK_NOPE), lambda h: (0, h)), lanes_out,
                   pl.BlockSpec((tp, QK_NOPE + V_HEAD), lambda h: (0, h)), lanes_out],
        out_shape=[jax.ShapeDtypeStruct((tp, MLA_HEADS * QK_NOPE), CDT), jax.ShapeDtypeStruct((MLA_HEADS, tp, LANES), F32),
                   jax.ShapeDtypeStruct((tp, MLA_HEADS * (QK_NOPE + V_HEAD)), CDT),
                   jax.ShapeDtypeStruct((MLA_HEADS, tp, LANES), F32)],
        scratch_shapes=[pltpu.VMEM((tp, depth), F32)],
        compiler_params=_params("parallel"),
        name="attn_bwd",
    )(qp, kp, kv, d_out, lse_rows, delta_rows)


def _mla_unprep(dqr_h, dkr_h, cos2, sin2):
    tp = dqr_h.shape[1]
    tr = _div_tile(tp, ROW_TILE, 8)
    wr = MLA_HEADS * QK_ROPE

    def body(dq_ref, dk_ref, c_ref, s_ref, dqr_out, dkr_out):
        c = c_ref[...]
        s = s_ref[...]
        for p in range(MLA_HEADS // 2):
            x = dq_ref[2 * p] + dq_ref[2 * p + 1]
            dqr_out[:, p * LANES:(p + 1) * LANES] = (x * c - _swap_halves(x) * s).astype(CDT)
        t = dk_ref[0]
        for h in range(1, MLA_HEADS):
            t = t + dk_ref[h]
        t = t * c - _swap_halves(t) * s
        t = t + pltpu.roll(t, QK_ROPE, 1)
        lane = lax.broadcasted_iota(jnp.int32, t.shape, 1)
        dkr_out[...] = jnp.where(lane < QK_ROPE, t, 0.0)

    per_head = pl.BlockSpec((MLA_HEADS, tr, LANES), lambda i: (0, i, 0))
    tab = pl.BlockSpec((tr, LANES), lambda i: (i, 0))
    return pl.pallas_call(
        body,
        grid=(tp // tr,),
        in_specs=[per_head, per_head, tab, tab],
        out_specs=[pl.BlockSpec((tr, wr), lambda i: (i, 0)), tab],
        out_shape=[jax.ShapeDtypeStruct((tp, wr), CDT), jax.ShapeDtypeStruct((tp, LANES), F32)],
        compiler_params=_params("parallel"),
        name="mla_unprep",
    )(dqr_h, dkr_h, cos2, sin2)


HALO = 8


def _softplus(x):
    return jnp.maximum(x, 0.0) + jnp.log1p(jnp.exp(-jnp.abs(x)))


def _one_minus_sq(log_a, a):
    return -jnp.tanh(log_a) * (a * a + 1.0)


def _gelu(y):
    k = math.sqrt(2.0 / math.pi)
    return 0.5 * y * (1.0 + jnp.tanh(k * (y + 0.044715 * (y * y * y))))


def _gelu_grad(y):
    k = math.sqrt(2.0 / math.pi)
    th = jnp.tanh(k * (y + 0.044715 * (y * y * y)))
    return 0.5 * (1.0 + th) + 0.5 * y * (1.0 - th * th) * (k * (1.0 + 3.0 * 0.044715 * (y * y)))


def _lru_gates_fwd(xy, conv_w, conv_b, w_ga, b_ga, w_gx, b_gx, lam):
    tp = xy.shape[0]
    dr = xy.shape[1] // 2
    bw = dr // RNN_BLOCKS
    tr = _div_tile(tp, ROW_TILE, 8)

    def body(x_ref, halo_ref, cw_ref, cb_ref, wa_ref, ba_ref, wx_ref, bx_ref, lam_ref,
             xc_ref, r_ref, i_ref, a_ref, b_ref, xs):
        i = pl.program_id(0)
        xs[0:HALO, :] = jnp.where(i == 0, 0.0, halo_ref[...])
        xs[HALO:, :] = x_ref[...]
        xc = cb_ref[...] + cw_ref[0:1, :] * xs[pl.ds(HALO - CONV_W + 1, tr), :]
        for j in range(1, CONV_W):
            xc = xc + cw_ref[j:j + 1, :] * xs[pl.ds(HALO - CONV_W + 1 + j, tr), :]
        xcb = xc.astype(CDT)
        r = _sigmoid(_dot(xcb, wa_ref[...]) + ba_ref[...])
        ig = _sigmoid(_dot(xcb, wx_ref[...]) + bx_ref[...])
        log_a = (-LRU_C * r) * _softplus(-lam_ref[...])
        a = jnp.exp(log_a)
        xc_ref[...] = xc
        r_ref[...] = r
        i_ref[...] = ig
        a_ref[...] = a
        b_ref[...] = jnp.sqrt(_one_minus_sq(log_a, a)) * (ig * xc)

    blk = pl.BlockSpec((tr, bw), lambda i, n: (i, n))
    vec = pl.BlockSpec((1, bw), lambda i, n: (0, n))
    mat = pl.BlockSpec((None, bw, bw), lambda i, n: (n, 0, 0))
    bias = pl.BlockSpec((None, 1, bw), lambda i, n: (n, 0, 0))
    out = jax.ShapeDtypeStruct((tp, dr), F32)
    return pl.pallas_call(
        body,
        grid=(tp // tr, RNN_BLOCKS),
        in_specs=[blk, pl.BlockSpec((HALO, bw), lambda i, n: (jnp.maximum(i * (tr // HALO) - 1, 0), n)),
                  pl.BlockSpec((CONV_W, bw), lambda i, n: (0, n)), vec, mat, bias, mat, bias, vec],
        out_specs=[blk] * 5,
        out_shape=[out] * 5,
        scratch_shapes=[pltpu.VMEM((tr + HALO, bw), F32)],
        compiler_params=_params("parallel", "parallel"),
        name="lru_gates_fwd",
    )(xy, xy, conv_w, conv_b.reshape(1, dr), w_ga.astype(CDT), b_ga.reshape(RNN_BLOCKS, 1, bw),
      w_gx.astype(CDT), b_gx.reshape(RNN_BLOCKS, 1, bw), lam.reshape(1, dr))


def _stack_rows(rows):
    idx = lax.broadcasted_iota(jnp.int32, (len(rows), rows[0].shape[1]), 0)
    out = jnp.broadcast_to(rows[0], idx.shape)
    for j in range(1, len(rows)):
        out = jnp.where(idx == j, jnp.broadcast_to(rows[j], idx.shape), out)
    return out


def _lru_scan_fwd(a, b, xy):
    tp, dr = a.shape
    cw = SCAN_COLS
    ycol0 = dr // cw

    def body(a_ref, b_ref, y_ref, hs_ref, m_ref):
        def group(g, h):
            base = pl.multiple_of(g * 8, 8)
            at = a_ref[pl.ds(base, 8), :]
            bt = b_ref[pl.ds(base, 8), :]
            rows = []
            for j in range(8):
                h = at[j:j + 1, :] * h + bt[j:j + 1, :]
                rows.append(h)
            hs_ref[pl.ds(base, 8), :] = _stack_rows(rows)
            return h

        lax.fori_loop(0, tp // 8, group, jnp.zeros((1, cw), F32))
        m_ref[...] = (hs_ref[...] * _gelu(y_ref[...])).astype(CDT)

    col = pl.BlockSpec((tp, cw), lambda n: (0, n))
    return pl.pallas_call(
        body,
        grid=(dr // cw,),
        in_specs=[col, col, pl.BlockSpec((tp, cw), lambda n: (0, ycol0 + n))],
        out_specs=[col, col],
        out_shape=[jax.ShapeDtypeStruct((tp, dr), F32), jax.ShapeDtypeStruct((tp, dr), CDT)],
        compiler_params=_params("parallel"),
        name="lru_scan_fwd",
    )(a, b, xy)


def _lru_scan_bwd(a, hs, dm, xy):
    tp, dr = a.shape
    cw = SCAN_COLS
    ycol0 = dr // cw
    ng = tp // 8

    def body(a_ref, hs_ref, dm_ref, y_ref, db_ref, da_ref, dy_ref):
        y = y_ref[...]
        dm = dm_ref[...]
        db_ref[...] = dm * _gelu(y)
        dy_ref[...] = (dm * hs_ref[...] * _gelu_grad(y)).astype(CDT)

        def group(k, carry):
            g_next, a_next = carry
            g = ng - 1 - k
            base = pl.multiple_of(g * 8, 8)
            prev = pl.multiple_of(jnp.maximum(g - 1, 0) * 8, 8)
            dt = db_ref[pl.ds(base, 8), :]
            at = a_ref[pl.ds(base, 8), :]
            ht = hs_ref[pl.ds(base, 8), :]
            h_before = jnp.where(g == 0, 0.0, hs_ref[pl.ds(prev, 8), :][7:8, :])
            g_rows = [None] * 8
            da_rows = [None] * 8
            for j in range(7, -1, -1):
                g_cur = dt[j:j + 1, :] + a_next * g_next
                g_rows[j] = g_cur
                da_rows[j] = g_cur * (ht[j - 1:j, :] if j > 0 else h_before)
                g_next = g_cur
                a_next = at[j:j + 1, :]
            db_ref[pl.ds(base, 8), :] = _stack_rows(g_rows)
            da_ref[pl.ds(base, 8), :] = _stack_rows(da_rows)
            return g_next, a_next

        zero = jnp.zeros((1, cw), F32)
        lax.fori_loop(0, ng, group, (zero, zero))

    col = pl.BlockSpec((tp, cw), lambda n: (0, n))
    return pl.pallas_call(
        body,
        grid=(dr // cw,),
        in_specs=[col, col, col, pl.BlockSpec((tp, cw), lambda n: (0, ycol0 + n))],
        out_specs=[col, col, col],
        out_shape=[jax.ShapeDtypeStruct((tp, dr), F32), jax.ShapeDtypeStruct((tp, dr), F32),
                   jax.ShapeDtypeStruct((tp, dr), CDT)],
        compiler_params=_params("parallel"),
        name="lru_scan_bwd",
    )(a, hs, dm, xy)


def _lru_gates_bwd(db, da, xc, r, ig, a, lam, w_ga, w_gx):
    tp, dr = xc.shape
    bw = dr // RNN_BLOCKS
    tr = _div_tile(tp, ROW_TILE, 8)
    nr = tp // tr

    def body(db_ref, da_ref, xc_ref, r_ref, i_ref, a_ref, lam_ref, wa_ref, wx_ref,
             dxc_ref, dwa_ref, dba_ref, dwx_ref, dbx_ref, dlam_ref):
        i = pl.program_id(1)
        xc = xc_ref[...]
        r = r_ref[...]
        ig = i_ref[...]
        a = a_ref[...]
        dbv = db_ref[...]
        sp = _softplus(-lam_ref[...])
        log_a = (-LRU_C * r) * sp
        s = jnp.sqrt(_one_minus_sq(log_a, a))
        d_ix = dbv * s
        d_s = dbv * (ig * xc)
        d_log_a = da_ref[...] * a - d_s * (a * a) / s
        d_r = d_log_a * (-LRU_C * sp)
        d_sp = jnp.sum(d_log_a * (-LRU_C * r), axis=0, keepdims=True)
        dzr = d_r * r * (1.0 - r)
        dzi = (d_ix * xc) * ig * (1.0 - ig)
        dzr_b = dzr.astype(CDT)
        dzi_b = dzi.astype(CDT)
        xcb = xc.astype(CDT)
        dxc_ref[...] = d_ix * ig + _dot_nt(dzr_b, wa_ref[...]) + _dot_nt(dzi_b, wx_ref[...])
        dwa = _dot_tn(xcb, dzr_b)
        dwx = _dot_tn(xcb, dzi_b)
        dba = jnp.sum(dzr, axis=0, keepdims=True)
        dbx = jnp.sum(dzi, axis=0, keepdims=True)

        @pl.when(i == 0)
        def _():
            dwa_ref[...] = dwa
            dwx_ref[...] = dwx
            dba_ref[...] = dba
            dbx_ref[...] = dbx
            dlam_ref[...] = d_sp

        @pl.when(i > 0)
        def _():
            dwa_ref[...] += dwa
            dwx_ref[...] += dwx
            dba_ref[...] += dba
            dbx_ref[...] += dbx
            dlam_ref[...] += d_sp

        @pl.when(i == nr - 1)
        def _():
            dlam_ref[...] = dlam_ref[...] * (-_sigmoid(-lam_ref[...]))

    blk = pl.BlockSpec((tr, bw), lambda n, i: (i, n))
    vec = pl.BlockSpec((1, bw), lambda n, i: (0, n))
    mat = pl.BlockSpec((None, bw, bw), lambda n, i: (n, 0, 0))
    bias = pl.BlockSpec((None, 1, bw), lambda n, i: (n, 0, 0))
    return pl.pallas_call(
        body,
        grid=(RNN_BLOCKS, nr),
        in_specs=[blk] * 6 + [vec, mat, mat],
        out_specs=[blk, mat, bias, mat, bias, vec],
        out_shape=[jax.ShapeDtypeStruct((tp, dr), F32),
                   jax.ShapeDtypeStruct((RNN_BLOCKS, bw, bw), F32), jax.ShapeDtypeStruct((RNN_BLOCKS, 1, bw), F32),
                   jax.ShapeDtypeStruct((RNN_BLOCKS, bw, bw), F32), jax.ShapeDtypeStruct((RNN_BLOCKS, 1, bw), F32),
                   jax.ShapeDtypeStruct((1, dr), F32)],
        compiler_params=_params("parallel", "arbitrary"),
        name="lru_gates_bwd",
    )(db, da, xc, r, ig, a, lam.reshape(1, dr), w_ga.astype(CDT), w_gx.astype(CDT))


def _lru_conv_bwd(dxc, xy, conv_w):
    tp, dr = dxc.shape
    bw = dr // RNN_BLOCKS
    tr = _div_tile(tp, ROW_TILE, 8)
    nr = tp // tr
    per = tr // HALO

    def body(d_ref, dnext_ref, x_ref, xprev_ref, cw_ref, dxb_ref, dcw_ref, dcb_ref, ds, xs):
        i = pl.program_id(1)
        d = d_ref[...]
        ds[0:tr, :] = d
        ds[tr:, :] = jnp.where(i == nr - 1, 0.0, dnext_ref[...])
        xs[0:HALO, :] = jnp.where(i == 0, 0.0, xprev_ref[...])
        xs[HALO:, :] = x_ref[...]
        dxb = cw_ref[0:1, :] * ds[pl.ds(CONV_W - 1, tr), :]
        for j in range(1, CONV_W):
            dxb = dxb + cw_ref[j:j + 1, :] * ds[pl.ds(CONV_W - 1 - j, tr), :]
        dxb_ref[...] = dxb.astype(CDT)
        dcb = jnp.sum(d, axis=0, keepdims=True)
        dcw = [jnp.sum(d * xs[pl.ds(HALO - CONV_W + 1 + j, tr), :], axis=0, keepdims=True) for j in range(CONV_W)]

        @pl.when(i == 0)
        def _():
            dcb_ref[...] = dcb
            for j in range(CONV_W):
                dcw_ref[j] = dcw[j]

        @pl.when(i > 0)
        def _():
            dcb_ref[...] += dcb
            for j in range(CONV_W):
                dcw_ref[j] += dcw[j]

    blk = pl.BlockSpec((tr, bw), lambda n, i: (i, n))
    return pl.pallas_call(
        body,
        grid=(RNN_BLOCKS, nr),
        in_specs=[blk, pl.BlockSpec((HALO, bw), lambda n, i: (jnp.minimum((i + 1) * per, tp // HALO - 1), n)),
                  blk, pl.BlockSpec((HALO, bw), lambda n, i: (jnp.maximum(i * per - 1, 0), n)),
                  pl.BlockSpec((CONV_W, bw), lambda n, i: (0, n))],
        out_specs=[blk, pl.BlockSpec((CONV_W, 1, bw), lambda n, i: (0, 0, n)), pl.BlockSpec((1, bw), lambda n, i: (0, n))],
        out_shape=[jax.ShapeDtypeStruct((tp, dr), CDT), jax.ShapeDtypeStruct((CONV_W, 1, dr), F32),
                   jax.ShapeDtypeStruct((1, dr), F32)],
        scratch_shapes=[pltpu.VMEM((tr + HALO, bw), F32), pltpu.VMEM((tr + HALO, bw), F32)],
        compiler_params=_params("parallel", "arbitrary"),
        name="lru_conv_bwd",
    )(dxc, dxc, xy, xy, conv_w)


def _me():
    return lax.axis_index("x"), lax.axis_index("y"), lax.axis_index("c")


def _peer(rel):
    x, y, c = _me()
    return (1 - x if rel & 4 else x, 1 - y if rel & 2 else y, 1 - c if rel & 1 else c)


def _chip_of(dev):
    return 2 * dev[0] + dev[1]


def _linear(dev):
    return 4 * dev[0] + 2 * dev[1] + dev[2]


CHIP_RELS = (4, 2, 6)
ALL_RELS = (1, 2, 3, 4, 5, 6, 7)
PAIR_RELS = (1,)


def _scatter_send(pieces, rels, piece_of, name):
    n = len(rels)

    def body(src_ref, recv_ref, send_sems, recv_sems):
        copies = []
        for k, rel in enumerate(rels):
            peer = _peer(rel)
            cp = pltpu.make_async_remote_copy(
                src_ref=src_ref.at[piece_of(peer)], dst_ref=recv_ref.at[k],
                send_sem=send_sems.at[k], recv_sem=recv_sems.at[k], device_id=peer, device_id_type=MESH)
            cp.start()
            copies.append(cp)
        for cp in copies:
            cp.wait()

    return pl.pallas_call(
        body,
        in_specs=[pl.BlockSpec(memory_space=pl.ANY)],
        out_specs=pl.BlockSpec(memory_space=pl.ANY),
        out_shape=jax.ShapeDtypeStruct((n,) + pieces.shape[1:], pieces.dtype),
        scratch_shapes=[pltpu.SemaphoreType.DMA((n,)), pltpu.SemaphoreType.DMA((n,))],
        name=name,
    )(pieces)


def _gather_send(piece, rels, n_slots, slot_of, name, n_chunks=1):
    n = len(rels)
    rows = piece.shape[0]
    if rows % (8 * n_chunks):
        n_chunks = 1
    rc = rows // n_chunks

    def body(src_ref, out_ref, send_sems, recv_sems, local_sems):
        me = _me()

        def part(ref, q):
            return ref.at[pl.ds(q * rc, rc)]

        def remote(k, q, slot_dev, to):
            return pltpu.make_async_remote_copy(
                src_ref=part(src_ref, q), dst_ref=part(out_ref.at[slot_of(slot_dev)], q),
                send_sem=send_sems.at[k * n_chunks + q], recv_sem=recv_sems.at[k * n_chunks + q],
                device_id=to, device_id_type=MESH)

        mine = [pltpu.make_async_copy(part(src_ref, q), part(out_ref.at[slot_of(me)], q), local_sems.at[q])
                for q in range(n_chunks)]
        for cp in mine:
            cp.start()
        sends = [remote(k, q, me, _peer(rel)) for k, rel in enumerate(rels) for q in range(n_chunks)]
        for cp in sends:
            cp.start()
        for k, rel in enumerate(rels):
            for q in range(n_chunks):
                remote(k, q, _peer(rel), _peer(rel)).wait_recv()
        for cp in sends:
            cp.wait_send()
        for cp in mine:
            cp.wait()

    return pl.pallas_call(
        body,
        in_specs=[pl.BlockSpec(memory_space=pl.ANY)],
        out_specs=pl.BlockSpec(memory_space=pl.ANY),
        out_shape=jax.ShapeDtypeStruct((n_slots,) + piece.shape, piece.dtype),
        scratch_shapes=[pltpu.SemaphoreType.DMA((n * n_chunks,)), pltpu.SemaphoreType.DMA((n * n_chunks,)),
                        pltpu.SemaphoreType.DMA((n_chunks,))],
        name=name,
    )(piece)


def _gather_chips(shard, name):
    return _gather_send(shard, CHIP_RELS, N_CHIPS, _chip_of, name)


def _sum_pieces(own, recv, name):
    rr, cc = own.shape
    n = recv.shape[0]
    tr = _div_tile(rr, max(8, (1 << 17) // cc // 8 * 8), 8)

    def body(own_ref, recv_ref, o_ref):
        acc = own_ref[...]
        for k in range(n):
            acc = acc + recv_ref[k].astype(F32)
        o_ref[...] = acc

    return pl.pallas_call(
        body,
        grid=(rr // tr,),
        in_specs=[pl.BlockSpec((tr, cc), lambda i: (i, 0)), pl.BlockSpec((n, tr, cc), lambda i: (0, i, 0))],
        out_specs=pl.BlockSpec((tr, cc), lambda i: (i, 0)),
        out_shape=jax.ShapeDtypeStruct((rr, cc), F32),
        compiler_params=_params("parallel"),
        name=name,
    )(own, recv)


def _reduce_to_owner(g8, payload_dtype, name):
    x, y, c = _me()
    own = lax.dynamic_index_in_dim(g8, _linear((x, y, c)), 0, keepdims=False)
    recv = _scatter_send(g8.astype(payload_dtype), ALL_RELS, _linear, name + "_scatter")
    return _sum_pieces(own, recv, name + "_sum")


def _reduce_sharded(g, name):
    n_chips, nl, rr, cc = g.shape
    half = (nl // 2) * rr
    red = _reduce_to_owner(g.reshape(N_DEV, half, cc), CDT, name)
    both = _gather_send(red, PAIR_RELS, 2, lambda dev: dev[2], name + "_pair", n_chunks=PAIR_CHUNKS)
    return both.reshape(nl, rr, cc)


def _adamw(w, g, m, v, name):
    rr, cc = w.shape
    tr = _div_tile(rr, max(8, (1 << 17) // cc // 8 * 8), 8)
    c1 = 1.0 - ADAM_B1 ** ADAM_STEP
    c2 = 1.0 - ADAM_B2 ** ADAM_STEP

    def body(w_ref, g_ref, m_ref, v_ref, d_ref, mo_ref, vo_ref):
        g_ = g_ref[...]
        m_ = ADAM_B1 * m_ref[...] + (1.0 - ADAM_B1) * g_
        v_ = ADAM_B2 * v_ref[...] + (1.0 - ADAM_B2) * (g_ * g_)
        d_ref[...] = -ADAM_LR * ((m_ / c1) / (jnp.sqrt(v_ / c2) + ADAM_EPS) + ADAM_WD * w_ref[...])
        mo_ref[...] = m_
        vo_ref[...] = v_

    blk = pl.BlockSpec((tr, cc), lambda i: (i, 0))
    out = jax.ShapeDtypeStruct((rr, cc), F32)
    return pl.pallas_call(
        body,
        grid=(rr // tr,),
        in_specs=[blk] * 4,
        out_specs=[blk] * 3,
        out_shape=[out] * 3,
        compiler_params=_params("parallel"),
        name=name,
    )(w, g, m, v)


def _pack(arrays, cols, row_mult):
    flat = jnp.concatenate([a.reshape(-1) for a in arrays])
    rows = -(-flat.shape[0] // cols)
    rows = -(-rows // row_mult) * row_mult
    return jnp.pad(flat, (0, rows * cols - flat.shape[0])).reshape(rows, cols)


def _unpack(buf, shapes):
    flat = buf.reshape(-1)
    out, off = [], 0
    for s in shapes:
        n = math.prod(s)
        out.append(flat[off:off + n].reshape(s))
        off += n
    return out


def kernel(x, meta_tokens, norm_mix, norm_ffn, norm_final, mla_w_in, mla_q_norm, mla_kv_norm, mla_w_uq, mla_w_ukv, mla_w_o, lru_w_in, lru_conv_w, lru_conv_b, lru_w_gate_a, lru_b_gate_a, lru_w_gate_x, lru_b_gate_x, lru_lambda, lru_w_o, ffn_w_gu, ffn_w_down, loss_target, m_meta_tokens, m_norm_mix, m_norm_ffn, m_norm_final, m_mla_w_in, m_mla_q_norm, m_mla_kv_norm, m_mla_w_uq, m_mla_w_ukv, m_mla_w_o, m_lru_w_in, m_lru_conv_w, m_lru_conv_b, m_lru_w_gate_a, m_lru_b_gate_a, m_lru_w_gate_x, m_lru_b_gate_x, m_lru_lambda, m_lru_w_o, m_ffn_w_gu, m_ffn_w_down, v_meta_tokens, v_norm_mix, v_norm_ffn, v_norm_final, v_mla_w_in, v_mla_q_norm, v_mla_kv_norm, v_mla_w_uq, v_mla_w_ukv, v_mla_w_o, v_lru_w_in, v_lru_conv_w, v_lru_conv_b, v_lru_w_gate_a, v_lru_b_gate_a, v_lru_w_gate_x, v_lru_b_gate_x, v_lru_lambda, v_lru_w_o, v_ffn_w_gu, v_ffn_w_down):
    d = D_MODEL
    t_real = N_META + SEQ
    tp = _t_pad()
    n_mla = mla_w_in.shape[0]
    n_lru = lru_w_in.shape[0]
    h_dim = MLA_HEADS * V_HEAD
    w_in_cols = Q_LORA + KV_LORA + QK_ROPE
    w_in_pad = Q_LORA + KV_LORA + LANES
    q_cols = MLA_HEADS * (QK_NOPE + QK_ROPE)
    tmm = _div_tile(tp, MM_ROW_TILE, 16)
    tkt = _div_tile(tp, 1408, 16)

    def tile(n, pref):
        return _div_tile(n, pref, LANES)

    w_in4 = _gather_chips(jnp.pad(mla_w_in, ((0, 0), (0, 0), (0, w_in_pad - w_in_cols))).astype(CDT), "gather_mla_w_in")
    w_uq4 = _gather_chips(mla_w_uq.astype(CDT), "gather_mla_w_uq")
    w_ukv4 = _gather_chips(mla_w_ukv.astype(CDT), "gather_mla_w_ukv")
    w_o4 = _gather_chips(mla_w_o.astype(CDT), "gather_mla_w_o")
    lw_in4 = _gather_chips(lru_w_in.astype(CDT), "gather_lru_w_in")
    lw_o4 = _gather_chips(lru_w_o.astype(CDT), "gather_lru_w_o")
    w_gu4 = _gather_chips(ffn_w_gu.astype(CDT), "gather_ffn_w_gu")
    w_down4 = _gather_chips(ffn_w_down.astype(CDT), "gather_ffn_w_down")
    small_shapes = [meta_tokens.shape, lru_conv_w.shape, lru_conv_b.shape, lru_lambda.shape]
    csh = meta_tokens.shape[1]
    small4 = _gather_chips(_pack([meta_tokens, lru_conv_w, lru_conv_b, lru_lambda], csh, 16), "gather_small")
    small_full = [jnp.concatenate(parts, axis=-1) for parts in zip(*[_unpack(small4[k], small_shapes) for k in range(N_CHIPS)])]
    meta_full, conv_w_full, conv_b_full, lam_full = small_full

    w_uq_full = jnp.moveaxis(w_uq4, 0, 2).reshape(n_mla, Q_LORA, MLA_HEADS, QK_NOPE + QK_ROPE)
    w_uq_perm = jnp.concatenate([w_uq_full[..., :QK_NOPE].reshape(n_mla, Q_LORA, -1),
                                 w_uq_full[..., QK_NOPE:].reshape(n_mla, Q_LORA, -1)], axis=-1)
    cos2, sin2 = _rope_tables(tp)

    h = jnp.concatenate([meta_full, x[0], jnp.zeros((tp - t_real, d), F32)], axis=0)
    saved = []
    for layer in range(DEPTH):
        j = layer // 2
        s = {"h_in": h}
        hn = _rms_fwd(h, norm_mix[layer], width=d, col_block=0, name="norm_mix_fwd")
        s["hn"] = hn
        if layer % 2 == 0:
            proj = _mm("nn", hn, w_in4, kind="row", layer=j, tm=tmm, tn=tile(w_in_pad, 1152), tk=tile(d // N_CHIPS, 512), name="mla_in")
            c_q = _rms_fwd(proj, mla_q_norm[j], width=Q_LORA, col_block=0, name="q_norm_fwd")
            c_kv = _rms_fwd(proj, mla_kv_norm[j], width=KV_LORA, col_block=Q_LORA // KV_LORA, name="kv_norm_fwd")
            q = _mm("nn", c_q, w_uq_perm[j], tm=tmm, tn=tile(q_cols, 1024), tk=Q_LORA, name="mla_uq")
            kv = _mm("nn", c_kv, w_ukv4, kind="col", layer=j, out_dtype=CDT, tm=tmm, tn=tile(w_ukv4.shape[3], 1024), tk=KV_LORA, name="mla_ukv")
            qp, kp = _mla_prep(q, kv, proj, cos2, sin2)
            att, lse = _attn_fwd(qp, kp, kv)
            h = _mm("nn", att, w_o4, kind="row", layer=j, resid=h, tm=tmm, tn=tile(d, 1024), tk=tile(h_dim // N_CHIPS, 512), name="mla_out")
            s.update(proj=proj, c_q=c_q, c_kv=c_kv, qp=qp, kp=kp, kv=kv, att=att, lse=lse)
        else:
            xy = _mm("nn", hn, lw_in4, kind="col", layer=j, tm=tmm, tn=tile(lw_in4.shape[3], 1024), tk=d, name="lru_in")
            xc, r, ig, a, b = _lru_gates_fwd(xy, conv_w_full[j], conv_b_full[j], lru_w_gate_a[j], lru_b_gate_a[j],
                                             lru_w_gate_x[j], lru_b_gate_x[j], lam_full[j])
            hs, mixed = _lru_scan_fwd(a, b, xy)
            h = _mm("nn", mixed, lw_o4, kind="row", layer=j, resid=h, tm=tmm, tn=tile(d, 1024), tk=tile(d // N_CHIPS, 512), name="lru_out")
            s.update(xy=xy, xc=xc, r=r, ig=ig, a=a, hs=hs, mixed=mixed)
        s["h_mid"] = h
        hn2 = _rms_fwd(h, norm_ffn[layer], width=d, col_block=0, name="norm_ffn_fwd")
        gu = _mm("nn", hn2, w_gu4, kind="col", layer=layer, tm=tmm, tn=tile(w_gu4.shape[3], 1408), tk=d, name="ffn_gu")
        act = _swiglu_fwd(gu)
        h = _mm("nn", act, w_down4, kind="row", layer=layer, resid=h, tm=tmm, tn=tile(d, 1024), tk=tile(D_FF // N_CHIPS, 1408), name="ffn_down")
        s.update(hn2=hn2, gu=gu, act=act)
        saved.append(s)

    target = jnp.concatenate([jnp.zeros((N_META, d), F32), loss_target[0], jnp.zeros((tp - t_real, d), F32)], axis=0)
    dh, dhb, g_norm_final, loss_part = _final_loss(h, norm_final, target)
    loss = lax.psum(loss_part[0, 0], ("x", "y", "c"))

    g_norm_mix, g_norm_ffn = [None] * DEPTH, [None] * DEPTH
    g_q_norm, g_kv_norm = [None] * n_mla, [None] * n_mla
    g_w_uq = [None] * n_mla
    g_gate = {k: [None] * n_lru for k in ("wa", "ba", "wx", "bx", "lam", "cw", "cb")}
    G = {k: None for k in ("w_in", "w_ukv", "w_o", "lw_in", "lw_o", "w_gu", "w_down")}

    def grad_w(key, a_op, b_op, kind, lyr, n_layers, tm, tn):
        G[key] = _mm_tn(a_op, b_op, into=G[key], kind=kind, layer=lyr, n_layers=n_layers, tm=tm, tn=tn, tk=tkt, name="grad_" + key)

    for layer in reversed(range(DEPTH)):
        j = layer // 2
        s = saved[layer]
        grad_w("w_down", s["act"], dhb, "row", layer, DEPTH, tile(D_FF // N_CHIPS, 1408), tile(d, 1024))
        d_act = _mm("nt", dhb, w_down4, kind="row", layer=layer, tm=tmm, tn=tile(D_FF // N_CHIPS, 1408), tk=d, name="ffn_down_bwd")
        dgu = _swiglu_bwd(s["gu"], d_act)
        grad_w("w_gu", s["hn2"], dgu, "col", layer, DEPTH, tile(d, 1024), tile(w_gu4.shape[3], 1408))
        dhn2 = _mm("nt", dgu, w_gu4, kind="col", layer=layer, tm=tmm, tn=tile(d, 1024), tk=tile(w_gu4.shape[3], 1408), name="ffn_gu_bwd")
        dh, dhb, g_norm_ffn[layer] = _rms_bwd(s["h_mid"], norm_ffn[layer], dhn2, dh, width=d, col_block=0, name="norm_ffn_bwd")
        if layer % 2 == 0:
            grad_w("w_o", s["att"], dhb, "row", j, n_mla, tile(h_dim // N_CHIPS, 512), tile(d, 1024))
            d_att = _mm("nt", dhb, w_o4, kind="row", layer=j, tm=tmm, tn=tile(h_dim // N_CHIPS, 512), tk=d, name="mla_out_bwd")
            delta = _attn_delta(d_att, s["att"])
            dqn, dqr_h, dkv, dkr_h = _attn_bwd(s["qp"], s["kp"], s["kv"], d_att, s["lse"], delta)
            dqr, dkr = _mla_unprep(dqr_h, dkr_h, cos2, sin2)
            dq = jnp.concatenate([dqn, dqr], axis=-1)
            g_w_uq[j] = _mm_tn(s["c_q"], dq, tm=Q_LORA, tn=tile(q_cols, 1024), tk=tkt, name="grad_w_uq")
            dc_q = _mm("nt", dq, w_uq_perm[j], tm=tmm, tn=Q_LORA, tk=tile(q_cols, 1024), name="mla_uq_bwd")
            grad_w("w_ukv", s["c_kv"], dkv, "col", j, n_mla, KV_LORA, tile(w_ukv4.shape[3], 1024))
            dc_kv = _mm("nt", dkv, w_ukv4, kind="col", layer=j, tm=tmm, tn=KV_LORA, tk=tile(w_ukv4.shape[3], 1024), name="mla_ukv_bwd")
            dpq, _, g_q_norm[j] = _rms_bwd(s["proj"], mla_q_norm[j], dc_q, None, width=Q_LORA, col_block=0, name="q_norm_bwd")
            dpkv, _, g_kv_norm[j] = _rms_bwd(s["proj"], mla_kv_norm[j], dc_kv, None, width=KV_LORA, col_block=Q_LORA // KV_LORA, name="kv_norm_bwd")
            dproj = jnp.concatenate([dpq, dpkv, dkr], axis=-1).astype(CDT)
            grad_w("w_in", s["hn"], dproj, "row", j, n_mla, tile(d // N_CHIPS, 512), tile(w_in_pad, 1152))
            dhn = _mm("nt", dproj, w_in4, kind="row", layer=j, tm=tmm, tn=tile(d // N_CHIPS, 512), tk=tile(w_in_pad, 1152), name="mla_in_bwd")
        else:
            grad_w("lw_o", s["mixed"], dhb, "row", j, n_lru, tile(d // N_CHIPS, 512), tile(d, 1024))
            dm = _mm("nt", dhb, lw_o4, kind="row", layer=j, tm=tmm, tn=tile(d // N_CHIPS, 512), tk=d, name="lru_out_bwd")
            db, da, dy = _lru_scan_bwd(s["a"], s["hs"], dm, s["xy"])
            dxc, g_gate["wa"][j], g_gate["ba"][j], g_gate["wx"][j], g_gate["bx"][j], g_gate["lam"][j] = _lru_gates_bwd(
                db, da, s["xc"], s["r"], s["ig"], s["a"], lam_full[j], lru_w_gate_a[j], lru_w_gate_x[j])
            dxb, g_gate["cw"][j], g_gate["cb"][j] = _lru_conv_bwd(dxc, s["xy"], conv_w_full[j])
            dxy = jnp.concatenate([dxb, dy], axis=-1)
            grad_w("lw_in", s["hn"], dxy, "col", j, n_lru, tile(d, 1024), tile(lw_in4.shape[3], 1024))
            dhn = _mm("nt", dxy, lw_in4, kind="col", layer=j, tm=tmm, tn=tile(d, 1024), tk=tile(lw_in4.shape[3], 1024), name="lru_in_bwd")
        dh, dhb, g_norm_mix[layer] = _rms_bwd(s["h_in"], norm_mix[layer], dhn, dh, width=d, col_block=0, name="norm_mix_bwd")

    grad_x = dh[N_META:t_real][None]
    g_meta_full = dh[:N_META]

    nope_w = MLA_HEADS * QK_NOPE
    g_uq = jnp.stack(g_w_uq)
    g_uq = jnp.concatenate([g_uq[..., :nope_w].reshape(n_mla, Q_LORA, MLA_HEADS, QK_NOPE),
                            g_uq[..., nope_w:].reshape(n_mla, Q_LORA, MLA_HEADS, QK_ROPE)], axis=-1)
    G["w_uq"] = jnp.moveaxis(g_uq.reshape(n_mla, Q_LORA, N_CHIPS, q_cols // N_CHIPS), 2, 0)

    big = [("w_in", mla_w_in, m_mla_w_in, v_mla_w_in), ("w_uq", mla_w_uq, m_mla_w_uq, v_mla_w_uq),
           ("w_ukv", mla_w_ukv, m_mla_w_ukv, v_mla_w_ukv), ("w_o", mla_w_o, m_mla_w_o, v_mla_w_o),
           ("lw_in", lru_w_in, m_lru_w_in, v_lru_w_in), ("lw_o", lru_w_o, m_lru_w_o, v_lru_w_o),
           ("w_gu", ffn_w_gu, m_ffn_w_gu, v_ffn_w_gu), ("w_down", ffn_w_down, m_ffn_w_down, v_ffn_w_down)]
    res = {}
    for key, w, m, v in big:
        g = _reduce_sharded(G[key], "reduce_" + key)
        if key == "w_in":
            g = g[:, :, :w_in_cols]
        shp = w.shape
        two_d = (shp[0] * shp[1], shp[2])
        dlt, nm, nv = _adamw(w.reshape(two_d), g.reshape(two_d), m.reshape(two_d), v.reshape(two_d), "adamw_" + key)
        res[key] = (g, dlt.reshape(shp), nm.reshape(shp), nv.reshape(shp))

    g_small_full = [g_meta_full, jnp.stack(g_gate["cw"]).reshape(n_lru, CONV_W, d), jnp.stack(g_gate["cb"]).reshape(n_lru, d),
                    jnp.stack(g_gate["lam"]).reshape(n_lru, d)]
    g_small4 = jnp.stack([_pack([a[..., k * csh:(k + 1) * csh] for a in g_small_full], csh, 16) for k in range(N_CHIPS)])
    rows_s = g_small4.shape[1]
    red = _reduce_to_owner(g_small4.reshape(N_DEV, rows_s // 2, csh), F32, "reduce_small")
    g_small = _gather_send(red, PAIR_RELS, 2, lambda dev: dev[2], "reduce_small_pair").reshape(rows_s, csh)
    small_w = [meta_tokens, lru_conv_w, lru_conv_b, lru_lambda]
    small_m = [m_meta_tokens, m_lru_conv_w, m_lru_conv_b, m_lru_lambda]
    small_v = [v_meta_tokens, v_lru_conv_w, v_lru_conv_b, v_lru_lambda]
    sd, sm, sv = _adamw(_pack(small_w, csh, 16), g_small, _pack(small_m, csh, 16), _pack(small_v, csh, 16), "adamw_small")
    small_out = [_unpack(buf, small_shapes) for buf in (g_small, sd, sm, sv)]

    rep_w = [norm_mix, norm_ffn, norm_final, mla_q_norm, mla_kv_norm, lru_w_gate_a, lru_b_gate_a, lru_w_gate_x, lru_b_gate_x]
    rep_m = [m_norm_mix, m_norm_ffn, m_norm_final, m_mla_q_norm, m_mla_kv_norm, m_lru_w_gate_a, m_lru_b_gate_a, m_lru_w_gate_x, m_lru_b_gate_x]
    rep_v = [v_norm_mix, v_norm_ffn, v_norm_final, v_mla_q_norm, v_mla_kv_norm, v_lru_w_gate_a, v_lru_b_gate_a, v_lru_w_gate_x, v_lru_b_gate_x]
    rep_g = [jnp.stack(g_norm_mix), jnp.stack(g_norm_ffn), g_norm_final, jnp.stack(g_q_norm), jnp.stack(g_kv_norm),
             jnp.stack(g_gate["wa"]), jnp.stack(g_gate["ba"]), jnp.stack(g_gate["wx"]), jnp.stack(g_gate["bx"])]
    rep_shapes = [w.shape for w in rep_w]
    g_rep = _pack(rep_g, LANES, 8 * N_DEV)
    rows_r = g_rep.shape[0]
    red = _reduce_to_owner(g_rep.reshape(N_DEV, rows_r // N_DEV, LANES), F32, "reduce_rep")
    g_rep = _gather_send(red, ALL_RELS, N_DEV, _linear, "reduce_rep_all").reshape(rows_r, LANES)
    rd, rm, rv = _adamw(_pack(rep_w, LANES, 8 * N_DEV), g_rep, _pack(rep_m, LANES, 8 * N_DEV), _pack(rep_v, LANES, 8 * N_DEV), "adamw_rep")
    rep_out = [_unpack(buf, rep_shapes) for buf in (g_rep, rd, rm, rv)]

    def leaf(kind):
        s_, r_ = small_out[kind], rep_out[kind]
        return [s_[0], r_[0], r_[1], r_[2], res["w_in"][kind], r_[3], r_[4], res["w_uq"][kind], res["w_ukv"][kind],
                res["w_o"][kind], res["lw_in"][kind], s_[1], s_[2], r_[5], r_[6], r_[7], r_[8], s_[3],
                res["lw_o"][kind], res["w_gu"][kind], res["w_down"][kind]]

    return (loss, grad_x, *leaf(0), *leaf(1), *leaf(2), *leaf(3))
```

```python
import math

import jax
import jax.numpy as jnp
from jax import lax
from jax.experimental import pallas as pl
from jax.experimental.pallas import tpu as pltpu

F32 = jnp.float32
CDT = jnp.bfloat16
MESH = pl.DeviceIdType.MESH

D_MODEL = 2048
SEQ = 4096
DEPTH = 4
CHUNK = 64
N_META = 16
MLA_HEADS = 16
Q_LORA = 512
KV_LORA = 512
QK_NOPE = 128
QK_ROPE = 64
V_HEAD = 128
ROPE_THETA = 10000.0
RNN_BLOCKS = 16
CONV_W = 4
LRU_C = 8.0
D_FF = 5632
RMS_EPS = 1e-6
NEG_BIG = -1e30
ADAM_LR = 0.001
ADAM_B1 = 0.9
ADAM_B2 = 0.999
ADAM_EPS = 1e-08
ADAM_WD = 0.01
ADAM_STEP = 10

N_CHIPS = 4
N_DEV = 8
LANES = 128
VMEM_LIMIT = 52 * 1024 * 1024
ROW_TILE = 384
MM_ROW_TILE = 704
ATT_TILE = 384
SCAN_COLS = 128
PAIR_CHUNKS = 8


def _div_tile(n, pref, mult):
    if n <= pref:
        return n
    d = (pref // mult) * mult
    while d >= mult:
        if n % d == 0:
            return d
        d -= mult
    raise ValueError(f"no tile for {n} <= {pref} (multiple of {mult})")


def _t_pad():
    t = N_META + SEQ
    step = math.lcm(_row_tile_unit(), 8)
    return -(-t // step) * step


def _row_tile_unit():
    return math.lcm(math.lcm(ROW_TILE, MM_ROW_TILE), ATT_TILE)


def _params(*sem):
    return pltpu.CompilerParams(dimension_semantics=sem, vmem_limit_bytes=VMEM_LIMIT)


def _b_spec(form, b, kind, layer, t_out, t_con):
    if kind == "plain":
        if form == "nn":
            return pl.BlockSpec((t_con, t_out), lambda i, j, k: (k, j))
        return pl.BlockSpec((t_out, t_con), lambda i, j, k: (j, k))
    rows, cols = b.shape[2], b.shape[3]
    if form == "nn":
        blk = (None, None, t_con, t_out)
        if kind == "row":
            per = rows // t_con
            return pl.BlockSpec(blk, lambda i, j, k: (k // per, layer, k % per, j))
        per = cols // t_out
        return pl.BlockSpec(blk, lambda i, j, k: (j // per, layer, k, j % per))
    blk = (None, None, t_out, t_con)
    if kind == "row":
        per = rows // t_out
        return pl.BlockSpec(blk, lambda i, j, k: (j // per, layer, j % per, k))
    per = cols // t_con
    return pl.BlockSpec(blk, lambda i, j, k: (k // per, layer, j, k % per))


def _mm_body(nk, dims, has_resid):
    def body(*refs):
        if has_resid:
            a_ref, b_ref, r_ref, o_ref = refs[:4]
        else:
            a_ref, b_ref, o_ref = refs[:3]
        prod = lax.dot_general(a_ref[...].astype(CDT), b_ref[...].astype(CDT), (dims, ((), ())),
                               preferred_element_type=F32)

        def finish(acc):
            if has_resid:
                acc = acc + r_ref[...]
            o_ref[...] = acc.astype(o_ref.dtype)

        if nk == 1:
            finish(prod)
            return
        acc_ref = refs[-1]
        k = pl.program_id(2)

        @pl.when(k == 0)
        def _():
            acc_ref[...] = prod

        @pl.when(k > 0)
        def _():
            acc_ref[...] += prod

        @pl.when(k == nk - 1)
        def _():
            finish(acc_ref[...])

    return body


def _mm(form, a, b, *, kind="plain", layer=0, out_dtype=F32, resid=None, tm, tn, tk, name):
    m, con = a.shape
    if kind == "plain":
        w_rows, w_cols = b.shape
    elif kind == "row":
        w_rows, w_cols = b.shape[0] * b.shape[2], b.shape[3]
    else:
        w_rows, w_cols = b.shape[2], b.shape[0] * b.shape[3]
    n_out = w_cols if form == "nn" else w_rows
    assert con == (w_rows if form == "nn" else w_cols), (name, a.shape, b.shape)
    nk = con // tk
    assert m % tm == 0 and n_out % tn == 0 and con % tk == 0, (name, m, n_out, con, tm, tn, tk)
    dims = ((1,), (0,)) if form == "nn" else ((1,), (1,))
    in_specs = [pl.BlockSpec((tm, tk), lambda i, j, k: (i, k)), _b_spec(form, b, kind, layer, tn, tk)]
    args = [a, b]
    if resid is not None:
        in_specs.append(pl.BlockSpec((tm, tn), lambda i, j, k: (i, j)))
        args.append(resid)
    return pl.pallas_call(
        _mm_body(nk, dims, resid is not None),
        grid=(m // tm, n_out // tn, nk),
        in_specs=in_specs,
        out_specs=pl.BlockSpec((tm, tn), lambda i, j, k: (i, j)),
        out_shape=jax.ShapeDtypeStruct((m, n_out), out_dtype),
        scratch_shapes=[pltpu.VMEM((tm, tn), F32)] if nk > 1 else [],
        compiler_params=_params("parallel", "parallel", "arbitrary"),
        name=name,
    )(*args)


def _mm_tn(a, b, *, into=None, kind="plain", layer=0, n_layers=1, tm, tn, tk, name):
    t, m = a.shape
    n = b.shape[1]
    nk = t // tk
    assert t % tk == 0 and m % tm == 0 and n % tn == 0, (name, t, m, n, tm, tn, tk)
    in_specs = [pl.BlockSpec((tk, tm), lambda i, j, k: (k, i)), pl.BlockSpec((tk, tn), lambda i, j, k: (k, j))]
    args = [a, b]
    aliases = {}
    if kind == "plain":
        out_shape = jax.ShapeDtypeStruct((m, n), F32)
        out_spec = pl.BlockSpec((tm, tn), lambda i, j, k: (i, j))
    else:
        blk = (None, None, tm, tn)
        if kind == "row":
            per = (m // N_CHIPS) // tm
            out_shape = jax.ShapeDtypeStruct((N_CHIPS, n_layers, m // N_CHIPS, n), F32)
            out_spec = pl.BlockSpec(blk, lambda i, j, k: (i // per, layer, i % per, j))
        else:
            per = (n // N_CHIPS) // tn
            out_shape = jax.ShapeDtypeStruct((N_CHIPS, n_layers, m, n // N_CHIPS), F32)
            out_spec = pl.BlockSpec(blk, lambda i, j, k: (j // per, layer, i, j % per))
        assert per >= 1, name
        if into is not None:
            in_specs.append(pl.BlockSpec(memory_space=pl.ANY))
            args.append(into)
            aliases = {2: 0}
    body = _mm_body(nk, ((0,), (0,)), False)

    def kernel_fn(*refs):
        if into is not None and kind != "plain":
            refs = refs[:2] + refs[3:]
        body(*refs)

    return pl.pallas_call(
        kernel_fn,
        grid=(m // tm, n // tn, nk),
        in_specs=in_specs,
        out_specs=out_spec,
        out_shape=out_shape,
        scratch_shapes=[pltpu.VMEM((tm, tn), F32)] if nk > 1 else [],
        input_output_aliases=aliases,
        compiler_params=_params("parallel", "parallel", "arbitrary"),
        name=name,
    )(*args)


def _rms_fwd(x, g, *, width, col_block, name):
    tp = x.shape[0]
    tr = _div_tile(tp, ROW_TILE, 8)

    def body(x_ref, g_ref, o_ref):
        xf = x_ref[...]
        r = lax.rsqrt(jnp.mean(xf * xf, axis=-1, keepdims=True) + RMS_EPS)
        o_ref[...] = ((xf * r) * g_ref[...]).astype(o_ref.dtype)

    return pl.pallas_call(
        body,
        grid=(tp // tr,),
        in_specs=[pl.BlockSpec((tr, width), lambda i: (i, col_block)), pl.BlockSpec((1, width), lambda i: (0, 0))],
        out_specs=pl.BlockSpec((tr, width), lambda i: (i, 0)),
        out_shape=jax.ShapeDtypeStruct((tp, width), CDT),
        compiler_params=_params("parallel"),
        name=name,
    )(x, g.reshape(1, width))


def _rms_bwd(x, g, dy, resid, *, width, col_block, name):
    tp = x.shape[0]
    tr = _div_tile(tp, ROW_TILE, 8)
    has_resid = resid is not None

    def body(*refs):
        if has_resid:
            x_ref, g_ref, dy_ref, res_ref, dx_ref, dxb_ref, dg_ref = refs
        else:
            x_ref, g_ref, dy_ref, dx_ref, dxb_ref, dg_ref = refs
        i = pl.program_id(0)
        xf = x_ref[...]
        r = lax.rsqrt(jnp.mean(xf * xf, axis=-1, keepdims=True) + RMS_EPS)
        xh = xf * r
        dy = dy_ref[...].astype(F32)
        dg = jnp.sum(dy * xh, axis=0, keepdims=True)
        dxh = dy * g_ref[...]
        dx = r * (dxh - xh * jnp.mean(dxh * xh, axis=-1, keepdims=True))
        if has_resid:
            dx = dx + res_ref[...]
        dx_ref[...] = dx
        dxb_ref[...] = dx.astype(CDT)

        @pl.when(i == 0)
        def _():
            dg_ref[...] = dg

        @pl.when(i > 0)
        def _():
            dg_ref[...] += dg

    row = pl.BlockSpec((tr, width), lambda i: (i, 0))
    in_specs = [pl.BlockSpec((tr, width), lambda i: (i, col_block)), pl.BlockSpec((1, width), lambda i: (0, 0)), row]
    args = [x, g.reshape(1, width), dy]
    if has_resid:
        in_specs.append(row)
        args.append(resid)
    return pl.pallas_call(
        body,
        grid=(tp // tr,),
        in_specs=in_specs,
        out_specs=[row, row, pl.BlockSpec((1, width), lambda i: (0, 0))],
        out_shape=[jax.ShapeDtypeStruct((tp, width), F32), jax.ShapeDtypeStruct((tp, width), CDT),
                   jax.ShapeDtypeStruct((1, width), F32)],
        compiler_params=_params("arbitrary"),
        name=name,
    )(*args)


def _final_loss(h, g, target):
    tp, d = h.shape
    tr = _div_tile(tp, ROW_TILE, 8)

    def body(h_ref, g_ref, t_ref, dh_ref, dhb_ref, dg_ref, loss_ref):
        i = pl.program_id(0)
        xf = h_ref[...]
        r = lax.rsqrt(jnp.mean(xf * xf, axis=-1, keepdims=True) + RMS_EPS)
        xh = xf * r
        gain = g_ref[...]
        y = xh * gain
        rows = i * tr + lax.broadcasted_iota(jnp.int32, (tr, 1), 0)
        valid = jnp.logical_and(rows >= N_META, rows < N_META + SEQ)
        err = jnp.where(valid, y - t_ref[...], 0.0)
        part = 0.5 * jnp.sum(jnp.mean(err * err, axis=-1, keepdims=True), axis=0, keepdims=True)
        dy = err * (1.0 / d)
        dg = jnp.sum(dy * xh, axis=0, keepdims=True)
        dxh = dy * gain
        dx = r * (dxh - xh * jnp.mean(dxh * xh, axis=-1, keepdims=True))
        dh_ref[...] = dx
        dhb_ref[...] = dx.astype(CDT)

        @pl.when(i == 0)
        def _():
            dg_ref[...] = dg
            loss_ref[...] = part

        @pl.when(i > 0)
        def _():
            dg_ref[...] += dg
            loss_ref[...] += part

    row = pl.BlockSpec((tr, d), lambda i: (i, 0))
    vec = pl.BlockSpec((1, d), lambda i: (0, 0))
    return pl.pallas_call(
        body,
        grid=(tp // tr,),
        in_specs=[row, vec, row],
        out_specs=[row, row, vec, pl.BlockSpec((1, 1), lambda i: (0, 0))],
        out_shape=[jax.ShapeDtypeStruct((tp, d), F32), jax.ShapeDtypeStruct((tp, d), CDT),
                   jax.ShapeDtypeStruct((1, d), F32), jax.ShapeDtypeStruct((1, 1), F32)],
        compiler_params=_params("arbitrary"),
        name="final_loss",
    )(h, g.reshape(1, d), target)


def _sigmoid(x):
    return 1.0 / (1.0 + jnp.exp(-x))


def _swiglu_fwd(gu):
    tp, f2 = gu.shape
    f = f2 // 2
    tr = _div_tile(tp, ROW_TILE, 8)
    tf = _div_tile(f, 1408, LANES)
    nf = f // tf

    def body(g_ref, u_ref, o_ref):
        g = g_ref[...]
        o_ref[...] = ((g * _sigmoid(g)) * u_ref[...]).astype(o_ref.dtype)

    return pl.pallas_call(
        body,
        grid=(tp // tr, nf),
        in_specs=[pl.BlockSpec((tr, tf), lambda i, j: (i, j)), pl.BlockSpec((tr, tf), lambda i, j: (i, j + nf))],
        out_specs=pl.BlockSpec((tr, tf), lambda i, j: (i, j)),
        out_shape=jax.ShapeDtypeStruct((tp, f), CDT),
        compiler_params=_params("parallel", "parallel"),
        name="swiglu_fwd",
    )(gu, gu)


def _swiglu_bwd(gu, da):
    tp, f2 = gu.shape
    f = f2 // 2
    tr = _div_tile(tp, ROW_TILE, 8)
    tf = _div_tile(f, 1408, LANES)
    nf = f // tf

    def body(g_ref, u_ref, da_ref, o_ref):
        j = pl.program_id(1)
        g = g_ref[...]
        da = da_ref[...]
        sg = _sigmoid(g)

        @pl.when(j < nf)
        def _():
            o_ref[...] = (da * u_ref[...] * (sg * (1.0 + g * (1.0 - sg)))).astype(o_ref.dtype)

        @pl.when(j >= nf)
        def _():
            o_ref[...] = (da * (g * sg)).astype(o_ref.dtype)

    return pl.pallas_call(
        body,
        grid=(tp // tr, 2 * nf),
        in_specs=[pl.BlockSpec((tr, tf), lambda i, j: (i, j % nf)),
                  pl.BlockSpec((tr, tf), lambda i, j: (i, nf + j % nf)),
                  pl.BlockSpec((tr, tf), lambda i, j: (i, j % nf))],
        out_specs=pl.BlockSpec((tr, tf), lambda i, j: (i, j)),
        out_shape=jax.ShapeDtypeStruct((tp, f2), CDT),
        compiler_params=_params("parallel", "parallel"),
        name="swiglu_bwd",
    )(gu, gu, da)


def _swap_halves(x):
    lane = lax.broadcasted_iota(jnp.int32, x.shape, x.ndim - 1)
    first = (lane % QK_ROPE) < (QK_ROPE // 2)
    return jnp.where(first, pltpu.roll(x, LANES - QK_ROPE // 2, x.ndim - 1), pltpu.roll(x, QK_ROPE // 2, x.ndim - 1))


def _rope_tables(tp):
    pos = jnp.arange(tp, dtype=F32)
    inv_freq = ROPE_THETA ** (-jnp.arange(0, QK_ROPE, 2, dtype=F32) / QK_ROPE)
    ang = pos[:, None] * inv_freq[None, :]
    cos, sin = jnp.cos(ang), jnp.sin(ang)
    reps = LANES // QK_ROPE
    return jnp.tile(jnp.concatenate([cos, cos], -1), (1, reps)), jnp.tile(jnp.concatenate([-sin, sin], -1), (1, reps))


def _chunk_of(pos):
    shift = CHUNK.bit_length() - 1
    assert CHUNK == 1 << shift
    return jnp.where(pos < N_META, 0, 1 + lax.shift_right_arithmetic(pos - N_META, shift))


def _head_half(x, h):
    lane = lax.broadcasted_iota(jnp.int32, x.shape, x.ndim - 1)
    return jnp.where((lane // QK_ROPE) == (h % 2), x, jnp.zeros_like(x))


def _mla_prep(q, kv, proj, cos2, sin2):
    tp = q.shape[0]
    tr = _div_tile(tp, ROW_TILE, 8)
    nope_w = MLA_HEADS * QK_NOPE
    kr_block = (Q_LORA + KV_LORA) // LANES
    depth = QK_NOPE + LANES

    def body(q_ref, kv_ref, kr_ref, c_ref, s_ref, qp_out, kp_out):
        c = c_ref[...]
        s = s_ref[...]
        k = kr_ref[...]
        k = k + pltpu.roll(k, QK_ROPE, 1)
        k = (k * c + _swap_halves(k) * s).astype(CDT)
        for p in range(MLA_HEADS // 2):
            x = q_ref[:, nope_w + p * LANES:nope_w + (p + 1) * LANES]
            pair = (x * c + _swap_halves(x) * s).astype(CDT)
            for h in (2 * p, 2 * p + 1):
                qp_out[h, :, :QK_NOPE] = q_ref[:, h * QK_NOPE:(h + 1) * QK_NOPE].astype(CDT)
                qp_out[h, :, QK_NOPE:] = _head_half(pair, h)
                kp_out[h, :, :QK_NOPE] = kv_ref[:, 2 * h * QK_NOPE:(2 * h + 1) * QK_NOPE]
                kp_out[h, :, QK_NOPE:] = k

    tab = pl.BlockSpec((tr, LANES), lambda i: (i, 0))
    per_head = pl.BlockSpec((MLA_HEADS, tr, depth), lambda i: (0, i, 0))
    out = jax.ShapeDtypeStruct((MLA_HEADS, tp, depth), CDT)
    return pl.pallas_call(
        body,
        grid=(tp // tr,),
        in_specs=[pl.BlockSpec((tr, q.shape[1]), lambda i: (i, 0)), pl.BlockSpec((tr, kv.shape[1]), lambda i: (i, 0)),
                  pl.BlockSpec((tr, LANES), lambda i: (i, kr_block)), tab, tab],
        out_specs=[per_head, per_head],
        out_shape=[out, out],
        compiler_params=_params("parallel"),
        name="mla_prep",
    )(q, kv, proj, cos2, sin2)


def _dot_nt(a, b):
    return lax.dot_general(a, b, (((1,), (1,)), ((), ())), preferred_element_type=F32)


def _dot_tn(a, b):
    return lax.dot_general(a, b, (((0,), (0,)), ((), ())), preferred_element_type=F32)


def _dot(a, b):
    return jnp.dot(a, b, preferred_element_type=F32)


def _chunk_scalar(p):
    return jnp.where(p < N_META, 0, 1 + jnp.maximum(p - N_META, 0) // CHUNK)


def _last_key_block(i, bq, bk, nk):
    cq = _chunk_scalar(i * bq + bq - 1)
    return jnp.minimum((N_META + CHUNK * cq - 1) // bk, nk - 1)


def _full_key_blocks(i, bq, bk):
    return (N_META + CHUNK * _chunk_scalar(i * bq)) // bk


def _first_query_block(j, bk, bq):
    p0 = N_META + CHUNK * (jnp.maximum(j * bk - N_META, 0) // CHUNK)
    return p0 // bq


def _first_full_query_block(j, bk, bq, nq):
    ck = _chunk_scalar(j * bk + bk - 1)
    p0 = jnp.where(ck == 0, 0, N_META + CHUNK * (ck - 1))
    return jnp.minimum((p0 + bq - 1) // bq, nq)


def _chunk_mask(q0, k0, shape, keys_on_rows):
    if keys_on_rows:
        kc = _chunk_of(k0 + lax.broadcasted_iota(jnp.int32, (shape[0], 1), 0))
        qc = _chunk_of(q0 + lax.broadcasted_iota(jnp.int32, (1, shape[1]), 1))
    else:
        qc = _chunk_of(q0 + lax.broadcasted_iota(jnp.int32, (shape[0], 1), 0))
        kc = _chunk_of(k0 + lax.broadcasted_iota(jnp.int32, (1, shape[1]), 1))
    return kc <= qc


def _attn_fwd(qp, kp, kv):
    tp = qp.shape[1]
    depth = qp.shape[2]
    bq = bk = _div_tile(tp, ATT_TILE, LANES)
    nq, nk = tp // bq, tp // bk
    scale = (QK_NOPE + QK_ROPE) ** -0.5

    def body(q_ref, k_ref, v_ref, o_ref, lse_ref):
        def q_block(i, _):
            q0 = pl.multiple_of(i * bq, bq)
            qb = q_ref[pl.ds(q0, bq), :]

            def k_step(masked, j, carry):
                m_old, l_old, acc = carry
                k0 = pl.multiple_of(j * bk, bk)
                s = _dot_nt(qb, k_ref[pl.ds(k0, bk), :]) * scale
                if masked:
                    s = jnp.where(_chunk_mask(q0, k0, s.shape, False), s, NEG_BIG)
                m_new = jnp.maximum(m_old, jnp.max(s, axis=-1, keepdims=True))
                alpha = jnp.exp(m_old - m_new)
                p = jnp.exp(s - m_new)
                l_new = alpha * l_old + jnp.sum(p, axis=-1, keepdims=True)
                acc = alpha * acc + _dot(p.astype(CDT), v_ref[pl.ds(k0, bk), :])
                return m_new, l_new, acc

            n_full = _full_key_blocks(i, bq, bk)
            carry = (jnp.full((bq, 1), NEG_BIG, F32), jnp.zeros((bq, 1), F32), jnp.zeros((bq, V_HEAD), F32))
            carry = lax.fori_loop(0, n_full, lambda j, c: k_step(False, j, c), carry)
            m_fin, l_fin, acc = lax.fori_loop(n_full, _last_key_block(i, bq, bk, nk) + 1,
                                              lambda j, c: k_step(True, j, c), carry)
            o_ref[pl.ds(q0, bq), :] = acc / l_fin
            lse_ref[pl.ds(q0, bq), :] = m_fin + jnp.log(l_fin)
            return 0

        lax.fori_loop(0, nq, q_block, 0)

    per_head = pl.BlockSpec((None, tp, depth), lambda h: (h, 0, 0))
    return pl.pallas_call(
        body,
        grid=(MLA_HEADS,),
        in_specs=[per_head, per_head, pl.BlockSpec((tp, V_HEAD), lambda h: (0, 2 * h + 1))],
        out_specs=[pl.BlockSpec((tp, V_HEAD), lambda h: (0, h)), pl.BlockSpec((None, tp, 1), lambda h: (h, 0, 0))],
        out_shape=[jax.ShapeDtypeStruct((tp, MLA_HEADS * V_HEAD), F32), jax.ShapeDtypeStruct((MLA_HEADS, tp, 1), F32)],
        compiler_params=_params("parallel"),
        name="attn_fwd",
    )(qp, kp, kv)


def _attn_delta(d_out, out):
    tp = out.shape[0]
    tr = _div_tile(tp, ROW_TILE, 8)

    def body(do_ref, o_ref, d_ref):
        for h in range(MLA_HEADS):
            cols = slice(h * V_HEAD, (h + 1) * V_HEAD)
            d_ref[h] = jnp.sum(do_ref[:, cols] * o_ref[:, cols], axis=-1, keepdims=True)

    row = pl.BlockSpec((tr, MLA_HEADS * V_HEAD), lambda i: (i, 0))
    return pl.pallas_call(
        body,
        grid=(tp // tr,),
        in_specs=[row, row],
        out_specs=pl.BlockSpec((MLA_HEADS, tr, 1), lambda i: (0, i, 0)),
        out_shape=jax.ShapeDtypeStruct((MLA_HEADS, tp, 1), F32),
        compiler_params=_params("parallel"),
        name="attn_delta",
    )(d_out, out)


def _attn_bwd(qp, kp, kv, d_out, lse, delta):
    tp = qp.shape[1]
    depth = qp.shape[2]
    bq = bk = _div_tile(tp, ATT_TILE, LANES)
    nq, nk = tp // bq, tp // bk
    scale = (QK_NOPE + QK_ROPE) ** -0.5
    lse_rows = lse.reshape(MLA_HEADS, nq, 1, bq)
    delta_rows = delta.reshape(MLA_HEADS, nq, 1, bq)

    def body(q_ref, k_ref, v_ref, do_ref, lse_ref, dl_ref, dqn_ref, dqr_ref, dkv_ref, dkr_ref, dq_acc):
        h = pl.program_id(0)
        dq_acc[...] = jnp.zeros(dq_acc.shape, F32)

        def k_block(j, _):
            k0 = pl.multiple_of(j * bk, bk)
            kb = k_ref[pl.ds(k0, bk), :]
            vb = v_ref[pl.ds(k0, bk), :]

            def q_step(masked, i, carry):
                dk, dv = carry
                q0 = pl.multiple_of(i * bq, bq)
                qb = q_ref[pl.ds(q0, bq), :]
                dob = do_ref[pl.ds(q0, bq), :].astype(CDT)
                s_t = _dot_nt(kb, qb) * scale
                if masked:
                    s_t = jnp.where(_chunk_mask(q0, k0, s_t.shape, True), s_t, NEG_BIG)
                p_t = jnp.exp(s_t - lse_ref[i])
                dv = dv + _dot(p_t.astype(CDT), dob)
                dp_t = _dot_nt(vb, dob)
                ds_t = (p_t * (dp_t - dl_ref[i]) * scale).astype(CDT)
                dk = dk + _dot(ds_t, qb)
                dq_acc[pl.ds(q0, bq), :] += _dot_tn(ds_t, kb)
                return dk, dv

            i_full = _first_full_query_block(j, bk, bq, nq)
            carry = (jnp.zeros((bk, depth), F32), jnp.zeros((bk, V_HEAD), F32))
            carry = lax.fori_loop(_first_query_block(j, bk, bq), i_full, lambda i, c: q_step(True, i, c), carry)
            dk, dv = lax.fori_loop(i_full, nq, lambda i, c: q_step(False, i, c), carry)
            dkv_ref[pl.ds(k0, bk), :QK_NOPE] = dk[:, :QK_NOPE].astype(CDT)
            dkv_ref[pl.ds(k0, bk), QK_NOPE:] = dv.astype(CDT)
            dkr_ref[pl.ds(k0, bk), :] = dk[:, QK_NOPE:]
            return 0

        lax.fori_loop(0, nk, k_block, 0)
        dqn_ref[...] = dq_acc[:, :QK_NOPE].astype(CDT)
        dqr_ref[...] = _head_half(dq_acc[:, QK_NOPE:], h)

    per_head = pl.BlockSpec((None, tp, depth), lambda h: (h, 0, 0))
    stat = pl.BlockSpec((None, nq, 1, bq), lambda h: (h, 0, 0, 0))
    lanes_out = pl.BlockSpec((None, tp, LANES), lambda h: (h, 0, 0))
    return pl.pallas_call(
        body,
        grid=(MLA_HEADS,),
        in_specs=[per_head, per_head, pl.BlockSpec((tp, V_HEAD), lambda h: (0, 2 * h + 1)),
                  pl.BlockSpec((tp, V_HEAD), lambda h: (0, h)), stat, stat],
        out_specs=[pl.BlockSpec((tp, QK_NOPE), lambda h: (0, h)), lanes_out,
                   pl.BlockSpec((tp, QK_NOPE + V_HEAD), lambda h: (0, h)), lanes_out],
        out_shape=[jax.ShapeDtypeStruct((tp, MLA_HEADS * QK_NOPE), CDT), jax.ShapeDtypeStruct((MLA_HEADS, tp, LANES), F32),
                   jax.ShapeDtypeStruct((tp, MLA_HEADS * (QK_NOPE + V_HEAD)), CDT),
                   jax.ShapeDtypeStruct((MLA_HEADS, tp, LANES), F32)],
        scratch_shapes=[pltpu.VMEM((tp, depth), F32)],
        compiler_params=_params("parallel"),
        name="attn_bwd",
    )(qp, kp, kv, d_out, lse_rows, delta_rows)


def _mla_unprep(dqr_h, dkr_h, cos2, sin2):
    tp = dqr_h.shape[1]
    tr = _div_tile(tp, ROW_TILE, 8)
    wr = MLA_HEADS * QK_ROPE

    def body(dq_ref, dk_ref, c_ref, s_ref, dqr_out, dkr_out):
        c = c_ref[...]
        s = s_ref[...]
        for p in range(MLA_HEADS // 2):
            x = dq_ref[2 * p] + dq_ref[2 * p + 1]
            dqr_out[:, p * LANES:(p + 1) * LANES] = (x * c - _swap_halves(x) * s).astype(CDT)
        t = dk_ref[0]
        for h in range(1, MLA_HEADS):
            t = t + dk_ref[h]
        t = t * c - _swap_halves(t) * s
        t = t + pltpu.roll(t, QK_ROPE, 1)
        lane = lax.broadcasted_iota(jnp.int32, t.shape, 1)
        dkr_out[...] = jnp.where(lane < QK_ROPE, t, 0.0)

    per_head = pl.BlockSpec((MLA_HEADS, tr, LANES), lambda i: (0, i, 0))
    tab = pl.BlockSpec((tr, LANES), lambda i: (i, 0))
    return pl.pallas_call(
        body,
        grid=(tp // tr,),
        in_specs=[per_head, per_head, tab, tab],
        out_specs=[pl.BlockSpec((tr, wr), lambda i: (i, 0)), tab],
        out_shape=[jax.ShapeDtypeStruct((tp, wr), CDT), jax.ShapeDtypeStruct((tp, LANES), F32)],
        compiler_params=_params("parallel"),
        name="mla_unprep",
    )(dqr_h, dkr_h, cos2, sin2)


HALO = 8


def _softplus(x):
    return jnp.maximum(x, 0.0) + jnp.log1p(jnp.exp(-jnp.abs(x)))


def _one_minus_sq(log_a, a):
    return -jnp.tanh(log_a) * (a * a + 1.0)


def _gelu(y):
    k = math.sqrt(2.0 / math.pi)
    return 0.5 * y * (1.0 + jnp.tanh(k * (y + 0.044715 * (y * y * y))))


def _gelu_grad(y):
    k = math.sqrt(2.0 / math.pi)
    th = jnp.tanh(k * (y + 0.044715 * (y * y * y)))
    return 0.5 * (1.0 + th) + 0.5 * y * (1.0 - th * th) * (k * (1.0 + 3.0 * 0.044715 * (y * y)))


def _lru_gates_fwd(xy, conv_w, conv_b, w_ga, b_ga, w_gx, b_gx, lam):
    tp = xy.shape[0]
    dr = xy.shape[1] // 2
    bw = dr // RNN_BLOCKS
    tr = _div_tile(tp, ROW_TILE, 8)

    def body(x_ref, halo_ref, cw_ref, cb_ref, wa_ref, ba_ref, wx_ref, bx_ref, lam_ref,
             xc_ref, r_ref, i_ref, a_ref, b_ref, xs):
        i = pl.program_id(0)
        xs[0:HALO, :] = jnp.where(i == 0, 0.0, halo_ref[...])
        xs[HALO:, :] = x_ref[...]
        xc = cb_ref[...] + cw_ref[0:1, :] * xs[pl.ds(HALO - CONV_W + 1, tr), :]
        for j in range(1, CONV_W):
            xc = xc + cw_ref[j:j + 1, :] * xs[pl.ds(HALO - CONV_W + 1 + j, tr), :]
        xcb = xc.astype(CDT)
        r = _sigmoid(_dot(xcb, wa_ref[...]) + ba_ref[...])
        ig = _sigmoid(_dot(xcb, wx_ref[...]) + bx_ref[...])
        log_a = (-LRU_C * r) * _softplus(-lam_ref[...])
        a = jnp.exp(log_a)
        xc_ref[...] = xc
        r_ref[...] = r
        i_ref[...] = ig
        a_ref[...] = a
        b_ref[...] = jnp.sqrt(_one_minus_sq(log_a, a)) * (ig * xc)

    blk = pl.BlockSpec((tr, bw), lambda i, n: (i, n))
    vec = pl.BlockSpec((1, bw), lambda i, n: (0, n))
    mat = pl.BlockSpec((None, bw, bw), lambda i, n: (n, 0, 0))
    bias = pl.BlockSpec((None, 1, bw), lambda i, n: (n, 0, 0))
    out = jax.ShapeDtypeStruct((tp, dr), F32)
    return pl.pallas_call(
        body,
        grid=(tp // tr, RNN_BLOCKS),
        in_specs=[blk, pl.BlockSpec((HALO, bw), lambda i, n: (jnp.maximum(i * (tr // HALO) - 1, 0), n)),
                  pl.BlockSpec((CONV_W, bw), lambda i, n: (0, n)), vec, mat, bias, mat, bias, vec],
        out_specs=[blk] * 5,
        out_shape=[out] * 5,
        scratch_shapes=[pltpu.VMEM((tr + HALO, bw), F32)],
        compiler_params=_params("parallel", "parallel"),
        name="lru_gates_fwd",
    )(xy, xy, conv_w, conv_b.reshape(1, dr), w_ga.astype(CDT), b_ga.reshape(RNN_BLOCKS, 1, bw),
      w_gx.astype(CDT), b_gx.reshape(RNN_BLOCKS, 1, bw), lam.reshape(1, dr))


def _stack_rows(rows):
    idx = lax.broadcasted_iota(jnp.int32, (len(rows), rows[0].shape[1]), 0)
    out = jnp.broadcast_to(rows[0], idx.shape)
    for j in range(1, len(rows)):
        out = jnp.where(idx == j, jnp.broadcast_to(rows[j], idx.shape), out)
    return out


def _lru_scan_fwd(a, b, xy):
    tp, dr = a.shape
    cw = SCAN_COLS
    ycol0 = dr // cw

    def body(a_ref, b_ref, y_ref, hs_ref, m_ref):
        def group(g, h):
            base = pl.multiple_of(g * 8, 8)
            at = a_ref[pl.ds(base, 8), :]
            bt = b_ref[pl.ds(base, 8), :]
            rows = []
            for j in range(8):
                h = at[j:j + 1, :] * h + bt[j:j + 1, :]
                rows.append(h)
            hs_ref[pl.ds(base, 8), :] = _stack_rows(rows)
            return h

        lax.fori_loop(0, tp // 8, group, jnp.zeros((1, cw), F32))
        m_ref[...] = (hs_ref[...] * _gelu(y_ref[...])).astype(CDT)

    col = pl.BlockSpec((tp, cw), lambda n: (0, n))
    return pl.pallas_call(
        body,
        grid=(dr // cw,),
        in_specs=[col, col, pl.BlockSpec((tp, cw), lambda n: (0, ycol0 + n))],
        out_specs=[col, col],
        out_shape=[jax.ShapeDtypeStruct((tp, dr), F32), jax.ShapeDtypeStruct((tp, dr), CDT)],
        compiler_params=_params("parallel"),
        name="lru_scan_fwd",
    )(a, b, xy)


def _lru_scan_bwd(a, hs, dm, xy):
    tp, dr = a.shape
    cw = SCAN_COLS
    ycol0 = dr // cw
    ng = tp // 8

    def body(a_ref, hs_ref, dm_ref, y_ref, db_ref, da_ref, dy_ref):
        y = y_ref[...]
        dm = dm_ref[...]
        db_ref[...] = dm * _gelu(y)
        dy_ref[...] = (dm * hs_ref[...] * _gelu_grad(y)).astype(CDT)

        def group(k, carry):
            g_next, a_next = carry
            g = ng - 1 - k
            base = pl.multiple_of(g * 8, 8)
            prev = pl.multiple_of(jnp.maximum(g - 1, 0) * 8, 8)
            dt = db_ref[pl.ds(base, 8), :]
            at = a_ref[pl.ds(base, 8), :]
            ht = hs_ref[pl.ds(base, 8), :]
            h_before = jnp.where(g == 0, 0.0, hs_ref[pl.ds(prev, 8), :][7:8, :])
            g_rows = [None] * 8
            da_rows = [None] * 8
            for j in range(7, -1, -1):
                g_cur = dt[j:j + 1, :] + a_next * g_next
                g_rows[j] = g_cur
                da_rows[j] = g_cur * (ht[j - 1:j, :] if j > 0 else h_before)
                g_next = g_cur
                a_next = at[j:j + 1, :]
            db_ref[pl.ds(base, 8), :] = _stack_rows(g_rows)
            da_ref[pl.ds(base, 8), :] = _stack_rows(da_rows)
            return g_next, a_next

        zero = jnp.zeros((1, cw), F32)
        lax.fori_loop(0, ng, group, (zero, zero))

    col = pl.BlockSpec((tp, cw), lambda n: (0, n))
    return pl.pallas_call(
        body,
        grid=(dr // cw,),
        in_specs=[col, col, col, pl.BlockSpec((tp, cw), lambda n: (0, ycol0 + n))],
        out_specs=[col, col, col],
        out_shape=[jax.ShapeDtypeStruct((tp, dr), F32), jax.ShapeDtypeStruct((tp, dr), F32),
                   jax.ShapeDtypeStruct((tp, dr), CDT)],
        compiler_params=_params("parallel"),
        name="lru_scan_bwd",
    )(a, hs, dm, xy)


def _lru_gates_bwd(db, da, xc, r, ig, a, lam, w_ga, w_gx):
    tp, dr = xc.shape
    bw = dr // RNN_BLOCKS
    tr = _div_tile(tp, ROW_TILE, 8)
    nr = tp // tr

    def body(db_ref, da_ref, xc_ref, r_ref, i_ref, a_ref, lam_ref, wa_ref, wx_ref,
             dxc_ref, dwa_ref, dba_ref, dwx_ref, dbx_ref, dlam_ref):
        i = pl.program_id(1)
        xc = xc_ref[...]
        r = r_ref[...]
        ig = i_ref[...]
        a = a_ref[...]
        dbv = db_ref[...]
        sp = _softplus(-lam_ref[...])
        log_a = (-LRU_C * r) * sp
        s = jnp.sqrt(_one_minus_sq(log_a, a))
        d_ix = dbv * s
        d_s = dbv * (ig * xc)
        d_log_a = da_ref[...] * a - d_s * (a * a) / s
        d_r = d_log_a * (-LRU_C * sp)
        d_sp = jnp.sum(d_log_a * (-LRU_C * r), axis=0, keepdims=True)
        dzr = d_r * r * (1.0 - r)
        dzi = (d_ix * xc) * ig * (1.0 - ig)
        dzr_b = dzr.astype(CDT)
        dzi_b = dzi.astype(CDT)
        xcb = xc.astype(CDT)
        dxc_ref[...] = d_ix * ig + _dot_nt(dzr_b, wa_ref[...]) + _dot_nt(dzi_b, wx_ref[...])
        dwa = _dot_tn(xcb, dzr_b)
        dwx = _dot_tn(xcb, dzi_b)
        dba = jnp.sum(dzr, axis=0, keepdims=True)
        dbx = jnp.sum(dzi, axis=0, keepdims=True)

        @pl.when(i == 0)
        def _():
            dwa_ref[...] = dwa
            dwx_ref[...] = dwx
            dba_ref[...] = dba
            dbx_ref[...] = dbx
            dlam_ref[...] = d_sp

        @pl.when(i > 0)
        def _():
            dwa_ref[...] += dwa
            dwx_ref[...] += dwx
            dba_ref[...] += dba
            dbx_ref[...] += dbx
            dlam_ref[...] += d_sp

        @pl.when(i == nr - 1)
        def _():
            dlam_ref[...] = dlam_ref[...] * (-_sigmoid(-lam_ref[...]))

    blk = pl.BlockSpec((tr, bw), lambda n, i: (i, n))
    vec = pl.BlockSpec((1, bw), lambda n, i: (0, n))
    mat = pl.BlockSpec((None, bw, bw), lambda n, i: (n, 0, 0))
    bias = pl.BlockSpec((None, 1, bw), lambda n, i: (n, 0, 0))
    return pl.pallas_call(
        body,
        grid=(RNN_BLOCKS, nr),
        in_specs=[blk] * 6 + [vec, mat, mat],
        out_specs=[blk, mat, bias, mat, bias, vec],
        out_shape=[jax.ShapeDtypeStruct((tp, dr), F32),
                   jax.ShapeDtypeStruct((RNN_BLOCKS, bw, bw), F32), jax.ShapeDtypeStruct((RNN_BLOCKS, 1, bw), F32),
                   jax.ShapeDtypeStruct((RNN_BLOCKS, bw, bw), F32), jax.ShapeDtypeStruct((RNN_BLOCKS, 1, bw), F32),
                   jax.ShapeDtypeStruct((1, dr), F32)],
        compiler_params=_params("parallel", "arbitrary"),
        name="lru_gates_bwd",
    )(db, da, xc, r, ig, a, lam.reshape(1, dr), w_ga.astype(CDT), w_gx.astype(CDT))


def _lru_conv_bwd(dxc, xy, conv_w):
    tp, dr = dxc.shape
    bw = dr // RNN_BLOCKS
    tr = _div_tile(tp, ROW_TILE, 8)
    nr = tp // tr
    per = tr // HALO

    def body(d_ref, dnext_ref, x_ref, xprev_ref, cw_ref, dxb_ref, dcw_ref, dcb_ref, ds, xs):
        i = pl.program_id(1)
        d = d_ref[...]
        ds[0:tr, :] = d
        ds[tr:, :] = jnp.where(i == nr - 1, 0.0, dnext_ref[...])
        xs[0:HALO, :] = jnp.where(i == 0, 0.0, xprev_ref[...])
        xs[HALO:, :] = x_ref[...]
        dxb = cw_ref[0:1, :] * ds[pl.ds(CONV_W - 1, tr), :]
        for j in range(1, CONV_W):
            dxb = dxb + cw_ref[j:j + 1, :] * ds[pl.ds(CONV_W - 1 - j, tr), :]
        dxb_ref[...] = dxb.astype(CDT)
        dcb = jnp.sum(d, axis=0, keepdims=True)
        dcw = [jnp.sum(d * xs[pl.ds(HALO - CONV_W + 1 + j, tr), :], axis=0, keepdims=True) for j in range(CONV_W)]

        @pl.when(i == 0)
        def _():
            dcb_ref[...] = dcb
            for j in range(CONV_W):
                dcw_ref[j] = dcw[j]

        @pl.when(i > 0)
        def _():
            dcb_ref[...] += dcb
            for j in range(CONV_W):
                dcw_ref[j] += dcw[j]

    blk = pl.BlockSpec((tr, bw), lambda n, i: (i, n))
    return pl.pallas_call(
        body,
        grid=(RNN_BLOCKS, nr),
        in_specs=[blk, pl.BlockSpec((HALO, bw), lambda n, i: (jnp.minimum((i + 1) * per, tp // HALO - 1), n)),
                  blk, pl.BlockSpec((HALO, bw), lambda n, i: (jnp.maximum(i * per - 1, 0), n)),
                  pl.BlockSpec((CONV_W, bw), lambda n, i: (0, n))],
        out_specs=[blk, pl.BlockSpec((CONV_W, 1, bw), lambda n, i: (0, 0, n)), pl.BlockSpec((1, bw), lambda n, i: (0, n))],
        out_shape=[jax.ShapeDtypeStruct((tp, dr), CDT), jax.ShapeDtypeStruct((CONV_W, 1, dr), F32),
                   jax.ShapeDtypeStruct((1, dr), F32)],
        scratch_shapes=[pltpu.VMEM((tr + HALO, bw), F32), pltpu.VMEM((tr + HALO, bw), F32)],
        compiler_params=_params("parallel", "arbitrary"),
        name="lru_conv_bwd",
    )(dxc, dxc, xy, xy, conv_w)


def _me():
    return lax.axis_index("x"), lax.axis_index("y"), lax.axis_index("c")


def _peer(rel):
    x, y, c = _me()
    return (1 - x if rel & 4 else x, 1 - y if rel & 2 else y, 1 - c if rel & 1 else c)


def _chip_of(dev):
    return 2 * dev[0] + dev[1]


def _linear(dev):
    return 4 * dev[0] + 2 * dev[1] + dev[2]


CHIP_RELS = (4, 2, 6)
ALL_RELS = (1, 2, 3, 4, 5, 6, 7)
PAIR_RELS = (1,)


def _scatter_send(pieces, rels, piece_of, name):
    n = len(rels)

    def body(src_ref, recv_ref, send_sems, recv_sems):
        copies = []
        for k, rel in enumerate(rels):
            peer = _peer(rel)
            cp = pltpu.make_async_remote_copy(
                src_ref=src_ref.at[piece_of(peer)], dst_ref=recv_ref.at[k],
                send_sem=send_sems.at[k], recv_sem=recv_sems.at[k], device_id=peer, device_id_type=MESH)
            cp.start()
            copies.append(cp)
        for cp in copies:
            cp.wait()

    return pl.pallas_call(
        body,
        in_specs=[pl.BlockSpec(memory_space=pl.ANY)],
        out_specs=pl.BlockSpec(memory_space=pl.ANY),
        out_shape=jax.ShapeDtypeStruct((n,) + pieces.shape[1:], pieces.dtype),
        scratch_shapes=[pltpu.SemaphoreType.DMA((n,)), pltpu.SemaphoreType.DMA((n,))],
        name=name,
    )(pieces)


def _gather_send(piece, rels, n_slots, slot_of, name, n_chunks=1):
    n = len(rels)
    rows = piece.shape[0]
    if rows % (8 * n_chunks):
        n_chunks = 1
    rc = rows // n_chunks

    def body(src_ref, out_ref, send_sems, recv_sems, local_sems):
        me = _me()

        def part(ref, q):
            return ref.at[pl.ds(q * rc, rc)]

        def remote(k, q, slot_dev, to):
            return pltpu.make_async_remote_copy(
                src_ref=part(src_ref, q), dst_ref=part(out_ref.at[slot_of(slot_dev)], q),
                send_sem=send_sems.at[k * n_chunks + q], recv_sem=recv_sems.at[k * n_chunks + q],
                device_id=to, device_id_type=MESH)

        mine = [pltpu.make_async_copy(part(src_ref, q), part(out_ref.at[slot_of(me)], q), local_sems.at[q])
                for q in range(n_chunks)]
        for cp in mine:
            cp.start()
        sends = [remote(k, q, me, _peer(rel)) for k, rel in enumerate(rels) for q in range(n_chunks)]
        for cp in sends:
            cp.start()
        for k, rel in enumerate(rels):
            for q in range(n_chunks):
                remote(k, q, _peer(rel), _peer(rel)).wait_recv()
        for cp in sends:
            cp.wait_send()
        for cp in mine:
            cp.wait()

    return pl.pallas_call(
        body,
        in_specs=[pl.BlockSpec(memory_space=pl.ANY)],
        out_specs=pl.BlockSpec(memory_space=pl.ANY),
        out_shape=jax.ShapeDtypeStruct((n_slots,) + piece.shape, piece.dtype),
        scratch_shapes=[pltpu.SemaphoreType.DMA((n * n_chunks,)), pltpu.SemaphoreType.DMA((n * n_chunks,)),
                        pltpu.SemaphoreType.DMA((n_chunks,))],
        name=name,
    )(piece)


def _gather_chips(shard, name):
    return _gather_send(shard, CHIP_RELS, N_CHIPS, _chip_of, name)


HBM_SPEC = pl.BlockSpec(memory_space=pltpu.HBM)
SEM_SPEC = pl.BlockSpec(memory_space=pltpu.SEMAPHORE)
DATAFLOW = pltpu.SideEffectType.DATAFLOW_SIDE_EFFECTING


def _gather_chips_start(shard, name):
    land = jnp.broadcast_to(shard[None], (N_CHIPS,) + shard.shape)

    def body(src_ref, land_ref, send_sem, recv_sem, src_thru, land_thru, token):
        me = _me()
        for rel in CHIP_RELS:
            pltpu.make_async_remote_copy(
                src_ref=src_ref, dst_ref=land_ref.at[_chip_of(me)], send_sem=send_sem, recv_sem=recv_sem,
                device_id=_peer(rel), device_id_type=MESH).start()
        token[...] = jnp.zeros(token.shape, token.dtype)

    return pl.pallas_call(
        body,
        name=name,
        out_shape=(pltpu.SemaphoreType.DMA(()), pltpu.SemaphoreType.DMA(()), pltpu.HBM(shard.shape, shard.dtype),
                   pltpu.HBM(land.shape, land.dtype), jax.ShapeDtypeStruct((8, LANES), F32)),
        in_specs=(HBM_SPEC, HBM_SPEC),
        out_specs=(SEM_SPEC, SEM_SPEC, HBM_SPEC, HBM_SPEC, pl.BlockSpec(memory_space=pltpu.VMEM)),
        input_output_aliases={0: 2, 1: 3},
        compiler_params=pltpu.CompilerParams(has_side_effects=DATAFLOW),
    )(pltpu.with_memory_space_constraint(shard, pltpu.HBM), pltpu.with_memory_space_constraint(land, pltpu.HBM))


def _gather_chips_wait(started, after, name):
    send_sem, recv_sem, src_thru, land_thru, _ = started
    n = len(CHIP_RELS)

    def body(src_ref, land_ref, send_sem, recv_sem, after_ref, src_dead, got_ref):
        three = land_ref.at[pl.ds(0, n)]
        arrivals = pltpu.make_async_remote_copy(
            src_ref=three, dst_ref=three, send_sem=send_sem, recv_sem=recv_sem, device_id=_me(), device_id_type=MESH)
        arrivals.wait_send()
        arrivals.wait_recv()

    return pl.pallas_call(
        body,
        name=name,
        out_shape=(pltpu.HBM(src_thru.shape, src_thru.dtype), pltpu.HBM(land_thru.shape, land_thru.dtype)),
        in_specs=(HBM_SPEC, HBM_SPEC, SEM_SPEC, SEM_SPEC, pl.BlockSpec(memory_space=pl.ANY)),
        out_specs=(HBM_SPEC, HBM_SPEC),
        input_output_aliases={0: 0, 1: 1},
        compiler_params=pltpu.CompilerParams(has_side_effects=DATAFLOW),
    )(src_thru, land_thru, send_sem, recv_sem, after)[1]


def _sum_pieces(own, recv, name):
    rr, cc = own.shape
    n = recv.shape[0]
    tr = _div_tile(rr, max(8, (1 << 17) // cc // 8 * 8), 8)

    def body(own_ref, recv_ref, o_ref):
        acc = own_ref[...]
        for k in range(n):
            acc = acc + recv_ref[k].astype(F32)
        o_ref[...] = acc

    return pl.pallas_call(
        body,
        grid=(rr // tr,),
        in_specs=[pl.BlockSpec((tr, cc), lambda i: (i, 0)), pl.BlockSpec((n, tr, cc), lambda i: (0, i, 0))],
        out_specs=pl.BlockSpec((tr, cc), lambda i: (i, 0)),
        out_shape=jax.ShapeDtypeStruct((rr, cc), F32),
        compiler_params=_params("parallel"),
        name=name,
    )(own, recv)


def _reduce_to_owner(g8, payload_dtype, name):
    x, y, c = _me()
    own = lax.dynamic_index_in_dim(g8, _linear((x, y, c)), 0, keepdims=False)
    recv = _scatter_send(g8.astype(payload_dtype), ALL_RELS, _linear, name + "_scatter")
    return _sum_pieces(own, recv, name + "_sum")


def _reduce_sharded(g, name):
    n_chips, nl, rr, cc = g.shape
    half = (nl // 2) * rr
    red = _reduce_to_owner(g.reshape(N_DEV, half, cc), CDT, name)
    both = _gather_send(red, PAIR_RELS, 2, lambda dev: dev[2], name + "_pair", n_chunks=PAIR_CHUNKS)
    return both.reshape(nl, rr, cc)


def _adamw(w, g, m, v, name):
    rr, cc = w.shape
    tr = _div_tile(rr, max(8, (1 << 17) // cc // 8 * 8), 8)
    c1 = 1.0 - ADAM_B1 ** ADAM_STEP
    c2 = 1.0 - ADAM_B2 ** ADAM_STEP

    def body(w_ref, g_ref, m_ref, v_ref, d_ref, mo_ref, vo_ref):
        g_ = g_ref[...]
        m_ = ADAM_B1 * m_ref[...] + (1.0 - ADAM_B1) * g_
        v_ = ADAM_B2 * v_ref[...] + (1.0 - ADAM_B2) * (g_ * g_)
        d_ref[...] = -ADAM_LR * ((m_ / c1) / (jnp.sqrt(v_ / c2) + ADAM_EPS) + ADAM_WD * w_ref[...])
        mo_ref[...] = m_
        vo_ref[...] = v_

    blk = pl.BlockSpec((tr, cc), lambda i: (i, 0))
    out = jax.ShapeDtypeStruct((rr, cc), F32)
    return pl.pallas_call(
        body,
        grid=(rr // tr,),
        in_specs=[blk] * 4,
        out_specs=[blk] * 3,
        out_shape=[out] * 3,
        compiler_params=_params("parallel"),
        name=name,
    )(w, g, m, v)


def _pack(arrays, cols, row_mult):
    flat = jnp.concatenate([a.reshape(-1) for a in arrays])
    rows = -(-flat.shape[0] // cols)
    rows = -(-rows // row_mult) * row_mult
    return jnp.pad(flat, (0, rows * cols - flat.shape[0])).reshape(rows, cols)


def _unpack(buf, shapes):
    flat = buf.reshape(-1)
    out, off = [], 0
    for s in shapes:
        n = math.prod(s)
        out.append(flat[off:off + n].reshape(s))
        off += n
    return out


def kernel(x, meta_tokens, norm_mix, norm_ffn, norm_final, mla_w_in, mla_q_norm, mla_kv_norm, mla_w_uq, mla_w_ukv, mla_w_o, lru_w_in, lru_conv_w, lru_conv_b, lru_w_gate_a, lru_b_gate_a, lru_w_gate_x, lru_b_gate_x, lru_lambda, lru_w_o, ffn_w_gu, ffn_w_down, loss_target, m_meta_tokens, m_norm_mix, m_norm_ffn, m_norm_final, m_mla_w_in, m_mla_q_norm, m_mla_kv_norm, m_mla_w_uq, m_mla_w_ukv, m_mla_w_o, m_lru_w_in, m_lru_conv_w, m_lru_conv_b, m_lru_w_gate_a, m_lru_b_gate_a, m_lru_w_gate_x, m_lru_b_gate_x, m_lru_lambda, m_lru_w_o, m_ffn_w_gu, m_ffn_w_down, v_meta_tokens, v_norm_mix, v_norm_ffn, v_norm_final, v_mla_w_in, v_mla_q_norm, v_mla_kv_norm, v_mla_w_uq, v_mla_w_ukv, v_mla_w_o, v_lru_w_in, v_lru_conv_w, v_lru_conv_b, v_lru_w_gate_a, v_lru_b_gate_a, v_lru_w_gate_x, v_lru_b_gate_x, v_lru_lambda, v_lru_w_o, v_ffn_w_gu, v_ffn_w_down):
    d = D_MODEL
    t_real = N_META + SEQ
    tp = _t_pad()
    n_mla = mla_w_in.shape[0]
    n_lru = lru_w_in.shape[0]
    h_dim = MLA_HEADS * V_HEAD
    w_in_cols = Q_LORA + KV_LORA + QK_ROPE
    w_in_pad = Q_LORA + KV_LORA + LANES
    q_cols = MLA_HEADS * (QK_NOPE + QK_ROPE)
    tmm = _div_tile(tp, MM_ROW_TILE, 16)
    tkt = _div_tile(tp, 1408, 16)

    def tile(n, pref):
        return _div_tile(n, pref, LANES)

    small_shapes = [meta_tokens.shape, lru_conv_w.shape, lru_conv_b.shape, lru_lambda.shape]
    csh = meta_tokens.shape[1]
    small4 = _gather_chips(_pack([meta_tokens, lru_conv_w, lru_conv_b, lru_lambda], csh, 16), "gather_small")
    small4, mla_w_in = lax.optimization_barrier((small4, mla_w_in))
    started = {}

    def start(key, shard):
        prev = list(started.values())[-1][4][0, 0] if started else 0.0
        started[key] = _gather_chips_start((shard + prev).astype(CDT), "gather_" + key + "_start")

    def arrived(key, after):
        return _gather_chips_wait(started[key], after, "gather_" + key + "_wait")

    def start_ffn(layer):
        start(f"w_gu{layer}", ffn_w_gu[layer:layer + 1])
        start(f"w_down{layer}", ffn_w_down[layer:layer + 1])

    start("w_in", jnp.pad(mla_w_in, ((0, 0), (0, 0), (0, w_in_pad - w_in_cols))))
    start("w_uq", mla_w_uq)
    start("w_ukv", mla_w_ukv)
    start("w_o", mla_w_o)
    start_ffn(0)
    start("lw_in", lru_w_in)
    start("lw_o", lru_w_o)
    for layer in range(1, DEPTH):
        start_ffn(layer)
    all_started = list(started.values())[-1][4][0, 0]
    n_gu = ffn_w_gu.shape[2]
    w_gu4, w_down4 = [None] * DEPTH, [None] * DEPTH
    small_full = [jnp.concatenate(parts, axis=-1) for parts in zip(*[_unpack(small4[k], small_shapes) for k in range(N_CHIPS)])]
    meta_full, conv_w_full, conv_b_full, lam_full = small_full

    cos2, sin2 = _rope_tables(tp)

    h = jnp.concatenate([meta_full, x[0], jnp.zeros((tp - t_real, d), F32)], axis=0) + all_started
    saved = []
    for layer in range(DEPTH):
        j = layer // 2
        s = {"h_in": h}
        hn = _rms_fwd(h, norm_mix[layer], width=d, col_block=0, name="norm_mix_fwd")
        s["hn"] = hn
        if layer == 0:
            w_in4, w_uq4, w_ukv4, w_o4 = (arrived(k, hn) for k in ("w_in", "w_uq", "w_ukv", "w_o"))
            w_uq_full = jnp.moveaxis(w_uq4, 0, 2).reshape(n_mla, Q_LORA, MLA_HEADS, QK_NOPE + QK_ROPE)
            w_uq_perm = jnp.concatenate([w_uq_full[..., :QK_NOPE].reshape(n_mla, Q_LORA, -1),
                                         w_uq_full[..., QK_NOPE:].reshape(n_mla, Q_LORA, -1)], axis=-1)
        if layer == 1:
            lw_in4, lw_o4 = arrived("lw_in", hn), arrived("lw_o", hn)
        if layer % 2 == 0:
            proj = _mm("nn", hn, w_in4, kind="row", layer=j, tm=tmm, tn=tile(w_in_pad, 1152), tk=tile(d // N_CHIPS, 512), name="mla_in")
            c_q = _rms_fwd(proj, mla_q_norm[j], width=Q_LORA, col_block=0, name="q_norm_fwd")
            c_kv = _rms_fwd(proj, mla_kv_norm[j], width=KV_LORA, col_block=Q_LORA // KV_LORA, name="kv_norm_fwd")
            q = _mm("nn", c_q, w_uq_perm[j], tm=tmm, tn=tile(q_cols, 1024), tk=Q_LORA, name="mla_uq")
            kv = _mm("nn", c_kv, w_ukv4, kind="col", layer=j, out_dtype=CDT, tm=tmm, tn=tile(w_ukv4.shape[3], 1024), tk=KV_LORA, name="mla_ukv")
            qp, kp = _mla_prep(q, kv, proj, cos2, sin2)
            att, lse = _attn_fwd(qp, kp, kv)
            h = _mm("nn", att, w_o4, kind="row", layer=j, resid=h, tm=tmm, tn=tile(d, 1024), tk=tile(h_dim // N_CHIPS, 512), name="mla_out")
            s.update(proj=proj, c_q=c_q, c_kv=c_kv, qp=qp, kp=kp, kv=kv, att=att, lse=lse)
        else:
            xy = _mm("nn", hn, lw_in4, kind="col", layer=j, tm=tmm, tn=tile(lw_in4.shape[3], 1024), tk=d, name="lru_in")
            xc, r, ig, a, b = _lru_gates_fwd(xy, conv_w_full[j], conv_b_full[j], lru_w_gate_a[j], lru_b_gate_a[j],
                                             lru_w_gate_x[j], lru_b_gate_x[j], lam_full[j])
            hs, mixed = _lru_scan_fwd(a, b, xy)
            h = _mm("nn", mixed, lw_o4, kind="row", layer=j, resid=h, tm=tmm, tn=tile(d, 1024), tk=tile(d // N_CHIPS, 512), name="lru_out")
            s.update(xy=xy, xc=xc, r=r, ig=ig, a=a, hs=hs, mixed=mixed)
        s["h_mid"] = h
        hn2 = _rms_fwd(h, norm_ffn[layer], width=d, col_block=0, name="norm_ffn_fwd")
        w_gu4[layer], w_down4[layer] = arrived(f"w_gu{layer}", hn2), arrived(f"w_down{layer}", hn2)
        gu = _mm("nn", hn2, w_gu4[layer], kind="col", tm=tmm, tn=tile(n_gu, 1408), tk=d, name="ffn_gu")
        act = _swiglu_fwd(gu)
        h = _mm("nn", act, w_down4[layer], kind="row", resid=h, tm=tmm, tn=tile(d, 1024), tk=tile(D_FF // N_CHIPS, 1408), name="ffn_down")
        s.update(hn2=hn2, gu=gu, act=act)
        saved.append(s)

    target = jnp.concatenate([jnp.zeros((N_META, d), F32), loss_target[0], jnp.zeros((tp - t_real, d), F32)], axis=0)
    dh, dhb, g_norm_final, loss_part = _final_loss(h, norm_final, target)
    loss = lax.psum(loss_part[0, 0], ("x", "y", "c"))

    g_norm_mix, g_norm_ffn = [None] * DEPTH, [None] * DEPTH
    g_q_norm, g_kv_norm = [None] * n_mla, [None] * n_mla
    g_w_uq = [None] * n_mla
    g_gate = {k: [None] * n_lru for k in ("wa", "ba", "wx", "bx", "lam", "cw", "cb")}
    G = {k: None for k in ("w_in", "w_ukv", "w_o", "lw_in", "lw_o", "w_gu", "w_down")}

    def grad_w(key, a_op, b_op, kind, lyr, n_layers, tm, tn):
        G[key] = _mm_tn(a_op, b_op, into=G[key], kind=kind, layer=lyr, n_layers=n_layers, tm=tm, tn=tn, tk=tkt, name="grad_" + key)

    for layer in reversed(range(DEPTH)):
        j = layer // 2
        s = saved[layer]
        grad_w("w_down", s["act"], dhb, "row", layer, DEPTH, tile(D_FF // N_CHIPS, 1408), tile(d, 1024))
        d_act = _mm("nt", dhb, w_down4[layer], kind="row", tm=tmm, tn=tile(D_FF // N_CHIPS, 1408), tk=d, name="ffn_down_bwd")
        dgu = _swiglu_bwd(s["gu"], d_act)
        grad_w("w_gu", s["hn2"], dgu, "col", layer, DEPTH, tile(d, 1024), tile(n_gu, 1408))
        dhn2 = _mm("nt", dgu, w_gu4[layer], kind="col", tm=tmm, tn=tile(d, 1024), tk=tile(n_gu, 1408), name="ffn_gu_bwd")
        dh, dhb, g_norm_ffn[layer] = _rms_bwd(s["h_mid"], norm_ffn[layer], dhn2, dh, width=d, col_block=0, name="norm_ffn_bwd")
        if layer % 2 == 0:
            grad_w("w_o", s["att"], dhb, "row", j, n_mla, tile(h_dim // N_CHIPS, 512), tile(d, 1024))
            d_att = _mm("nt", dhb, w_o4, kind="row", layer=j, tm=tmm, tn=tile(h_dim // N_CHIPS, 512), tk=d, name="mla_out_bwd")
            delta = _attn_delta(d_att, s["att"])
            dqn, dqr_h, dkv, dkr_h = _attn_bwd(s["qp"], s["kp"], s["kv"], d_att, s["lse"], delta)
            dqr, dkr = _mla_unprep(dqr_h, dkr_h, cos2, sin2)
            dq = jnp.concatenate([dqn, dqr], axis=-1)
            g_w_uq[j] = _mm_tn(s["c_q"], dq, tm=Q_LORA, tn=tile(q_cols, 1024), tk=tkt, name="grad_w_uq")
            dc_q = _mm("nt", dq, w_uq_perm[j], tm=tmm, tn=Q_LORA, tk=tile(q_cols, 1024), name="mla_uq_bwd")
            grad_w("w_ukv", s["c_kv"], dkv, "col", j, n_mla, KV_LORA, tile(w_ukv4.shape[3], 1024))
            dc_kv = _mm("nt", dkv, w_ukv4, kind="col", layer=j, tm=tmm, tn=KV_LORA, tk=tile(w_ukv4.shape[3], 1024), name="mla_ukv_bwd")
            dpq, _, g_q_norm[j] = _rms_bwd(s["proj"], mla_q_norm[j], dc_q, None, width=Q_LORA, col_block=0, name="q_norm_bwd")
            dpkv, _, g_kv_norm[j] = _rms_bwd(s["proj"], mla_kv_norm[j], dc_kv, None, width=KV_LORA, col_block=Q_LORA // KV_LORA, name="kv_norm_bwd")
            dproj = jnp.concatenate([dpq, dpkv, dkr], axis=-1).astype(CDT)
            grad_w("w_in", s["hn"], dproj, "row", j, n_mla, tile(d // N_CHIPS, 512), tile(w_in_pad, 1152))
            dhn = _mm("nt", dproj, w_in4, kind="row", layer=j, tm=tmm, tn=tile(d // N_CHIPS, 512), tk=tile(w_in_pad, 1152), name="mla_in_bwd")
        else:
            grad_w("lw_o", s["mixed"], dhb, "row", j, n_lru, tile(d // N_CHIPS, 512), tile(d, 1024))
            dm = _mm("nt", dhb, lw_o4, kind="row", layer=j, tm=tmm, tn=tile(d // N_CHIPS, 512), tk=d, name="lru_out_bwd")
            db, da, dy = _lru_scan_bwd(s["a"], s["hs"], dm, s["xy"])
            dxc, g_gate["wa"][j], g_gate["ba"][j], g_gate["wx"][j], g_gate["bx"][j], g_gate["lam"][j] = _lru_gates_bwd(
                db, da, s["xc"], s["r"], s["ig"], s["a"], lam_full[j], lru_w_gate_a[j], lru_w_gate_x[j])
            dxb, g_gate["cw"][j], g_gate["cb"][j] = _lru_conv_bwd(dxc, s["xy"], conv_w_full[j])
            dxy = jnp.concatenate([dxb, dy], axis=-1)
            grad_w("lw_in", s["hn"], dxy, "col", j, n_lru, tile(d, 1024), tile(lw_in4.shape[3], 1024))
            dhn = _mm("nt", dxy, lw_in4, kind="col", layer=j, tm=tmm, tn=tile(d, 1024), tk=tile(lw_in4.shape[3], 1024), name="lru_in_bwd")
        dh, dhb, g_norm_mix[layer] = _rms_bwd(s["h_in"], norm_mix[layer], dhn, dh, width=d, col_block=0, name="norm_mix_bwd")

    grad_x = dh[N_META:t_real][None]
    g_meta_full = dh[:N_META]

    nope_w = MLA_HEADS * QK_NOPE
    g_uq = jnp.stack(g_w_uq)
    g_uq = jnp.concatenate([g_uq[..., :nope_w].reshape(n_mla, Q_LORA, MLA_HEADS, QK_NOPE),
                            g_uq[..., nope_w:].reshape(n_mla, Q_LORA, MLA_HEADS, QK_ROPE)], axis=-1)
    G["w_uq"] = jnp.moveaxis(g_uq.reshape(n_mla, Q_LORA, N_CHIPS, q_cols // N_CHIPS), 2, 0)

    big = [("w_in", mla_w_in, m_mla_w_in, v_mla_w_in), ("w_uq", mla_w_uq, m_mla_w_uq, v_mla_w_uq),
           ("w_ukv", mla_w_ukv, m_mla_w_ukv, v_mla_w_ukv), ("w_o", mla_w_o, m_mla_w_o, v_mla_w_o),
           ("lw_in", lru_w_in, m_lru_w_in, v_lru_w_in), ("lw_o", lru_w_o, m_lru_w_o, v_lru_w_o),
           ("w_gu", ffn_w_gu, m_ffn_w_gu, v_ffn_w_gu), ("w_down", ffn_w_down, m_ffn_w_down, v_ffn_w_down)]
    res = {}
    for key, w, m, v in big:
        g = _reduce_sharded(G[key], "reduce_" + key)
        if key == "w_in":
            g = g[:, :, :w_in_cols]
        shp = w.shape
        two_d = (shp[0] * shp[1], shp[2])
        dlt, nm, nv = _adamw(w.reshape(two_d), g.reshape(two_d), m.reshape(two_d), v.reshape(two_d), "adamw_" + key)
        res[key] = (g, dlt.reshape(shp), nm.reshape(shp), nv.reshape(shp))

    g_small_full = [g_meta_full, jnp.stack(g_gate["cw"]).reshape(n_lru, CONV_W, d), jnp.stack(g_gate["cb"]).reshape(n_lru, d),
                    jnp.stack(g_gate["lam"]).reshape(n_lru, d)]
    g_small4 = jnp.stack([_pack([a[..., k * csh:(k + 1) * csh] for a in g_small_full], csh, 16) for k in range(N_CHIPS)])
    rows_s = g_small4.shape[1]
    red = _reduce_to_owner(g_small4.reshape(N_DEV, rows_s // 2, csh), F32, "reduce_small")
    g_small = _gather_send(red, PAIR_RELS, 2, lambda dev: dev[2], "reduce_small_pair").reshape(rows_s, csh)
    small_w = [meta_tokens, lru_conv_w, lru_conv_b, lru_lambda]
    small_m = [m_meta_tokens, m_lru_conv_w, m_lru_conv_b, m_lru_lambda]
    small_v = [v_meta_tokens, v_lru_conv_w, v_lru_conv_b, v_lru_lambda]
    sd, sm, sv = _adamw(_pack(small_w, csh, 16), g_small, _pack(small_m, csh, 16), _pack(small_v, csh, 16), "adamw_small")
    small_out = [_unpack(buf, small_shapes) for buf in (g_small, sd, sm, sv)]

    rep_w = [norm_mix, norm_ffn, norm_final, mla_q_norm, mla_kv_norm, lru_w_gate_a, lru_b_gate_a, lru_w_gate_x, lru_b_gate_x]
    rep_m = [m_norm_mix, m_norm_ffn, m_norm_final, m_mla_q_norm, m_mla_kv_norm, m_lru_w_gate_a, m_lru_b_gate_a, m_lru_w_gate_x, m_lru_b_gate_x]
    rep_v = [v_norm_mix, v_norm_ffn, v_norm_final, v_mla_q_norm, v_mla_kv_norm, v_lru_w_gate_a, v_lru_b_gate_a, v_lru_w_gate_x, v_lru_b_gate_x]
    rep_g = [jnp.stack(g_norm_mix), jnp.stack(g_norm_ffn), g_norm_final, jnp.stack(g_q_norm), jnp.stack(g_kv_norm),
             jnp.stack(g_gate["wa"]), jnp.stack(g_gate["ba"]), jnp.stack(g_gate["wx"]), jnp.stack(g_gate["bx"])]
    rep_shapes = [w.shape for w in rep_w]
    g_rep = _pack(rep_g, LANES, 8 * N_DEV)
    rows_r = g_rep.shape[0]
    red = _reduce_to_owner(g_rep.reshape(N_DEV, rows_r // N_DEV, LANES), F32, "reduce_rep")
    g_rep = _gather_send(red, ALL_RELS, N_DEV, _linear, "reduce_rep_all").reshape(rows_r, LANES)
    rd, rm, rv = _adamw(_pack(rep_w, LANES, 8 * N_DEV), g_rep, _pack(rep_m, LANES, 8 * N_DEV), _pack(rep_v, LANES, 8 * N_DEV), "adamw_rep")
    rep_out = [_unpack(buf, rep_shapes) for buf in (g_rep, rd, rm, rv)]

    def leaf(kind):
        s_, r_ = small_out[kind], rep_out[kind]
        return [s_[0], r_[0], r_[1], r_[2], res["w_in"][kind], r_[3], r_[4], res["w_uq"][kind], res["w_ukv"][kind],
                res["w_o"][kind], res["lw_in"][kind], s_[1], s_[2], r_[5], r_[6], r_[7], r_[8], s_[3],
                res["lw_o"][kind], res["w_gu"][kind], res["w_down"][kind]]

    return (loss, grad_x, *leaf(0), *leaf(1), *leaf(2), *leaf(3))
```

```python
import math

import jax
import jax.numpy as jnp
from jax import lax
from jax.experimental import pallas as pl
from jax.experimental.pallas import tpu as pltpu

F32 = jnp.float32
CDT = jnp.bfloat16
MESH = pl.DeviceIdType.MESH

D_MODEL = 2048
SEQ = 4096
DEPTH = 4
CHUNK = 64
N_META = 16
MLA_HEADS = 16
Q_LORA = 512
KV_LORA = 512
QK_NOPE = 128
QK_ROPE = 64
V_HEAD = 128
ROPE_THETA = 10000.0
RNN_BLOCKS = 16
CONV_W = 4
LRU_C = 8.0
D_FF = 5632
RMS_EPS = 1e-6
NEG_BIG = -1e30
ADAM_LR = 0.001
ADAM_B1 = 0.9
ADAM_B2 = 0.999
ADAM_EPS = 1e-08
ADAM_WD = 0.01
ADAM_STEP = 10

N_CHIPS = 4
N_DEV = 8
LANES = 128
VMEM_LIMIT = 52 * 1024 * 1024
ROW_TILE = 384
MM_ROW_TILE = 704
ATT_TILE = 384
SCAN_COLS = 128
PAIR_CHUNKS = 8


def _div_tile(n, pref, mult):
    if n <= pref:
        return n
    d = (pref // mult) * mult
    while d >= mult:
        if n % d == 0:
            return d
        d -= mult
    raise ValueError(f"no tile for {n} <= {pref} (multiple of {mult})")


def _t_pad():
    t = N_META + SEQ
    step = math.lcm(_row_tile_unit(), 8)
    return -(-t // step) * step


def _row_tile_unit():
    return math.lcm(math.lcm(ROW_TILE, MM_ROW_TILE), ATT_TILE)


def _params(*sem):
    return pltpu.CompilerParams(dimension_semantics=sem, vmem_limit_bytes=VMEM_LIMIT)


def _b_spec(form, b, kind, layer, t_out, t_con):
    if kind == "plain":
        if form == "nn":
            return pl.BlockSpec((t_con, t_out), lambda i, j, k: (k, j))
        return pl.BlockSpec((t_out, t_con), lambda i, j, k: (j, k))
    rows, cols = b.shape[2], b.shape[3]
    if form == "nn":
        blk = (None, None, t_con, t_out)
        if kind == "row":
            per = rows // t_con
            return pl.BlockSpec(blk, lambda i, j, k: (k // per, layer, k % per, j))
        per = cols // t_out
        return pl.BlockSpec(blk, lambda i, j, k: (j // per, layer, k, j % per))
    blk = (None, None, t_out, t_con)
    if kind == "row":
        per = rows // t_out
        return pl.BlockSpec(blk, lambda i, j, k: (j // per, layer, j % per, k))
    per = cols // t_con
    return pl.BlockSpec(blk, lambda i, j, k: (k // per, layer, j, k % per))


def _mm_body(nk, dims, has_resid):
    def body(*refs):
        if has_resid:
            a_ref, b_ref, r_ref, o_ref = refs[:4]
        else:
            a_ref, b_ref, o_ref = refs[:3]
        prod = lax.dot_general(a_ref[...].astype(CDT), b_ref[...].astype(CDT), (dims, ((), ())),
                               preferred_element_type=F32)

        def finish(acc):
            if has_resid:
                acc = acc + r_ref[...]
            o_ref[...] = acc.astype(o_ref.dtype)

        if nk == 1:
            finish(prod)
            return
        acc_ref = refs[-1]
        k = pl.program_id(2)

        @pl.when(k == 0)
        def _():
            acc_ref[...] = prod

        @pl.when(k > 0)
        def _():
            acc_ref[...] += prod

        @pl.when(k == nk - 1)
        def _():
            finish(acc_ref[...])

    return body


def _mm(form, a, b, *, kind="plain", layer=0, out_dtype=F32, resid=None, tm, tn, tk, name):
    m, con = a.shape
    if kind == "plain":
        w_rows, w_cols = b.shape
    elif kind == "row":
        w_rows, w_cols = b.shape[0] * b.shape[2], b.shape[3]
    else:
        w_rows, w_cols = b.shape[2], b.shape[0] * b.shape[3]
    n_out = w_cols if form == "nn" else w_rows
    assert con == (w_rows if form == "nn" else w_cols), (name, a.shape, b.shape)
    nk = con // tk
    assert m % tm == 0 and n_out % tn == 0 and con % tk == 0, (name, m, n_out, con, tm, tn, tk)
    dims = ((1,), (0,)) if form == "nn" else ((1,), (1,))
    in_specs = [pl.BlockSpec((tm, tk), lambda i, j, k: (i, k)), _b_spec(form, b, kind, layer, tn, tk)]
    args = [a, b]
    if resid is not None:
        in_specs.append(pl.BlockSpec((tm, tn), lambda i, j, k: (i, j)))
        args.append(resid)
    return pl.pallas_call(
        _mm_body(nk, dims, resid is not None),
        grid=(m // tm, n_out // tn, nk),
        in_specs=in_specs,
        out_specs=pl.BlockSpec((tm, tn), lambda i, j, k: (i, j)),
        out_shape=jax.ShapeDtypeStruct((m, n_out), out_dtype),
        scratch_shapes=[pltpu.VMEM((tm, tn), F32)] if nk > 1 else [],
        compiler_params=_params("parallel", "parallel", "arbitrary"),
        name=name,
    )(*args)


def _mm_tn(a, b, *, kind="plain", tm, tn, tk, name):
    t, m = a.shape
    n = b.shape[1]
    nk = t // tk
    assert t % tk == 0 and m % tm == 0 and n % tn == 0, (name, t, m, n, tm, tn, tk)
    if kind == "plain":
        out_shape = jax.ShapeDtypeStruct((m, n), F32)
        out_spec = pl.BlockSpec((tm, tn), lambda i, j, k: (i, j))
    elif kind == "row":
        per = (m // N_CHIPS) // tm
        assert per >= 2 and per % 2 == 0, (name, per)
        out_shape = jax.ShapeDtypeStruct((N_DEV, m // N_DEV, n), CDT)
        out_spec = pl.BlockSpec((None, tm, tn), lambda i, j, k: (2 * (i // per) + (i % per) // (per // 2), (i % per) % (per // 2), j))
    elif kind == "row_colhalves":
        per = (m // N_CHIPS) // tm
        nt = n // tn
        assert per >= 1 and nt % 2 == 0, (name, per, nt)
        out_shape = jax.ShapeDtypeStruct((N_DEV, m // N_CHIPS, n // 2), CDT)
        out_spec = pl.BlockSpec((None, tm, tn), lambda i, j, k: (2 * (i // per) + j // (nt // 2), i % per, j % (nt // 2)))
    else:
        per = (n // N_CHIPS) // tn
        mt = m // tm
        assert per >= 1 and mt % 2 == 0, (name, per, mt)
        out_shape = jax.ShapeDtypeStruct((N_DEV, m // 2, n // N_CHIPS), CDT)
        out_spec = pl.BlockSpec((None, tm, tn), lambda i, j, k: (2 * (j // per) + i // (mt // 2), i % (mt // 2), j % per))
    return pl.pallas_call(
        _mm_body(nk, ((0,), (0,)), False),
        grid=(m // tm, n // tn, nk),
        in_specs=[pl.BlockSpec((tk, tm), lambda i, j, k: (k, i)), pl.BlockSpec((tk, tn), lambda i, j, k: (k, j))],
        out_specs=out_spec,
        out_shape=out_shape,
        scratch_shapes=[pltpu.VMEM((tm, tn), F32)] if nk > 1 else [],
        compiler_params=_params("parallel", "parallel", "arbitrary"),
        name=name,
    )(a, b)


def _rms_fwd(x, g, *, width, col_block, name):
    tp = x.shape[0]
    tr = _div_tile(tp, ROW_TILE, 8)

    def body(x_ref, g_ref, o_ref):
        xf = x_ref[...]
        r = lax.rsqrt(jnp.mean(xf * xf, axis=-1, keepdims=True) + RMS_EPS)
        o_ref[...] = ((xf * r) * g_ref[...]).astype(o_ref.dtype)

    return pl.pallas_call(
        body,
        grid=(tp // tr,),
        in_specs=[pl.BlockSpec((tr, width), lambda i: (i, col_block)), pl.BlockSpec((1, width), lambda i: (0, 0))],
        out_specs=pl.BlockSpec((tr, width), lambda i: (i, 0)),
        out_shape=jax.ShapeDtypeStruct((tp, width), CDT),
        compiler_params=_params("parallel"),
        name=name,
    )(x, g.reshape(1, width))


def _rms_bwd(x, g, dy, resid, *, width, col_block, name):
    tp = x.shape[0]
    tr = _div_tile(tp, ROW_TILE, 8)
    has_resid = resid is not None

    def body(*refs):
        if has_resid:
            x_ref, g_ref, dy_ref, res_ref, dx_ref, dxb_ref, dg_ref = refs
        else:
            x_ref, g_ref, dy_ref, dx_ref, dxb_ref, dg_ref = refs
        i = pl.program_id(0)
        xf = x_ref[...]
        r = lax.rsqrt(jnp.mean(xf * xf, axis=-1, keepdims=True) + RMS_EPS)
        xh = xf * r
        dy = dy_ref[...].astype(F32)
        dg = jnp.sum(dy * xh, axis=0, keepdims=True)
        dxh = dy * g_ref[...]
        dx = r * (dxh - xh * jnp.mean(dxh * xh, axis=-1, keepdims=True))
        if has_resid:
            dx = dx + res_ref[...]
        dx_ref[...] = dx
        dxb_ref[...] = dx.astype(CDT)

        @pl.when(i == 0)
        def _():
            dg_ref[...] = dg

        @pl.when(i > 0)
        def _():
            dg_ref[...] += dg

    row = pl.BlockSpec((tr, width), lambda i: (i, 0))
    in_specs = [pl.BlockSpec((tr, width), lambda i: (i, col_block)), pl.BlockSpec((1, width), lambda i: (0, 0)), row]
    args = [x, g.reshape(1, width), dy]
    if has_resid:
        in_specs.append(row)
        args.append(resid)
    return pl.pallas_call(
        body,
        grid=(tp // tr,),
        in_specs=in_specs,
        out_specs=[row, row, pl.BlockSpec((1, width), lambda i: (0, 0))],
        out_shape=[jax.ShapeDtypeStruct((tp, width), F32), jax.ShapeDtypeStruct((tp, width), CDT),
                   jax.ShapeDtypeStruct((1, width), F32)],
        compiler_params=_params("arbitrary"),
        name=name,
    )(*args)


def _final_loss(h, g, target):
    tp, d = h.shape
    tr = _div_tile(tp, ROW_TILE, 8)

    def body(h_ref, g_ref, t_ref, dh_ref, dhb_ref, dg_ref, loss_ref):
        i = pl.program_id(0)
        xf = h_ref[...]
        r = lax.rsqrt(jnp.mean(xf * xf, axis=-1, keepdims=True) + RMS_EPS)
        xh = xf * r
        gain = g_ref[...]
        y = xh * gain
        rows = i * tr + lax.broadcasted_iota(jnp.int32, (tr, 1), 0)
        valid = jnp.logical_and(rows >= N_META, rows < N_META + SEQ)
        err = jnp.where(valid, y - t_ref[...], 0.0)
        part = 0.5 * jnp.sum(jnp.mean(err * err, axis=-1, keepdims=True), axis=0, keepdims=True)
        dy = err * (1.0 / d)
        dg = jnp.sum(dy * xh, axis=0, keepdims=True)
        dxh = dy * gain
        dx = r * (dxh - xh * jnp.mean(dxh * xh, axis=-1, keepdims=True))
        dh_ref[...] = dx
        dhb_ref[...] = dx.astype(CDT)

        @pl.when(i == 0)
        def _():
            dg_ref[...] = dg
            loss_ref[...] = part

        @pl.when(i > 0)
        def _():
            dg_ref[...] += dg
            loss_ref[...] += part

    row = pl.BlockSpec((tr, d), lambda i: (i, 0))
    vec = pl.BlockSpec((1, d), lambda i: (0, 0))
    return pl.pallas_call(
        body,
        grid=(tp // tr,),
        in_specs=[row, vec, row],
        out_specs=[row, row, vec, pl.BlockSpec((1, 1), lambda i: (0, 0))],
        out_shape=[jax.ShapeDtypeStruct((tp, d), F32), jax.ShapeDtypeStruct((tp, d), CDT),
                   jax.ShapeDtypeStruct((1, d), F32), jax.ShapeDtypeStruct((1, 1), F32)],
        compiler_params=_params("arbitrary"),
        name="final_loss",
    )(h, g.reshape(1, d), target)


def _sigmoid(x):
    return 1.0 / (1.0 + jnp.exp(-x))


def _swiglu_fwd(gu):
    tp, f2 = gu.shape
    f = f2 // 2
    tr = _div_tile(tp, ROW_TILE, 8)
    tf = _div_tile(f, 1408, LANES)
    nf = f // tf

    def body(g_ref, u_ref, o_ref):
        g = g_ref[...]
        o_ref[...] = ((g * _sigmoid(g)) * u_ref[...]).astype(o_ref.dtype)

    return pl.pallas_call(
        body,
        grid=(tp // tr, nf),
        in_specs=[pl.BlockSpec((tr, tf), lambda i, j: (i, j)), pl.BlockSpec((tr, tf), lambda i, j: (i, j + nf))],
        out_specs=pl.BlockSpec((tr, tf), lambda i, j: (i, j)),
        out_shape=jax.ShapeDtypeStruct((tp, f), CDT),
        compiler_params=_params("parallel", "parallel"),
        name="swiglu_fwd",
    )(gu, gu)


def _swiglu_bwd(gu, da):
    tp, f2 = gu.shape
    f = f2 // 2
    tr = _div_tile(tp, ROW_TILE, 8)
    tf = _div_tile(f, 1408, LANES)
    nf = f // tf

    def body(g_ref, u_ref, da_ref, o_ref):
        j = pl.program_id(1)
        g = g_ref[...]
        da = da_ref[...]
        sg = _sigmoid(g)

        @pl.when(j < nf)
        def _():
            o_ref[...] = (da * u_ref[...] * (sg * (1.0 + g * (1.0 - sg)))).astype(o_ref.dtype)

        @pl.when(j >= nf)
        def _():
            o_ref[...] = (da * (g * sg)).astype(o_ref.dtype)

    return pl.pallas_call(
        body,
        grid=(tp // tr, 2 * nf),
        in_specs=[pl.BlockSpec((tr, tf), lambda i, j: (i, j % nf)),
                  pl.BlockSpec((tr, tf), lambda i, j: (i, nf + j % nf)),
                  pl.BlockSpec((tr, tf), lambda i, j: (i, j % nf))],
        out_specs=pl.BlockSpec((tr, tf), lambda i, j: (i, j)),
        out_shape=jax.ShapeDtypeStruct((tp, f2), CDT),
        compiler_params=_params("parallel", "parallel"),
        name="swiglu_bwd",
    )(gu, gu, da)


def _swap_halves(x):
    lane = lax.broadcasted_iota(jnp.int32, x.shape, x.ndim - 1)
    first = (lane % QK_ROPE) < (QK_ROPE // 2)
    return jnp.where(first, pltpu.roll(x, LANES - QK_ROPE // 2, x.ndim - 1), pltpu.roll(x, QK_ROPE // 2, x.ndim - 1))


def _rope_tables(tp):
    pos = jnp.arange(tp, dtype=F32)
    inv_freq = ROPE_THETA ** (-jnp.arange(0, QK_ROPE, 2, dtype=F32) / QK_ROPE)
    ang = pos[:, None] * inv_freq[None, :]
    cos, sin = jnp.cos(ang), jnp.sin(ang)
    reps = LANES // QK_ROPE
    return jnp.tile(jnp.concatenate([cos, cos], -1), (1, reps)), jnp.tile(jnp.concatenate([-sin, sin], -1), (1, reps))


def _chunk_of(pos):
    shift = CHUNK.bit_length() - 1
    assert CHUNK == 1 << shift
    return jnp.where(pos < N_META, 0, 1 + lax.shift_right_arithmetic(pos - N_META, shift))


def _head_half(x, h):
    lane = lax.broadcasted_iota(jnp.int32, x.shape, x.ndim - 1)
    return jnp.where((lane // QK_ROPE) == (h % 2), x, jnp.zeros_like(x))


def _mla_prep(q, kv, proj, cos2, sin2):
    tp = q.shape[0]
    tr = _div_tile(tp, ROW_TILE, 8)
    nope_w = MLA_HEADS * QK_NOPE
    kr_block = (Q_LORA + KV_LORA) // LANES
    depth = QK_NOPE + LANES

    def body(q_ref, kv_ref, kr_ref, c_ref, s_ref, qp_out, kp_out):
        c = c_ref[...]
        s = s_ref[...]
        k = kr_ref[...]
        k = k + pltpu.roll(k, QK_ROPE, 1)
        k = (k * c + _swap_halves(k) * s).astype(CDT)
        for p in range(MLA_HEADS // 2):
            x = q_ref[:, nope_w + p * LANES:nope_w + (p + 1) * LANES]
            pair = (x * c + _swap_halves(x) * s).astype(CDT)
            for h in (2 * p, 2 * p + 1):
                qp_out[h, :, :QK_NOPE] = q_ref[:, h * QK_NOPE:(h + 1) * QK_NOPE].astype(CDT)
                qp_out[h, :, QK_NOPE:] = _head_half(pair, h)
                kp_out[h, :, :QK_NOPE] = kv_ref[:, 2 * h * QK_NOPE:(2 * h + 1) * QK_NOPE]
                kp_out[h, :, QK_NOPE:] = k

    tab = pl.BlockSpec((tr, LANES), lambda i: (i, 0))
    per_head = pl.BlockSpec((MLA_HEADS, tr, depth), lambda i: (0, i, 0))
    out = jax.ShapeDtypeStruct((MLA_HEADS, tp, depth), CDT)
    return pl.pallas_call(
        body,
        grid=(tp // tr,),
        in_specs=[pl.BlockSpec((tr, q.shape[1]), lambda i: (i, 0)), pl.BlockSpec((tr, kv.shape[1]), lambda i: (i, 0)),
                  pl.BlockSpec((tr, LANES), lambda i: (i, kr_block)), tab, tab],
        out_specs=[per_head, per_head],
        out_shape=[out, out],
        compiler_params=_params("parallel"),
        name="mla_prep",
    )(q, kv, proj, cos2, sin2)


def _dot_nt(a, b):
    return lax.dot_general(a, b, (((1,), (1,)), ((), ())), preferred_element_type=F32)


def _dot_tn(a, b):
    return lax.dot_general(a, b, (((0,), (0,)), ((), ())), preferred_element_type=F32)


def _dot(a, b):
    return jnp.dot(a, b, preferred_element_type=F32)


def _chunk_scalar(p):
    return jnp.where(p < N_META, 0, 1 + jnp.maximum(p - N_META, 0) // CHUNK)


def _last_key_block(i, bq, bk, nk):
    cq = _chunk_scalar(i * bq + bq - 1)
    return jnp.minimum((N_META + CHUNK * cq - 1) // bk, nk - 1)


def _full_key_blocks(i, bq, bk):
    return (N_META + CHUNK * _chunk_scalar(i * bq)) // bk


def _first_query_block(j, bk, bq):
    p0 = N_META + CHUNK * (jnp.maximum(j * bk - N_META, 0) // CHUNK)
    return p0 // bq


def _first_full_query_block(j, bk, bq, nq):
    ck = _chunk_scalar(j * bk + bk - 1)
    p0 = jnp.where(ck == 0, 0, N_META + CHUNK * (ck - 1))
    return jnp.minimum((p0 + bq - 1) // bq, nq)


def _chunk_mask(q0, k0, shape, keys_on_rows):
    if keys_on_rows:
        kc = _chunk_of(k0 + lax.broadcasted_iota(jnp.int32, (shape[0], 1), 0))
        qc = _chunk_of(q0 + lax.broadcasted_iota(jnp.int32, (1, shape[1]), 1))
    else:
        qc = _chunk_of(q0 + lax.broadcasted_iota(jnp.int32, (shape[0], 1), 0))
        kc = _chunk_of(k0 + lax.broadcasted_iota(jnp.int32, (1, shape[1]), 1))
    return kc <= qc


def _attn_fwd(qp, kp, kv):
    tp = qp.shape[1]
    depth = qp.shape[2]
    bq = bk = _div_tile(tp, ATT_TILE, LANES)
    nq, nk = tp // bq, tp // bk
    scale = (QK_NOPE + QK_ROPE) ** -0.5

    def body(q_ref, k_ref, v_ref, o_ref, lse_ref):
        def q_block(i, _):
            q0 = pl.multiple_of(i * bq, bq)
            qb = q_ref[pl.ds(q0, bq), :]

            def k_step(masked, j, carry):
                m_old, l_old, acc = carry
                k0 = pl.multiple_of(j * bk, bk)
                s = _dot_nt(qb, k_ref[pl.ds(k0, bk), :]) * scale
                if masked:
                    s = jnp.where(_chunk_mask(q0, k0, s.shape, False), s, NEG_BIG)
                m_new = jnp.maximum(m_old, jnp.max(s, axis=-1, keepdims=True))
                alpha = jnp.exp(m_old - m_new)
                p = jnp.exp(s - m_new)
                l_new = alpha * l_old + jnp.sum(p, axis=-1, keepdims=True)
                acc = alpha * acc + _dot(p.astype(CDT), v_ref[pl.ds(k0, bk), :])
                return m_new, l_new, acc

            n_full = _full_key_blocks(i, bq, bk)
            carry = (jnp.full((bq, 1), NEG_BIG, F32), jnp.zeros((bq, 1), F32), jnp.zeros((bq, V_HEAD), F32))
            carry = lax.fori_loop(0, n_full, lambda j, c: k_step(False, j, c), carry)
            m_fin, l_fin, acc = lax.fori_loop(n_full, _last_key_block(i, bq, bk, nk) + 1,
                                              lambda j, c: k_step(True, j, c), carry)
            o_ref[pl.ds(q0, bq), :] = acc / l_fin
            lse_ref[pl.ds(q0, bq), :] = m_fin + jnp.log(l_fin)
            return 0

        lax.fori_loop(0, nq, q_block, 0)

    per_head = pl.BlockSpec((None, tp, depth), lambda h: (h, 0, 0))
    return pl.pallas_call(
        body,
        grid=(MLA_HEADS,),
        in_specs=[per_head, per_head, pl.BlockSpec((tp, V_HEAD), lambda h: (0, 2 * h + 1))],
        out_specs=[pl.BlockSpec((tp, V_HEAD), lambda h: (0, h)), pl.BlockSpec((None, tp, 1), lambda h: (h, 0, 0))],
        out_shape=[jax.ShapeDtypeStruct((tp, MLA_HEADS * V_HEAD), F32), jax.ShapeDtypeStruct((MLA_HEADS, tp, 1), F32)],
        compiler_params=_params("parallel"),
        name="attn_fwd",
    )(qp, kp, kv)


def _attn_delta(d_out, out):
    tp = out.shape[0]
    tr = _div_tile(tp, ROW_TILE, 8)

    def body(do_ref, o_ref, d_ref):
        for h in range(MLA_HEADS):
            cols = slice(h * V_HEAD, (h + 1) * V_HEAD)
            d_ref[h] = jnp.sum(do_ref[:, cols] * o_ref[:, cols], axis=-1, keepdims=True)

    row = pl.BlockSpec((tr, MLA_HEADS * V_HEAD), lambda i: (i, 0))
    return pl.pallas_call(
        body,
        grid=(tp // tr,),
        in_specs=[row, row],
        out_specs=pl.BlockSpec((MLA_HEADS, tr, 1), lambda i: (0, i, 0)),
        out_shape=jax.ShapeDtypeStruct((MLA_HEADS, tp, 1), F32),
        compiler_params=_params("parallel"),
        name="attn_delta",
    )(d_out, out)


def _attn_bwd(qp, kp, kv, d_out, lse, delta):
    tp = qp.shape[1]
    depth = qp.shape[2]
    bq = bk = _div_tile(tp, ATT_TILE, LANES)
    nq, nk = tp // bq, tp // bk
    scale = (QK_NOPE + QK_ROPE) ** -0.5
    lse_rows = lse.reshape(MLA_HEADS, nq, 1, bq)
    delta_rows = delta.reshape(MLA_HEADS, nq, 1, bq)

    def body(q_ref, k_ref, v_ref, do_ref, lse_ref, dl_ref, dqn_ref, dqr_ref, dkv_ref, dkr_ref, dq_acc):
        h = pl.program_id(0)
        dq_acc[...] = jnp.zeros(dq_acc.shape, F32)

        def k_block(j, _):
            k0 = pl.multiple_of(j * bk, bk)
            kb = k_ref[pl.ds(k0, bk), :]
            vb = v_ref[pl.ds(k0, bk), :]

            def q_step(masked, i, carry):
                dk, dv = carry
                q0 = pl.multiple_of(i * bq, bq)
                qb = q_ref[pl.ds(q0, bq), :]
                dob = do_ref[pl.ds(q0, bq), :].astype(CDT)
                s_t = _dot_nt(kb, qb) * scale
                if masked:
                    s_t = jnp.where(_chunk_mask(q0, k0, s_t.shape, True), s_t, NEG_BIG)
                p_t = jnp.exp(s_t - lse_ref[i])
                dv = dv + _dot(p_t.astype(CDT), dob)
                dp_t = _dot_nt(vb, dob)
                ds_t = (p_t * (dp_t - dl_ref[i]) * scale).astype(CDT)
                dk = dk + _dot(ds_t, qb)
                dq_acc[pl.ds(q0, bq), :] += _dot_tn(ds_t, kb)
                return dk, dv

            i_full = _first_full_query_block(j, bk, bq, nq)
            carry = (jnp.zeros((bk, depth), F32), jnp.zeros((bk, V_HEAD), F32))
            carry = lax.fori_loop(_first_query_block(j, bk, bq), i_full, lambda i, c: q_step(True, i, c), carry)
            dk, dv = lax.fori_loop(i_full, nq, lambda i, c: q_step(False, i, c), carry)
            dkv_ref[pl.ds(k0, bk), :QK_NOPE] = dk[:, :QK_NOPE].astype(CDT)
            dkv_ref[pl.ds(k0, bk), QK_NOPE:] = dv.astype(CDT)
            dkr_ref[pl.ds(k0, bk), :] = dk[:, QK_NOPE:]
            return 0

        lax.fori_loop(0, nk, k_block, 0)
        dqn_ref[...] = dq_acc[:, :QK_NOPE].astype(CDT)
        dqr_ref[...] = _head_half(dq_acc[:, QK_NOPE:], h)

    per_head = pl.BlockSpec((None, tp, depth), lambda h: (h, 0, 0))
    stat = pl.BlockSpec((None, nq, 1, bq), lambda h: (h, 0, 0, 0))
    lanes_out = pl.BlockSpec((None, tp, LANES), lambda h: (h, 0, 0))
    return pl.pallas_call(
        body,
        grid=(MLA_HEADS,),
        in_specs=[per_head, per_head, pl.BlockSpec((tp, V_HEAD), lambda h: (0, 2 * h + 1)),
                  pl.BlockSpec((tp, V_HEAD), lambda h: (0, h)), stat, stat],
        out_specs=[pl.BlockSpec((tp, QK_NOPE), lambda h: (0, h)), lanes_out,
                   pl.BlockSpec((tp, QK_NOPE + V_HEAD), lambda h: (0, h)), lanes_out],
        out_shape=[jax.ShapeDtypeStruct((tp, MLA_HEADS * QK_NOPE), CDT), jax.ShapeDtypeStruct((MLA_HEADS, tp, LANES), F32),
                   jax.ShapeDtypeStruct((tp, MLA_HEADS * (QK_NOPE + V_HEAD)), CDT),
                   jax.ShapeDtypeStruct((MLA_HEADS, tp, LANES), F32)],
        scratch_shapes=[pltpu.VMEM((tp, depth), F32)],
        compiler_params=_params("parallel"),
        name="attn_bwd",
    )(qp, kp, kv, d_out, lse_rows, delta_rows)


def _mla_unprep(dqr_h, dkr_h, cos2, sin2):
    tp = dqr_h.shape[1]
    tr = _div_tile(tp, ROW_TILE, 8)
    wr = MLA_HEADS * QK_ROPE

    def body(dq_ref, dk_ref, c_ref, s_ref, dqr_out, dkr_out):
        c = c_ref[...]
        s = s_ref[...]
        for p in range(MLA_HEADS // 2):
            x = dq_ref[2 * p] + dq_ref[2 * p + 1]
            dqr_out[:, p * LANES:(p + 1) * LANES] = (x * c - _swap_halves(x) * s).astype(CDT)
        t = dk_ref[0]
        for h in range(1, MLA_HEADS):
            t = t + dk_ref[h]
        t = t * c - _swap_halves(t) * s
        t = t + pltpu.roll(t, QK_ROPE, 1)
        lane = lax.broadcasted_iota(jnp.int32, t.shape, 1)
        dkr_out[...] = jnp.where(lane < QK_ROPE, t, 0.0)

    per_head = pl.BlockSpec((MLA_HEADS, tr, LANES), lambda i: (0, i, 0))
    tab = pl.BlockSpec((tr, LANES), lambda i: (i, 0))
    return pl.pallas_call(
        body,
        grid=(tp // tr,),
        in_specs=[per_head, per_head, tab, tab],
        out_specs=[pl.BlockSpec((tr, wr), lambda i: (i, 0)), tab],
        out_shape=[jax.ShapeDtypeStruct((tp, wr), CDT), jax.ShapeDtypeStruct((tp, LANES), F32)],
        compiler_params=_params("parallel"),
        name="mla_unprep",
    )(dqr_h, dkr_h, cos2, sin2)


HALO = 8


def _softplus(x):
    return jnp.maximum(x, 0.0) + jnp.log1p(jnp.exp(-jnp.abs(x)))


def _one_minus_sq(log_a, a):
    return -jnp.tanh(log_a) * (a * a + 1.0)


def _gelu(y):
    k = math.sqrt(2.0 / math.pi)
    return 0.5 * y * (1.0 + jnp.tanh(k * (y + 0.044715 * (y * y * y))))


def _gelu_grad(y):
    k = math.sqrt(2.0 / math.pi)
    th = jnp.tanh(k * (y + 0.044715 * (y * y * y)))
    return 0.5 * (1.0 + th) + 0.5 * y * (1.0 - th * th) * (k * (1.0 + 3.0 * 0.044715 * (y * y)))


def _lru_gates_fwd(xy, conv_w, conv_b, w_ga, b_ga, w_gx, b_gx, lam):
    tp = xy.shape[0]
    dr = xy.shape[1] // 2
    bw = dr // RNN_BLOCKS
    tr = _div_tile(tp, ROW_TILE, 8)

    def body(x_ref, halo_ref, cw_ref, cb_ref, wa_ref, ba_ref, wx_ref, bx_ref, lam_ref,
             xc_ref, r_ref, i_ref, a_ref, b_ref, xs):
        i = pl.program_id(0)
        xs[0:HALO, :] = jnp.where(i == 0, 0.0, halo_ref[...])
        xs[HALO:, :] = x_ref[...]
        xc = cb_ref[...] + cw_ref[0:1, :] * xs[pl.ds(HALO - CONV_W + 1, tr), :]
        for j in range(1, CONV_W):
            xc = xc + cw_ref[j:j + 1, :] * xs[pl.ds(HALO - CONV_W + 1 + j, tr), :]
        xcb = xc.astype(CDT)
        r = _sigmoid(_dot(xcb, wa_ref[...]) + ba_ref[...])
        ig = _sigmoid(_dot(xcb, wx_ref[...]) + bx_ref[...])
        log_a = (-LRU_C * r) * _softplus(-lam_ref[...])
        a = jnp.exp(log_a)
        xc_ref[...] = xc
        r_ref[...] = r
        i_ref[...] = ig
        a_ref[...] = a
        b_ref[...] = jnp.sqrt(_one_minus_sq(log_a, a)) * (ig * xc)

    blk = pl.BlockSpec((tr, bw), lambda i, n: (i, n))
    vec = pl.BlockSpec((1, bw), lambda i, n: (0, n))
    mat = pl.BlockSpec((None, bw, bw), lambda i, n: (n, 0, 0))
    bias = pl.BlockSpec((None, 1, bw), lambda i, n: (n, 0, 0))
    out = jax.ShapeDtypeStruct((tp, dr), F32)
    return pl.pallas_call(
        body,
        grid=(tp // tr, RNN_BLOCKS),
        in_specs=[blk, pl.BlockSpec((HALO, bw), lambda i, n: (jnp.maximum(i * (tr // HALO) - 1, 0), n)),
                  pl.BlockSpec((CONV_W, bw), lambda i, n: (0, n)), vec, mat, bias, mat, bias, vec],
        out_specs=[blk] * 5,
        out_shape=[out] * 5,
        scratch_shapes=[pltpu.VMEM((tr + HALO, bw), F32)],
        compiler_params=_params("parallel", "parallel"),
        name="lru_gates_fwd",
    )(xy, xy, conv_w, conv_b.reshape(1, dr), w_ga.astype(CDT), b_ga.reshape(RNN_BLOCKS, 1, bw),
      w_gx.astype(CDT), b_gx.reshape(RNN_BLOCKS, 1, bw), lam.reshape(1, dr))


def _stack_rows(rows):
    idx = lax.broadcasted_iota(jnp.int32, (len(rows), rows[0].shape[1]), 0)
    out = jnp.broadcast_to(rows[0], idx.shape)
    for j in range(1, len(rows)):
        out = jnp.where(idx == j, jnp.broadcast_to(rows[j], idx.shape), out)
    return out


def _lru_scan_fwd(a, b, xy):
    tp, dr = a.shape
    cw = SCAN_COLS
    ycol0 = dr // cw

    def body(a_ref, b_ref, y_ref, hs_ref, m_ref):
        def group(g, h):
            base = pl.multiple_of(g * 8, 8)
            at = a_ref[pl.ds(base, 8), :]
            bt = b_ref[pl.ds(base, 8), :]
            rows = []
            for j in range(8):
                h = at[j:j + 1, :] * h + bt[j:j + 1, :]
                rows.append(h)
            hs_ref[pl.ds(base, 8), :] = _stack_rows(rows)
            return h

        lax.fori_loop(0, tp // 8, group, jnp.zeros((1, cw), F32))
        m_ref[...] = (hs_ref[...] * _gelu(y_ref[...])).astype(CDT)

    col = pl.BlockSpec((tp, cw), lambda n: (0, n))
    return pl.pallas_call(
        body,
        grid=(dr // cw,),
        in_specs=[col, col, pl.BlockSpec((tp, cw), lambda n: (0, ycol0 + n))],
        out_specs=[col, col],
        out_shape=[jax.ShapeDtypeStruct((tp, dr), F32), jax.ShapeDtypeStruct((tp, dr), CDT)],
        compiler_params=_params("parallel"),
        name="lru_scan_fwd",
    )(a, b, xy)


def _lru_scan_bwd(a, hs, dm, xy):
    tp, dr = a.shape
    cw = SCAN_COLS
    ycol0 = dr // cw
    ng = tp // 8

    def body(a_ref, hs_ref, dm_ref, y_ref, db_ref, da_ref, dy_ref):
        y = y_ref[...]
        dm = dm_ref[...]
        db_ref[...] = dm * _gelu(y)
        dy_ref[...] = (dm * hs_ref[...] * _gelu_grad(y)).astype(CDT)

        def group(k, carry):
            g_next, a_next = carry
            g = ng - 1 - k
            base = pl.multiple_of(g * 8, 8)
            prev = pl.multiple_of(jnp.maximum(g - 1, 0) * 8, 8)
            dt = db_ref[pl.ds(base, 8), :]
            at = a_ref[pl.ds(base, 8), :]
            ht = hs_ref[pl.ds(base, 8), :]
            h_before = jnp.where(g == 0, 0.0, hs_ref[pl.ds(prev, 8), :][7:8, :])
            g_rows = [None] * 8
            da_rows = [None] * 8
            for j in range(7, -1, -1):
                g_cur = dt[j:j + 1, :] + a_next * g_next
                g_rows[j] = g_cur
                da_rows[j] = g_cur * (ht[j - 1:j, :] if j > 0 else h_before)
                g_next = g_cur
                a_next = at[j:j + 1, :]
            db_ref[pl.ds(base, 8), :] = _stack_rows(g_rows)
            da_ref[pl.ds(base, 8), :] = _stack_rows(da_rows)
            return g_next, a_next

        zero = jnp.zeros((1, cw), F32)
        lax.fori_loop(0, ng, group, (zero, zero))

    col = pl.BlockSpec((tp, cw), lambda n: (0, n))
    return pl.pallas_call(
        body,
        grid=(dr // cw,),
        in_specs=[col, col, col, pl.BlockSpec((tp, cw), lambda n: (0, ycol0 + n))],
        out_specs=[col, col, col],
        out_shape=[jax.ShapeDtypeStruct((tp, dr), F32), jax.ShapeDtypeStruct((tp, dr), F32),
                   jax.ShapeDtypeStruct((tp, dr), CDT)],
        compiler_params=_params("parallel"),
        name="lru_scan_bwd",
    )(a, hs, dm, xy)


def _lru_gates_bwd(db, da, xc, r, ig, a, lam, w_ga, w_gx):
    tp, dr = xc.shape
    bw = dr // RNN_BLOCKS
    tr = _div_tile(tp, ROW_TILE, 8)
    nr = tp // tr

    def body(db_ref, da_ref, xc_ref, r_ref, i_ref, a_ref, lam_ref, wa_ref, wx_ref,
             dxc_ref, dwa_ref, dba_ref, dwx_ref, dbx_ref, dlam_ref):
        i = pl.program_id(1)
        xc = xc_ref[...]
        r = r_ref[...]
        ig = i_ref[...]
        a = a_ref[...]
        dbv = db_ref[...]
        sp = _softplus(-lam_ref[...])
        log_a = (-LRU_C * r) * sp
        s = jnp.sqrt(_one_minus_sq(log_a, a))
        d_ix = dbv * s
        d_s = dbv * (ig * xc)
        d_log_a = da_ref[...] * a - d_s * (a * a) / s
        d_r = d_log_a * (-LRU_C * sp)
        d_sp = jnp.sum(d_log_a * (-LRU_C * r), axis=0, keepdims=True)
        dzr = d_r * r * (1.0 - r)
        dzi = (d_ix * xc) * ig * (1.0 - ig)
        dzr_b = dzr.astype(CDT)
        dzi_b = dzi.astype(CDT)
        xcb = xc.astype(CDT)
        dxc_ref[...] = d_ix * ig + _dot_nt(dzr_b, wa_ref[...]) + _dot_nt(dzi_b, wx_ref[...])
        dwa = _dot_tn(xcb, dzr_b)
        dwx = _dot_tn(xcb, dzi_b)
        dba = jnp.sum(dzr, axis=0, keepdims=True)
        dbx = jnp.sum(dzi, axis=0, keepdims=True)

        @pl.when(i == 0)
        def _():
            dwa_ref[...] = dwa
            dwx_ref[...] = dwx
            dba_ref[...] = dba
            dbx_ref[...] = dbx
            dlam_ref[...] = d_sp

        @pl.when(i > 0)
        def _():
            dwa_ref[...] += dwa
            dwx_ref[...] += dwx
            dba_ref[...] += dba
            dbx_ref[...] += dbx
            dlam_ref[...] += d_sp

        @pl.when(i == nr - 1)
        def _():
            dlam_ref[...] = dlam_ref[...] * (-_sigmoid(-lam_ref[...]))

    blk = pl.BlockSpec((tr, bw), lambda n, i: (i, n))
    vec = pl.BlockSpec((1, bw), lambda n, i: (0, n))
    mat = pl.BlockSpec((None, bw, bw), lambda n, i: (n, 0, 0))
    bias = pl.BlockSpec((None, 1, bw), lambda n, i: (n, 0, 0))
    return pl.pallas_call(
        body,
        grid=(RNN_BLOCKS, nr),
        in_specs=[blk] * 6 + [vec, mat, mat],
        out_specs=[blk, mat, bias, mat, bias, vec],
        out_shape=[jax.ShapeDtypeStruct((tp, dr), F32),
                   jax.ShapeDtypeStruct((RNN_BLOCKS, bw, bw), F32), jax.ShapeDtypeStruct((RNN_BLOCKS, 1, bw), F32),
                   jax.ShapeDtypeStruct((RNN_BLOCKS, bw, bw), F32), jax.ShapeDtypeStruct((RNN_BLOCKS, 1, bw), F32),
                   jax.ShapeDtypeStruct((1, dr), F32)],
        compiler_params=_params("parallel", "arbitrary"),
        name="lru_gates_bwd",
    )(db, da, xc, r, ig, a, lam.reshape(1, dr), w_ga.astype(CDT), w_gx.astype(CDT))


def _lru_conv_bwd(dxc, xy, conv_w):
    tp, dr = dxc.shape
    bw = dr // RNN_BLOCKS
    tr = _div_tile(tp, ROW_TILE, 8)
    nr = tp // tr
    per = tr // HALO

    def body(d_ref, dnext_ref, x_ref, xprev_ref, cw_ref, dxb_ref, dcw_ref, dcb_ref, ds, xs):
        i = pl.program_id(1)
        d = d_ref[...]
        ds[0:tr, :] = d
        ds[tr:, :] = jnp.where(i == nr - 1, 0.0, dnext_ref[...])
        xs[0:HALO, :] = jnp.where(i == 0, 0.0, xprev_ref[...])
        xs[HALO:, :] = x_ref[...]
        dxb = cw_ref[0:1, :] * ds[pl.ds(CONV_W - 1, tr), :]
        for j in range(1, CONV_W):
            dxb = dxb + cw_ref[j:j + 1, :] * ds[pl.ds(CONV_W - 1 - j, tr), :]
        dxb_ref[...] = dxb.astype(CDT)
        dcb = jnp.sum(d, axis=0, keepdims=True)
        dcw = [jnp.sum(d * xs[pl.ds(HALO - CONV_W + 1 + j, tr), :], axis=0, keepdims=True) for j in range(CONV_W)]

        @pl.when(i == 0)
        def _():
            dcb_ref[...] = dcb
            for j in range(CONV_W):
                dcw_ref[j] = dcw[j]

        @pl.when(i > 0)
        def _():
            dcb_ref[...] += dcb
            for j in range(CONV_W):
                dcw_ref[j] += dcw[j]

    blk = pl.BlockSpec((tr, bw), lambda n, i: (i, n))
    return pl.pallas_call(
        body,
        grid=(RNN_BLOCKS, nr),
        in_specs=[blk, pl.BlockSpec((HALO, bw), lambda n, i: (jnp.minimum((i + 1) * per, tp // HALO - 1), n)),
                  blk, pl.BlockSpec((HALO, bw), lambda n, i: (jnp.maximum(i * per - 1, 0), n)),
                  pl.BlockSpec((CONV_W, bw), lambda n, i: (0, n))],
        out_specs=[blk, pl.BlockSpec((CONV_W, 1, bw), lambda n, i: (0, 0, n)), pl.BlockSpec((1, bw), lambda n, i: (0, n))],
        out_shape=[jax.ShapeDtypeStruct((tp, dr), CDT), jax.ShapeDtypeStruct((CONV_W, 1, dr), F32),
                   jax.ShapeDtypeStruct((1, dr), F32)],
        scratch_shapes=[pltpu.VMEM((tr + HALO, bw), F32), pltpu.VMEM((tr + HALO, bw), F32)],
        compiler_params=_params("parallel", "arbitrary"),
        name="lru_conv_bwd",
    )(dxc, dxc, xy, xy, conv_w)


def _me():
    return lax.axis_index("x"), lax.axis_index("y"), lax.axis_index("c")


def _peer(rel):
    x, y, c = _me()
    return (1 - x if rel & 4 else x, 1 - y if rel & 2 else y, 1 - c if rel & 1 else c)


def _chip_of(dev):
    return 2 * dev[0] + dev[1]


def _linear(dev):
    return 4 * dev[0] + 2 * dev[1] + dev[2]


CHIP_RELS = (4, 2, 6)
ALL_RELS = (1, 2, 3, 4, 5, 6, 7)
PAIR_RELS = (1,)


def _scatter_send(pieces, rels, piece_of, name):
    n = len(rels)

    def body(src_ref, recv_ref, send_sems, recv_sems):
        copies = []
        for k, rel in enumerate(rels):
            peer = _peer(rel)
            cp = pltpu.make_async_remote_copy(
                src_ref=src_ref.at[piece_of(peer)], dst_ref=recv_ref.at[k],
                send_sem=send_sems.at[k], recv_sem=recv_sems.at[k], device_id=peer, device_id_type=MESH)
            cp.start()
            copies.append(cp)
        for cp in copies:
            cp.wait()

    return pl.pallas_call(
        body,
        in_specs=[pl.BlockSpec(memory_space=pl.ANY)],
        out_specs=pl.BlockSpec(memory_space=pl.ANY),
        out_shape=jax.ShapeDtypeStruct((n,) + pieces.shape[1:], pieces.dtype),
        scratch_shapes=[pltpu.SemaphoreType.DMA((n,)), pltpu.SemaphoreType.DMA((n,))],
        name=name,
    )(pieces)


def _gather_send(piece, rels, n_slots, slot_of, name, n_chunks=1):
    n = len(rels)
    rows = piece.shape[0]
    if rows % (8 * n_chunks):
        n_chunks = 1
    rc = rows // n_chunks

    def body(src_ref, out_ref, send_sems, recv_sems, local_sems):
        me = _me()

        def part(ref, q):
            return ref.at[pl.ds(q * rc, rc)]

        def remote(k, q, slot_dev, to):
            return pltpu.make_async_remote_copy(
                src_ref=part(src_ref, q), dst_ref=part(out_ref.at[slot_of(slot_dev)], q),
                send_sem=send_sems.at[k * n_chunks + q], recv_sem=recv_sems.at[k * n_chunks + q],
                device_id=to, device_id_type=MESH)

        mine = [pltpu.make_async_copy(part(src_ref, q), part(out_ref.at[slot_of(me)], q), local_sems.at[q])
                for q in range(n_chunks)]
        for cp in mine:
            cp.start()
        sends = [remote(k, q, me, _peer(rel)) for k, rel in enumerate(rels) for q in range(n_chunks)]
        for cp in sends:
            cp.start()
        for k, rel in enumerate(rels):
            for q in range(n_chunks):
                remote(k, q, _peer(rel), _peer(rel)).wait_recv()
        for cp in sends:
            cp.wait_send()
        for cp in mine:
            cp.wait()

    return pl.pallas_call(
        body,
        in_specs=[pl.BlockSpec(memory_space=pl.ANY)],
        out_specs=pl.BlockSpec(memory_space=pl.ANY),
        out_shape=jax.ShapeDtypeStruct((n_slots,) + piece.shape, piece.dtype),
        scratch_shapes=[pltpu.SemaphoreType.DMA((n * n_chunks,)), pltpu.SemaphoreType.DMA((n * n_chunks,)),
                        pltpu.SemaphoreType.DMA((n_chunks,))],
        name=name,
    )(piece)


def _gather_chips(shard, name):
    return _gather_send(shard, CHIP_RELS, N_CHIPS, _chip_of, name)


HBM_SPEC = pl.BlockSpec(memory_space=pltpu.HBM)
SEM_SPEC = pl.BlockSpec(memory_space=pltpu.SEMAPHORE)
DATAFLOW = pltpu.SideEffectType.DATAFLOW_SIDE_EFFECTING


def _split_start(src, land, copies, name):
    def body(src_ref, land_ref, send_sem, recv_sem, src_thru, land_thru, token):
        for s_ref, d_ref, peer in copies(src_ref, land_ref):
            pltpu.make_async_remote_copy(src_ref=s_ref, dst_ref=d_ref, send_sem=send_sem, recv_sem=recv_sem,
                                         device_id=peer, device_id_type=MESH).start()
        token[...] = jnp.zeros(token.shape, token.dtype)

    return pl.pallas_call(
        body,
        name=name,
        out_shape=(pltpu.SemaphoreType.DMA(()), pltpu.SemaphoreType.DMA(()), pltpu.HBM(src.shape, src.dtype),
                   pltpu.HBM(land.shape, land.dtype), jax.ShapeDtypeStruct((8, LANES), F32)),
        in_specs=(HBM_SPEC, HBM_SPEC),
        out_specs=(SEM_SPEC, SEM_SPEC, HBM_SPEC, HBM_SPEC, pl.BlockSpec(memory_space=pltpu.VMEM)),
        input_output_aliases={0: 2, 1: 3},
        compiler_params=pltpu.CompilerParams(has_side_effects=DATAFLOW),
    )(pltpu.with_memory_space_constraint(src, pltpu.HBM), pltpu.with_memory_space_constraint(land, pltpu.HBM))


def _split_gather_start(piece, rels, n_slots, slot_of, name):
    land = jnp.broadcast_to(piece[None], (n_slots,) + piece.shape)
    return _split_start(piece, land, lambda s, l: [(s, l.at[slot_of(_me())], _peer(rel)) for rel in rels], name)


def _gather_chips_start(shard, name):
    return _split_gather_start(shard, CHIP_RELS, N_CHIPS, _chip_of, name)


def _split_scatter_start(pieces, name):
    land = lax.empty((len(ALL_RELS),) + pieces.shape[1:], pieces.dtype)
    return _split_start(pieces, land,
                        lambda s, l: [(s.at[_linear(_peer(rel))], l.at[k], _peer(rel)) for k, rel in enumerate(ALL_RELS)], name)


def _split_wait(started, n, after, name):
    send_sem, recv_sem, src_thru, land_thru, _ = started

    def body(src_ref, land_ref, send_sem, recv_sem, after_ref, src_dead, got_ref):
        all_n = land_ref.at[pl.ds(0, n)]
        arrivals = pltpu.make_async_remote_copy(
            src_ref=all_n, dst_ref=all_n, send_sem=send_sem, recv_sem=recv_sem, device_id=_me(), device_id_type=MESH)
        arrivals.wait_send()
        arrivals.wait_recv()

    return pl.pallas_call(
        body,
        name=name,
        out_shape=(pltpu.HBM(src_thru.shape, src_thru.dtype), pltpu.HBM(land_thru.shape, land_thru.dtype)),
        in_specs=(HBM_SPEC, HBM_SPEC, SEM_SPEC, SEM_SPEC, pl.BlockSpec(memory_space=pl.ANY)),
        out_specs=(HBM_SPEC, HBM_SPEC),
        input_output_aliases={0: 0, 1: 1},
        compiler_params=pltpu.CompilerParams(has_side_effects=DATAFLOW),
    )(src_thru, land_thru, send_sem, recv_sem, after)[1]


def _sum_pieces(pieces, recv, name):
    _, rr, cc = pieces.shape
    n = recv.shape[0]
    tr = _div_tile(rr, max(8, (1 << 17) // cc // 8 * 8), 8)

    def body(own_ref, recv_ref, o_ref):
        acc = own_ref[...].astype(F32)
        for k in range(n):
            acc = acc + recv_ref[k].astype(F32)
        o_ref[...] = acc

    return pl.pallas_call(
        body,
        grid=(rr // tr,),
        in_specs=[pl.BlockSpec((None, tr, cc), lambda i: (_linear(_me()), i, 0)),
                  pl.BlockSpec((n, tr, cc), lambda i: (0, i, 0))],
        out_specs=pl.BlockSpec((tr, cc), lambda i: (i, 0)),
        out_shape=jax.ShapeDtypeStruct((rr, cc), F32),
        compiler_params=_params("parallel"),
        name=name,
    )(pieces, recv)


def _reduce_to_owner(g8, payload_dtype, name):
    recv = _scatter_send(g8.astype(payload_dtype), ALL_RELS, _linear, name + "_scatter")
    return _sum_pieces(g8, recv, name + "_sum")


def _adamw_layer(w, g, m, v, outs, layer, name):
    nl, rr, cc = w.shape
    _, gr, gc = g.shape
    tr = _div_tile(gr, max(8, (1 << 17) // gc // 8 * 8), 8)
    steps = gr // tr
    c1 = 1.0 - ADAM_B1 ** ADAM_STEP
    c2 = 1.0 - ADAM_B2 ** ADAM_STEP
    if gc == cc:
        assert 2 * gr == rr, (name, g.shape, w.shape)
        slab = pl.BlockSpec((None, tr, gc), lambda h, i: (layer, h * steps + i, 0))
    else:
        assert gr == rr and 2 * gc == cc, (name, g.shape, w.shape)
        slab = pl.BlockSpec((None, tr, gc), lambda h, i: (layer, i, h))

    def body(w_ref, g_ref, m_ref, v_ref, *rest):
        go_ref, d_ref, mo_ref, vo_ref = rest[-4:]
        g_ = g_ref[...]
        m_ = ADAM_B1 * m_ref[...] + (1.0 - ADAM_B1) * g_
        v_ = ADAM_B2 * v_ref[...] + (1.0 - ADAM_B2) * (g_ * g_)
        go_ref[...] = g_
        d_ref[...] = -ADAM_LR * ((m_ / c1) / (jnp.sqrt(v_ / c2) + ADAM_EPS) + ADAM_WD * w_ref[...])
        mo_ref[...] = m_
        vo_ref[...] = v_

    out = jax.ShapeDtypeStruct((nl, rr, cc), F32)
    in_specs = [slab, pl.BlockSpec((None, tr, gc), lambda h, i: (h, i, 0)), slab, slab]
    args = [w, g, m, v]
    aliases = {}
    if outs is not None:
        in_specs += [pl.BlockSpec(memory_space=pl.ANY)] * 4
        args += list(outs)
        aliases = {4 + k: k for k in range(4)}
    return pl.pallas_call(
        body,
        grid=(2, steps),
        in_specs=in_specs,
        out_specs=[slab] * 4,
        out_shape=[out] * 4,
        input_output_aliases=aliases,
        compiler_params=_params("parallel", "parallel"),
        name=name,
    )(*args)


def _adamw(w, g, m, v, name):
    rr, cc = w.shape
    tr = _div_tile(rr, max(8, (1 << 17) // cc // 8 * 8), 8)
    c1 = 1.0 - ADAM_B1 ** ADAM_STEP
    c2 = 1.0 - ADAM_B2 ** ADAM_STEP

    def body(w_ref, g_ref, m_ref, v_ref, d_ref, mo_ref, vo_ref):
        g_ = g_ref[...]
        m_ = ADAM_B1 * m_ref[...] + (1.0 - ADAM_B1) * g_
        v_ = ADAM_B2 * v_ref[...] + (1.0 - ADAM_B2) * (g_ * g_)
        d_ref[...] = -ADAM_LR * ((m_ / c1) / (jnp.sqrt(v_ / c2) + ADAM_EPS) + ADAM_WD * w_ref[...])
        mo_ref[...] = m_
        vo_ref[...] = v_

    blk = pl.BlockSpec((tr, cc), lambda i: (i, 0))
    out = jax.ShapeDtypeStruct((rr, cc), F32)
    return pl.pallas_call(
        body,
        grid=(rr // tr,),
        in_specs=[blk] * 4,
        out_specs=[blk] * 3,
        out_shape=[out] * 3,
        compiler_params=_params("parallel"),
        name=name,
    )(w, g, m, v)


def _pack(arrays, cols, row_mult):
    flat = jnp.concatenate([a.reshape(-1) for a in arrays])
    rows = -(-flat.shape[0] // cols)
    rows = -(-rows // row_mult) * row_mult
    return jnp.pad(flat, (0, rows * cols - flat.shape[0])).reshape(rows, cols)


def _unpack(buf, shapes):
    flat = buf.reshape(-1)
    out, off = [], 0
    for s in shapes:
        n = math.prod(s)
        out.append(flat[off:off + n].reshape(s))
        off += n
    return out


def kernel(x, meta_tokens, norm_mix, norm_ffn, norm_final, mla_w_in, mla_q_norm, mla_kv_norm, mla_w_uq, mla_w_ukv, mla_w_o, lru_w_in, lru_conv_w, lru_conv_b, lru_w_gate_a, lru_b_gate_a, lru_w_gate_x, lru_b_gate_x, lru_lambda, lru_w_o, ffn_w_gu, ffn_w_down, loss_target, m_meta_tokens, m_norm_mix, m_norm_ffn, m_norm_final, m_mla_w_in, m_mla_q_norm, m_mla_kv_norm, m_mla_w_uq, m_mla_w_ukv, m_mla_w_o, m_lru_w_in, m_lru_conv_w, m_lru_conv_b, m_lru_w_gate_a, m_lru_b_gate_a, m_lru_w_gate_x, m_lru_b_gate_x, m_lru_lambda, m_lru_w_o, m_ffn_w_gu, m_ffn_w_down, v_meta_tokens, v_norm_mix, v_norm_ffn, v_norm_final, v_mla_w_in, v_mla_q_norm, v_mla_kv_norm, v_mla_w_uq, v_mla_w_ukv, v_mla_w_o, v_lru_w_in, v_lru_conv_w, v_lru_conv_b, v_lru_w_gate_a, v_lru_b_gate_a, v_lru_w_gate_x, v_lru_b_gate_x, v_lru_lambda, v_lru_w_o, v_ffn_w_gu, v_ffn_w_down):
    d = D_MODEL
    t_real = N_META + SEQ
    tp = _t_pad()
    n_mla = mla_w_in.shape[0]
    n_lru = lru_w_in.shape[0]
    h_dim = MLA_HEADS * V_HEAD
    w_in_cols = Q_LORA + KV_LORA + QK_ROPE
    w_in_pad = Q_LORA + KV_LORA + LANES
    q_cols = MLA_HEADS * (QK_NOPE + QK_ROPE)
    tmm = _div_tile(tp, MM_ROW_TILE, 16)
    tkt = _div_tile(tp, 1408, 16)

    def tile(n, pref):
        return _div_tile(n, pref, LANES)

    small_shapes = [meta_tokens.shape, lru_conv_w.shape, lru_conv_b.shape, lru_lambda.shape]
    csh = meta_tokens.shape[1]
    small4 = _gather_chips(_pack([meta_tokens, lru_conv_w, lru_conv_b, lru_lambda], csh, 16), "gather_small")
    small4, mla_w_in = lax.optimization_barrier((small4, mla_w_in))
    started = {}

    def start(key, shard):
        prev = list(started.values())[-1][4][0, 0] if started else 0.0
        started[key] = _gather_chips_start((shard + prev).astype(CDT), "gather_" + key + "_start")

    def arrived(key, after):
        return _split_wait(started[key], len(CHIP_RELS), after, "gather_" + key + "_wait")

    def start_ffn(layer):
        start(f"w_gu{layer}", ffn_w_gu[layer:layer + 1])
        start(f"w_down{layer}", ffn_w_down[layer:layer + 1])

    start("w_in", jnp.pad(mla_w_in, ((0, 0), (0, 0), (0, w_in_pad - w_in_cols))))
    start("w_uq", mla_w_uq)
    start("w_ukv", mla_w_ukv)
    start("w_o", mla_w_o)
    start_ffn(0)
    start("lw_in", lru_w_in)
    start("lw_o", lru_w_o)
    for layer in range(1, DEPTH):
        start_ffn(layer)
    all_started = list(started.values())[-1][4][0, 0]
    n_gu = ffn_w_gu.shape[2]
    w_gu4, w_down4 = [None] * DEPTH, [None] * DEPTH
    small_full = [jnp.concatenate(parts, axis=-1) for parts in zip(*[_unpack(small4[k], small_shapes) for k in range(N_CHIPS)])]
    meta_full, conv_w_full, conv_b_full, lam_full = small_full

    cos2, sin2 = _rope_tables(tp)

    h = jnp.concatenate([meta_full, x[0], jnp.zeros((tp - t_real, d), F32)], axis=0) + all_started
    saved = []
    for layer in range(DEPTH):
        j = layer // 2
        s = {"h_in": h}
        hn = _rms_fwd(h, norm_mix[layer], width=d, col_block=0, name="norm_mix_fwd")
        s["hn"] = hn
        if layer == 0:
            w_in4, w_uq4, w_ukv4, w_o4 = (arrived(k, hn) for k in ("w_in", "w_uq", "w_ukv", "w_o"))
            w_uq_full = jnp.moveaxis(w_uq4, 0, 2).reshape(n_mla, Q_LORA, MLA_HEADS, QK_NOPE + QK_ROPE)
            w_uq_perm = jnp.concatenate([w_uq_full[..., :QK_NOPE].reshape(n_mla, Q_LORA, -1),
                                         w_uq_full[..., QK_NOPE:].reshape(n_mla, Q_LORA, -1)], axis=-1)
        if layer == 1:
            lw_in4, lw_o4 = arrived("lw_in", hn), arrived("lw_o", hn)
        if layer % 2 == 0:
            proj = _mm("nn", hn, w_in4, kind="row", layer=j, tm=tmm, tn=tile(w_in_pad, 1152), tk=tile(d // N_CHIPS, 512), name="mla_in")
            c_q = _rms_fwd(proj, mla_q_norm[j], width=Q_LORA, col_block=0, name="q_norm_fwd")
            c_kv = _rms_fwd(proj, mla_kv_norm[j], width=KV_LORA, col_block=Q_LORA // KV_LORA, name="kv_norm_fwd")
            q = _mm("nn", c_q, w_uq_perm[j], tm=tmm, tn=tile(q_cols, 1024), tk=Q_LORA, name="mla_uq")
            kv = _mm("nn", c_kv, w_ukv4, kind="col", layer=j, out_dtype=CDT, tm=tmm, tn=tile(w_ukv4.shape[3], 1024), tk=KV_LORA, name="mla_ukv")
            qp, kp = _mla_prep(q, kv, proj, cos2, sin2)
            att, lse = _attn_fwd(qp, kp, kv)
            h = _mm("nn", att, w_o4, kind="row", layer=j, resid=h, tm=tmm, tn=tile(d, 1024), tk=tile(h_dim // N_CHIPS, 512), name="mla_out")
            s.update(proj=proj, c_q=c_q, c_kv=c_kv, qp=qp, kp=kp, kv=kv, att=att, lse=lse)
        else:
            xy = _mm("nn", hn, lw_in4, kind="col", layer=j, tm=tmm, tn=tile(lw_in4.shape[3], 1024), tk=d, name="lru_in")
            xc, r, ig, a, b = _lru_gates_fwd(xy, conv_w_full[j], conv_b_full[j], lru_w_gate_a[j], lru_b_gate_a[j],
                                             lru_w_gate_x[j], lru_b_gate_x[j], lam_full[j])
            hs, mixed = _lru_scan_fwd(a, b, xy)
            h = _mm("nn", mixed, lw_o4, kind="row", layer=j, resid=h, tm=tmm, tn=tile(d, 1024), tk=tile(d // N_CHIPS, 512), name="lru_out")
            s.update(xy=xy, xc=xc, r=r, ig=ig, a=a, hs=hs, mixed=mixed)
        s["h_mid"] = h
        hn2 = _rms_fwd(h, norm_ffn[layer], width=d, col_block=0, name="norm_ffn_fwd")
        w_gu4[layer], w_down4[layer] = arrived(f"w_gu{layer}", hn2), arrived(f"w_down{layer}", hn2)
        gu = _mm("nn", hn2, w_gu4[layer], kind="col", tm=tmm, tn=tile(n_gu, 1408), tk=d, name="ffn_gu")
        act = _swiglu_fwd(gu)
        h = _mm("nn", act, w_down4[layer], kind="row", resid=h, tm=tmm, tn=tile(d, 1024), tk=tile(D_FF // N_CHIPS, 1408), name="ffn_down")
        s.update(hn2=hn2, gu=gu, act=act)
        saved.append(s)

    target = jnp.concatenate([jnp.zeros((N_META, d), F32), loss_target[0], jnp.zeros((tp - t_real, d), F32)], axis=0)
    dh, dhb, g_norm_final, loss_part = _final_loss(h, norm_final, target)
    loss = lax.psum(loss_part[0, 0], ("x", "y", "c"))

    g_norm_mix, g_norm_ffn = [None] * DEPTH, [None] * DEPTH
    g_q_norm, g_kv_norm = [None] * n_mla, [None] * n_mla
    g_w_uq = [None] * n_mla
    g_gate = {k: [None] * n_lru for k in ("wa", "ba", "wx", "bx", "lam", "cw", "cb")}
    weights = {"w_in": (mla_w_in, m_mla_w_in, v_mla_w_in), "w_uq": (mla_w_uq, m_mla_w_uq, v_mla_w_uq),
               "w_ukv": (mla_w_ukv, m_mla_w_ukv, v_mla_w_ukv), "w_o": (mla_w_o, m_mla_w_o, v_mla_w_o),
               "lw_in": (lru_w_in, m_lru_w_in, v_lru_w_in), "lw_o": (lru_w_o, m_lru_w_o, v_lru_w_o),
               "w_gu": (ffn_w_gu, m_ffn_w_gu, v_ffn_w_gu), "w_down": (ffn_w_down, m_ffn_w_down, v_ffn_w_down)}
    res = {key: None for key in weights}
    units = []

    def reduce_start(key, lyr, pieces):
        tag = f"{key}{lyr}"
        units.append({"key": key, "layer": lyr, "tag": tag, "pieces": pieces, "stage": 0, "age": 0,
                      "copy": _split_scatter_start(pieces, "reduce_" + tag + "_scatter_start")})
        return units[-1]["copy"][4]

    def reduce_advance(after, everything=False):
        tokens = []
        for u in units:
            key, lyr, tag = u["key"], u["layer"], u["tag"]
            if u["stage"] == 1:
                both = _split_wait(u["copy"], 1, after, "reduce_" + tag + "_pair_wait")
                g = both[:, :, :w_in_cols] if key == "w_in" else both
                w, m, v = weights[key]
                res[key] = _adamw_layer(w, g, m, v, res[key], lyr, "adamw_" + key)
                u["stage"] = 2
            elif u["stage"] == 0 and (u["age"] > 0 or everything):
                recv = _split_wait(u["copy"], len(ALL_RELS), after, "reduce_" + tag + "_scatter_wait")
                red = _sum_pieces(u["pieces"], recv, "reduce_" + key + "_sum")
                u["copy"] = _split_gather_start(red, PAIR_RELS, 2, lambda dev: dev[2], "reduce_" + tag + "_pair_start")
                tokens.append(u["copy"][4])
                u["stage"] = 1
            u["age"] += 1
        return tokens

    def grad_w(key, a_op, b_op, kind, lyr, tm, tn):
        return reduce_start(key, lyr, _mm_tn(a_op, b_op, kind=kind, tm=tm, tn=tn, tk=tkt, name="grad_" + key))

    nope_w = MLA_HEADS * QK_NOPE
    for layer in reversed(range(DEPTH)):
        tokens = []
        j = layer // 2
        s = saved[layer]
        tokens.append(grad_w("w_down", s["act"], dhb, "row_colhalves", layer, tile(D_FF // N_CHIPS, 1408), tile(d // 2, 1024)))
        d_act = _mm("nt", dhb, w_down4[layer], kind="row", tm=tmm, tn=tile(D_FF // N_CHIPS, 1408), tk=d, name="ffn_down_bwd")
        dgu = _swiglu_bwd(s["gu"], d_act)
        tokens.append(grad_w("w_gu", s["hn2"], dgu, "col", layer, tile(d // 2, 1024), tile(n_gu, 1408)))
        dhn2 = _mm("nt", dgu, w_gu4[layer], kind="col", tm=tmm, tn=tile(d, 1024), tk=tile(n_gu, 1408), name="ffn_gu_bwd")
        dh, dhb, g_norm_ffn[layer] = _rms_bwd(s["h_mid"], norm_ffn[layer], dhn2, dh, width=d, col_block=0, name="norm_ffn_bwd")
        if layer % 2 == 0:
            tokens.append(grad_w("w_o", s["att"], dhb, "row", j, tile(h_dim // N_DEV, 256), tile(d, 1024)))
            d_att = _mm("nt", dhb, w_o4, kind="row", layer=j, tm=tmm, tn=tile(h_dim // N_CHIPS, 512), tk=d, name="mla_out_bwd")
            delta = _attn_delta(d_att, s["att"])
            dqn, dqr_h, dkv, dkr_h = _attn_bwd(s["qp"], s["kp"], s["kv"], d_att, s["lse"], delta)
            dqr, dkr = _mla_unprep(dqr_h, dkr_h, cos2, sin2)
            dq = jnp.concatenate([dqn, dqr], axis=-1)
            g_uq = _mm_tn(s["c_q"], dq, tm=Q_LORA, tn=tile(q_cols, 1024), tk=tkt, name="grad_w_uq")
            g_uq = jnp.concatenate([g_uq[:, :nope_w].reshape(Q_LORA, MLA_HEADS, QK_NOPE),
                                    g_uq[:, nope_w:].reshape(Q_LORA, MLA_HEADS, QK_ROPE)], axis=-1)
            g_uq = g_uq.reshape(2, Q_LORA // 2, N_CHIPS, q_cols // N_CHIPS).transpose(2, 0, 1, 3)
            tokens.append(reduce_start("w_uq", j, g_uq.reshape(N_DEV, Q_LORA // 2, q_cols // N_CHIPS).astype(CDT)))
            dc_q = _mm("nt", dq, w_uq_perm[j], tm=tmm, tn=Q_LORA, tk=tile(q_cols, 1024), name="mla_uq_bwd")
            tokens.append(grad_w("w_ukv", s["c_kv"], dkv, "col", j, tile(KV_LORA // 2, 256), tile(w_ukv4.shape[3], 1024)))
            dc_kv = _mm("nt", dkv, w_ukv4, kind="col", layer=j, tm=tmm, tn=KV_LORA, tk=tile(w_ukv4.shape[3], 1024), name="mla_ukv_bwd")
            dpq, _, g_q_norm[j] = _rms_bwd(s["proj"], mla_q_norm[j], dc_q, None, width=Q_LORA, col_block=0, name="q_norm_bwd")
            dpkv, _, g_kv_norm[j] = _rms_bwd(s["proj"], mla_kv_norm[j], dc_kv, None, width=KV_LORA, col_block=Q_LORA // KV_LORA, name="kv_norm_bwd")
            dproj = jnp.concatenate([dpq, dpkv, dkr], axis=-1).astype(CDT)
            tokens.append(grad_w("w_in", s["hn"], dproj, "row", j, tile(d // N_DEV, 256), tile(w_in_pad, 1152)))
            dhn = _mm("nt", dproj, w_in4, kind="row", layer=j, tm=tmm, tn=tile(d // N_CHIPS, 512), tk=tile(w_in_pad, 1152), name="mla_in_bwd")
        else:
            tokens.append(grad_w("lw_o", s["mixed"], dhb, "row", j, tile(d // N_DEV, 256), tile(d, 1024)))
            dm = _mm("nt", dhb, lw_o4, kind="row", layer=j, tm=tmm, tn=tile(d // N_CHIPS, 512), tk=d, name="lru_out_bwd")
            db, da, dy = _lru_scan_bwd(s["a"], s["hs"], dm, s["xy"])
            dxc, g_gate["wa"][j], g_gate["ba"][j], g_gate["wx"][j], g_gate["bx"][j], g_gate["lam"][j] = _lru_gates_bwd(
                db, da, s["xc"], s["r"], s["ig"], s["a"], lam_full[j], lru_w_gate_a[j], lru_w_gate_x[j])
            dxb, g_gate["cw"][j], g_gate["cb"][j] = _lru_conv_bwd(dxc, s["xy"], conv_w_full[j])
            dxy = jnp.concatenate([dxb, dy], axis=-1)
            tokens.append(grad_w("lw_in", s["hn"], dxy, "col", j, tile(d // 2, 1024), tile(lw_in4.shape[3], 1024)))
            dhn = _mm("nt", dxy, lw_in4, kind="col", layer=j, tm=tmm, tn=tile(d, 1024), tk=tile(lw_in4.shape[3], 1024), name="lru_in_bwd")
        dh, dhb, g_norm_mix[layer] = _rms_bwd(s["h_in"], norm_mix[layer], dhn, dh, width=d, col_block=0, name="norm_mix_bwd")
        tokens += reduce_advance(dh)
        if layer > 0:
            dhb = dhb + sum(tok[0, 0] for tok in tokens).astype(CDT)
        else:
            dh = dh + sum(tok[0, 0] for tok in tokens)

    grad_x = dh[N_META:t_real][None]
    g_meta_full = dh[:N_META]

    for _ in range(3):
        reduce_advance(dh, everything=True)

    g_small_full = [g_meta_full, jnp.stack(g_gate["cw"]).reshape(n_lru, CONV_W, d), jnp.stack(g_gate["cb"]).reshape(n_lru, d),
                    jnp.stack(g_gate["lam"]).reshape(n_lru, d)]
    g_small4 = jnp.stack([_pack([a[..., k * csh:(k + 1) * csh] for a in g_small_full], csh, 16) for k in range(N_CHIPS)])
    rows_s = g_small4.shape[1]
    red = _reduce_to_owner(g_small4.reshape(N_DEV, rows_s // 2, csh), F32, "reduce_small")
    g_small = _gather_send(red, PAIR_RELS, 2, lambda dev: dev[2], "reduce_small_pair").reshape(rows_s, csh)
    small_w = [meta_tokens, lru_conv_w, lru_conv_b, lru_lambda]
    small_m = [m_meta_tokens, m_lru_conv_w, m_lru_conv_b, m_lru_lambda]
    small_v = [v_meta_tokens, v_lru_conv_w, v_lru_conv_b, v_lru_lambda]
    sd, sm, sv = _adamw(_pack(small_w, csh, 16), g_small, _pack(small_m, csh, 16), _pack(small_v, csh, 16), "adamw_small")
    small_out = [_unpack(buf, small_shapes) for buf in (g_small, sd, sm, sv)]

    rep_w = [norm_mix, norm_ffn, norm_final, mla_q_norm, mla_kv_norm, lru_w_gate_a, lru_b_gate_a, lru_w_gate_x, lru_b_gate_x]
    rep_m = [m_norm_mix, m_norm_ffn, m_norm_final, m_mla_q_norm, m_mla_kv_norm, m_lru_w_gate_a, m_lru_b_gate_a, m_lru_w_gate_x, m_lru_b_gate_x]
    rep_v = [v_norm_mix, v_norm_ffn, v_norm_final, v_mla_q_norm, v_mla_kv_norm, v_lru_w_gate_a, v_lru_b_gate_a, v_lru_w_gate_x, v_lru_b_gate_x]
    rep_g = [jnp.stack(g_norm_mix), jnp.stack(g_norm_ffn), g_norm_final, jnp.stack(g_q_norm), jnp.stack(g_kv_norm),
             jnp.stack(g_gate["wa"]), jnp.stack(g_gate["ba"]), jnp.stack(g_gate["wx"]), jnp.stack(g_gate["bx"])]
    rep_shapes = [w.shape for w in rep_w]
    g_rep = _pack(rep_g, LANES, 8 * N_DEV)
    rows_r = g_rep.shape[0]
    red = _reduce_to_owner(g_rep.reshape(N_DEV, rows_r // N_DEV, LANES), F32, "reduce_rep")
    g_rep = _gather_send(red, ALL_RELS, N_DEV, _linear, "reduce_rep_all").reshape(rows_r, LANES)
    rd, rm, rv = _adamw(_pack(rep_w, LANES, 8 * N_DEV), g_rep, _pack(rep_m, LANES, 8 * N_DEV), _pack(rep_v, LANES, 8 * N_DEV), "adamw_rep")
    rep_out = [_unpack(buf, rep_shapes) for buf in (g_rep, rd, rm, rv)]

    def leaf(kind):
        s_, r_ = small_out[kind], rep_out[kind]
        return [s_[0], r_[0], r_[1], r_[2], res["w_in"][kind], r_[3], r_[4], res["w_uq"][kind], res["w_ukv"][kind],
                res["w_o"][kind], res["lw_in"][kind], s_[1], s_[2], r_[5], r_[6], r_[7], r_[8], s_[3],
                res["lw_o"][kind], res["w_gu"][kind], res["w_down"][kind]]

    return (loss, grad_x, *leaf(0), *leaf(1), *leaf(2), *leaf(3))
```

```python
import math

import jax
import jax.numpy as jnp
from jax import lax
from jax.experimental import pallas as pl
from jax.experimental.pallas import tpu as pltpu

F32 = jnp.float32
CDT = jnp.bfloat16
MESH = pl.DeviceIdType.MESH

D_MODEL = 2048
SEQ = 4096
DEPTH = 4
CHUNK = 64
N_META = 16
MLA_HEADS = 16
Q_LORA = 512
KV_LORA = 512
QK_NOPE = 128
QK_ROPE = 64
V_HEAD = 128
ROPE_THETA = 10000.0
RNN_BLOCKS = 16
CONV_W = 4
LRU_C = 8.0
D_FF = 5632
RMS_EPS = 1e-6
NEG_BIG = -1e30
ADAM_LR = 0.001
ADAM_B1 = 0.9
ADAM_B2 = 0.999
ADAM_EPS = 1e-08
ADAM_WD = 0.01
ADAM_STEP = 10

N_CHIPS = 4
N_DEV = 8
LANES = 128
VMEM_LIMIT = 52 * 1024 * 1024
ROW_TILE = 384
MM_ROW_TILE = 704
ATT_TILE = 384
SCAN_COLS = 128
ATT_STRIP = 32


def _div_tile(n, pref, mult):
    if n <= pref:
        return n
    d = (pref // mult) * mult
    while d >= mult:
        if n % d == 0:
            return d
        d -= mult
    raise ValueError(f"no tile for {n} <= {pref} (multiple of {mult})")


def _t_pad():
    t = N_META + SEQ
    step = math.lcm(_row_tile_unit(), 8)
    return -(-t // step) * step


def _row_tile_unit():
    return math.lcm(math.lcm(ROW_TILE, MM_ROW_TILE), ATT_TILE)


def _params(*sem):
    return pltpu.CompilerParams(dimension_semantics=sem, vmem_limit_bytes=VMEM_LIMIT)


def _b_spec(form, b, kind, layer, t_out, t_con):
    if kind == "plain":
        if form == "nn":
            return pl.BlockSpec((t_con, t_out), lambda i, j, k: (k, j))
        return pl.BlockSpec((t_out, t_con), lambda i, j, k: (j, k))
    rows, cols = b.shape[2], b.shape[3]
    if form == "nn":
        blk = (None, None, t_con, t_out)
        if kind == "row":
            per = rows // t_con
            return pl.BlockSpec(blk, lambda i, j, k: (k // per, layer, k % per, j))
        per = cols // t_out
        return pl.BlockSpec(blk, lambda i, j, k: (j // per, layer, k, j % per))
    blk = (None, None, t_out, t_con)
    if kind == "row":
        per = rows // t_out
        return pl.BlockSpec(blk, lambda i, j, k: (j // per, layer, j % per, k))
    per = cols // t_con
    return pl.BlockSpec(blk, lambda i, j, k: (k // per, layer, j, k % per))


def _mm_body(nk, dims, has_resid):
    def body(*refs):
        if has_resid:
            a_ref, b_ref, r_ref, o_ref = refs[:4]
        else:
            a_ref, b_ref, o_ref = refs[:3]
        prod = lax.dot_general(a_ref[...].astype(CDT), b_ref[...].astype(CDT), (dims, ((), ())),
                               preferred_element_type=F32)

        def finish(acc):
            if has_resid:
                acc = acc + r_ref[...]
            o_ref[...] = acc.astype(o_ref.dtype)

        if nk == 1:
            finish(prod)
            return
        acc_ref = refs[-1]
        k = pl.program_id(2)

        @pl.when(k == 0)
        def _():
            acc_ref[...] = prod

        @pl.when(k > 0)
        def _():
            acc_ref[...] += prod

        @pl.when(k == nk - 1)
        def _():
            finish(acc_ref[...])

    return body


def _mm(form, a, b, *, kind="plain", layer=0, out_dtype=F32, resid=None, tm, tn, tk, name):
    m, con = a.shape
    if kind == "plain":
        w_rows, w_cols = b.shape
    elif kind == "row":
        w_rows, w_cols = b.shape[0] * b.shape[2], b.shape[3]
    else:
        w_rows, w_cols = b.shape[2], b.shape[0] * b.shape[3]
    n_out = w_cols if form == "nn" else w_rows
    assert con == (w_rows if form == "nn" else w_cols), (name, a.shape, b.shape)
    nk = con // tk
    assert m % tm == 0 and n_out % tn == 0 and con % tk == 0, (name, m, n_out, con, tm, tn, tk)
    dims = ((1,), (0,)) if form == "nn" else ((1,), (1,))
    in_specs = [pl.BlockSpec((tm, tk), lambda i, j, k: (i, k)), _b_spec(form, b, kind, layer, tn, tk)]
    args = [a, b]
    if resid is not None:
        in_specs.append(pl.BlockSpec((tm, tn), lambda i, j, k: (i, j)))
        args.append(resid)
    return pl.pallas_call(
        _mm_body(nk, dims, resid is not None),
        grid=(m // tm, n_out // tn, nk),
        in_specs=in_specs,
        out_specs=pl.BlockSpec((tm, tn), lambda i, j, k: (i, j)),
        out_shape=jax.ShapeDtypeStruct((m, n_out), out_dtype),
        scratch_shapes=[pltpu.VMEM((tm, tn), F32)] if nk > 1 else [],
        compiler_params=_params("parallel", "parallel", "arbitrary"),
        name=name,
    )(*args)


def _mm_tn(a, b, *, kind="plain", tm, tn, tk, name):
    t, m = a.shape
    n = b.shape[1]
    nk = t // tk
    assert t % tk == 0 and m % tm == 0 and n % tn == 0, (name, t, m, n, tm, tn, tk)
    if kind == "plain":
        out_shape = jax.ShapeDtypeStruct((m, n), F32)
        out_spec = pl.BlockSpec((tm, tn), lambda i, j, k: (i, j))
    elif kind == "row":
        per = (m // N_CHIPS) // tm
        assert per >= 2 and per % 2 == 0, (name, per)
        out_shape = jax.ShapeDtypeStruct((N_DEV, m // N_DEV, n), CDT)
        out_spec = pl.BlockSpec((None, tm, tn), lambda i, j, k: (2 * (i // per) + (i % per) // (per // 2), (i % per) % (per // 2), j))
    elif kind == "row_colhalves":
        per = (m // N_CHIPS) // tm
        nt = n // tn
        assert per >= 1 and nt % 2 == 0, (name, per, nt)
        out_shape = jax.ShapeDtypeStruct((N_DEV, m // N_CHIPS, n // 2), CDT)
        out_spec = pl.BlockSpec((None, tm, tn), lambda i, j, k: (2 * (i // per) + j // (nt // 2), i % per, j % (nt // 2)))
    else:
        per = (n // N_CHIPS) // tn
        mt = m // tm
        assert per >= 1 and mt % 2 == 0, (name, per, mt)
        out_shape = jax.ShapeDtypeStruct((N_DEV, m // 2, n // N_CHIPS), CDT)
        out_spec = pl.BlockSpec((None, tm, tn), lambda i, j, k: (2 * (j // per) + i // (mt // 2), i % (mt // 2), j % per))
    return pl.pallas_call(
        _mm_body(nk, ((0,), (0,)), False),
        grid=(m // tm, n // tn, nk),
        in_specs=[pl.BlockSpec((tk, tm), lambda i, j, k: (k, i)), pl.BlockSpec((tk, tn), lambda i, j, k: (k, j))],
        out_specs=out_spec,
        out_shape=out_shape,
        scratch_shapes=[pltpu.VMEM((tm, tn), F32)] if nk > 1 else [],
        compiler_params=_params("parallel", "parallel", "arbitrary"),
        name=name,
    )(a, b)


def _rms_fwd(x, g, *, width, col_block, name):
    tp = x.shape[0]
    tr = _div_tile(tp, ROW_TILE, 8)

    def body(x_ref, g_ref, o_ref):
        xf = x_ref[...]
        r = lax.rsqrt(jnp.mean(xf * xf, axis=-1, keepdims=True) + RMS_EPS)
        o_ref[...] = ((xf * r) * g_ref[...]).astype(o_ref.dtype)

    return pl.pallas_call(
        body,
        grid=(tp // tr,),
        in_specs=[pl.BlockSpec((tr, width), lambda i: (i, col_block)), pl.BlockSpec((1, width), lambda i: (0, 0))],
        out_specs=pl.BlockSpec((tr, width), lambda i: (i, 0)),
        out_shape=jax.ShapeDtypeStruct((tp, width), CDT),
        compiler_params=_params("parallel"),
        name=name,
    )(x, g.reshape(1, width))


def _rms_bwd(x, g, dy, resid, *, width, col_block, name):
    tp = x.shape[0]
    tr = _div_tile(tp, ROW_TILE, 8)
    has_resid = resid is not None

    def body(*refs):
        if has_resid:
            x_ref, g_ref, dy_ref, res_ref, dx_ref, dxb_ref, dg_ref = refs
        else:
            x_ref, g_ref, dy_ref, dx_ref, dxb_ref, dg_ref = refs
        i = pl.program_id(0)
        xf = x_ref[...]
        r = lax.rsqrt(jnp.mean(xf * xf, axis=-1, keepdims=True) + RMS_EPS)
        xh = xf * r
        dy = dy_ref[...].astype(F32)
        dg = jnp.sum(dy * xh, axis=0, keepdims=True)
        dxh = dy * g_ref[...]
        dx = r * (dxh - xh * jnp.mean(dxh * xh, axis=-1, keepdims=True))
        if has_resid:
            dx = dx + res_ref[...]
        dx_ref[...] = dx
        dxb_ref[...] = dx.astype(CDT)

        @pl.when(i == 0)
        def _():
            dg_ref[...] = dg

        @pl.when(i > 0)
        def _():
            dg_ref[...] += dg

    row = pl.BlockSpec((tr, width), lambda i: (i, 0))
    in_specs = [pl.BlockSpec((tr, width), lambda i: (i, col_block)), pl.BlockSpec((1, width), lambda i: (0, 0)), row]
    args = [x, g.reshape(1, width), dy]
    if has_resid:
        in_specs.append(row)
        args.append(resid)
    return pl.pallas_call(
        body,
        grid=(tp // tr,),
        in_specs=in_specs,
        out_specs=[row, row, pl.BlockSpec((1, width), lambda i: (0, 0))],
        out_shape=[jax.ShapeDtypeStruct((tp, width), F32), jax.ShapeDtypeStruct((tp, width), CDT),
                   jax.ShapeDtypeStruct((1, width), F32)],
        compiler_params=_params("arbitrary"),
        name=name,
    )(*args)


def _final_loss(h, g, target):
    tp, d = h.shape
    tr = _div_tile(tp, ROW_TILE, 8)

    def body(h_ref, g_ref, t_ref, dh_ref, dhb_ref, dg_ref, loss_ref):
        i = pl.program_id(0)
        xf = h_ref[...]
        r = lax.rsqrt(jnp.mean(xf * xf, axis=-1, keepdims=True) + RMS_EPS)
        xh = xf * r
        gain = g_ref[...]
        y = xh * gain
        rows = i * tr + lax.broadcasted_iota(jnp.int32, (tr, 1), 0)
        valid = jnp.logical_and(rows >= N_META, rows < N_META + SEQ)
        err = jnp.where(valid, y - t_ref[...], 0.0)
        part = 0.5 * jnp.sum(jnp.mean(err * err, axis=-1, keepdims=True), axis=0, keepdims=True)
        dy = err * (1.0 / d)
        dg = jnp.sum(dy * xh, axis=0, keepdims=True)
        dxh = dy * gain
        dx = r * (dxh - xh * jnp.mean(dxh * xh, axis=-1, keepdims=True))
        dh_ref[...] = dx
        dhb_ref[...] = dx.astype(CDT)

        @pl.when(i == 0)
        def _():
            dg_ref[...] = dg
            loss_ref[...] = part

        @pl.when(i > 0)
        def _():
            dg_ref[...] += dg
            loss_ref[...] += part

    row = pl.BlockSpec((tr, d), lambda i: (i, 0))
    vec = pl.BlockSpec((1, d), lambda i: (0, 0))
    return pl.pallas_call(
        body,
        grid=(tp // tr,),
        in_specs=[row, vec, row],
        out_specs=[row, row, vec, pl.BlockSpec((1, 1), lambda i: (0, 0))],
        out_shape=[jax.ShapeDtypeStruct((tp, d), F32), jax.ShapeDtypeStruct((tp, d), CDT),
                   jax.ShapeDtypeStruct((1, d), F32), jax.ShapeDtypeStruct((1, 1), F32)],
        compiler_params=_params("arbitrary"),
        name="final_loss",
    )(h, g.reshape(1, d), target)


def _sigmoid(x):
    return 1.0 / (1.0 + jnp.exp(-x))


def _swiglu_fwd(gu):
    tp, f2 = gu.shape
    f = f2 // 2
    tr = _div_tile(tp, ROW_TILE, 8)
    tf = _div_tile(f, 1408, LANES)
    nf = f // tf

    def body(g_ref, u_ref, o_ref):
        g = g_ref[...]
        o_ref[...] = ((g * _sigmoid(g)) * u_ref[...]).astype(o_ref.dtype)

    return pl.pallas_call(
        body,
        grid=(tp // tr, nf),
        in_specs=[pl.BlockSpec((tr, tf), lambda i, j: (i, j)), pl.BlockSpec((tr, tf), lambda i, j: (i, j + nf))],
        out_specs=pl.BlockSpec((tr, tf), lambda i, j: (i, j)),
        out_shape=jax.ShapeDtypeStruct((tp, f), CDT),
        compiler_params=_params("parallel", "parallel"),
        name="swiglu_fwd",
    )(gu, gu)


def _swiglu_bwd(gu, da):
    tp, f2 = gu.shape
    f = f2 // 2
    tr = _div_tile(tp, 64, 16)

    def body(g_ref, u_ref, da_ref, o_ref):
        g = g_ref[...]
        da = da_ref[...]
        sg = _sigmoid(g)
        o_ref[:, :f] = (da * u_ref[...] * (sg * (1.0 + g * (1.0 - sg)))).astype(o_ref.dtype)
        o_ref[:, f:] = (da * (g * sg)).astype(o_ref.dtype)

    return pl.pallas_call(
        body,
        grid=(tp // tr,),
        in_specs=[pl.BlockSpec((tr, f), lambda i: (i, 0)), pl.BlockSpec((tr, f), lambda i: (i, 1)),
                  pl.BlockSpec((tr, f), lambda i: (i, 0))],
        out_specs=pl.BlockSpec((tr, f2), lambda i: (i, 0)),
        out_shape=jax.ShapeDtypeStruct((tp, f2), CDT),
        compiler_params=_params("parallel"),
        name="swiglu_bwd",
    )(gu, gu, da)


def _swap_halves(x):
    lane = lax.broadcasted_iota(jnp.int32, x.shape, x.ndim - 1)
    first = (lane % QK_ROPE) < (QK_ROPE // 2)
    return jnp.where(first, pltpu.roll(x, LANES - QK_ROPE // 2, x.ndim - 1), pltpu.roll(x, QK_ROPE // 2, x.ndim - 1))


def _rope_tables(tp):
    pos = jnp.arange(tp, dtype=F32)
    inv_freq = ROPE_THETA ** (-jnp.arange(0, QK_ROPE, 2, dtype=F32) / QK_ROPE)
    ang = pos[:, None] * inv_freq[None, :]
    cos, sin = jnp.cos(ang), jnp.sin(ang)
    reps = LANES // QK_ROPE
    return jnp.tile(jnp.concatenate([cos, cos], -1), (1, reps)), jnp.tile(jnp.concatenate([-sin, sin], -1), (1, reps))


def _chunk_of(pos):
    shift = CHUNK.bit_length() - 1
    assert CHUNK == 1 << shift
    return jnp.where(pos < N_META, 0, 1 + lax.shift_right_arithmetic(pos - N_META, shift))


def _head_half(x, h):
    lane = lax.broadcasted_iota(jnp.int32, x.shape, x.ndim - 1)
    return jnp.where((lane // QK_ROPE) == (h % 2), x, jnp.zeros_like(x))


def _mla_prep(q, kv, proj, cos2, sin2):
    tp = q.shape[0]
    tr = _div_tile(tp, ROW_TILE, 8)
    nope_w = MLA_HEADS * QK_NOPE
    kr_block = (Q_LORA + KV_LORA) // LANES
    depth = QK_NOPE + LANES

    def body(q_ref, kv_ref, kr_ref, c_ref, s_ref, qp_out, kp_out):
        c = c_ref[...]
        s = s_ref[...]
        k = kr_ref[...]
        k = k + pltpu.roll(k, QK_ROPE, 1)
        k = (k * c + _swap_halves(k) * s).astype(CDT)
        for p in range(MLA_HEADS // 2):
            x = q_ref[:, nope_w + p * LANES:nope_w + (p + 1) * LANES]
            pair = (x * c + _swap_halves(x) * s).astype(CDT)
            for h in (2 * p, 2 * p + 1):
                qp_out[h, :, :QK_NOPE] = q_ref[:, h * QK_NOPE:(h + 1) * QK_NOPE].astype(CDT)
                qp_out[h, :, QK_NOPE:] = _head_half(pair, h)
                kp_out[h, :, :QK_NOPE] = kv_ref[:, 2 * h * QK_NOPE:(2 * h + 1) * QK_NOPE]
                kp_out[h, :, QK_NOPE:] = k

    tab = pl.BlockSpec((tr, LANES), lambda i: (i, 0))
    per_head = pl.BlockSpec((MLA_HEADS, tr, depth), lambda i: (0, i, 0))
    out = jax.ShapeDtypeStruct((MLA_HEADS, tp, depth), CDT)
    return pl.pallas_call(
        body,
        grid=(tp // tr,),
        in_specs=[pl.BlockSpec((tr, q.shape[1]), lambda i: (i, 0)), pl.BlockSpec((tr, kv.shape[1]), lambda i: (i, 0)),
                  pl.BlockSpec((tr, LANES), lambda i: (i, kr_block)), tab, tab],
        out_specs=[per_head, per_head],
        out_shape=[out, out],
        compiler_params=_params("parallel"),
        name="mla_prep",
    )(q, kv, proj, cos2, sin2)


def _dot_nt(a, b):
    return lax.dot_general(a, b, (((1,), (1,)), ((), ())), preferred_element_type=F32)


def _dot_tn(a, b):
    return lax.dot_general(a, b, (((0,), (0,)), ((), ())), preferred_element_type=F32)


def _dot(a, b):
    return jnp.dot(a, b, preferred_element_type=F32)


def _chunk_scalar(p):
    return jnp.where(p < N_META, 0, 1 + jnp.maximum(p - N_META, 0) // CHUNK)


def _last_key_block(i, bq, bk, nk):
    cq = _chunk_scalar(i * bq + bq - 1)
    return jnp.minimum((N_META + CHUNK * cq - 1) // bk, nk - 1)


def _full_key_blocks(i, bq, bk):
    return (N_META + CHUNK * _chunk_scalar(i * bq)) // bk


def _first_query_block(j, bk, bq):
    p0 = N_META + CHUNK * (jnp.maximum(j * bk - N_META, 0) // CHUNK)
    return p0 // bq


def _first_full_query_block(j, bk, bq, nq):
    ck = _chunk_scalar(j * bk + bk - 1)
    p0 = jnp.where(ck == 0, 0, N_META + CHUNK * (ck - 1))
    return jnp.minimum((p0 + bq - 1) // bq, nq)


def _chunk_mask(q0, k0, shape, keys_on_rows):
    if keys_on_rows:
        kc = _chunk_of(k0 + lax.broadcasted_iota(jnp.int32, (shape[0], 1), 0))
        qc = _chunk_of(q0 + lax.broadcasted_iota(jnp.int32, (1, shape[1]), 1))
    else:
        qc = _chunk_of(q0 + lax.broadcasted_iota(jnp.int32, (shape[0], 1), 0))
        kc = _chunk_of(k0 + lax.broadcasted_iota(jnp.int32, (1, shape[1]), 1))
    return kc <= qc


def _attn_fwd(qp, kp, kv):
    tp = qp.shape[1]
    depth = qp.shape[2]
    bq = bk = _div_tile(tp, ATT_TILE, LANES)
    nq, nk = tp // bq, tp // bk
    scale = (QK_NOPE + QK_ROPE) ** -0.5

    def body(q_ref, k_ref, v_ref, o_ref, lse_ref):
        def q_block(i, _):
            q0 = pl.multiple_of(i * bq, bq)
            qb = q_ref[pl.ds(q0, bq), :]

            def k_step(masked, j, carry):
                m_old, l_old, acc = carry
                k0 = pl.multiple_of(j * bk, bk)
                s = _dot_nt(qb, k_ref[pl.ds(k0, bk), :]) * scale
                if masked:
                    s = jnp.where(_chunk_mask(q0, k0, s.shape, False), s, NEG_BIG)
                m_new = jnp.maximum(m_old, jnp.max(s, axis=-1, keepdims=True))
                alpha = jnp.exp(m_old - m_new)
                p = jnp.exp(s - m_new)
                l_new = alpha * l_old + jnp.sum(p, axis=-1, keepdims=True)
                acc = alpha * acc + _dot(p.astype(CDT), v_ref[pl.ds(k0, bk), :])
                return m_new, l_new, acc

            n_full = _full_key_blocks(i, bq, bk)
            carry = (jnp.full((bq, 1), NEG_BIG, F32), jnp.zeros((bq, 1), F32), jnp.zeros((bq, V_HEAD), F32))
            carry = lax.fori_loop(0, n_full, lambda j, c: k_step(False, j, c), carry)
            m_fin, l_fin, acc = lax.fori_loop(n_full, _last_key_block(i, bq, bk, nk) + 1,
                                              lambda j, c: k_step(True, j, c), carry)
            o_ref[pl.ds(q0, bq), :] = acc / l_fin
            lse_ref[pl.ds(q0, bq), :] = m_fin + jnp.log(l_fin)
            return 0

        lax.fori_loop(0, nq, q_block, 0)

    per_head = pl.BlockSpec((None, tp, depth), lambda h: (h, 0, 0))
    return pl.pallas_call(
        body,
        grid=(MLA_HEADS,),
        in_specs=[per_head, per_head, pl.BlockSpec((tp, V_HEAD), lambda h: (0, 2 * h + 1))],
        out_specs=[pl.BlockSpec((tp, V_HEAD), lambda h: (0, h)), pl.BlockSpec((None, tp, 1), lambda h: (h, 0, 0))],
        out_shape=[jax.ShapeDtypeStruct((tp, MLA_HEADS * V_HEAD), F32), jax.ShapeDtypeStruct((MLA_HEADS, tp, 1), F32)],
        compiler_params=_params("parallel"),
        name="attn_fwd",
    )(qp, kp, kv)


def _attn_delta(d_out, out):
    tp = out.shape[0]
    tr = _div_tile(tp, ROW_TILE, 8)

    def body(do_ref, o_ref, d_ref):
        for h in range(MLA_HEADS):
            cols = slice(h * V_HEAD, (h + 1) * V_HEAD)
            d_ref[h] = jnp.sum(do_ref[:, cols] * o_ref[:, cols], axis=-1, keepdims=True)

    row = pl.BlockSpec((tr, MLA_HEADS * V_HEAD), lambda i: (i, 0))
    return pl.pallas_call(
        body,
        grid=(tp // tr,),
        in_specs=[row, row],
        out_specs=pl.BlockSpec((MLA_HEADS, tr, 1), lambda i: (0, i, 0)),
        out_shape=jax.ShapeDtypeStruct((MLA_HEADS, tp, 1), F32),
        compiler_params=_params("parallel"),
        name="attn_delta",
    )(d_out, out)


def _attn_bwd(qp, kp, kv, d_out, lse, delta):
    tp = qp.shape[1]
    depth = qp.shape[2]
    bq = bk = _div_tile(tp, ATT_TILE, LANES)
    nq, nk = tp // bq, tp // bk
    scale = (QK_NOPE + QK_ROPE) ** -0.5
    lse_rows = lse.reshape(MLA_HEADS, nq, 1, bq)
    delta_rows = delta.reshape(MLA_HEADS, nq, 1, bq)

    strip = _div_tile(bk, ATT_STRIP, 16)

    def body(q_ref, k_ref, v_ref, do_ref, lse_ref, dl_ref, dqn_ref, dqr_ref, dkv_ref, dkr_ref,
             dq_acc, dk_acc, dv_acc, s_scr, dp_scr, p_scr, ds_scr):
        h = pl.program_id(0)
        dq_acc[...] = jnp.zeros(dq_acc.shape, F32)

        def k_block(j, _):
            k0 = pl.multiple_of(j * bk, bk)
            kb = k_ref[pl.ds(k0, bk), :]
            vb = v_ref[pl.ds(k0, bk), :]
            dk_acc[...] = jnp.zeros(dk_acc.shape, F32)
            dv_acc[...] = jnp.zeros(dv_acc.shape, F32)

            def q_step(masked, i, _):
                q0 = pl.multiple_of(i * bq, bq)
                qb = q_ref[pl.ds(q0, bq), :]
                dob = do_ref[pl.ds(q0, bq), :].astype(CDT)
                s_scr[...] = _dot_nt(kb, qb)
                dp_scr[...] = _dot_nt(vb, dob)
                lse_row = lse_ref[i]
                delta_row = dl_ref[i]
                for r0 in range(0, bk, strip):
                    rows = slice(r0, r0 + strip)
                    s_t = s_scr[rows, :] * scale
                    if masked:
                        s_t = jnp.where(_chunk_mask(q0, k0 + r0, s_t.shape, True), s_t, NEG_BIG)
                    p_t = jnp.exp(s_t - lse_row)
                    p_scr[rows, :] = p_t.astype(CDT)
                    ds_scr[rows, :] = (p_t * (dp_scr[rows, :] - delta_row) * scale).astype(CDT)
                ds_t = ds_scr[...]
                dv_acc[...] += _dot(p_scr[...], dob)
                dk_acc[...] += _dot(ds_t, qb)
                dq_acc[pl.ds(q0, bq), :] += _dot_tn(ds_t, kb)
                return 0

            i_full = _first_full_query_block(j, bk, bq, nq)
            lax.fori_loop(_first_query_block(j, bk, bq), i_full, lambda i, c: q_step(True, i, c), 0)
            lax.fori_loop(i_full, nq, lambda i, c: q_step(False, i, c), 0)
            dkv_ref[pl.ds(k0, bk), :QK_NOPE] = dk_acc[:, :QK_NOPE].astype(CDT)
            dkv_ref[pl.ds(k0, bk), QK_NOPE:] = dv_acc[...].astype(CDT)
            dkr_ref[pl.ds(k0, bk), :] = dk_acc[:, QK_NOPE:]
            return 0

        lax.fori_loop(0, nk, k_block, 0)
        dqn_ref[...] = dq_acc[:, :QK_NOPE].astype(CDT)
        dqr_ref[...] = _head_half(dq_acc[:, QK_NOPE:], h)

    per_head = pl.BlockSpec((None, tp, depth), lambda h: (h, 0, 0))
    stat = pl.BlockSpec((None, nq, 1, bq), lambda h: (h, 0, 0, 0))
    lanes_out = pl.BlockSpec((None, tp, LANES), lambda h: (h, 0, 0))
    return pl.pallas_call(
        body,
        grid=(MLA_HEADS,),
        in_specs=[per_head, per_head, pl.BlockSpec((tp, V_HEAD), lambda h: (0, 2 * h + 1)),
                  pl.BlockSpec((tp, V_HEAD), lambda h: (0, h)), stat, stat],
        out_specs=[pl.BlockSpec((tp, QK_NOPE), lambda h: (0, h)), lanes_out,
                   pl.BlockSpec((tp, QK_NOPE + V_HEAD), lambda h: (0, h)), lanes_out],
        out_shape=[jax.ShapeDtypeStruct((tp, MLA_HEADS * QK_NOPE), CDT), jax.ShapeDtypeStruct((MLA_HEADS, tp, LANES), F32),
                   jax.ShapeDtypeStruct((tp, MLA_HEADS * (QK_NOPE + V_HEAD)), CDT),
                   jax.ShapeDtypeStruct((MLA_HEADS, tp, LANES), F32)],
        scratch_shapes=[pltpu.VMEM((tp, depth), F32), pltpu.VMEM((bk, depth), F32), pltpu.VMEM((bk, V_HEAD), F32),
                        pltpu.VMEM((bk, bq), F32), pltpu.VMEM((bk, bq), F32), pltpu.VMEM((bk, bq), CDT),
                        pltpu.VMEM((bk, bq), CDT)],
        compiler_params=_params("parallel"),
        name="attn_bwd",
    )(qp, kp, kv, d_out, lse_rows, delta_rows)


def _mla_unprep(dqr_h, dkr_h, cos2, sin2):
    tp = dqr_h.shape[1]
    tr = _div_tile(tp, ROW_TILE, 8)
    wr = MLA_HEADS * QK_ROPE

    def body(dq_ref, dk_ref, c_ref, s_ref, dqr_out, dkr_out):
        c = c_ref[...]
        s = s_ref[...]
        for p in range(MLA_HEADS // 2):
            x = dq_ref[2 * p] + dq_ref[2 * p + 1]
            dqr_out[:, p * LANES:(p + 1) * LANES] = (x * c - _swap_halves(x) * s).astype(CDT)
        t = dk_ref[0]
        for h in range(1, MLA_HEADS):
            t = t + dk_ref[h]
        t = t * c - _swap_halves(t) * s
        t = t + pltpu.roll(t, QK_ROPE, 1)
        lane = lax.broadcasted_iota(jnp.int32, t.shape, 1)
        dkr_out[...] = jnp.where(lane < QK_ROPE, t, 0.0)

    per_head = pl.BlockSpec((MLA_HEADS, tr, LANES), lambda i: (0, i, 0))
    tab = pl.BlockSpec((tr, LANES), lambda i: (i, 0))
    return pl.pallas_call(
        body,
        grid=(tp // tr,),
        in_specs=[per_head, per_head, tab, tab],
        out_specs=[pl.BlockSpec((tr, wr), lambda i: (i, 0)), tab],
        out_shape=[jax.ShapeDtypeStruct((tp, wr), CDT), jax.ShapeDtypeStruct((tp, LANES), F32)],
        compiler_params=_params("parallel"),
        name="mla_unprep",
    )(dqr_h, dkr_h, cos2, sin2)


HALO = 8


def _softplus(x):
    return jnp.maximum(x, 0.0) + jnp.log1p(jnp.exp(-jnp.abs(x)))


def _one_minus_sq(log_a, a):
    return -jnp.tanh(log_a) * (a * a + 1.0)


def _gelu(y):
    k = math.sqrt(2.0 / math.pi)
    return 0.5 * y * (1.0 + jnp.tanh(k * (y + 0.044715 * (y * y * y))))


def _gelu_grad(y):
    k = math.sqrt(2.0 / math.pi)
    th = jnp.tanh(k * (y + 0.044715 * (y * y * y)))
    return 0.5 * (1.0 + th) + 0.5 * y * (1.0 - th * th) * (k * (1.0 + 3.0 * 0.044715 * (y * y)))


def _lru_gates_fwd(xy, conv_w, conv_b, w_ga, b_ga, w_gx, b_gx, lam):
    tp = xy.shape[0]
    dr = xy.shape[1] // 2
    bw = dr // RNN_BLOCKS
    tr = _div_tile(tp, ROW_TILE, 8)

    def body(x_ref, halo_ref, cw_ref, cb_ref, wa_ref, ba_ref, wx_ref, bx_ref, lam_ref,
             xc_ref, r_ref, i_ref, a_ref, b_ref, xs):
        i = pl.program_id(0)
        xs[0:HALO, :] = jnp.where(i == 0, 0.0, halo_ref[...])
        xs[HALO:, :] = x_ref[...]
        xc = cb_ref[...] + cw_ref[0:1, :] * xs[pl.ds(HALO - CONV_W + 1, tr), :]
        for j in range(1, CONV_W):
            xc = xc + cw_ref[j:j + 1, :] * xs[pl.ds(HALO - CONV_W + 1 + j, tr), :]
        xcb = xc.astype(CDT)
        r = _sigmoid(_dot(xcb, wa_ref[...]) + ba_ref[...])
        ig = _sigmoid(_dot(xcb, wx_ref[...]) + bx_ref[...])
        log_a = (-LRU_C * r) * _softplus(-lam_ref[...])
        a = jnp.exp(log_a)
        xc_ref[...] = xc
        r_ref[...] = r
        i_ref[...] = ig
        a_ref[...] = a
        b_ref[...] = jnp.sqrt(_one_minus_sq(log_a, a)) * (ig * xc)

    blk = pl.BlockSpec((tr, bw), lambda i, n: (i, n))
    vec = pl.BlockSpec((1, bw), lambda i, n: (0, n))
    mat = pl.BlockSpec((None, bw, bw), lambda i, n: (n, 0, 0))
    bias = pl.BlockSpec((None, 1, bw), lambda i, n: (n, 0, 0))
    out = jax.ShapeDtypeStruct((tp, dr), F32)
    return pl.pallas_call(
        body,
        grid=(tp // tr, RNN_BLOCKS),
        in_specs=[blk, pl.BlockSpec((HALO, bw), lambda i, n: (jnp.maximum(i * (tr // HALO) - 1, 0), n)),
                  pl.BlockSpec((CONV_W, bw), lambda i, n: (0, n)), vec, mat, bias, mat, bias, vec],
        out_specs=[blk] * 5,
        out_shape=[out] * 5,
        scratch_shapes=[pltpu.VMEM((tr + HALO, bw), F32)],
        compiler_params=_params("parallel", "parallel"),
        name="lru_gates_fwd",
    )(xy, xy, conv_w, conv_b.reshape(1, dr), w_ga.astype(CDT), b_ga.reshape(RNN_BLOCKS, 1, bw),
      w_gx.astype(CDT), b_gx.reshape(RNN_BLOCKS, 1, bw), lam.reshape(1, dr))


def _stack_rows(rows):
    idx = lax.broadcasted_iota(jnp.int32, (len(rows), rows[0].shape[1]), 0)
    out = jnp.broadcast_to(rows[0], idx.shape)
    for j in range(1, len(rows)):
        out = jnp.where(idx == j, jnp.broadcast_to(rows[j], idx.shape), out)
    return out


def _lru_scan_fwd(a, b, xy):
    tp, dr = a.shape
    cw = min(2 * SCAN_COLS, dr)
    ycol0 = dr // cw
    ch = _div_tile(tp, ROW_TILE, 16)

    def body(a_ref, b_ref, y_ref, hs_ref, m_ref):
        def group(g, h):
            base = pl.multiple_of(g * 8, 8)
            at = a_ref[pl.ds(base, 8), :]
            bt = b_ref[pl.ds(base, 8), :]
            rows = []
            for j in range(8):
                h = at[j:j + 1, :] * h + bt[j:j + 1, :]
                rows.append(h)
            hs_ref[pl.ds(base, 8), :] = _stack_rows(rows)
            return h

        lax.fori_loop(0, tp // 8, group, jnp.zeros((1, cw), F32))

        def gate(c, _):
            r0 = pl.multiple_of(c * ch, ch)
            m_ref[pl.ds(r0, ch), :] = (hs_ref[pl.ds(r0, ch), :] * _gelu(y_ref[pl.ds(r0, ch), :])).astype(CDT)
            return 0

        lax.fori_loop(0, tp // ch, gate, 0)

    col = pl.BlockSpec((tp, cw), lambda n: (0, n))
    return pl.pallas_call(
        body,
        grid=(dr // cw,),
        in_specs=[col, col, pl.BlockSpec((tp, cw), lambda n: (0, ycol0 + n))],
        out_specs=[col, col],
        out_shape=[jax.ShapeDtypeStruct((tp, dr), F32), jax.ShapeDtypeStruct((tp, dr), CDT)],
        compiler_params=_params("parallel"),
        name="lru_scan_fwd",
    )(a, b, xy)


def _lru_scan_bwd(a, hs, dm, xy):
    tp, dr = a.shape
    cw = SCAN_COLS
    ycol0 = dr // cw
    ng = tp // 8

    def body(a_ref, hs_ref, dm_ref, y_ref, db_ref, da_ref, dy_ref):
        y = y_ref[...]
        dm = dm_ref[...]
        db_ref[...] = dm * _gelu(y)
        dy_ref[...] = (dm * hs_ref[...] * _gelu_grad(y)).astype(CDT)

        def group(k, carry):
            g_next, a_next = carry
            g = ng - 1 - k
            base = pl.multiple_of(g * 8, 8)
            prev = pl.multiple_of(jnp.maximum(g - 1, 0) * 8, 8)
            dt = db_ref[pl.ds(base, 8), :]
            at = a_ref[pl.ds(base, 8), :]
            ht = hs_ref[pl.ds(base, 8), :]
            h_before = jnp.where(g == 0, 0.0, hs_ref[pl.ds(prev, 8), :][7:8, :])
            g_rows = [None] * 8
            da_rows = [None] * 8
            for j in range(7, -1, -1):
                g_cur = dt[j:j + 1, :] + a_next * g_next
                g_rows[j] = g_cur
                da_rows[j] = g_cur * (ht[j - 1:j, :] if j > 0 else h_before)
                g_next = g_cur
                a_next = at[j:j + 1, :]
            db_ref[pl.ds(base, 8), :] = _stack_rows(g_rows)
            da_ref[pl.ds(base, 8), :] = _stack_rows(da_rows)
            return g_next, a_next

        zero = jnp.zeros((1, cw), F32)
        lax.fori_loop(0, ng, group, (zero, zero))

    col = pl.BlockSpec((tp, cw), lambda n: (0, n))
    return pl.pallas_call(
        body,
        grid=(dr // cw,),
        in_specs=[col, col, col, pl.BlockSpec((tp, cw), lambda n: (0, ycol0 + n))],
        out_specs=[col, col, col],
        out_shape=[jax.ShapeDtypeStruct((tp, dr), F32), jax.ShapeDtypeStruct((tp, dr), F32),
                   jax.ShapeDtypeStruct((tp, dr), CDT)],
        compiler_params=_params("parallel"),
        name="lru_scan_bwd",
    )(a, hs, dm, xy)


def _lru_gates_bwd(db, da, xc, r, ig, a, lam, w_ga, w_gx):
    tp, dr = xc.shape
    bw = dr // RNN_BLOCKS
    tr = _div_tile(tp, ROW_TILE, 8)
    nr = tp // tr

    def body(db_ref, da_ref, xc_ref, r_ref, i_ref, a_ref, lam_ref, wa_ref, wx_ref,
             dxc_ref, dwa_ref, dba_ref, dwx_ref, dbx_ref, dlam_ref):
        i = pl.program_id(1)
        xc = xc_ref[...]
        r = r_ref[...]
        ig = i_ref[...]
        a = a_ref[...]
        dbv = db_ref[...]
        sp = _softplus(-lam_ref[...])
        log_a = (-LRU_C * r) * sp
        s = jnp.sqrt(_one_minus_sq(log_a, a))
        d_ix = dbv * s
        d_s = dbv * (ig * xc)
        d_log_a = da_ref[...] * a - d_s * (a * a) / s
        d_r = d_log_a * (-LRU_C * sp)
        d_sp = jnp.sum(d_log_a * (-LRU_C * r), axis=0, keepdims=True)
        dzr = d_r * r * (1.0 - r)
        dzi = (d_ix * xc) * ig * (1.0 - ig)
        dzr_b = dzr.astype(CDT)
        dzi_b = dzi.astype(CDT)
        xcb = xc.astype(CDT)
        dxc_ref[...] = d_ix * ig + _dot_nt(dzr_b, wa_ref[...]) + _dot_nt(dzi_b, wx_ref[...])
        dwa = _dot_tn(xcb, dzr_b)
        dwx = _dot_tn(xcb, dzi_b)
        dba = jnp.sum(dzr, axis=0, keepdims=True)
        dbx = jnp.sum(dzi, axis=0, keepdims=True)

        @pl.when(i == 0)
        def _():
            dwa_ref[...] = dwa
            dwx_ref[...] = dwx
            dba_ref[...] = dba
            dbx_ref[...] = dbx
            dlam_ref[...] = d_sp

        @pl.when(i > 0)
        def _():
            dwa_ref[...] += dwa
            dwx_ref[...] += dwx
            dba_ref[...] += dba
            dbx_ref[...] += dbx
            dlam_ref[...] += d_sp

        @pl.when(i == nr - 1)
        def _():
            dlam_ref[...] = dlam_ref[...] * (-_sigmoid(-lam_ref[...]))

    blk = pl.BlockSpec((tr, bw), lambda n, i: (i, n))
    vec = pl.BlockSpec((1, bw), lambda n, i: (0, n))
    mat = pl.BlockSpec((None, bw, bw), lambda n, i: (n, 0, 0))
    bias = pl.BlockSpec((None, 1, bw), lambda n, i: (n, 0, 0))
    return pl.pallas_call(
        body,
        grid=(RNN_BLOCKS, nr),
        in_specs=[blk] * 6 + [vec, mat, mat],
        out_specs=[blk, mat, bias, mat, bias, vec],
        out_shape=[jax.ShapeDtypeStruct((tp, dr), F32),
                   jax.ShapeDtypeStruct((RNN_BLOCKS, bw, bw), F32), jax.ShapeDtypeStruct((RNN_BLOCKS, 1, bw), F32),
                   jax.ShapeDtypeStruct((RNN_BLOCKS, bw, bw), F32), jax.ShapeDtypeStruct((RNN_BLOCKS, 1, bw), F32),
                   jax.ShapeDtypeStruct((1, dr), F32)],
        compiler_params=_params("parallel", "arbitrary"),
        name="lru_gates_bwd",
    )(db, da, xc, r, ig, a, lam.reshape(1, dr), w_ga.astype(CDT), w_gx.astype(CDT))


def _lru_conv_bwd(dxc, xy, conv_w):
    tp, dr = dxc.shape
    bw = dr // RNN_BLOCKS
    tr = _div_tile(tp, ROW_TILE, 8)
    nr = tp // tr
    per = tr // HALO

    def body(d_ref, dnext_ref, x_ref, xprev_ref, cw_ref, dxb_ref, dcw_ref, dcb_ref, ds, xs):
        i = pl.program_id(1)
        d = d_ref[...]
        ds[0:tr, :] = d
        ds[tr:, :] = jnp.where(i == nr - 1, 0.0, dnext_ref[...])
        xs[0:HALO, :] = jnp.where(i == 0, 0.0, xprev_ref[...])
        xs[HALO:, :] = x_ref[...]
        dxb = cw_ref[0:1, :] * ds[pl.ds(CONV_W - 1, tr), :]
        for j in range(1, CONV_W):
            dxb = dxb + cw_ref[j:j + 1, :] * ds[pl.ds(CONV_W - 1 - j, tr), :]
        dxb_ref[...] = dxb.astype(CDT)
        dcb = jnp.sum(d, axis=0, keepdims=True)
        dcw = [jnp.sum(d * xs[pl.ds(HALO - CONV_W + 1 + j, tr), :], axis=0, keepdims=True) for j in range(CONV_W)]

        @pl.when(i == 0)
        def _():
            dcb_ref[...] = dcb
            for j in range(CONV_W):
                dcw_ref[j] = dcw[j]

        @pl.when(i > 0)
        def _():
            dcb_ref[...] += dcb
            for j in range(CONV_W):
                dcw_ref[j] += dcw[j]

    blk = pl.BlockSpec((tr, bw), lambda n, i: (i, n))
    return pl.pallas_call(
        body,
        grid=(RNN_BLOCKS, nr),
        in_specs=[blk, pl.BlockSpec((HALO, bw), lambda n, i: (jnp.minimum((i + 1) * per, tp // HALO - 1), n)),
                  blk, pl.BlockSpec((HALO, bw), lambda n, i: (jnp.maximum(i * per - 1, 0), n)),
                  pl.BlockSpec((CONV_W, bw), lambda n, i: (0, n))],
        out_specs=[blk, pl.BlockSpec((CONV_W, 1, bw), lambda n, i: (0, 0, n)), pl.BlockSpec((1, bw), lambda n, i: (0, n))],
        out_shape=[jax.ShapeDtypeStruct((tp, dr), CDT), jax.ShapeDtypeStruct((CONV_W, 1, dr), F32),
                   jax.ShapeDtypeStruct((1, dr), F32)],
        scratch_shapes=[pltpu.VMEM((tr + HALO, bw), F32), pltpu.VMEM((tr + HALO, bw), F32)],
        compiler_params=_params("parallel", "arbitrary"),
        name="lru_conv_bwd",
    )(dxc, dxc, xy, xy, conv_w)


def _me():
    return lax.axis_index("x"), lax.axis_index("y"), lax.axis_index("c")


def _peer(rel):
    x, y, c = _me()
    return (1 - x if rel & 4 else x, 1 - y if rel & 2 else y, 1 - c if rel & 1 else c)


def _chip_of(dev):
    return 2 * dev[0] + dev[1]


def _linear(dev):
    return 4 * dev[0] + 2 * dev[1] + dev[2]


CHIP_RELS = (4, 2, 6)
ALL_RELS = (1, 2, 3, 4, 5, 6, 7)
PAIR_RELS = (1,)


def _scatter_send(pieces, rels, piece_of, name):
    n = len(rels)

    def body(src_ref, recv_ref, send_sems, recv_sems):
        copies = []
        for k, rel in enumerate(rels):
            peer = _peer(rel)
            cp = pltpu.make_async_remote_copy(
                src_ref=src_ref.at[piece_of(peer)], dst_ref=recv_ref.at[k],
                send_sem=send_sems.at[k], recv_sem=recv_sems.at[k], device_id=peer, device_id_type=MESH)
            cp.start()
            copies.append(cp)
        for cp in copies:
            cp.wait()

    return pl.pallas_call(
        body,
        in_specs=[pl.BlockSpec(memory_space=pl.ANY)],
        out_specs=pl.BlockSpec(memory_space=pl.ANY),
        out_shape=jax.ShapeDtypeStruct((n,) + pieces.shape[1:], pieces.dtype),
        scratch_shapes=[pltpu.SemaphoreType.DMA((n,)), pltpu.SemaphoreType.DMA((n,))],
        name=name,
    )(pieces)


def _gather_send(piece, rels, n_slots, slot_of, name, n_chunks=1):
    n = len(rels)
    rows = piece.shape[0]
    if rows % (8 * n_chunks):
        n_chunks = 1
    rc = rows // n_chunks

    def body(src_ref, out_ref, send_sems, recv_sems, local_sems):
        me = _me()

        def part(ref, q):
            return ref.at[pl.ds(q * rc, rc)]

        def remote(k, q, slot_dev, to):
            return pltpu.make_async_remote_copy(
                src_ref=part(src_ref, q), dst_ref=part(out_ref.at[slot_of(slot_dev)], q),
                send_sem=send_sems.at[k * n_chunks + q], recv_sem=recv_sems.at[k * n_chunks + q],
                device_id=to, device_id_type=MESH)

        mine = [pltpu.make_async_copy(part(src_ref, q), part(out_ref.at[slot_of(me)], q), local_sems.at[q])
                for q in range(n_chunks)]
        for cp in mine:
            cp.start()
        sends = [remote(k, q, me, _peer(rel)) for k, rel in enumerate(rels) for q in range(n_chunks)]
        for cp in sends:
            cp.start()
        for k, rel in enumerate(rels):
            for q in range(n_chunks):
                remote(k, q, _peer(rel), _peer(rel)).wait_recv()
        for cp in sends:
            cp.wait_send()
        for cp in mine:
            cp.wait()

    return pl.pallas_call(
        body,
        in_specs=[pl.BlockSpec(memory_space=pl.ANY)],
        out_specs=pl.BlockSpec(memory_space=pl.ANY),
        out_shape=jax.ShapeDtypeStruct((n_slots,) + piece.shape, piece.dtype),
        scratch_shapes=[pltpu.SemaphoreType.DMA((n * n_chunks,)), pltpu.SemaphoreType.DMA((n * n_chunks,)),
                        pltpu.SemaphoreType.DMA((n_chunks,))],
        name=name,
    )(piece)


def _gather_chips(shard, name):
    return _gather_send(shard, CHIP_RELS, N_CHIPS, _chip_of, name)


HBM_SPEC = pl.BlockSpec(memory_space=pltpu.HBM)
SEM_SPEC = pl.BlockSpec(memory_space=pltpu.SEMAPHORE)
DATAFLOW = pltpu.SideEffectType.DATAFLOW_SIDE_EFFECTING


def _split_start(src, land, copies, name):
    def body(src_ref, land_ref, send_sem, recv_sem, src_thru, land_thru, token):
        for s_ref, d_ref, peer in copies(src_ref, land_ref):
            pltpu.make_async_remote_copy(src_ref=s_ref, dst_ref=d_ref, send_sem=send_sem, recv_sem=recv_sem,
                                         device_id=peer, device_id_type=MESH).start()
        token[...] = jnp.zeros(token.shape, token.dtype)

    return pl.pallas_call(
        body,
        name=name,
        out_shape=(pltpu.SemaphoreType.DMA(()), pltpu.SemaphoreType.DMA(()), pltpu.HBM(src.shape, src.dtype),
                   pltpu.HBM(land.shape, land.dtype), jax.ShapeDtypeStruct((8, LANES), F32)),
        in_specs=(HBM_SPEC, HBM_SPEC),
        out_specs=(SEM_SPEC, SEM_SPEC, HBM_SPEC, HBM_SPEC, pl.BlockSpec(memory_space=pltpu.VMEM)),
        input_output_aliases={0: 2, 1: 3},
        compiler_params=pltpu.CompilerParams(has_side_effects=DATAFLOW),
    )(pltpu.with_memory_space_constraint(src, pltpu.HBM), pltpu.with_memory_space_constraint(land, pltpu.HBM))


def _split_gather_start(piece, rels, n_slots, slot_of, name):
    land = jnp.broadcast_to(piece[None], (n_slots,) + piece.shape)
    return _split_start(piece, land, lambda s, l: [(s, l.at[slot_of(_me())], _peer(rel)) for rel in rels], name)


def _gather_chips_start(shard, name):
    return _split_gather_start(shard, CHIP_RELS, N_CHIPS, _chip_of, name)


def _split_scatter_start(pieces, name):
    land = lax.empty((len(ALL_RELS),) + pieces.shape[1:], pieces.dtype)
    return _split_start(pieces, land,
                        lambda s, l: [(s.at[_linear(_peer(rel))], l.at[k], _peer(rel)) for k, rel in enumerate(ALL_RELS)], name)


def _split_wait(started, n, after, name, with_src=False):
    send_sem, recv_sem, src_thru, land_thru, _ = started

    def body(src_ref, land_ref, send_sem, recv_sem, after_ref, src_dead, got_ref):
        all_n = land_ref.at[pl.ds(0, n)]
        arrivals = pltpu.make_async_remote_copy(
            src_ref=all_n, dst_ref=all_n, send_sem=send_sem, recv_sem=recv_sem, device_id=_me(), device_id_type=MESH)
        arrivals.wait_send()
        arrivals.wait_recv()

    out = pl.pallas_call(
        body,
        name=name,
        out_shape=(pltpu.HBM(src_thru.shape, src_thru.dtype), pltpu.HBM(land_thru.shape, land_thru.dtype)),
        in_specs=(HBM_SPEC, HBM_SPEC, SEM_SPEC, SEM_SPEC, pl.BlockSpec(memory_space=pl.ANY)),
        out_specs=(HBM_SPEC, HBM_SPEC),
        input_output_aliases={0: 0, 1: 1},
        compiler_params=pltpu.CompilerParams(has_side_effects=DATAFLOW),
    )(src_thru, land_thru, send_sem, recv_sem, after)
    return (out[1], out[0]) if with_src else out[1]


def _sum_pieces(pieces, recv, name):
    _, rr, cc = pieces.shape
    n = recv.shape[0]
    tr = _div_tile(rr, max(8, (1 << 17) // cc // 8 * 8), 8)

    def body(own_ref, recv_ref, o_ref):
        acc = own_ref[...].astype(F32)
        for k in range(n):
            acc = acc + recv_ref[k].astype(F32)
        o_ref[...] = acc

    return pl.pallas_call(
        body,
        grid=(rr // tr,),
        in_specs=[pl.BlockSpec((None, tr, cc), lambda i: (_linear(_me()), i, 0)),
                  pl.BlockSpec((n, tr, cc), lambda i: (0, i, 0))],
        out_specs=pl.BlockSpec((tr, cc), lambda i: (i, 0)),
        out_shape=jax.ShapeDtypeStruct((rr, cc), F32),
        compiler_params=_params("parallel"),
        name=name,
    )(pieces, recv)


def _reduce_to_owner(g8, payload_dtype, name):
    recv = _scatter_send(g8.astype(payload_dtype), ALL_RELS, _linear, name + "_scatter")
    return _sum_pieces(g8, recv, name + "_sum")


def _adamw_layer(w, g, m, v, outs, layer, name):
    nl, rr, cc = w.shape
    _, gr, gc = g.shape
    tr = _div_tile(gr, max(8, (1 << 17) // gc // 8 * 8), 8)
    steps = gr // tr
    c1 = 1.0 - ADAM_B1 ** ADAM_STEP
    c2 = 1.0 - ADAM_B2 ** ADAM_STEP
    if gc == cc:
        assert 2 * gr == rr, (name, g.shape, w.shape)
        slab = pl.BlockSpec((None, tr, gc), lambda h, i: (layer, h * steps + i, 0))
    else:
        assert gr == rr and 2 * gc == cc, (name, g.shape, w.shape)
        slab = pl.BlockSpec((None, tr, gc), lambda h, i: (layer, i, h))

    def body(w_ref, g_ref, m_ref, v_ref, *rest):
        go_ref, d_ref, mo_ref, vo_ref = rest[-4:]
        g_ = g_ref[...]
        m_ = ADAM_B1 * m_ref[...] + (1.0 - ADAM_B1) * g_
        v_ = ADAM_B2 * v_ref[...] + (1.0 - ADAM_B2) * (g_ * g_)
        go_ref[...] = g_
        d_ref[...] = -ADAM_LR * ((m_ / c1) / (jnp.sqrt(v_ / c2) + ADAM_EPS) + ADAM_WD * w_ref[...])
        mo_ref[...] = m_
        vo_ref[...] = v_

    out = jax.ShapeDtypeStruct((nl, rr, cc), F32)
    in_specs = [slab, pl.BlockSpec((None, tr, gc), lambda h, i: (h, i, 0)), slab, slab]
    args = [w, g, m, v]
    aliases = {}
    if outs is not None:
        in_specs += [pl.BlockSpec(memory_space=pl.ANY)] * 4
        args += list(outs)
        aliases = {4 + k: k for k in range(4)}
    return pl.pallas_call(
        body,
        grid=(2, steps),
        in_specs=in_specs,
        out_specs=[slab] * 4,
        out_shape=[out] * 4,
        input_output_aliases=aliases,
        compiler_params=_params("parallel", "parallel"),
        name=name,
    )(*args)


def _adamw(w, g, m, v, name):
    rr, cc = w.shape
    tr = _div_tile(rr, max(8, (1 << 17) // cc // 8 * 8), 8)
    c1 = 1.0 - ADAM_B1 ** ADAM_STEP
    c2 = 1.0 - ADAM_B2 ** ADAM_STEP

    def body(w_ref, g_ref, m_ref, v_ref, d_ref, mo_ref, vo_ref):
        g_ = g_ref[...]
        m_ = ADAM_B1 * m_ref[...] + (1.0 - ADAM_B1) * g_
        v_ = ADAM_B2 * v_ref[...] + (1.0 - ADAM_B2) * (g_ * g_)
        d_ref[...] = -ADAM_LR * ((m_ / c1) / (jnp.sqrt(v_ / c2) + ADAM_EPS) + ADAM_WD * w_ref[...])
        mo_ref[...] = m_
        vo_ref[...] = v_

    blk = pl.BlockSpec((tr, cc), lambda i: (i, 0))
    out = jax.ShapeDtypeStruct((rr, cc), F32)
    return pl.pallas_call(
        body,
        grid=(rr // tr,),
        in_specs=[blk] * 4,
        out_specs=[blk] * 3,
        out_shape=[out] * 3,
        compiler_params=_params("parallel"),
        name=name,
    )(w, g, m, v)


def _pack(arrays, cols, row_mult):
    flat = jnp.concatenate([a.reshape(-1) for a in arrays])
    rows = -(-flat.shape[0] // cols)
    rows = -(-rows // row_mult) * row_mult
    return jnp.pad(flat, (0, rows * cols - flat.shape[0])).reshape(rows, cols)


def _unpack(buf, shapes):
    flat = buf.reshape(-1)
    out, off = [], 0
    for s in shapes:
        n = math.prod(s)
        out.append(flat[off:off + n].reshape(s))
        off += n
    return out


def kernel(x, meta_tokens, norm_mix, norm_ffn, norm_final, mla_w_in, mla_q_norm, mla_kv_norm, mla_w_uq, mla_w_ukv, mla_w_o, lru_w_in, lru_conv_w, lru_conv_b, lru_w_gate_a, lru_b_gate_a, lru_w_gate_x, lru_b_gate_x, lru_lambda, lru_w_o, ffn_w_gu, ffn_w_down, loss_target, m_meta_tokens, m_norm_mix, m_norm_ffn, m_norm_final, m_mla_w_in, m_mla_q_norm, m_mla_kv_norm, m_mla_w_uq, m_mla_w_ukv, m_mla_w_o, m_lru_w_in, m_lru_conv_w, m_lru_conv_b, m_lru_w_gate_a, m_lru_b_gate_a, m_lru_w_gate_x, m_lru_b_gate_x, m_lru_lambda, m_lru_w_o, m_ffn_w_gu, m_ffn_w_down, v_meta_tokens, v_norm_mix, v_norm_ffn, v_norm_final, v_mla_w_in, v_mla_q_norm, v_mla_kv_norm, v_mla_w_uq, v_mla_w_ukv, v_mla_w_o, v_lru_w_in, v_lru_conv_w, v_lru_conv_b, v_lru_w_gate_a, v_lru_b_gate_a, v_lru_w_gate_x, v_lru_b_gate_x, v_lru_lambda, v_lru_w_o, v_ffn_w_gu, v_ffn_w_down):
    d = D_MODEL
    t_real = N_META + SEQ
    tp = _t_pad()
    n_mla = mla_w_in.shape[0]
    n_lru = lru_w_in.shape[0]
    h_dim = MLA_HEADS * V_HEAD
    w_in_cols = Q_LORA + KV_LORA + QK_ROPE
    w_in_pad = Q_LORA + KV_LORA + LANES
    q_cols = MLA_HEADS * (QK_NOPE + QK_ROPE)
    tmm = _div_tile(tp, MM_ROW_TILE, 16)
    tkt = _div_tile(tp, 1408, 16)

    def tile(n, pref):
        return _div_tile(n, pref, LANES)

    small_shapes = [meta_tokens.shape, lru_conv_w.shape, lru_conv_b.shape, lru_lambda.shape]
    csh = meta_tokens.shape[1]
    small4 = _gather_chips(_pack([meta_tokens, lru_conv_w, lru_conv_b, lru_lambda], csh, 16), "gather_small")
    small4, mla_w_in = lax.optimization_barrier((small4, mla_w_in))
    started = {}

    def start(key, shard):
        prev = list(started.values())[-1][4][0, 0] if started else 0.0
        started[key] = _gather_chips_start((shard + prev).astype(CDT), "gather_" + key + "_start")

    def arrived(key, after):
        return _split_wait(started[key], len(CHIP_RELS), after, "gather_" + key + "_wait")

    def start_ffn(layer):
        start(f"w_gu{layer}", ffn_w_gu[layer:layer + 1])
        start(f"w_down{layer}", ffn_w_down[layer:layer + 1])

    start("w_in", jnp.pad(mla_w_in, ((0, 0), (0, 0), (0, w_in_pad - w_in_cols))))
    start("w_uq", mla_w_uq)
    start("w_ukv", mla_w_ukv)
    start("w_o", mla_w_o)
    start_ffn(0)
    start("lw_in", lru_w_in)
    start("lw_o", lru_w_o)
    for layer in range(1, DEPTH):
        start_ffn(layer)
    all_started = list(started.values())[-1][4][0, 0]
    n_gu = ffn_w_gu.shape[2]
    w_gu4, w_down4 = [None] * DEPTH, [None] * DEPTH
    small_full = [jnp.concatenate(parts, axis=-1) for parts in zip(*[_unpack(small4[k], small_shapes) for k in range(N_CHIPS)])]
    meta_full, conv_w_full, conv_b_full, lam_full = small_full

    cos2, sin2 = _rope_tables(tp)

    h = jnp.concatenate([meta_full, x[0], jnp.zeros((tp - t_real, d), F32)], axis=0) + all_started
    saved = []
    for layer in range(DEPTH):
        j = layer // 2
        s = {"h_in": h}
        hn = _rms_fwd(h, norm_mix[layer], width=d, col_block=0, name="norm_mix_fwd")
        s["hn"] = hn
        if layer == 0:
            w_in4, w_uq4, w_ukv4, w_o4 = (arrived(k, hn) for k in ("w_in", "w_uq", "w_ukv", "w_o"))
            w_uq_full = jnp.moveaxis(w_uq4, 0, 2).reshape(n_mla, Q_LORA, MLA_HEADS, QK_NOPE + QK_ROPE)
            w_uq_perm = jnp.concatenate([w_uq_full[..., :QK_NOPE].reshape(n_mla, Q_LORA, -1),
                                         w_uq_full[..., QK_NOPE:].reshape(n_mla, Q_LORA, -1)], axis=-1)
        if layer == 1:
            lw_in4, lw_o4 = arrived("lw_in", hn), arrived("lw_o", hn)
        if layer % 2 == 0:
            proj = _mm("nn", hn, w_in4, kind="row", layer=j, tm=tmm, tn=tile(w_in_pad, 1152), tk=tile(d // N_CHIPS, 512), name="mla_in")
            c_q = _rms_fwd(proj, mla_q_norm[j], width=Q_LORA, col_block=0, name="q_norm_fwd")
            c_kv = _rms_fwd(proj, mla_kv_norm[j], width=KV_LORA, col_block=Q_LORA // KV_LORA, name="kv_norm_fwd")
            q = _mm("nn", c_q, w_uq_perm[j], tm=tmm, tn=tile(q_cols, 1024), tk=Q_LORA, name="mla_uq")
            kv = _mm("nn", c_kv, w_ukv4, kind="col", layer=j, out_dtype=CDT, tm=tmm, tn=tile(w_ukv4.shape[3], 1024), tk=KV_LORA, name="mla_ukv")
            qp, kp = _mla_prep(q, kv, proj, cos2, sin2)
            att, lse = _attn_fwd(qp, kp, kv)
            h = _mm("nn", att, w_o4, kind="row", layer=j, resid=h, tm=tmm, tn=tile(d, 1024), tk=tile(h_dim // N_CHIPS, 512), name="mla_out")
            s.update(proj=proj, c_q=c_q, c_kv=c_kv, qp=qp, kp=kp, kv=kv, att=att, lse=lse)
        else:
            xy = _mm("nn", hn, lw_in4, kind="col", layer=j, tm=tmm, tn=tile(lw_in4.shape[3], 1024), tk=d, name="lru_in")
            xc, r, ig, a, b = _lru_gates_fwd(xy, conv_w_full[j], conv_b_full[j], lru_w_gate_a[j], lru_b_gate_a[j],
                                             lru_w_gate_x[j], lru_b_gate_x[j], lam_full[j])
            hs, mixed = _lru_scan_fwd(a, b, xy)
            h = _mm("nn", mixed, lw_o4, kind="row", layer=j, resid=h, tm=tmm, tn=tile(d, 1024), tk=tile(d // N_CHIPS, 512), name="lru_out")
            s.update(xy=xy, xc=xc, r=r, ig=ig, a=a, hs=hs, mixed=mixed)
        s["h_mid"] = h
        hn2 = _rms_fwd(h, norm_ffn[layer], width=d, col_block=0, name="norm_ffn_fwd")
        w_gu4[layer], w_down4[layer] = arrived(f"w_gu{layer}", hn2), arrived(f"w_down{layer}", hn2)
        gu = _mm("nn", hn2, w_gu4[layer], kind="col", tm=tmm, tn=tile(n_gu, 1408), tk=d, name="ffn_gu")
        act = _swiglu_fwd(gu)
        h = _mm("nn", act, w_down4[layer], kind="row", resid=h, tm=tmm, tn=tile(d, 1024), tk=tile(D_FF // N_CHIPS, 1408), name="ffn_down")
        s.update(hn2=hn2, gu=gu, act=act)
        saved.append(s)

    target = jnp.concatenate([jnp.zeros((N_META, d), F32), loss_target[0], jnp.zeros((tp - t_real, d), F32)], axis=0)
    dh, dhb, g_norm_final, loss_part = _final_loss(h, norm_final, target)
    loss = lax.psum(loss_part[0, 0], ("x", "y", "c"))

    g_norm_mix, g_norm_ffn = [None] * DEPTH, [None] * DEPTH
    g_q_norm, g_kv_norm = [None] * n_mla, [None] * n_mla
    g_w_uq = [None] * n_mla
    g_gate = {k: [None] * n_lru for k in ("wa", "ba", "wx", "bx", "lam", "cw", "cb")}
    weights = {"w_in": (mla_w_in, m_mla_w_in, v_mla_w_in), "w_uq": (mla_w_uq, m_mla_w_uq, v_mla_w_uq),
               "w_ukv": (mla_w_ukv, m_mla_w_ukv, v_mla_w_ukv), "w_o": (mla_w_o, m_mla_w_o, v_mla_w_o),
               "lw_in": (lru_w_in, m_lru_w_in, v_lru_w_in), "lw_o": (lru_w_o, m_lru_w_o, v_lru_w_o),
               "w_gu": (ffn_w_gu, m_ffn_w_gu, v_ffn_w_gu), "w_down": (ffn_w_down, m_ffn_w_down, v_ffn_w_down)}
    res = {key: None for key in weights}
    units = []

    def reduce_start(key, lyr, pieces):
        tag = f"{key}{lyr}"
        units.append({"key": key, "layer": lyr, "tag": tag, "pieces": pieces, "stage": 0, "age": 0,
                      "copy": _split_scatter_start(pieces, "reduce_" + tag + "_scatter_start")})
        return units[-1]["copy"][4]

    def reduce_advance(after, everything=False):
        tokens = []
        for u in units:
            key, lyr, tag = u["key"], u["layer"], u["tag"]
            if u["stage"] == 1:
                both = _split_wait(u["copy"], 1, after, "reduce_" + tag + "_pair_wait")
                g = both[:, :, :w_in_cols] if key == "w_in" else both
                w, m, v = weights[key]
                res[key] = _adamw_layer(w, g, m, v, res[key], lyr, "adamw_" + key)
                u["stage"] = 2
            elif u["stage"] == 0 and (u["age"] > 0 or everything):
                recv, pieces = _split_wait(u["copy"], len(ALL_RELS), after, "reduce_" + tag + "_scatter_wait", with_src=True)
                red = _sum_pieces(pieces, recv, "reduce_" + key + "_sum")
                u["copy"] = _split_gather_start(red, PAIR_RELS, 2, lambda dev: dev[2], "reduce_" + tag + "_pair_start")
                tokens.append(u["copy"][4])
                u["stage"] = 1
            u["age"] += 1
        return tokens

    def grad_w(key, a_op, b_op, kind, lyr, tm, tn):
        return reduce_start(key, lyr, _mm_tn(a_op, b_op, kind=kind, tm=tm, tn=tn, tk=tkt, name="grad_" + key))

    nope_w = MLA_HEADS * QK_NOPE
    for layer in reversed(range(DEPTH)):
        tokens = []
        j = layer // 2
        s = saved[layer]
        tokens.append(grad_w("w_down", s["act"], dhb, "row_colhalves", layer, tile(D_FF // N_CHIPS, 1408), tile(d // 2, 1024)))
        d_act = _mm("nt", dhb, w_down4[layer], kind="row", tm=tmm, tn=tile(D_FF // N_CHIPS, 1408), tk=d, name="ffn_down_bwd")
        dgu = _swiglu_bwd(s["gu"], d_act)
        tokens.append(grad_w("w_gu", s["hn2"], dgu, "col", layer, tile(d // 2, 1024), tile(n_gu, 1408)))
        dhn2 = _mm("nt", dgu, w_gu4[layer], kind="col", tm=tmm, tn=tile(d, 1024), tk=tile(n_gu, 1408), name="ffn_gu_bwd")
        dh, dhb, g_norm_ffn[layer] = _rms_bwd(s["h_mid"], norm_ffn[layer], dhn2, dh, width=d, col_block=0, name="norm_ffn_bwd")
        if layer % 2 == 0:
            tokens.append(grad_w("w_o", s["att"], dhb, "row", j, tile(h_dim // N_DEV, 256), tile(d, 1024)))
            d_att = _mm("nt", dhb, w_o4, kind="row", layer=j, tm=tmm, tn=tile(h_dim // N_CHIPS, 512), tk=d, name="mla_out_bwd")
            delta = _attn_delta(d_att, s["att"])
            dqn, dqr_h, dkv, dkr_h = _attn_bwd(s["qp"], s["kp"], s["kv"], d_att, s["lse"], delta)
            dqr, dkr = _mla_unprep(dqr_h, dkr_h, cos2, sin2)
            dq = jnp.concatenate([dqn, dqr], axis=-1)
            g_uq = _mm_tn(s["c_q"], dq, tm=Q_LORA, tn=tile(q_cols, 1024), tk=tkt, name="grad_w_uq")
            g_uq = jnp.concatenate([g_uq[:, :nope_w].reshape(Q_LORA, MLA_HEADS, QK_NOPE),
                                    g_uq[:, nope_w:].reshape(Q_LORA, MLA_HEADS, QK_ROPE)], axis=-1)
            g_uq = g_uq.reshape(2, Q_LORA // 2, N_CHIPS, q_cols // N_CHIPS).transpose(2, 0, 1, 3)
            tokens.append(reduce_start("w_uq", j, g_uq.reshape(N_DEV, Q_LORA // 2, q_cols // N_CHIPS).astype(CDT)))
            dc_q = _mm("nt", dq, w_uq_perm[j], tm=tmm, tn=Q_LORA, tk=tile(q_cols, 1024), name="mla_uq_bwd")
            tokens.append(grad_w("w_ukv", s["c_kv"], dkv, "col", j, tile(KV_LORA // 2, 256), tile(w_ukv4.shape[3], 1024)))
            dc_kv = _mm("nt", dkv, w_ukv4, kind="col", layer=j, tm=tmm, tn=KV_LORA, tk=tile(w_ukv4.shape[3], 1024), name="mla_ukv_bwd")
            dpq, _, g_q_norm[j] = _rms_bwd(s["proj"], mla_q_norm[j], dc_q, None, width=Q_LORA, col_block=0, name="q_norm_bwd")
            dpkv, _, g_kv_norm[j] = _rms_bwd(s["proj"], mla_kv_norm[j], dc_kv, None, width=KV_LORA, col_block=Q_LORA // KV_LORA, name="kv_norm_bwd")
            dproj = jnp.concatenate([dpq, dpkv, dkr], axis=-1).astype(CDT)
            tokens.append(grad_w("w_in", s["hn"], dproj, "row", j, tile(d // N_DEV, 256), tile(w_in_pad, 1152)))
            dhn = _mm("nt", dproj, w_in4, kind="row", layer=j, tm=tmm, tn=tile(d // N_CHIPS, 512), tk=tile(w_in_pad, 1152), name="mla_in_bwd")
        else:
            tokens.append(grad_w("lw_o", s["mixed"], dhb, "row", j, tile(d // N_DEV, 256), tile(d, 1024)))
            dm = _mm("nt", dhb, lw_o4, kind="row", layer=j, tm=tmm, tn=tile(d // N_CHIPS, 512), tk=d, name="lru_out_bwd")
            db, da, dy = _lru_scan_bwd(s["a"], s["hs"], dm, s["xy"])
            dxc, g_gate["wa"][j], g_gate["ba"][j], g_gate["wx"][j], g_gate["bx"][j], g_gate["lam"][j] = _lru_gates_bwd(
                db, da, s["xc"], s["r"], s["ig"], s["a"], lam_full[j], lru_w_gate_a[j], lru_w_gate_x[j])
            dxb, g_gate["cw"][j], g_gate["cb"][j] = _lru_conv_bwd(dxc, s["xy"], conv_w_full[j])
            dxy = jnp.concatenate([dxb, dy], axis=-1)
            tokens.append(grad_w("lw_in", s["hn"], dxy, "col", j, tile(d // 2, 1024), tile(lw_in4.shape[3], 1024)))
            dhn = _mm("nt", dxy, lw_in4, kind="col", layer=j, tm=tmm, tn=tile(d, 1024), tk=tile(lw_in4.shape[3], 1024), name="lru_in_bwd")
        dh, dhb, g_norm_mix[layer] = _rms_bwd(s["h_in"], norm_mix[layer], dhn, dh, width=d, col_block=0, name="norm_mix_bwd")
        tokens += reduce_advance(dh)
        if layer > 0:
            dhb = dhb + sum(tok[0, 0] for tok in tokens).astype(CDT)
        else:
            dh = dh + sum(tok[0, 0] for tok in tokens)

    grad_x = dh[N_META:t_real][None]
    g_meta_full = dh[:N_META]

    for _ in range(3):
        reduce_advance(dh, everything=True)

    g_small_full = [g_meta_full, jnp.stack(g_gate["cw"]).reshape(n_lru, CONV_W, d), jnp.stack(g_gate["cb"]).reshape(n_lru, d),
                    jnp.stack(g_gate["lam"]).reshape(n_lru, d)]
    g_small4 = jnp.stack([_pack([a[..., k * csh:(k + 1) * csh] for a in g_small_full], csh, 16) for k in range(N_CHIPS)])
    rows_s = g_small4.shape[1]
    red = _reduce_to_owner(g_small4.reshape(N_DEV, rows_s // 2, csh), F32, "reduce_small")
    g_small = _gather_send(red, PAIR_RELS, 2, lambda dev: dev[2], "reduce_small_pair").reshape(rows_s, csh)
    small_w = [meta_tokens, lru_conv_w, lru_conv_b, lru_lambda]
    small_m = [m_meta_tokens, m_lru_conv_w, m_lru_conv_b, m_lru_lambda]
    small_v = [v_meta_tokens, v_lru_conv_w, v_lru_conv_b, v_lru_lambda]
    sd, sm, sv = _adamw(_pack(small_w, csh, 16), g_small, _pack(small_m, csh, 16), _pack(small_v, csh, 16), "adamw_small")
    small_out = [_unpack(buf, small_shapes) for buf in (g_small, sd, sm, sv)]

    rep_w = [norm_mix, norm_ffn, norm_final, mla_q_norm, mla_kv_norm, lru_w_gate_a, lru_b_gate_a, lru_w_gate_x, lru_b_gate_x]
    rep_m = [m_norm_mix, m_norm_ffn, m_norm_final, m_mla_q_norm, m_mla_kv_norm, m_lru_w_gate_a, m_lru_b_gate_a, m_lru_w_gate_x, m_lru_b_gate_x]
    rep_v = [v_norm_mix, v_norm_ffn, v_norm_final, v_mla_q_norm, v_mla_kv_norm, v_lru_w_gate_a, v_lru_b_gate_a, v_lru_w_gate_x, v_lru_b_gate_x]
    rep_g = [jnp.stack(g_norm_mix), jnp.stack(g_norm_ffn), g_norm_final, jnp.stack(g_q_norm), jnp.stack(g_kv_norm),
             jnp.stack(g_gate["wa"]), jnp.stack(g_gate["ba"]), jnp.stack(g_gate["wx"]), jnp.stack(g_gate["bx"])]
    rep_shapes = [w.shape for w in rep_w]
    g_rep = _pack(rep_g, LANES, 8 * N_DEV)
    rows_r = g_rep.shape[0]
    red = _reduce_to_owner(g_rep.reshape(N_DEV, rows_r // N_DEV, LANES), F32, "reduce_rep")
    g_rep = _gather_send(red, ALL_RELS, N_DEV, _linear, "reduce_rep_all").reshape(rows_r, LANES)
    rd, rm, rv = _adamw(_pack(rep_w, LANES, 8 * N_DEV), g_rep, _pack(rep_m, LANES, 8 * N_DEV), _pack(rep_v, LANES, 8 * N_DEV), "adamw_rep")
    rep_out = [_unpack(buf, rep_shapes) for buf in (g_rep, rd, rm, rv)]

    def leaf(kind):
        s_, r_ = small_out[kind], rep_out[kind]
        return [s_[0], r_[0], r_[1], r_[2], res["w_in"][kind], r_[3], r_[4], res["w_uq"][kind], res["w_ukv"][kind],
                res["w_o"][kind], res["lw_in"][kind], s_[1], s_[2], r_[5], r_[6], r_[7], r_[8], s_[3],
                res["lw_o"][kind], res["w_gu"][kind], res["w_down"][kind]]

    return (loss, grad_x, *leaf(0), *leaf(1), *leaf(2), *leaf(3))
```

```python
import math

import jax
import jax.numpy as jnp
from jax import lax
from jax.experimental import pallas as pl
from jax.experimental.pallas import tpu as pltpu

F32 = jnp.float32
CDT = jnp.bfloat16
MESH = pl.DeviceIdType.MESH

D_MODEL = 2048
SEQ = 4096
DEPTH = 4
CHUNK = 64
N_META = 16
MLA_HEADS = 16
Q_LORA = 512
KV_LORA = 512
QK_NOPE = 128
QK_ROPE = 64
V_HEAD = 128
ROPE_THETA = 10000.0
RNN_BLOCKS = 16
CONV_W = 4
LRU_C = 8.0
D_FF = 5632
RMS_EPS = 1e-6
NEG_BIG = -1e30
ADAM_LR = 0.001
ADAM_B1 = 0.9
ADAM_B2 = 0.999
ADAM_EPS = 1e-08
ADAM_WD = 0.01
ADAM_STEP = 10

N_CHIPS = 4
N_DEV = 8
LANES = 128
VMEM_LIMIT = 52 * 1024 * 1024
ROW_TILE = 384
MM_ROW_TILE = 704
ATT_TILE = 384
SCAN_COLS = 128
ATT_STRIP = 32


def _div_tile(n, pref, mult):
    if n <= pref:
        return n
    d = (pref // mult) * mult
    while d >= mult:
        if n % d == 0:
            return d
        d -= mult
    raise ValueError(f"no tile for {n} <= {pref} (multiple of {mult})")


def _t_pad():
    t = N_META + SEQ
    step = math.lcm(_row_tile_unit(), 8)
    return -(-t // step) * step


def _row_tile_unit():
    return math.lcm(math.lcm(ROW_TILE, MM_ROW_TILE), ATT_TILE)


def _params(*sem):
    return pltpu.CompilerParams(dimension_semantics=sem, vmem_limit_bytes=VMEM_LIMIT)


def _b_spec(form, b, kind, layer, t_out, t_con):
    if kind == "plain":
        if form == "nn":
            return pl.BlockSpec((t_con, t_out), lambda i, j, k: (k, j))
        return pl.BlockSpec((t_out, t_con), lambda i, j, k: (j, k))
    rows, cols = b.shape[2], b.shape[3]
    if form == "nn":
        blk = (None, None, t_con, t_out)
        if kind == "row":
            per = rows // t_con
            return pl.BlockSpec(blk, lambda i, j, k: (k // per, layer, k % per, j))
        per = cols // t_out
        return pl.BlockSpec(blk, lambda i, j, k: (j // per, layer, k, j % per))
    blk = (None, None, t_out, t_con)
    if kind == "row":
        per = rows // t_out
        return pl.BlockSpec(blk, lambda i, j, k: (j // per, layer, j % per, k))
    per = cols // t_con
    return pl.BlockSpec(blk, lambda i, j, k: (k // per, layer, j, k % per))


def _mm_body(nk, dims, has_resid):
    def body(*refs):
        if has_resid:
            a_ref, b_ref, r_ref, o_ref = refs[:4]
        else:
            a_ref, b_ref, o_ref = refs[:3]
        prod = lax.dot_general(a_ref[...].astype(CDT), b_ref[...].astype(CDT), (dims, ((), ())),
                               preferred_element_type=F32)

        def finish(acc):
            if has_resid:
                acc = acc + r_ref[...]
            o_ref[...] = acc.astype(o_ref.dtype)

        if nk == 1:
            finish(prod)
            return
        acc_ref = refs[-1]
        k = pl.program_id(2)

        @pl.when(k == 0)
        def _():
            acc_ref[...] = prod

        @pl.when(k > 0)
        def _():
            acc_ref[...] += prod

        @pl.when(k == nk - 1)
        def _():
            finish(acc_ref[...])

    return body


def _mm(form, a, b, *, kind="plain", layer=0, out_dtype=F32, resid=None, tm, tn, tk, name):
    m, con = a.shape
    if kind == "plain":
        w_rows, w_cols = b.shape
    elif kind == "row":
        w_rows, w_cols = b.shape[0] * b.shape[2], b.shape[3]
    else:
        w_rows, w_cols = b.shape[2], b.shape[0] * b.shape[3]
    n_out = w_cols if form == "nn" else w_rows
    assert con == (w_rows if form == "nn" else w_cols), (name, a.shape, b.shape)
    nk = con // tk
    assert m % tm == 0 and n_out % tn == 0 and con % tk == 0, (name, m, n_out, con, tm, tn, tk)
    dims = ((1,), (0,)) if form == "nn" else ((1,), (1,))
    in_specs = [pl.BlockSpec((tm, tk), lambda i, j, k: (i, k)), _b_spec(form, b, kind, layer, tn, tk)]
    args = [a, b]
    if resid is not None:
        in_specs.append(pl.BlockSpec((tm, tn), lambda i, j, k: (i, j)))
        args.append(resid)
    return pl.pallas_call(
        _mm_body(nk, dims, resid is not None),
        grid=(m // tm, n_out // tn, nk),
        in_specs=in_specs,
        out_specs=pl.BlockSpec((tm, tn), lambda i, j, k: (i, j)),
        out_shape=jax.ShapeDtypeStruct((m, n_out), out_dtype),
        scratch_shapes=[pltpu.VMEM((tm, tn), F32)] if nk > 1 else [],
        compiler_params=_params("parallel", "parallel", "arbitrary"),
        name=name,
    )(*args)


def _mm_tn(a, b, *, kind="plain", tm, tn, tk, name):
    t, m = a.shape
    n = b.shape[1]
    nk = t // tk
    assert t % tk == 0 and m % tm == 0 and n % tn == 0, (name, t, m, n, tm, tn, tk)
    if kind == "plain":
        out_shape = jax.ShapeDtypeStruct((m, n), F32)
        out_spec = pl.BlockSpec((tm, tn), lambda i, j, k: (i, j))
    elif kind == "row":
        per = (m // N_CHIPS) // tm
        assert per >= 2 and per % 2 == 0, (name, per)
        out_shape = jax.ShapeDtypeStruct((N_DEV, m // N_DEV, n), CDT)
        out_spec = pl.BlockSpec((None, tm, tn), lambda i, j, k: (2 * (i // per) + (i % per) // (per // 2), (i % per) % (per // 2), j))
    elif kind == "row_colhalves":
        per = (m // N_CHIPS) // tm
        nt = n // tn
        assert per >= 1 and nt % 2 == 0, (name, per, nt)
        out_shape = jax.ShapeDtypeStruct((N_DEV, m // N_CHIPS, n // 2), CDT)
        out_spec = pl.BlockSpec((None, tm, tn), lambda i, j, k: (2 * (i // per) + j // (nt // 2), i % per, j % (nt // 2)))
    else:
        per = (n // N_CHIPS) // tn
        mt = m // tm
        assert per >= 1 and mt % 2 == 0, (name, per, mt)
        out_shape = jax.ShapeDtypeStruct((N_DEV, m // 2, n // N_CHIPS), CDT)
        out_spec = pl.BlockSpec((None, tm, tn), lambda i, j, k: (2 * (j // per) + i // (mt // 2), i % (mt // 2), j % per))
    return pl.pallas_call(
        _mm_body(nk, ((0,), (0,)), False),
        grid=(m // tm, n // tn, nk),
        in_specs=[pl.BlockSpec((tk, tm), lambda i, j, k: (k, i)), pl.BlockSpec((tk, tn), lambda i, j, k: (k, j))],
        out_specs=out_spec,
        out_shape=out_shape,
        scratch_shapes=[pltpu.VMEM((tm, tn), F32)] if nk > 1 else [],
        compiler_params=_params("parallel", "parallel", "arbitrary"),
        name=name,
    )(a, b)


def _rms_fwd(x, g, *, width, col_block, name):
    tp = x.shape[0]
    tr = _div_tile(tp, ROW_TILE, 8)

    def body(x_ref, g_ref, o_ref):
        xf = x_ref[...]
        r = lax.rsqrt(jnp.mean(xf * xf, axis=-1, keepdims=True) + RMS_EPS)
        o_ref[...] = ((xf * r) * g_ref[...]).astype(o_ref.dtype)

    return pl.pallas_call(
        body,
        grid=(tp // tr,),
        in_specs=[pl.BlockSpec((tr, width), lambda i: (i, col_block)), pl.BlockSpec((1, width), lambda i: (0, 0))],
        out_specs=pl.BlockSpec((tr, width), lambda i: (i, 0)),
        out_shape=jax.ShapeDtypeStruct((tp, width), CDT),
        compiler_params=_params("parallel"),
        name=name,
    )(x, g.reshape(1, width))


def _rms_bwd(x, g, dy, resid, *, width, col_block, name):
    tp = x.shape[0]
    tr = _div_tile(tp, ROW_TILE, 8)
    has_resid = resid is not None

    def body(*refs):
        if has_resid:
            x_ref, g_ref, dy_ref, res_ref, dx_ref, dxb_ref, dg_ref = refs
        else:
            x_ref, g_ref, dy_ref, dx_ref, dxb_ref, dg_ref = refs
        i = pl.program_id(0)
        xf = x_ref[...]
        r = lax.rsqrt(jnp.mean(xf * xf, axis=-1, keepdims=True) + RMS_EPS)
        xh = xf * r
        dy = dy_ref[...].astype(F32)
        dg = jnp.sum(dy * xh, axis=0, keepdims=True)
        dxh = dy * g_ref[...]
        dx = r * (dxh - xh * jnp.mean(dxh * xh, axis=-1, keepdims=True))
        if has_resid:
            dx = dx + res_ref[...]
        dx_ref[...] = dx
        dxb_ref[...] = dx.astype(CDT)

        @pl.when(i == 0)
        def _():
            dg_ref[...] = dg

        @pl.when(i > 0)
        def _():
            dg_ref[...] += dg

    row = pl.BlockSpec((tr, width), lambda i: (i, 0))
    in_specs = [pl.BlockSpec((tr, width), lambda i: (i, col_block)), pl.BlockSpec((1, width), lambda i: (0, 0)), row]
    args = [x, g.reshape(1, width), dy]
    if has_resid:
        in_specs.append(row)
        args.append(resid)
    return pl.pallas_call(
        body,
        grid=(tp // tr,),
        in_specs=in_specs,
        out_specs=[row, row, pl.BlockSpec((1, width), lambda i: (0, 0))],
        out_shape=[jax.ShapeDtypeStruct((tp, width), F32), jax.ShapeDtypeStruct((tp, width), CDT),
                   jax.ShapeDtypeStruct((1, width), F32)],
        compiler_params=_params("arbitrary"),
        name=name,
    )(*args)


def _final_loss(h, g, target):
    tp, d = h.shape
    tr = _div_tile(tp, ROW_TILE, 8)

    def body(h_ref, g_ref, t_ref, dh_ref, dhb_ref, dg_ref, loss_ref):
        i = pl.program_id(0)
        xf = h_ref[...]
        r = lax.rsqrt(jnp.mean(xf * xf, axis=-1, keepdims=True) + RMS_EPS)
        xh = xf * r
        gain = g_ref[...]
        y = xh * gain
        rows = i * tr + lax.broadcasted_iota(jnp.int32, (tr, 1), 0)
        valid = jnp.logical_and(rows >= N_META, rows < N_META + SEQ)
        err = jnp.where(valid, y - t_ref[...], 0.0)
        part = 0.5 * jnp.sum(jnp.mean(err * err, axis=-1, keepdims=True), axis=0, keepdims=True)
        dy = err * (1.0 / d)
        dg = jnp.sum(dy * xh, axis=0, keepdims=True)
        dxh = dy * gain
        dx = r * (dxh - xh * jnp.mean(dxh * xh, axis=-1, keepdims=True))
        dh_ref[...] = dx
        dhb_ref[...] = dx.astype(CDT)

        @pl.when(i == 0)
        def _():
            dg_ref[...] = dg
            loss_ref[...] = part

        @pl.when(i > 0)
        def _():
            dg_ref[...] += dg
            loss_ref[...] += part

    row = pl.BlockSpec((tr, d), lambda i: (i, 0))
    vec = pl.BlockSpec((1, d), lambda i: (0, 0))
    return pl.pallas_call(
        body,
        grid=(tp // tr,),
        in_specs=[row, vec, row],
        out_specs=[row, row, vec, pl.BlockSpec((1, 1), lambda i: (0, 0))],
        out_shape=[jax.ShapeDtypeStruct((tp, d), F32), jax.ShapeDtypeStruct((tp, d), CDT),
                   jax.ShapeDtypeStruct((1, d), F32), jax.ShapeDtypeStruct((1, 1), F32)],
        compiler_params=_params("arbitrary"),
        name="final_loss",
    )(h, g.reshape(1, d), target)


def _sigmoid(x):
    return 1.0 / (1.0 + jnp.exp(-x))


def _swiglu_fwd(gu):
    tp, f2 = gu.shape
    f = f2 // 2
    tr = _div_tile(tp, ROW_TILE, 8)
    tf = _div_tile(f, 1408, LANES)
    nf = f // tf

    def body(g_ref, u_ref, o_ref):
        g = g_ref[...].astype(F32)
        o_ref[...] = ((g * _sigmoid(g)) * u_ref[...].astype(F32)).astype(o_ref.dtype)

    return pl.pallas_call(
        body,
        grid=(tp // tr, nf),
        in_specs=[pl.BlockSpec((tr, tf), lambda i, j: (i, j)), pl.BlockSpec((tr, tf), lambda i, j: (i, j + nf))],
        out_specs=pl.BlockSpec((tr, tf), lambda i, j: (i, j)),
        out_shape=jax.ShapeDtypeStruct((tp, f), CDT),
        compiler_params=_params("parallel", "parallel"),
        name="swiglu_fwd",
    )(gu, gu)


def _swiglu_bwd(gu, da):
    tp, f2 = gu.shape
    f = f2 // 2
    tr = _div_tile(tp, 64, 16)

    def body(g_ref, u_ref, da_ref, o_ref):
        g = g_ref[...].astype(F32)
        da = da_ref[...].astype(F32)
        sg = _sigmoid(g)
        o_ref[:, :f] = (da * u_ref[...].astype(F32) * (sg * (1.0 + g * (1.0 - sg)))).astype(o_ref.dtype)
        o_ref[:, f:] = (da * (g * sg)).astype(o_ref.dtype)

    return pl.pallas_call(
        body,
        grid=(tp // tr,),
        in_specs=[pl.BlockSpec((tr, f), lambda i: (i, 0)), pl.BlockSpec((tr, f), lambda i: (i, 1)),
                  pl.BlockSpec((tr, f), lambda i: (i, 0))],
        out_specs=pl.BlockSpec((tr, f2), lambda i: (i, 0)),
        out_shape=jax.ShapeDtypeStruct((tp, f2), CDT),
        compiler_params=_params("parallel"),
        name="swiglu_bwd",
    )(gu, gu, da)


def _swap_halves(x):
    lane = lax.broadcasted_iota(jnp.int32, x.shape, x.ndim - 1)
    first = (lane % QK_ROPE) < (QK_ROPE // 2)
    return jnp.where(first, pltpu.roll(x, LANES - QK_ROPE // 2, x.ndim - 1), pltpu.roll(x, QK_ROPE // 2, x.ndim - 1))


def _rope_tables(tp):
    pos = jnp.arange(tp, dtype=F32)
    inv_freq = ROPE_THETA ** (-jnp.arange(0, QK_ROPE, 2, dtype=F32) / QK_ROPE)
    ang = pos[:, None] * inv_freq[None, :]
    cos, sin = jnp.cos(ang), jnp.sin(ang)
    reps = LANES // QK_ROPE
    return jnp.tile(jnp.concatenate([cos, cos], -1), (1, reps)), jnp.tile(jnp.concatenate([-sin, sin], -1), (1, reps))


def _chunk_of(pos):
    shift = CHUNK.bit_length() - 1
    assert CHUNK == 1 << shift
    return jnp.where(pos < N_META, 0, 1 + lax.shift_right_arithmetic(pos - N_META, shift))


def _head_half(x, h):
    lane = lax.broadcasted_iota(jnp.int32, x.shape, x.ndim - 1)
    return jnp.where((lane // QK_ROPE) == (h % 2), x, jnp.zeros_like(x))


def _mla_prep(q, kv, proj, cos2, sin2):
    tp = q.shape[0]
    tr = _div_tile(tp, ROW_TILE, 8)
    nope_w = MLA_HEADS * QK_NOPE
    kr_block = (Q_LORA + KV_LORA) // LANES
    depth = QK_NOPE + LANES

    def body(q_ref, kv_ref, kr_ref, c_ref, s_ref, qp_out, kp_out):
        c = c_ref[...]
        s = s_ref[...]
        k = kr_ref[...]
        k = k + pltpu.roll(k, QK_ROPE, 1)
        k = (k * c + _swap_halves(k) * s).astype(CDT)
        for p in range(MLA_HEADS // 2):
            x = q_ref[:, nope_w + p * LANES:nope_w + (p + 1) * LANES]
            pair = (x * c + _swap_halves(x) * s).astype(CDT)
            for h in (2 * p, 2 * p + 1):
                qp_out[h, :, :QK_NOPE] = q_ref[:, h * QK_NOPE:(h + 1) * QK_NOPE].astype(CDT)
                qp_out[h, :, QK_NOPE:] = _head_half(pair, h)
                kp_out[h, :, :QK_NOPE] = kv_ref[:, 2 * h * QK_NOPE:(2 * h + 1) * QK_NOPE]
                kp_out[h, :, QK_NOPE:] = k

    tab = pl.BlockSpec((tr, LANES), lambda i: (i, 0))
    per_head = pl.BlockSpec((MLA_HEADS, tr, depth), lambda i: (0, i, 0))
    out = jax.ShapeDtypeStruct((MLA_HEADS, tp, depth), CDT)
    return pl.pallas_call(
        body,
        grid=(tp // tr,),
        in_specs=[pl.BlockSpec((tr, q.shape[1]), lambda i: (i, 0)), pl.BlockSpec((tr, kv.shape[1]), lambda i: (i, 0)),
                  pl.BlockSpec((tr, LANES), lambda i: (i, kr_block)), tab, tab],
        out_specs=[per_head, per_head],
        out_shape=[out, out],
        compiler_params=_params("parallel"),
        name="mla_prep",
    )(q, kv, proj, cos2, sin2)


def _dot_nt(a, b):
    return lax.dot_general(a, b, (((1,), (1,)), ((), ())), preferred_element_type=F32)


def _dot_tn(a, b):
    return lax.dot_general(a, b, (((0,), (0,)), ((), ())), preferred_element_type=F32)


def _dot(a, b):
    return jnp.dot(a, b, preferred_element_type=F32)


def _chunk_scalar(p):
    return jnp.where(p < N_META, 0, 1 + jnp.maximum(p - N_META, 0) // CHUNK)


def _last_key_block(i, bq, bk, nk):
    cq = _chunk_scalar(i * bq + bq - 1)
    return jnp.minimum((N_META + CHUNK * cq - 1) // bk, nk - 1)


def _full_key_blocks(i, bq, bk):
    return (N_META + CHUNK * _chunk_scalar(i * bq)) // bk


def _first_query_block(j, bk, bq):
    p0 = N_META + CHUNK * (jnp.maximum(j * bk - N_META, 0) // CHUNK)
    return p0 // bq


def _first_full_query_block(j, bk, bq, nq):
    ck = _chunk_scalar(j * bk + bk - 1)
    p0 = jnp.where(ck == 0, 0, N_META + CHUNK * (ck - 1))
    return jnp.minimum((p0 + bq - 1) // bq, nq)


def _chunk_mask(q0, k0, shape, keys_on_rows):
    if keys_on_rows:
        kc = _chunk_of(k0 + lax.broadcasted_iota(jnp.int32, (shape[0], 1), 0))
        qc = _chunk_of(q0 + lax.broadcasted_iota(jnp.int32, (1, shape[1]), 1))
    else:
        qc = _chunk_of(q0 + lax.broadcasted_iota(jnp.int32, (shape[0], 1), 0))
        kc = _chunk_of(k0 + lax.broadcasted_iota(jnp.int32, (1, shape[1]), 1))
    return kc <= qc


def _attn_fwd(qp, kp, kv):
    tp = qp.shape[1]
    depth = qp.shape[2]
    bq = bk = _div_tile(tp, ATT_TILE, LANES)
    nq, nk = tp // bq, tp // bk
    scale = (QK_NOPE + QK_ROPE) ** -0.5

    def body(q_ref, k_ref, v_ref, o_ref, lse_ref):
        def q_block(i, _):
            q0 = pl.multiple_of(i * bq, bq)
            qb = q_ref[pl.ds(q0, bq), :]

            def k_step(masked, j, carry):
                m_old, l_old, acc = carry
                k0 = pl.multiple_of(j * bk, bk)
                s = _dot_nt(qb, k_ref[pl.ds(k0, bk), :]) * scale
                if masked:
                    s = jnp.where(_chunk_mask(q0, k0, s.shape, False), s, NEG_BIG)
                m_new = jnp.maximum(m_old, jnp.max(s, axis=-1, keepdims=True))
                alpha = jnp.exp(m_old - m_new)
                p = jnp.exp(s - m_new)
                l_new = alpha * l_old + jnp.sum(p, axis=-1, keepdims=True)
                acc = alpha * acc + _dot(p.astype(CDT), v_ref[pl.ds(k0, bk), :])
                return m_new, l_new, acc

            n_full = _full_key_blocks(i, bq, bk)
            carry = (jnp.full((bq, 1), NEG_BIG, F32), jnp.zeros((bq, 1), F32), jnp.zeros((bq, V_HEAD), F32))
            carry = lax.fori_loop(0, n_full, lambda j, c: k_step(False, j, c), carry)
            m_fin, l_fin, acc = lax.fori_loop(n_full, _last_key_block(i, bq, bk, nk) + 1,
                                              lambda j, c: k_step(True, j, c), carry)
            o_ref[pl.ds(q0, bq), :] = acc / l_fin
            lse_ref[pl.ds(q0, bq), :] = m_fin + jnp.log(l_fin)
            return 0

        lax.fori_loop(0, nq, q_block, 0)

    per_head = pl.BlockSpec((None, tp, depth), lambda h: (h, 0, 0))
    return pl.pallas_call(
        body,
        grid=(MLA_HEADS,),
        in_specs=[per_head, per_head, pl.BlockSpec((tp, V_HEAD), lambda h: (0, 2 * h + 1))],
        out_specs=[pl.BlockSpec((tp, V_HEAD), lambda h: (0, h)), pl.BlockSpec((None, tp, 1), lambda h: (h, 0, 0))],
        out_shape=[jax.ShapeDtypeStruct((tp, MLA_HEADS * V_HEAD), F32), jax.ShapeDtypeStruct((MLA_HEADS, tp, 1), F32)],
        compiler_params=_params("parallel"),
        name="attn_fwd",
    )(qp, kp, kv)


def _attn_delta(d_out, out):
    tp = out.shape[0]
    tr = _div_tile(tp, ROW_TILE, 8)

    def body(do_ref, o_ref, d_ref):
        for h in range(MLA_HEADS):
            cols = slice(h * V_HEAD, (h + 1) * V_HEAD)
            d_ref[h] = jnp.sum(do_ref[:, cols] * o_ref[:, cols], axis=-1, keepdims=True)

    row = pl.BlockSpec((tr, MLA_HEADS * V_HEAD), lambda i: (i, 0))
    return pl.pallas_call(
        body,
        grid=(tp // tr,),
        in_specs=[row, row],
        out_specs=pl.BlockSpec((MLA_HEADS, tr, 1), lambda i: (0, i, 0)),
        out_shape=jax.ShapeDtypeStruct((MLA_HEADS, tp, 1), F32),
        compiler_params=_params("parallel"),
        name="attn_delta",
    )(d_out, out)


def _attn_bwd(qp, kp, kv, d_out, lse, delta):
    tp = qp.shape[1]
    depth = qp.shape[2]
    bq = bk = _div_tile(tp, ATT_TILE, LANES)
    nq, nk = tp // bq, tp // bk
    scale = (QK_NOPE + QK_ROPE) ** -0.5
    lse_rows = lse.reshape(MLA_HEADS, nq, 1, bq)
    delta_rows = delta.reshape(MLA_HEADS, nq, 1, bq)

    strip = _div_tile(bk, ATT_STRIP, 16)

    def body(q_ref, k_ref, v_ref, do_ref, lse_ref, dl_ref, dqn_ref, dqr_ref, dkv_ref, dkr_ref,
             dq_acc, dk_acc, dv_acc, s_scr, dp_scr, p_scr, ds_scr):
        h = pl.program_id(0)
        dq_acc[...] = jnp.zeros(dq_acc.shape, F32)

        def k_block(j, _):
            k0 = pl.multiple_of(j * bk, bk)
            kb = k_ref[pl.ds(k0, bk), :]
            vb = v_ref[pl.ds(k0, bk), :]
            dk_acc[...] = jnp.zeros(dk_acc.shape, F32)
            dv_acc[...] = jnp.zeros(dv_acc.shape, F32)

            def q_step(masked, i, _):
                q0 = pl.multiple_of(i * bq, bq)
                qb = q_ref[pl.ds(q0, bq), :]
                dob = do_ref[pl.ds(q0, bq), :].astype(CDT)
                s_scr[...] = _dot_nt(kb, qb)
                dp_scr[...] = _dot_nt(vb, dob)
                lse_row = lse_ref[i]
                delta_row = dl_ref[i]
                for r0 in range(0, bk, strip):
                    rows = slice(r0, r0 + strip)
                    s_t = s_scr[rows, :] * scale
                    if masked:
                        s_t = jnp.where(_chunk_mask(q0, k0 + r0, s_t.shape, True), s_t, NEG_BIG)
                    p_t = jnp.exp(s_t - lse_row)
                    p_scr[rows, :] = p_t.astype(CDT)
                    ds_scr[rows, :] = (p_t * (dp_scr[rows, :] - delta_row) * scale).astype(CDT)
                ds_t = ds_scr[...]
                dv_acc[...] += _dot(p_scr[...], dob)
                dk_acc[...] += _dot(ds_t, qb)
                dq_acc[pl.ds(q0, bq), :] += _dot_tn(ds_t, kb)
                return 0

            i_full = _first_full_query_block(j, bk, bq, nq)
            lax.fori_loop(_first_query_block(j, bk, bq), i_full, lambda i, c: q_step(True, i, c), 0)
            lax.fori_loop(i_full, nq, lambda i, c: q_step(False, i, c), 0)
            dkv_ref[pl.ds(k0, bk), :QK_NOPE] = dk_acc[:, :QK_NOPE].astype(CDT)
            dkv_ref[pl.ds(k0, bk), QK_NOPE:] = dv_acc[...].astype(CDT)
            dkr_ref[pl.ds(k0, bk), :] = dk_acc[:, QK_NOPE:]
            return 0

        lax.fori_loop(0, nk, k_block, 0)
        dqn_ref[...] = dq_acc[:, :QK_NOPE].astype(CDT)
        dqr_ref[...] = _head_half(dq_acc[:, QK_NOPE:], h)

    per_head = pl.BlockSpec((None, tp, depth), lambda h: (h, 0, 0))
    stat = pl.BlockSpec((None, nq, 1, bq), lambda h: (h, 0, 0, 0))
    lanes_out = pl.BlockSpec((None, tp, LANES), lambda h: (h, 0, 0))
    return pl.pallas_call(
        body,
        grid=(MLA_HEADS,),
        in_specs=[per_head, per_head, pl.BlockSpec((tp, V_HEAD), lambda h: (0, 2 * h + 1)),
                  pl.BlockSpec((tp, V_HEAD), lambda h: (0, h)), stat, stat],
        out_specs=[pl.BlockSpec((tp, QK_NOPE), lambda h: (0, h)), lanes_out,
                   pl.BlockSpec((tp, QK_NOPE + V_HEAD), lambda h: (0, h)), lanes_out],
        out_shape=[jax.ShapeDtypeStruct((tp, MLA_HEADS * QK_NOPE), CDT), jax.ShapeDtypeStruct((MLA_HEADS, tp, LANES), F32),
                   jax.ShapeDtypeStruct((tp, MLA_HEADS * (QK_NOPE + V_HEAD)), CDT),
                   jax.ShapeDtypeStruct((MLA_HEADS, tp, LANES), F32)],
        scratch_shapes=[pltpu.VMEM((tp, depth), F32), pltpu.VMEM((bk, depth), F32), pltpu.VMEM((bk, V_HEAD), F32),
                        pltpu.VMEM((bk, bq), F32), pltpu.VMEM((bk, bq), F32), pltpu.VMEM((bk, bq), CDT),
                        pltpu.VMEM((bk, bq), CDT)],
        compiler_params=_params("parallel"),
        name="attn_bwd",
    )(qp, kp, kv, d_out, lse_rows, delta_rows)


def _mla_unprep(dqr_h, dkr_h, cos2, sin2):
    tp = dqr_h.shape[1]
    tr = _div_tile(tp, ROW_TILE, 8)
    wr = MLA_HEADS * QK_ROPE

    def body(dq_ref, dk_ref, c_ref, s_ref, dqr_out, dkr_out):
        c = c_ref[...]
        s = s_ref[...]
        for p in range(MLA_HEADS // 2):
            x = dq_ref[2 * p] + dq_ref[2 * p + 1]
            dqr_out[:, p * LANES:(p + 1) * LANES] = (x * c - _swap_halves(x) * s).astype(CDT)
        t = dk_ref[0]
        for h in range(1, MLA_HEADS):
            t = t + dk_ref[h]
        t = t * c - _swap_halves(t) * s
        t = t + pltpu.roll(t, QK_ROPE, 1)
        lane = lax.broadcasted_iota(jnp.int32, t.shape, 1)
        dkr_out[...] = jnp.where(lane < QK_ROPE, t, 0.0)

    per_head = pl.BlockSpec((MLA_HEADS, tr, LANES), lambda i: (0, i, 0))
    tab = pl.BlockSpec((tr, LANES), lambda i: (i, 0))
    return pl.pallas_call(
        body,
        grid=(tp // tr,),
        in_specs=[per_head, per_head, tab, tab],
        out_specs=[pl.BlockSpec((tr, wr), lambda i: (i, 0)), tab],
        out_shape=[jax.ShapeDtypeStruct((tp, wr), CDT), jax.ShapeDtypeStruct((tp, LANES), F32)],
        compiler_params=_params("parallel"),
        name="mla_unprep",
    )(dqr_h, dkr_h, cos2, sin2)


HALO = 8


def _softplus(x):
    return jnp.maximum(x, 0.0) + jnp.log1p(jnp.exp(-jnp.abs(x)))


def _one_minus_sq(log_a, a):
    return -jnp.tanh(log_a) * (a * a + 1.0)


def _gelu(y):
    k = math.sqrt(2.0 / math.pi)
    return 0.5 * y * (1.0 + jnp.tanh(k * (y + 0.044715 * (y * y * y))))


def _gelu_grad(y):
    k = math.sqrt(2.0 / math.pi)
    th = jnp.tanh(k * (y + 0.044715 * (y * y * y)))
    return 0.5 * (1.0 + th) + 0.5 * y * (1.0 - th * th) * (k * (1.0 + 3.0 * 0.044715 * (y * y)))


def _lru_gates_fwd(xy, conv_w, conv_b, w_ga, b_ga, w_gx, b_gx, lam):
    tp = xy.shape[0]
    dr = xy.shape[1] // 2
    bw = dr // RNN_BLOCKS
    tr = _div_tile(tp, ROW_TILE, 8)

    def body(x_ref, halo_ref, cw_ref, cb_ref, wa_ref, ba_ref, wx_ref, bx_ref, lam_ref,
             xc_ref, r_ref, i_ref, a_ref, b_ref, xs):
        i = pl.program_id(0)
        xs[0:HALO, :] = jnp.where(i == 0, 0.0, halo_ref[...])
        xs[HALO:, :] = x_ref[...]
        xc = cb_ref[...] + cw_ref[0:1, :] * xs[pl.ds(HALO - CONV_W + 1, tr), :]
        for j in range(1, CONV_W):
            xc = xc + cw_ref[j:j + 1, :] * xs[pl.ds(HALO - CONV_W + 1 + j, tr), :]
        xcb = xc.astype(CDT)
        r = _sigmoid(_dot(xcb, wa_ref[...]) + ba_ref[...])
        ig = _sigmoid(_dot(xcb, wx_ref[...]) + bx_ref[...])
        log_a = (-LRU_C * r) * _softplus(-lam_ref[...])
        a = jnp.exp(log_a)
        xc_ref[...] = xc
        r_ref[...] = r
        i_ref[...] = ig
        a_ref[...] = a
        b_ref[...] = jnp.sqrt(_one_minus_sq(log_a, a)) * (ig * xc)

    blk = pl.BlockSpec((tr, bw), lambda i, n: (i, n))
    vec = pl.BlockSpec((1, bw), lambda i, n: (0, n))
    mat = pl.BlockSpec((None, bw, bw), lambda i, n: (n, 0, 0))
    bias = pl.BlockSpec((None, 1, bw), lambda i, n: (n, 0, 0))
    out = jax.ShapeDtypeStruct((tp, dr), F32)
    return pl.pallas_call(
        body,
        grid=(tp // tr, RNN_BLOCKS),
        in_specs=[blk, pl.BlockSpec((HALO, bw), lambda i, n: (jnp.maximum(i * (tr // HALO) - 1, 0), n)),
                  pl.BlockSpec((CONV_W, bw), lambda i, n: (0, n)), vec, mat, bias, mat, bias, vec],
        out_specs=[blk] * 5,
        out_shape=[out] * 5,
        scratch_shapes=[pltpu.VMEM((tr + HALO, bw), F32)],
        compiler_params=_params("parallel", "parallel"),
        name="lru_gates_fwd",
    )(xy, xy, conv_w, conv_b.reshape(1, dr), w_ga.astype(CDT), b_ga.reshape(RNN_BLOCKS, 1, bw),
      w_gx.astype(CDT), b_gx.reshape(RNN_BLOCKS, 1, bw), lam.reshape(1, dr))


def _stack_rows(rows):
    idx = lax.broadcasted_iota(jnp.int32, (len(rows), rows[0].shape[1]), 0)
    out = jnp.broadcast_to(rows[0], idx.shape)
    for j in range(1, len(rows)):
        out = jnp.where(idx == j, jnp.broadcast_to(rows[j], idx.shape), out)
    return out


def _lru_scan_fwd(a, b, xy):
    tp, dr = a.shape
    cw = min(2 * SCAN_COLS, dr)
    ycol0 = dr // cw
    ch = _div_tile(tp, ROW_TILE, 16)

    def body(a_ref, b_ref, y_ref, hs_ref, m_ref):
        def group(g, h):
            base = pl.multiple_of(g * 8, 8)
            at = a_ref[pl.ds(base, 8), :]
            bt = b_ref[pl.ds(base, 8), :]
            rows = []
            for j in range(8):
                h = at[j:j + 1, :] * h + bt[j:j + 1, :]
                rows.append(h)
            hs_ref[pl.ds(base, 8), :] = _stack_rows(rows)
            return h

        lax.fori_loop(0, tp // 8, group, jnp.zeros((1, cw), F32))

        def gate(c, _):
            r0 = pl.multiple_of(c * ch, ch)
            m_ref[pl.ds(r0, ch), :] = (hs_ref[pl.ds(r0, ch), :] * _gelu(y_ref[pl.ds(r0, ch), :])).astype(CDT)
            return 0

        lax.fori_loop(0, tp // ch, gate, 0)

    col = pl.BlockSpec((tp, cw), lambda n: (0, n))
    return pl.pallas_call(
        body,
        grid=(dr // cw,),
        in_specs=[col, col, pl.BlockSpec((tp, cw), lambda n: (0, ycol0 + n))],
        out_specs=[col, col],
        out_shape=[jax.ShapeDtypeStruct((tp, dr), F32), jax.ShapeDtypeStruct((tp, dr), CDT)],
        compiler_params=_params("parallel"),
        name="lru_scan_fwd",
    )(a, b, xy)


def _lru_scan_bwd(a, hs, dm, xy):
    tp, dr = a.shape
    cw = SCAN_COLS
    ycol0 = dr // cw
    ng = tp // 8

    def body(a_ref, hs_ref, dm_ref, y_ref, db_ref, da_ref, dy_ref):
        y = y_ref[...]
        dm = dm_ref[...]
        db_ref[...] = dm * _gelu(y)
        dy_ref[...] = (dm * hs_ref[...] * _gelu_grad(y)).astype(CDT)

        def group(k, carry):
            g_next, a_next = carry
            g = ng - 1 - k
            base = pl.multiple_of(g * 8, 8)
            prev = pl.multiple_of(jnp.maximum(g - 1, 0) * 8, 8)
            dt = db_ref[pl.ds(base, 8), :]
            at = a_ref[pl.ds(base, 8), :]
            ht = hs_ref[pl.ds(base, 8), :]
            h_before = jnp.where(g == 0, 0.0, hs_ref[pl.ds(prev, 8), :][7:8, :])
            g_rows = [None] * 8
            da_rows = [None] * 8
            for j in range(7, -1, -1):
                g_cur = dt[j:j + 1, :] + a_next * g_next
                g_rows[j] = g_cur
                da_rows[j] = g_cur * (ht[j - 1:j, :] if j > 0 else h_before)
                g_next = g_cur
                a_next = at[j:j + 1, :]
            db_ref[pl.ds(base, 8), :] = _stack_rows(g_rows)
            da_ref[pl.ds(base, 8), :] = _stack_rows(da_rows)
            return g_next, a_next

        zero = jnp.zeros((1, cw), F32)
        lax.fori_loop(0, ng, group, (zero, zero))

    col = pl.BlockSpec((tp, cw), lambda n: (0, n))
    return pl.pallas_call(
        body,
        grid=(dr // cw,),
        in_specs=[col, col, col, pl.BlockSpec((tp, cw), lambda n: (0, ycol0 + n))],
        out_specs=[col, col, col],
        out_shape=[jax.ShapeDtypeStruct((tp, dr), F32), jax.ShapeDtypeStruct((tp, dr), F32),
                   jax.ShapeDtypeStruct((tp, dr), CDT)],
        compiler_params=_params("parallel"),
        name="lru_scan_bwd",
    )(a, hs, dm, xy)


def _lru_gates_bwd(db, da, xc, r, ig, a, lam, w_ga, w_gx):
    tp, dr = xc.shape
    bw = dr // RNN_BLOCKS
    tr = _div_tile(tp, ROW_TILE, 8)
    nr = tp // tr

    def body(db_ref, da_ref, xc_ref, r_ref, i_ref, a_ref, lam_ref, wa_ref, wx_ref,
             dxc_ref, dwa_ref, dba_ref, dwx_ref, dbx_ref, dlam_ref):
        i = pl.program_id(1)
        xc = xc_ref[...]
        r = r_ref[...]
        ig = i_ref[...]
        a = a_ref[...]
        dbv = db_ref[...]
        sp = _softplus(-lam_ref[...])
        log_a = (-LRU_C * r) * sp
        s = jnp.sqrt(_one_minus_sq(log_a, a))
        d_ix = dbv * s
        d_s = dbv * (ig * xc)
        d_log_a = da_ref[...] * a - d_s * (a * a) / s
        d_r = d_log_a * (-LRU_C * sp)
        d_sp = jnp.sum(d_log_a * (-LRU_C * r), axis=0, keepdims=True)
        dzr = d_r * r * (1.0 - r)
        dzi = (d_ix * xc) * ig * (1.0 - ig)
        dzr_b = dzr.astype(CDT)
        dzi_b = dzi.astype(CDT)
        xcb = xc.astype(CDT)
        dxc_ref[...] = d_ix * ig + _dot_nt(dzr_b, wa_ref[...]) + _dot_nt(dzi_b, wx_ref[...])
        dwa = _dot_tn(xcb, dzr_b)
        dwx = _dot_tn(xcb, dzi_b)
        dba = jnp.sum(dzr, axis=0, keepdims=True)
        dbx = jnp.sum(dzi, axis=0, keepdims=True)

        @pl.when(i == 0)
        def _():
            dwa_ref[...] = dwa
            dwx_ref[...] = dwx
            dba_ref[...] = dba
            dbx_ref[...] = dbx
            dlam_ref[...] = d_sp

        @pl.when(i > 0)
        def _():
            dwa_ref[...] += dwa
            dwx_ref[...] += dwx
            dba_ref[...] += dba
            dbx_ref[...] += dbx
            dlam_ref[...] += d_sp

        @pl.when(i == nr - 1)
        def _():
            dlam_ref[...] = dlam_ref[...] * (-_sigmoid(-lam_ref[...]))

    blk = pl.BlockSpec((tr, bw), lambda n, i: (i, n))
    vec = pl.BlockSpec((1, bw), lambda n, i: (0, n))
    mat = pl.BlockSpec((None, bw, bw), lambda n, i: (n, 0, 0))
    bias = pl.BlockSpec((None, 1, bw), lambda n, i: (n, 0, 0))
    return pl.pallas_call(
        body,
        grid=(RNN_BLOCKS, nr),
        in_specs=[blk] * 6 + [vec, mat, mat],
        out_specs=[blk, mat, bias, mat, bias, vec],
        out_shape=[jax.ShapeDtypeStruct((tp, dr), F32),
                   jax.ShapeDtypeStruct((RNN_BLOCKS, bw, bw), F32), jax.ShapeDtypeStruct((RNN_BLOCKS, 1, bw), F32),
                   jax.ShapeDtypeStruct((RNN_BLOCKS, bw, bw), F32), jax.ShapeDtypeStruct((RNN_BLOCKS, 1, bw), F32),
                   jax.ShapeDtypeStruct((1, dr), F32)],
        compiler_params=_params("parallel", "arbitrary"),
        name="lru_gates_bwd",
    )(db, da, xc, r, ig, a, lam.reshape(1, dr), w_ga.astype(CDT), w_gx.astype(CDT))


def _lru_conv_bwd(dxc, xy, conv_w):
    tp, dr = dxc.shape
    bw = dr // RNN_BLOCKS
    tr = _div_tile(tp, ROW_TILE, 8)
    nr = tp // tr
    per = tr // HALO

    def body(d_ref, dnext_ref, x_ref, xprev_ref, cw_ref, dxb_ref, dcw_ref, dcb_ref, ds, xs):
        i = pl.program_id(1)
        d = d_ref[...]
        ds[0:tr, :] = d
        ds[tr:, :] = jnp.where(i == nr - 1, 0.0, dnext_ref[...])
        xs[0:HALO, :] = jnp.where(i == 0, 0.0, xprev_ref[...])
        xs[HALO:, :] = x_ref[...]
        dxb = cw_ref[0:1, :] * ds[pl.ds(CONV_W - 1, tr), :]
        for j in range(1, CONV_W):
            dxb = dxb + cw_ref[j:j + 1, :] * ds[pl.ds(CONV_W - 1 - j, tr), :]
        dxb_ref[...] = dxb.astype(CDT)
        dcb = jnp.sum(d, axis=0, keepdims=True)
        dcw = [jnp.sum(d * xs[pl.ds(HALO - CONV_W + 1 + j, tr), :], axis=0, keepdims=True) for j in range(CONV_W)]

        @pl.when(i == 0)
        def _():
            dcb_ref[...] = dcb
            for j in range(CONV_W):
                dcw_ref[j] = dcw[j]

        @pl.when(i > 0)
        def _():
            dcb_ref[...] += dcb
            for j in range(CONV_W):
                dcw_ref[j] += dcw[j]

    blk = pl.BlockSpec((tr, bw), lambda n, i: (i, n))
    return pl.pallas_call(
        body,
        grid=(RNN_BLOCKS, nr),
        in_specs=[blk, pl.BlockSpec((HALO, bw), lambda n, i: (jnp.minimum((i + 1) * per, tp // HALO - 1), n)),
                  blk, pl.BlockSpec((HALO, bw), lambda n, i: (jnp.maximum(i * per - 1, 0), n)),
                  pl.BlockSpec((CONV_W, bw), lambda n, i: (0, n))],
        out_specs=[blk, pl.BlockSpec((CONV_W, 1, bw), lambda n, i: (0, 0, n)), pl.BlockSpec((1, bw), lambda n, i: (0, n))],
        out_shape=[jax.ShapeDtypeStruct((tp, dr), CDT), jax.ShapeDtypeStruct((CONV_W, 1, dr), F32),
                   jax.ShapeDtypeStruct((1, dr), F32)],
        scratch_shapes=[pltpu.VMEM((tr + HALO, bw), F32), pltpu.VMEM((tr + HALO, bw), F32)],
        compiler_params=_params("parallel", "arbitrary"),
        name="lru_conv_bwd",
    )(dxc, dxc, xy, xy, conv_w)


def _me():
    return lax.axis_index("x"), lax.axis_index("y"), lax.axis_index("c")


def _peer(rel):
    x, y, c = _me()
    return (1 - x if rel & 4 else x, 1 - y if rel & 2 else y, 1 - c if rel & 1 else c)


def _chip_of(dev):
    return 2 * dev[0] + dev[1]


def _linear(dev):
    return 4 * dev[0] + 2 * dev[1] + dev[2]


CHIP_RELS = (4, 2, 6)
ALL_RELS = (1, 2, 3, 4, 5, 6, 7)
PAIR_RELS = (1,)


def _scatter_send(pieces, rels, piece_of, name):
    n = len(rels)

    def body(src_ref, recv_ref, send_sems, recv_sems):
        copies = []
        for k, rel in enumerate(rels):
            peer = _peer(rel)
            cp = pltpu.make_async_remote_copy(
                src_ref=src_ref.at[piece_of(peer)], dst_ref=recv_ref.at[k],
                send_sem=send_sems.at[k], recv_sem=recv_sems.at[k], device_id=peer, device_id_type=MESH)
            cp.start()
            copies.append(cp)
        for cp in copies:
            cp.wait()

    return pl.pallas_call(
        body,
        in_specs=[pl.BlockSpec(memory_space=pl.ANY)],
        out_specs=pl.BlockSpec(memory_space=pl.ANY),
        out_shape=jax.ShapeDtypeStruct((n,) + pieces.shape[1:], pieces.dtype),
        scratch_shapes=[pltpu.SemaphoreType.DMA((n,)), pltpu.SemaphoreType.DMA((n,))],
        name=name,
    )(pieces)


def _gather_send(piece, rels, n_slots, slot_of, name, n_chunks=1):
    n = len(rels)
    rows = piece.shape[0]
    if rows % (8 * n_chunks):
        n_chunks = 1
    rc = rows // n_chunks

    def body(src_ref, out_ref, send_sems, recv_sems, local_sems):
        me = _me()

        def part(ref, q):
            return ref.at[pl.ds(q * rc, rc)]

        def remote(k, q, slot_dev, to):
            return pltpu.make_async_remote_copy(
                src_ref=part(src_ref, q), dst_ref=part(out_ref.at[slot_of(slot_dev)], q),
                send_sem=send_sems.at[k * n_chunks + q], recv_sem=recv_sems.at[k * n_chunks + q],
                device_id=to, device_id_type=MESH)

        mine = [pltpu.make_async_copy(part(src_ref, q), part(out_ref.at[slot_of(me)], q), local_sems.at[q])
                for q in range(n_chunks)]
        for cp in mine:
            cp.start()
        sends = [remote(k, q, me, _peer(rel)) for k, rel in enumerate(rels) for q in range(n_chunks)]
        for cp in sends:
            cp.start()
        for k, rel in enumerate(rels):
            for q in range(n_chunks):
                remote(k, q, _peer(rel), _peer(rel)).wait_recv()
        for cp in sends:
            cp.wait_send()
        for cp in mine:
            cp.wait()

    return pl.pallas_call(
        body,
        in_specs=[pl.BlockSpec(memory_space=pl.ANY)],
        out_specs=pl.BlockSpec(memory_space=pl.ANY),
        out_shape=jax.ShapeDtypeStruct((n_slots,) + piece.shape, piece.dtype),
        scratch_shapes=[pltpu.SemaphoreType.DMA((n * n_chunks,)), pltpu.SemaphoreType.DMA((n * n_chunks,)),
                        pltpu.SemaphoreType.DMA((n_chunks,))],
        name=name,
    )(piece)


def _gather_chips(shard, name):
    return _gather_send(shard, CHIP_RELS, N_CHIPS, _chip_of, name)


HBM_SPEC = pl.BlockSpec(memory_space=pltpu.HBM)
SEM_SPEC = pl.BlockSpec(memory_space=pltpu.SEMAPHORE)
DATAFLOW = pltpu.SideEffectType.DATAFLOW_SIDE_EFFECTING


def _split_start(src, land, copies, name):
    def body(src_ref, land_ref, send_sem, recv_sem, src_thru, land_thru, token):
        for s_ref, d_ref, peer in copies(src_ref, land_ref):
            pltpu.make_async_remote_copy(src_ref=s_ref, dst_ref=d_ref, send_sem=send_sem, recv_sem=recv_sem,
                                         device_id=peer, device_id_type=MESH).start()
        token[...] = jnp.zeros(token.shape, token.dtype)

    return pl.pallas_call(
        body,
        name=name,
        out_shape=(pltpu.SemaphoreType.DMA(()), pltpu.SemaphoreType.DMA(()), pltpu.HBM(src.shape, src.dtype),
                   pltpu.HBM(land.shape, land.dtype), jax.ShapeDtypeStruct((8, LANES), F32)),
        in_specs=(HBM_SPEC, HBM_SPEC),
        out_specs=(SEM_SPEC, SEM_SPEC, HBM_SPEC, HBM_SPEC, pl.BlockSpec(memory_space=pltpu.VMEM)),
        input_output_aliases={0: 2, 1: 3},
        compiler_params=pltpu.CompilerParams(has_side_effects=DATAFLOW),
    )(pltpu.with_memory_space_constraint(src, pltpu.HBM), pltpu.with_memory_space_constraint(land, pltpu.HBM))


def _split_gather_start(piece, rels, n_slots, slot_of, name):
    land = jnp.broadcast_to(piece[None], (n_slots,) + piece.shape)
    return _split_start(piece, land, lambda s, l: [(s, l.at[slot_of(_me())], _peer(rel)) for rel in rels], name)


def _gather_chips_start(shard, name):
    return _split_gather_start(shard, CHIP_RELS, N_CHIPS, _chip_of, name)


def _split_scatter_start(pieces, name):
    land = lax.empty((len(ALL_RELS),) + pieces.shape[1:], pieces.dtype)
    return _split_start(pieces, land,
                        lambda s, l: [(s.at[_linear(_peer(rel))], l.at[k], _peer(rel)) for k, rel in enumerate(ALL_RELS)], name)


def _split_wait(started, n, after, name, with_src=False):
    send_sem, recv_sem, src_thru, land_thru, _ = started

    def body(src_ref, land_ref, send_sem, recv_sem, after_ref, src_dead, got_ref):
        all_n = land_ref.at[pl.ds(0, n)]
        arrivals = pltpu.make_async_remote_copy(
            src_ref=all_n, dst_ref=all_n, send_sem=send_sem, recv_sem=recv_sem, device_id=_me(), device_id_type=MESH)
        arrivals.wait_send()
        arrivals.wait_recv()

    out = pl.pallas_call(
        body,
        name=name,
        out_shape=(pltpu.HBM(src_thru.shape, src_thru.dtype), pltpu.HBM(land_thru.shape, land_thru.dtype)),
        in_specs=(HBM_SPEC, HBM_SPEC, SEM_SPEC, SEM_SPEC, pl.BlockSpec(memory_space=pl.ANY)),
        out_specs=(HBM_SPEC, HBM_SPEC),
        input_output_aliases={0: 0, 1: 1},
        compiler_params=pltpu.CompilerParams(has_side_effects=DATAFLOW),
    )(src_thru, land_thru, send_sem, recv_sem, after)
    return (out[1], out[0]) if with_src else out[1]


def _sum_pieces(pieces, recv, name):
    _, rr, cc = pieces.shape
    n = recv.shape[0]
    tr = _div_tile(rr, max(8, (1 << 17) // cc // 8 * 8), 8)

    def body(own_ref, recv_ref, o_ref):
        acc = own_ref[...].astype(F32)
        for k in range(n):
            acc = acc + recv_ref[k].astype(F32)
        o_ref[...] = acc

    return pl.pallas_call(
        body,
        grid=(rr // tr,),
        in_specs=[pl.BlockSpec((None, tr, cc), lambda i: (_linear(_me()), i, 0)),
                  pl.BlockSpec((n, tr, cc), lambda i: (0, i, 0))],
        out_specs=pl.BlockSpec((tr, cc), lambda i: (i, 0)),
        out_shape=jax.ShapeDtypeStruct((rr, cc), F32),
        compiler_params=_params("parallel"),
        name=name,
    )(pieces, recv)


def _reduce_to_owner(g8, payload_dtype, name):
    recv = _scatter_send(g8.astype(payload_dtype), ALL_RELS, _linear, name + "_scatter")
    return _sum_pieces(g8, recv, name + "_sum")


def _adamw_layer(w, g, m, v, outs, layer, name):
    nl, rr, cc = w.shape
    _, gr, gc = g.shape
    tr = _div_tile(gr, max(8, (1 << 17) // gc // 8 * 8), 8)
    steps = gr // tr
    c1 = 1.0 - ADAM_B1 ** ADAM_STEP
    c2 = 1.0 - ADAM_B2 ** ADAM_STEP
    if gc == cc:
        assert 2 * gr == rr, (name, g.shape, w.shape)
        slab = pl.BlockSpec((None, tr, gc), lambda h, i: (layer, h * steps + i, 0))
    else:
        assert gr == rr and 2 * gc == cc, (name, g.shape, w.shape)
        slab = pl.BlockSpec((None, tr, gc), lambda h, i: (layer, i, h))

    def body(w_ref, g_ref, m_ref, v_ref, *rest):
        go_ref, d_ref, mo_ref, vo_ref = rest[-4:]
        g_ = g_ref[...]
        m_ = ADAM_B1 * m_ref[...] + (1.0 - ADAM_B1) * g_
        v_ = ADAM_B2 * v_ref[...] + (1.0 - ADAM_B2) * (g_ * g_)
        go_ref[...] = g_
        d_ref[...] = -ADAM_LR * ((m_ / c1) / (jnp.sqrt(v_ / c2) + ADAM_EPS) + ADAM_WD * w_ref[...])
        mo_ref[...] = m_
        vo_ref[...] = v_

    out = jax.ShapeDtypeStruct((nl, rr, cc), F32)
    in_specs = [slab, pl.BlockSpec((None, tr, gc), lambda h, i: (h, i, 0)), slab, slab]
    args = [w, g, m, v]
    aliases = {}
    if outs is not None:
        in_specs += [pl.BlockSpec(memory_space=pl.ANY)] * 4
        args += list(outs)
        aliases = {4 + k: k for k in range(4)}
    return pl.pallas_call(
        body,
        grid=(2, steps),
        in_specs=in_specs,
        out_specs=[slab] * 4,
        out_shape=[out] * 4,
        input_output_aliases=aliases,
        compiler_params=_params("parallel", "parallel"),
        name=name,
    )(*args)


def _adamw(w, g, m, v, name):
    rr, cc = w.shape
    tr = _div_tile(rr, max(8, (1 << 17) // cc // 8 * 8), 8)
    c1 = 1.0 - ADAM_B1 ** ADAM_STEP
    c2 = 1.0 - ADAM_B2 ** ADAM_STEP

    def body(w_ref, g_ref, m_ref, v_ref, d_ref, mo_ref, vo_ref):
        g_ = g_ref[...]
        m_ = ADAM_B1 * m_ref[...] + (1.0 - ADAM_B1) * g_
        v_ = ADAM_B2 * v_ref[...] + (1.0 - ADAM_B2) * (g_ * g_)
        d_ref[...] = -ADAM_LR * ((m_ / c1) / (jnp.sqrt(v_ / c2) + ADAM_EPS) + ADAM_WD * w_ref[...])
        mo_ref[...] = m_
        vo_ref[...] = v_

    blk = pl.BlockSpec((tr, cc), lambda i: (i, 0))
    out = jax.ShapeDtypeStruct((rr, cc), F32)
    return pl.pallas_call(
        body,
        grid=(rr // tr,),
        in_specs=[blk] * 4,
        out_specs=[blk] * 3,
        out_shape=[out] * 3,
        compiler_params=_params("parallel"),
        name=name,
    )(w, g, m, v)


def _pack(arrays, cols, row_mult):
    flat = jnp.concatenate([a.reshape(-1) for a in arrays])
    rows = -(-flat.shape[0] // cols)
    rows = -(-rows // row_mult) * row_mult
    return jnp.pad(flat, (0, rows * cols - flat.shape[0])).reshape(rows, cols)


def _unpack(buf, shapes):
    flat = buf.reshape(-1)
    out, off = [], 0
    for s in shapes:
        n = math.prod(s)
        out.append(flat[off:off + n].reshape(s))
        off += n
    return out


def kernel(x, meta_tokens, norm_mix, norm_ffn, norm_final, mla_w_in, mla_q_norm, mla_kv_norm, mla_w_uq, mla_w_ukv, mla_w_o, lru_w_in, lru_conv_w, lru_conv_b, lru_w_gate_a, lru_b_gate_a, lru_w_gate_x, lru_b_gate_x, lru_lambda, lru_w_o, ffn_w_gu, ffn_w_down, loss_target, m_meta_tokens, m_norm_mix, m_norm_ffn, m_norm_final, m_mla_w_in, m_mla_q_norm, m_mla_kv_norm, m_mla_w_uq, m_mla_w_ukv, m_mla_w_o, m_lru_w_in, m_lru_conv_w, m_lru_conv_b, m_lru_w_gate_a, m_lru_b_gate_a, m_lru_w_gate_x, m_lru_b_gate_x, m_lru_lambda, m_lru_w_o, m_ffn_w_gu, m_ffn_w_down, v_meta_tokens, v_norm_mix, v_norm_ffn, v_norm_final, v_mla_w_in, v_mla_q_norm, v_mla_kv_norm, v_mla_w_uq, v_mla_w_ukv, v_mla_w_o, v_lru_w_in, v_lru_conv_w, v_lru_conv_b, v_lru_w_gate_a, v_lru_b_gate_a, v_lru_w_gate_x, v_lru_b_gate_x, v_lru_lambda, v_lru_w_o, v_ffn_w_gu, v_ffn_w_down):
    d = D_MODEL
    t_real = N_META + SEQ
    tp = _t_pad()
    n_mla = mla_w_in.shape[0]
    n_lru = lru_w_in.shape[0]
    h_dim = MLA_HEADS * V_HEAD
    w_in_cols = Q_LORA + KV_LORA + QK_ROPE
    w_in_pad = Q_LORA + KV_LORA + LANES
    q_cols = MLA_HEADS * (QK_NOPE + QK_ROPE)
    tmm = _div_tile(tp, MM_ROW_TILE, 16)
    tkt = _div_tile(tp, 1408, 16)

    def tile(n, pref):
        return _div_tile(n, pref, LANES)

    small_shapes = [meta_tokens.shape, lru_conv_w.shape, lru_conv_b.shape, lru_lambda.shape]
    csh = meta_tokens.shape[1]
    small4 = _gather_chips(_pack([meta_tokens, lru_conv_w, lru_conv_b, lru_lambda], csh, 16), "gather_small")
    small4, mla_w_in = lax.optimization_barrier((small4, mla_w_in))
    started = {}

    def start(key, shard):
        prev = list(started.values())[-1][4][0, 0] if started else 0.0
        started[key] = _gather_chips_start((shard + prev).astype(CDT), "gather_" + key + "_start")

    def arrived(key, after):
        return _split_wait(started[key], len(CHIP_RELS), after, "gather_" + key + "_wait")

    def start_ffn(layer):
        start(f"w_gu{layer}", ffn_w_gu[layer:layer + 1])
        start(f"w_down{layer}", ffn_w_down[layer:layer + 1])

    start("w_in", jnp.pad(mla_w_in, ((0, 0), (0, 0), (0, w_in_pad - w_in_cols))))
    start("w_uq", mla_w_uq)
    start("w_ukv", mla_w_ukv)
    start("w_o", mla_w_o)
    start_ffn(0)
    start("lw_in", lru_w_in)
    start("lw_o", lru_w_o)
    for layer in range(1, DEPTH):
        start_ffn(layer)
    all_started = list(started.values())[-1][4][0, 0]
    n_gu = ffn_w_gu.shape[2]
    w_gu4, w_down4 = [None] * DEPTH, [None] * DEPTH
    small_full = [jnp.concatenate(parts, axis=-1) for parts in zip(*[_unpack(small4[k], small_shapes) for k in range(N_CHIPS)])]
    meta_full, conv_w_full, conv_b_full, lam_full = small_full

    cos2, sin2 = _rope_tables(tp)

    h = jnp.concatenate([meta_full, x[0], jnp.zeros((tp - t_real, d), F32)], axis=0) + all_started
    saved = []
    for layer in range(DEPTH):
        j = layer // 2
        s = {"h_in": h}
        hn = _rms_fwd(h, norm_mix[layer], width=d, col_block=0, name="norm_mix_fwd")
        s["hn"] = hn
        if layer == 0:
            w_in4, w_uq4, w_ukv4, w_o4 = (arrived(k, hn) for k in ("w_in", "w_uq", "w_ukv", "w_o"))
            w_uq_full = jnp.moveaxis(w_uq4, 0, 2).reshape(n_mla, Q_LORA, MLA_HEADS, QK_NOPE + QK_ROPE)
            w_uq_perm = jnp.concatenate([w_uq_full[..., :QK_NOPE].reshape(n_mla, Q_LORA, -1),
                                         w_uq_full[..., QK_NOPE:].reshape(n_mla, Q_LORA, -1)], axis=-1)
        if layer == 1:
            lw_in4, lw_o4 = arrived("lw_in", hn), arrived("lw_o", hn)
        if layer % 2 == 0:
            proj = _mm("nn", hn, w_in4, kind="row", layer=j, tm=tmm, tn=tile(w_in_pad, 1152), tk=tile(d // N_CHIPS, 512), name="mla_in")
            c_q = _rms_fwd(proj, mla_q_norm[j], width=Q_LORA, col_block=0, name="q_norm_fwd")
            c_kv = _rms_fwd(proj, mla_kv_norm[j], width=KV_LORA, col_block=Q_LORA // KV_LORA, name="kv_norm_fwd")
            q = _mm("nn", c_q, w_uq_perm[j], tm=tmm, tn=tile(q_cols, 1024), tk=Q_LORA, name="mla_uq")
            kv = _mm("nn", c_kv, w_ukv4, kind="col", layer=j, out_dtype=CDT, tm=tmm, tn=tile(w_ukv4.shape[3], 1024), tk=KV_LORA, name="mla_ukv")
            qp, kp = _mla_prep(q, kv, proj, cos2, sin2)
            att, lse = _attn_fwd(qp, kp, kv)
            h = _mm("nn", att, w_o4, kind="row", layer=j, resid=h, tm=tmm, tn=tile(d, 1024), tk=tile(h_dim // N_CHIPS, 512), name="mla_out")
            s.update(proj=proj, c_q=c_q, c_kv=c_kv, qp=qp, kp=kp, kv=kv, att=att, lse=lse)
        else:
            xy = _mm("nn", hn, lw_in4, kind="col", layer=j, tm=tmm, tn=tile(lw_in4.shape[3], 1024), tk=d, name="lru_in")
            xc, r, ig, a, b = _lru_gates_fwd(xy, conv_w_full[j], conv_b_full[j], lru_w_gate_a[j], lru_b_gate_a[j],
                                             lru_w_gate_x[j], lru_b_gate_x[j], lam_full[j])
            hs, mixed = _lru_scan_fwd(a, b, xy)
            h = _mm("nn", mixed, lw_o4, kind="row", layer=j, resid=h, tm=tmm, tn=tile(d, 1024), tk=tile(d // N_CHIPS, 512), name="lru_out")
            s.update(xy=xy, xc=xc, r=r, ig=ig, a=a, hs=hs, mixed=mixed)
        s["h_mid"] = h
        hn2 = _rms_fwd(h, norm_ffn[layer], width=d, col_block=0, name="norm_ffn_fwd")
        w_gu4[layer], w_down4[layer] = arrived(f"w_gu{layer}", hn2), arrived(f"w_down{layer}", hn2)
        gu = _mm("nn", hn2, w_gu4[layer], kind="col", out_dtype=CDT, tm=tmm, tn=tile(n_gu, 1408), tk=d, name="ffn_gu")
        act = _swiglu_fwd(gu)
        h = _mm("nn", act, w_down4[layer], kind="row", resid=h, tm=tmm, tn=tile(d, 1024), tk=tile(D_FF // N_CHIPS, 1408), name="ffn_down")
        s.update(hn2=hn2, gu=gu, act=act)
        saved.append(s)

    target = jnp.concatenate([jnp.zeros((N_META, d), F32), loss_target[0], jnp.zeros((tp - t_real, d), F32)], axis=0)
    dh, dhb, g_norm_final, loss_part = _final_loss(h, norm_final, target)
    loss = lax.psum(loss_part[0, 0], ("x", "y", "c"))

    g_norm_mix, g_norm_ffn = [None] * DEPTH, [None] * DEPTH
    g_q_norm, g_kv_norm = [None] * n_mla, [None] * n_mla
    g_w_uq = [None] * n_mla
    g_gate = {k: [None] * n_lru for k in ("wa", "ba", "wx", "bx", "lam", "cw", "cb")}
    weights = {"w_in": (mla_w_in, m_mla_w_in, v_mla_w_in), "w_uq": (mla_w_uq, m_mla_w_uq, v_mla_w_uq),
               "w_ukv": (mla_w_ukv, m_mla_w_ukv, v_mla_w_ukv), "w_o": (mla_w_o, m_mla_w_o, v_mla_w_o),
               "lw_in": (lru_w_in, m_lru_w_in, v_lru_w_in), "lw_o": (lru_w_o, m_lru_w_o, v_lru_w_o),
               "w_gu": (ffn_w_gu, m_ffn_w_gu, v_ffn_w_gu), "w_down": (ffn_w_down, m_ffn_w_down, v_ffn_w_down)}
    res = {key: None for key in weights}
    units = []

    def reduce_start(key, lyr, pieces):
        tag = f"{key}{lyr}"
        units.append({"key": key, "layer": lyr, "tag": tag, "pieces": pieces, "stage": 0, "age": 0,
                      "copy": _split_scatter_start(pieces, "reduce_" + tag + "_scatter_start")})
        return units[-1]["copy"][4]

    def reduce_advance(after, everything=False):
        tokens = []
        for u in units:
            key, lyr, tag = u["key"], u["layer"], u["tag"]
            if u["stage"] == 1:
                both = _split_wait(u["copy"], 1, after, "reduce_" + tag + "_pair_wait")
                g = both[:, :, :w_in_cols] if key == "w_in" else both
                w, m, v = weights[key]
                res[key] = _adamw_layer(w, g, m, v, res[key], lyr, "adamw_" + key)
                u["stage"] = 2
            elif u["stage"] == 0 and (u["age"] > 0 or everything):
                recv, pieces = _split_wait(u["copy"], len(ALL_RELS), after, "reduce_" + tag + "_scatter_wait", with_src=True)
                red = _sum_pieces(pieces, recv, "reduce_" + key + "_sum")
                u["copy"] = _split_gather_start(red, PAIR_RELS, 2, lambda dev: dev[2], "reduce_" + tag + "_pair_start")
                tokens.append(u["copy"][4])
                u["stage"] = 1
            u["age"] += 1
        return tokens

    def grad_w(key, a_op, b_op, kind, lyr, tm, tn):
        return reduce_start(key, lyr, _mm_tn(a_op, b_op, kind=kind, tm=tm, tn=tn, tk=tkt, name="grad_" + key))

    nope_w = MLA_HEADS * QK_NOPE
    for layer in reversed(range(DEPTH)):
        tokens = []
        j = layer // 2
        s = saved[layer]
        t_down = grad_w("w_down", s["act"], dhb, "row_colhalves", layer, tile(D_FF // N_CHIPS, 1408), tile(d // 2, 1024))
        d_act = _mm("nt", dhb, w_down4[layer], kind="row", out_dtype=CDT, tm=tmm, tn=tile(D_FF // N_CHIPS, 1408), tk=d, name="ffn_down_bwd")
        dgu = _swiglu_bwd(s["gu"], d_act)
        t_gu = grad_w("w_gu", s["hn2"], dgu, "col", layer, tile(d // 2, 1024), tile(n_gu, 1408))
        dhn2 = _mm("nt", dgu, w_gu4[layer], kind="col", tm=tmm, tn=tile(d, 1024), tk=tile(n_gu, 1408), name="ffn_gu_bwd")
        dh, dhb, g_norm_ffn[layer] = _rms_bwd(s["h_mid"], norm_ffn[layer] + (t_down[0, 0] + t_gu[0, 0]), dhn2, dh,
                                              width=d, col_block=0, name="norm_ffn_bwd")
        if layer % 2 == 0:
            t_mix = grad_w("w_o", s["att"], dhb, "row", j, tile(h_dim // N_DEV, 256), tile(d, 1024))[0, 0]
            d_att = _mm("nt", dhb, w_o4, kind="row", layer=j, tm=tmm, tn=tile(h_dim // N_CHIPS, 512), tk=d, name="mla_out_bwd")
            delta = _attn_delta(d_att, s["att"])
            dqn, dqr_h, dkv, dkr_h = _attn_bwd(s["qp"], s["kp"], s["kv"], d_att, s["lse"], delta)
            dqr, dkr = _mla_unprep(dqr_h, dkr_h, cos2, sin2)
            dq = jnp.concatenate([dqn, dqr], axis=-1)
            g_uq = _mm_tn(s["c_q"], dq, tm=Q_LORA, tn=tile(q_cols, 1024), tk=tkt, name="grad_w_uq")
            g_uq = jnp.concatenate([g_uq[:, :nope_w].reshape(Q_LORA, MLA_HEADS, QK_NOPE),
                                    g_uq[:, nope_w:].reshape(Q_LORA, MLA_HEADS, QK_ROPE)], axis=-1)
            g_uq = g_uq.reshape(2, Q_LORA // 2, N_CHIPS, q_cols // N_CHIPS).transpose(2, 0, 1, 3)
            t_mix = t_mix + reduce_start("w_uq", j, g_uq.reshape(N_DEV, Q_LORA // 2, q_cols // N_CHIPS).astype(CDT))[0, 0]
            dc_q = _mm("nt", dq, w_uq_perm[j], tm=tmm, tn=Q_LORA, tk=tile(q_cols, 1024), name="mla_uq_bwd")
            t_mix = t_mix + grad_w("w_ukv", s["c_kv"], dkv, "col", j, tile(KV_LORA // 2, 256), tile(w_ukv4.shape[3], 1024))[0, 0]
            dc_kv = _mm("nt", dkv, w_ukv4, kind="col", layer=j, tm=tmm, tn=KV_LORA, tk=tile(w_ukv4.shape[3], 1024), name="mla_ukv_bwd")
            dpq, _, g_q_norm[j] = _rms_bwd(s["proj"], mla_q_norm[j] + t_mix, dc_q, None, width=Q_LORA, col_block=0, name="q_norm_bwd")
            dpkv, _, g_kv_norm[j] = _rms_bwd(s["proj"], mla_kv_norm[j], dc_kv, None, width=KV_LORA, col_block=Q_LORA // KV_LORA, name="kv_norm_bwd")
            dproj = jnp.concatenate([dpq, dpkv, dkr], axis=-1).astype(CDT)
            tokens.append(grad_w("w_in", s["hn"], dproj, "row", j, tile(d // N_DEV, 256), tile(w_in_pad, 1152)))
            dhn = _mm("nt", dproj, w_in4, kind="row", layer=j, tm=tmm, tn=tile(d // N_CHIPS, 512), tk=tile(w_in_pad, 1152), name="mla_in_bwd")
        else:
            t_mix = grad_w("lw_o", s["mixed"], dhb, "row", j, tile(d // N_DEV, 256), tile(d, 1024))[0, 0]
            dm = _mm("nt", dhb, lw_o4, kind="row", layer=j, tm=tmm, tn=tile(d // N_CHIPS, 512), tk=d, name="lru_out_bwd")
            db, da, dy = _lru_scan_bwd(s["a"], s["hs"], dm, s["xy"])
            dxc, g_gate["wa"][j], g_gate["ba"][j], g_gate["wx"][j], g_gate["bx"][j], g_gate["lam"][j] = _lru_gates_bwd(
                db, da, s["xc"], s["r"], s["ig"], s["a"], lam_full[j] + t_mix, lru_w_gate_a[j], lru_w_gate_x[j])
            dxb, g_gate["cw"][j], g_gate["cb"][j] = _lru_conv_bwd(dxc, s["xy"], conv_w_full[j])
            dxy = jnp.concatenate([dxb, dy], axis=-1)
            tokens.append(grad_w("lw_in", s["hn"], dxy, "col", j, tile(d // 2, 1024), tile(lw_in4.shape[3], 1024)))
            dhn = _mm("nt", dxy, lw_in4, kind="col", layer=j, tm=tmm, tn=tile(d, 1024), tk=tile(lw_in4.shape[3], 1024), name="lru_in_bwd")
        dh, dhb, g_norm_mix[layer] = _rms_bwd(s["h_in"], norm_mix[layer] + t_mix, dhn, dh, width=d, col_block=0, name="norm_mix_bwd")
        tokens += reduce_advance(dh)
        if layer > 0:
            dhb = dhb + sum(tok[0, 0] for tok in tokens).astype(CDT)
        else:
            dh = dh + sum(tok[0, 0] for tok in tokens)

    grad_x = dh[N_META:t_real][None]
    g_meta_full = dh[:N_META]

    for _ in range(3):
        reduce_advance(dh, everything=True)

    g_small_full = [g_meta_full, jnp.stack(g_gate["cw"]).reshape(n_lru, CONV_W, d), jnp.stack(g_gate["cb"]).reshape(n_lru, d),
                    jnp.stack(g_gate["lam"]).reshape(n_lru, d)]
    g_small4 = jnp.stack([_pack([a[..., k * csh:(k + 1) * csh] for a in g_small_full], csh, 16) for k in range(N_CHIPS)])
    rows_s = g_small4.shape[1]
    red = _reduce_to_owner(g_small4.reshape(N_DEV, rows_s // 2, csh), F32, "reduce_small")
    g_small = _gather_send(red, PAIR_RELS, 2, lambda dev: dev[2], "reduce_small_pair").reshape(rows_s, csh)
    small_w = [meta_tokens, lru_conv_w, lru_conv_b, lru_lambda]
    small_m = [m_meta_tokens, m_lru_conv_w, m_lru_conv_b, m_lru_lambda]
    small_v = [v_meta_tokens, v_lru_conv_w, v_lru_conv_b, v_lru_lambda]
    sd, sm, sv = _adamw(_pack(small_w, csh, 16), g_small, _pack(small_m, csh, 16), _pack(small_v, csh, 16), "adamw_small")
    small_out = [_unpack(buf, small_shapes) for buf in (g_small, sd, sm, sv)]

    rep_w = [norm_mix, norm_ffn, norm_final, mla_q_norm, mla_kv_norm, lru_w_gate_a, lru_b_gate_a, lru_w_gate_x, lru_b_gate_x]
    rep_m = [m_norm_mix, m_norm_ffn, m_norm_final, m_mla_q_norm, m_mla_kv_norm, m_lru_w_gate_a, m_lru_b_gate_a, m_lru_w_gate_x, m_lru_b_gate_x]
    rep_v = [v_norm_mix, v_norm_ffn, v_norm_final, v_mla_q_norm, v_mla_kv_norm, v_lru_w_gate_a, v_lru_b_gate_a, v_lru_w_gate_x, v_lru_b_gate_x]
    rep_g = [jnp.stack(g_norm_mix), jnp.stack(g_norm_ffn), g_norm_final, jnp.stack(g_q_norm), jnp.stack(g_kv_norm),
             jnp.stack(g_gate["wa"]), jnp.stack(g_gate["ba"]), jnp.stack(g_gate["wx"]), jnp.stack(g_gate["bx"])]
    rep_shapes = [w.shape for w in rep_w]
    g_rep = _pack(rep_g, LANES, 8 * N_DEV)
    rows_r = g_rep.shape[0]
    red = _reduce_to_owner(g_rep.reshape(N_DEV, rows_r // N_DEV, LANES), F32, "reduce_rep")
    g_rep = _gather_send(red, ALL_RELS, N_DEV, _linear, "reduce_rep_all").reshape(rows_r, LANES)
    rd, rm, rv = _adamw(_pack(rep_w, LANES, 8 * N_DEV), g_rep, _pack(rep_m, LANES, 8 * N_DEV), _pack(rep_v, LANES, 8 * N_DEV), "adamw_rep")
    rep_out = [_unpack(buf, rep_shapes) for buf in (g_rep, rd, rm, rv)]

    def leaf(kind):
        s_, r_ = small_out[kind], rep_out[kind]
        return [s_[0], r_[0], r_[1], r_[2], res["w_in"][kind], r_[3], r_[4], res["w_uq"][kind], res["w_ukv"][kind],
                res["w_o"][kind], res["lw_in"][kind], s_[1], s_[2], r_[5], r_[6], r_[7], r_[8], s_[3],
                res["lw_o"][kind], res["w_gu"][kind], res["w_down"][kind]]

    return (loss, grad_x, *leaf(0), *leaf(1), *leaf(2), *leaf(3))
```

```python
import math

import jax
import jax.numpy as jnp
from jax import lax
from jax.experimental import pallas as pl
from jax.experimental.pallas import tpu as pltpu

F32 = jnp.float32
CDT = jnp.bfloat16
MESH = pl.DeviceIdType.MESH

D_MODEL = 2048
SEQ = 4096
DEPTH = 4
CHUNK = 64
N_META = 16
MLA_HEADS = 16
Q_LORA = 512
KV_LORA = 512
QK_NOPE = 128
QK_ROPE = 64
V_HEAD = 128
ROPE_THETA = 10000.0
RNN_BLOCKS = 16
CONV_W = 4
LRU_C = 8.0
D_FF = 5632
RMS_EPS = 1e-6
NEG_BIG = -1e30
ADAM_LR = 0.001
ADAM_B1 = 0.9
ADAM_B2 = 0.999
ADAM_EPS = 1e-08
ADAM_WD = 0.01
ADAM_STEP = 10

N_CHIPS = 4
N_DEV = 8
LANES = 128
VMEM_LIMIT = 52 * 1024 * 1024
ROW_TILE = 384
MM_ROW_TILE = 704
ATT_TILE = 384
SCAN_COLS = 128
ATT_STRIP = 32


def _div_tile(n, pref, mult):
    if n <= pref:
        return n
    d = (pref // mult) * mult
    while d >= mult:
        if n % d == 0:
            return d
        d -= mult
    raise ValueError(f"no tile for {n} <= {pref} (multiple of {mult})")


def _t_pad():
    t = N_META + SEQ
    step = math.lcm(_row_tile_unit(), 8)
    return -(-t // step) * step


def _row_tile_unit():
    return math.lcm(math.lcm(ROW_TILE, MM_ROW_TILE), ATT_TILE)


def _params(*sem):
    return pltpu.CompilerParams(dimension_semantics=sem, vmem_limit_bytes=VMEM_LIMIT)


def _b_spec(form, b, kind, layer, t_out, t_con):
    if kind == "plain":
        if form == "nn":
            return pl.BlockSpec((t_con, t_out), lambda i, j, k: (k, j))
        return pl.BlockSpec((t_out, t_con), lambda i, j, k: (j, k))
    rows, cols = b.shape[2], b.shape[3]
    if form == "nn":
        blk = (None, None, t_con, t_out)
        if kind == "row":
            per = rows // t_con
            return pl.BlockSpec(blk, lambda i, j, k: (k // per, layer, k % per, j))
        per = cols // t_out
        return pl.BlockSpec(blk, lambda i, j, k: (j // per, layer, k, j % per))
    blk = (None, None, t_out, t_con)
    if kind == "row":
        per = rows // t_out
        return pl.BlockSpec(blk, lambda i, j, k: (j // per, layer, j % per, k))
    per = cols // t_con
    return pl.BlockSpec(blk, lambda i, j, k: (k // per, layer, j, k % per))


def _mm_body(nk, dims, has_resid):
    def body(*refs):
        if has_resid:
            a_ref, b_ref, r_ref, o_ref = refs[:4]
        else:
            a_ref, b_ref, o_ref = refs[:3]
        prod = lax.dot_general(a_ref[...].astype(CDT), b_ref[...].astype(CDT), (dims, ((), ())),
                               preferred_element_type=F32)

        def finish(acc):
            if has_resid:
                acc = acc + r_ref[...]
            o_ref[...] = acc.astype(o_ref.dtype)

        if nk == 1:
            finish(prod)
            return
        acc_ref = refs[-1]
        k = pl.program_id(2)

        @pl.when(k == 0)
        def _():
            acc_ref[...] = prod

        @pl.when(k > 0)
        def _():
            acc_ref[...] += prod

        @pl.when(k == nk - 1)
        def _():
            finish(acc_ref[...])

    return body


def _mm(form, a, b, *, kind="plain", layer=0, out_dtype=F32, resid=None, tm, tn, tk, name):
    m, con = a.shape
    if kind == "plain":
        w_rows, w_cols = b.shape
    elif kind == "row":
        w_rows, w_cols = b.shape[0] * b.shape[2], b.shape[3]
    else:
        w_rows, w_cols = b.shape[2], b.shape[0] * b.shape[3]
    n_out = w_cols if form == "nn" else w_rows
    assert con == (w_rows if form == "nn" else w_cols), (name, a.shape, b.shape)
    nk = con // tk
    assert m % tm == 0 and n_out % tn == 0 and con % tk == 0, (name, m, n_out, con, tm, tn, tk)
    dims = ((1,), (0,)) if form == "nn" else ((1,), (1,))
    in_specs = [pl.BlockSpec((tm, tk), lambda i, j, k: (i, k)), _b_spec(form, b, kind, layer, tn, tk)]
    args = [a, b]
    if resid is not None:
        in_specs.append(pl.BlockSpec((tm, tn), lambda i, j, k: (i, j)))
        args.append(resid)
    return pl.pallas_call(
        _mm_body(nk, dims, resid is not None),
        grid=(m // tm, n_out // tn, nk),
        in_specs=in_specs,
        out_specs=pl.BlockSpec((tm, tn), lambda i, j, k: (i, j)),
        out_shape=jax.ShapeDtypeStruct((m, n_out), out_dtype),
        scratch_shapes=[pltpu.VMEM((tm, tn), F32)] if nk > 1 else [],
        compiler_params=_params("parallel", "parallel", "arbitrary"),
        name=name,
    )(*args)


def _mm_tn(a, b, *, kind="plain", tm, tn, tk, name):
    t, m = a.shape
    n = b.shape[1]
    nk = t // tk
    assert t % tk == 0 and m % tm == 0 and n % tn == 0, (name, t, m, n, tm, tn, tk)
    if kind == "plain":
        out_shape = jax.ShapeDtypeStruct((m, n), F32)
        out_spec = pl.BlockSpec((tm, tn), lambda i, j, k: (i, j))
    elif kind == "row":
        per = (m // N_CHIPS) // tm
        assert per >= 2 and per % 2 == 0, (name, per)
        out_shape = jax.ShapeDtypeStruct((N_DEV, m // N_DEV, n), CDT)
        out_spec = pl.BlockSpec((None, tm, tn), lambda i, j, k: (2 * (i // per) + (i % per) // (per // 2), (i % per) % (per // 2), j))
    elif kind == "row_colhalves":
        per = (m // N_CHIPS) // tm
        nt = n // tn
        assert per >= 1 and nt % 2 == 0, (name, per, nt)
        out_shape = jax.ShapeDtypeStruct((N_DEV, m // N_CHIPS, n // 2), CDT)
        out_spec = pl.BlockSpec((None, tm, tn), lambda i, j, k: (2 * (i // per) + j // (nt // 2), i % per, j % (nt // 2)))
    else:
        per = (n // N_CHIPS) // tn
        mt = m // tm
        assert per >= 1 and mt % 2 == 0, (name, per, mt)
        out_shape = jax.ShapeDtypeStruct((N_DEV, m // 2, n // N_CHIPS), CDT)
        out_spec = pl.BlockSpec((None, tm, tn), lambda i, j, k: (2 * (j // per) + i // (mt // 2), i % (mt // 2), j % per))
    return pl.pallas_call(
        _mm_body(nk, ((0,), (0,)), False),
        grid=(m // tm, n // tn, nk),
        in_specs=[pl.BlockSpec((tk, tm), lambda i, j, k: (k, i)), pl.BlockSpec((tk, tn), lambda i, j, k: (k, j))],
        out_specs=out_spec,
        out_shape=out_shape,
        scratch_shapes=[pltpu.VMEM((tm, tn), F32)] if nk > 1 else [],
        compiler_params=_params("parallel", "parallel", "arbitrary"),
        name=name,
    )(a, b)


def _rms_fwd(x, g, *, width, col_block, name):
    tp = x.shape[0]
    tr = _div_tile(tp, ROW_TILE, 8)

    def body(x_ref, g_ref, o_ref):
        xf = x_ref[...]
        r = lax.rsqrt(jnp.mean(xf * xf, axis=-1, keepdims=True) + RMS_EPS)
        o_ref[...] = ((xf * r) * g_ref[...]).astype(o_ref.dtype)

    return pl.pallas_call(
        body,
        grid=(tp // tr,),
        in_specs=[pl.BlockSpec((tr, width), lambda i: (i, col_block)), pl.BlockSpec((1, width), lambda i: (0, 0))],
        out_specs=pl.BlockSpec((tr, width), lambda i: (i, 0)),
        out_shape=jax.ShapeDtypeStruct((tp, width), CDT),
        compiler_params=_params("parallel"),
        name=name,
    )(x, g.reshape(1, width))


def _rms_bwd(x, g, dy, resid, *, width, col_block, name):
    tp = x.shape[0]
    tr = _div_tile(tp, ROW_TILE, 8)
    has_resid = resid is not None

    def body(*refs):
        if has_resid:
            x_ref, g_ref, dy_ref, res_ref, dx_ref, dxb_ref, dg_ref = refs
        else:
            x_ref, g_ref, dy_ref, dx_ref, dxb_ref, dg_ref = refs
        i = pl.program_id(0)
        xf = x_ref[...]
        r = lax.rsqrt(jnp.mean(xf * xf, axis=-1, keepdims=True) + RMS_EPS)
        xh = xf * r
        dy = dy_ref[...].astype(F32)
        dg = jnp.sum(dy * xh, axis=0, keepdims=True)
        dxh = dy * g_ref[...]
        dx = r * (dxh - xh * jnp.mean(dxh * xh, axis=-1, keepdims=True))
        if has_resid:
            dx = dx + res_ref[...]
        dx_ref[...] = dx
        dxb_ref[...] = dx.astype(CDT)

        @pl.when(i == 0)
        def _():
            dg_ref[...] = dg

        @pl.when(i > 0)
        def _():
            dg_ref[...] += dg

    row = pl.BlockSpec((tr, width), lambda i: (i, 0))
    in_specs = [pl.BlockSpec((tr, width), lambda i: (i, col_block)), pl.BlockSpec((1, width), lambda i: (0, 0)), row]
    args = [x, g.reshape(1, width), dy]
    if has_resid:
        in_specs.append(row)
        args.append(resid)
    return pl.pallas_call(
        body,
        grid=(tp // tr,),
        in_specs=in_specs,
        out_specs=[row, row, pl.BlockSpec((1, width), lambda i: (0, 0))],
        out_shape=[jax.ShapeDtypeStruct((tp, width), F32), jax.ShapeDtypeStruct((tp, width), CDT),
                   jax.ShapeDtypeStruct((1, width), F32)],
        compiler_params=_params("arbitrary"),
        name=name,
    )(*args)


def _final_loss(h, g, target):
    tp, d = h.shape
    tr = _div_tile(tp, ROW_TILE, 8)

    def body(h_ref, g_ref, t_ref, dh_ref, dhb_ref, dg_ref, loss_ref):
        i = pl.program_id(0)
        xf = h_ref[...]
        r = lax.rsqrt(jnp.mean(xf * xf, axis=-1, keepdims=True) + RMS_EPS)
        xh = xf * r
        gain = g_ref[...]
        y = xh * gain
        rows = i * tr + lax.broadcasted_iota(jnp.int32, (tr, 1), 0)
        valid = jnp.logical_and(rows >= N_META, rows < N_META + SEQ)
        err = jnp.where(valid, y - t_ref[...], 0.0)
        part = 0.5 * jnp.sum(jnp.mean(err * err, axis=-1, keepdims=True), axis=0, keepdims=True)
        dy = err * (1.0 / d)
        dg = jnp.sum(dy * xh, axis=0, keepdims=True)
        dxh = dy * gain
        dx = r * (dxh - xh * jnp.mean(dxh * xh, axis=-1, keepdims=True))
        dh_ref[...] = dx
        dhb_ref[...] = dx.astype(CDT)

        @pl.when(i == 0)
        def _():
            dg_ref[...] = dg
            loss_ref[...] = part

        @pl.when(i > 0)
        def _():
            dg_ref[...] += dg
            loss_ref[...] += part

    row = pl.BlockSpec((tr, d), lambda i: (i, 0))
    vec = pl.BlockSpec((1, d), lambda i: (0, 0))
    return pl.pallas_call(
        body,
        grid=(tp // tr,),
        in_specs=[row, vec, row],
        out_specs=[row, row, vec, pl.BlockSpec((1, 1), lambda i: (0, 0))],
        out_shape=[jax.ShapeDtypeStruct((tp, d), F32), jax.ShapeDtypeStruct((tp, d), CDT),
                   jax.ShapeDtypeStruct((1, d), F32), jax.ShapeDtypeStruct((1, 1), F32)],
        compiler_params=_params("arbitrary"),
        name="final_loss",
    )(h, g.reshape(1, d), target)


def _sigmoid(x):
    return 1.0 / (1.0 + jnp.exp(-x))


def _swiglu_fwd(gu):
    tp, f2 = gu.shape
    f = f2 // 2
    tr = _div_tile(tp, ROW_TILE, 8)
    tf = _div_tile(f, 1408, LANES)
    nf = f // tf

    def body(g_ref, u_ref, o_ref):
        g = g_ref[...].astype(F32)
        o_ref[...] = ((g * _sigmoid(g)) * u_ref[...].astype(F32)).astype(o_ref.dtype)

    return pl.pallas_call(
        body,
        grid=(tp // tr, nf),
        in_specs=[pl.BlockSpec((tr, tf), lambda i, j: (i, j)), pl.BlockSpec((tr, tf), lambda i, j: (i, j + nf))],
        out_specs=pl.BlockSpec((tr, tf), lambda i, j: (i, j)),
        out_shape=jax.ShapeDtypeStruct((tp, f), CDT),
        compiler_params=_params("parallel", "parallel"),
        name="swiglu_fwd",
    )(gu, gu)


def _swiglu_bwd(gu, da):
    tp, f2 = gu.shape
    f = f2 // 2
    tr = _div_tile(tp, 64, 16)

    def body(g_ref, u_ref, da_ref, o_ref):
        g = g_ref[...].astype(F32)
        da = da_ref[...].astype(F32)
        sg = _sigmoid(g)
        o_ref[:, :f] = (da * u_ref[...].astype(F32) * (sg * (1.0 + g * (1.0 - sg)))).astype(o_ref.dtype)
        o_ref[:, f:] = (da * (g * sg)).astype(o_ref.dtype)

    return pl.pallas_call(
        body,
        grid=(tp // tr,),
        in_specs=[pl.BlockSpec((tr, f), lambda i: (i, 0)), pl.BlockSpec((tr, f), lambda i: (i, 1)),
                  pl.BlockSpec((tr, f), lambda i: (i, 0))],
        out_specs=pl.BlockSpec((tr, f2), lambda i: (i, 0)),
        out_shape=jax.ShapeDtypeStruct((tp, f2), CDT),
        compiler_params=_params("parallel"),
        name="swiglu_bwd",
    )(gu, gu, da)


def _swap_halves(x):
    lane = lax.broadcasted_iota(jnp.int32, x.shape, x.ndim - 1)
    first = (lane % QK_ROPE) < (QK_ROPE // 2)
    return jnp.where(first, pltpu.roll(x, LANES - QK_ROPE // 2, x.ndim - 1), pltpu.roll(x, QK_ROPE // 2, x.ndim - 1))


def _rope_tables(tp):
    pos = jnp.arange(tp, dtype=F32)
    inv_freq = ROPE_THETA ** (-jnp.arange(0, QK_ROPE, 2, dtype=F32) / QK_ROPE)
    ang = pos[:, None] * inv_freq[None, :]
    cos, sin = jnp.cos(ang), jnp.sin(ang)
    reps = LANES // QK_ROPE
    return jnp.tile(jnp.concatenate([cos, cos], -1), (1, reps)), jnp.tile(jnp.concatenate([-sin, sin], -1), (1, reps))


def _chunk_of(pos):
    shift = CHUNK.bit_length() - 1
    assert CHUNK == 1 << shift
    return jnp.where(pos < N_META, 0, 1 + lax.shift_right_arithmetic(pos - N_META, shift))


def _head_half(x, h):
    lane = lax.broadcasted_iota(jnp.int32, x.shape, x.ndim - 1)
    return jnp.where((lane // QK_ROPE) == (h % 2), x, jnp.zeros_like(x))


def _mla_prep(q, kv, proj, cos2, sin2):
    tp = q.shape[0]
    tr = _div_tile(tp, ROW_TILE, 8)
    nope_w = MLA_HEADS * QK_NOPE
    kr_block = (Q_LORA + KV_LORA) // LANES
    depth = QK_NOPE + LANES

    def body(q_ref, kv_ref, kr_ref, c_ref, s_ref, qp_out, kp_out):
        c = c_ref[...]
        s = s_ref[...]
        k = kr_ref[...]
        k = k + pltpu.roll(k, QK_ROPE, 1)
        k = (k * c + _swap_halves(k) * s).astype(CDT)
        for p in range(MLA_HEADS // 2):
            x = q_ref[:, nope_w + p * LANES:nope_w + (p + 1) * LANES]
            pair = (x * c + _swap_halves(x) * s).astype(CDT)
            for h in (2 * p, 2 * p + 1):
                qp_out[h, :, :QK_NOPE] = q_ref[:, h * QK_NOPE:(h + 1) * QK_NOPE].astype(CDT)
                qp_out[h, :, QK_NOPE:] = _head_half(pair, h)
                kp_out[h, :, :QK_NOPE] = kv_ref[:, 2 * h * QK_NOPE:(2 * h + 1) * QK_NOPE]
                kp_out[h, :, QK_NOPE:] = k

    tab = pl.BlockSpec((tr, LANES), lambda i: (i, 0))
    per_head = pl.BlockSpec((MLA_HEADS, tr, depth), lambda i: (0, i, 0))
    out = jax.ShapeDtypeStruct((MLA_HEADS, tp, depth), CDT)
    return pl.pallas_call(
        body,
        grid=(tp // tr,),
        in_specs=[pl.BlockSpec((tr, q.shape[1]), lambda i: (i, 0)), pl.BlockSpec((tr, kv.shape[1]), lambda i: (i, 0)),
                  pl.BlockSpec((tr, LANES), lambda i: (i, kr_block)), tab, tab],
        out_specs=[per_head, per_head],
        out_shape=[out, out],
        compiler_params=_params("parallel"),
        name="mla_prep",
    )(q, kv, proj, cos2, sin2)


def _dot_nt(a, b):
    return lax.dot_general(a, b, (((1,), (1,)), ((), ())), preferred_element_type=F32)


def _dot_tn(a, b):
    return lax.dot_general(a, b, (((0,), (0,)), ((), ())), preferred_element_type=F32)


def _dot(a, b):
    return jnp.dot(a, b, preferred_element_type=F32)


def _chunk_scalar(p):
    return jnp.where(p < N_META, 0, 1 + jnp.maximum(p - N_META, 0) // CHUNK)


def _last_key_block(i, bq, bk, nk):
    cq = _chunk_scalar(i * bq + bq - 1)
    return jnp.minimum((N_META + CHUNK * cq - 1) // bk, nk - 1)


def _full_key_blocks(i, bq, bk):
    return (N_META + CHUNK * _chunk_scalar(i * bq)) // bk


def _first_query_block(j, bk, bq):
    p0 = N_META + CHUNK * (jnp.maximum(j * bk - N_META, 0) // CHUNK)
    return p0 // bq


def _first_full_query_block(j, bk, bq, nq):
    ck = _chunk_scalar(j * bk + bk - 1)
    p0 = jnp.where(ck == 0, 0, N_META + CHUNK * (ck - 1))
    return jnp.minimum((p0 + bq - 1) // bq, nq)


def _chunk_mask(q0, k0, shape, keys_on_rows):
    if keys_on_rows:
        kc = _chunk_of(k0 + lax.broadcasted_iota(jnp.int32, (shape[0], 1), 0))
        qc = _chunk_of(q0 + lax.broadcasted_iota(jnp.int32, (1, shape[1]), 1))
    else:
        qc = _chunk_of(q0 + lax.broadcasted_iota(jnp.int32, (shape[0], 1), 0))
        kc = _chunk_of(k0 + lax.broadcasted_iota(jnp.int32, (1, shape[1]), 1))
    return kc <= qc


def _attn_fwd(qp, kp, kv):
    tp = qp.shape[1]
    depth = qp.shape[2]
    bq = bk = _div_tile(tp, ATT_TILE, LANES)
    nq, nk = tp // bq, tp // bk
    scale = (QK_NOPE + QK_ROPE) ** -0.5

    def body(q_ref, k_ref, v_ref, o_ref, lse_ref):
        def q_block(i, _):
            q0 = pl.multiple_of(i * bq, bq)
            qb = q_ref[pl.ds(q0, bq), :]

            def k_step(masked, j, carry):
                m_old, l_old, acc = carry
                k0 = pl.multiple_of(j * bk, bk)
                s = _dot_nt(qb, k_ref[pl.ds(k0, bk), :]) * scale
                if masked:
                    s = jnp.where(_chunk_mask(q0, k0, s.shape, False), s, NEG_BIG)
                m_new = jnp.maximum(m_old, jnp.max(s, axis=-1, keepdims=True))
                alpha = jnp.exp(m_old - m_new)
                p = jnp.exp(s - m_new)
                l_new = alpha * l_old + jnp.sum(p, axis=-1, keepdims=True)
                acc = alpha * acc + _dot(p.astype(CDT), v_ref[pl.ds(k0, bk), :])
                return m_new, l_new, acc

            n_full = _full_key_blocks(i, bq, bk)
            carry = (jnp.full((bq, 1), NEG_BIG, F32), jnp.zeros((bq, 1), F32), jnp.zeros((bq, V_HEAD), F32))
            carry = lax.fori_loop(0, n_full, lambda j, c: k_step(False, j, c), carry)
            m_fin, l_fin, acc = lax.fori_loop(n_full, _last_key_block(i, bq, bk, nk) + 1,
                                              lambda j, c: k_step(True, j, c), carry)
            o_ref[pl.ds(q0, bq), :] = acc / l_fin
            lse_ref[pl.ds(q0, bq), :] = m_fin + jnp.log(l_fin)
            return 0

        lax.fori_loop(0, nq, q_block, 0)

    per_head = pl.BlockSpec((None, tp, depth), lambda h: (h, 0, 0))
    return pl.pallas_call(
        body,
        grid=(MLA_HEADS,),
        in_specs=[per_head, per_head, pl.BlockSpec((tp, V_HEAD), lambda h: (0, 2 * h + 1))],
        out_specs=[pl.BlockSpec((tp, V_HEAD), lambda h: (0, h)), pl.BlockSpec((None, tp, 1), lambda h: (h, 0, 0))],
        out_shape=[jax.ShapeDtypeStruct((tp, MLA_HEADS * V_HEAD), F32), jax.ShapeDtypeStruct((MLA_HEADS, tp, 1), F32)],
        compiler_params=_params("parallel"),
        name="attn_fwd",
    )(qp, kp, kv)


def _attn_delta(d_out, out):
    tp = out.shape[0]
    tr = _div_tile(tp, ROW_TILE, 8)

    def body(do_ref, o_ref, d_ref):
        for h in range(MLA_HEADS):
            cols = slice(h * V_HEAD, (h + 1) * V_HEAD)
            d_ref[h] = jnp.sum(do_ref[:, cols] * o_ref[:, cols], axis=-1, keepdims=True)

    row = pl.BlockSpec((tr, MLA_HEADS * V_HEAD), lambda i: (i, 0))
    return pl.pallas_call(
        body,
        grid=(tp // tr,),
        in_specs=[row, row],
        out_specs=pl.BlockSpec((MLA_HEADS, tr, 1), lambda i: (0, i, 0)),
        out_shape=jax.ShapeDtypeStruct((MLA_HEADS, tp, 1), F32),
        compiler_params=_params("parallel"),
        name="attn_delta",
    )(d_out, out)


def _attn_bwd(qp, kp, kv, d_out, lse, delta):
    tp = qp.shape[1]
    depth = qp.shape[2]
    bq = bk = _div_tile(tp, ATT_TILE, LANES)
    nq, nk = tp // bq, tp // bk
    scale = (QK_NOPE + QK_ROPE) ** -0.5
    lse_rows = lse.reshape(MLA_HEADS, nq, 1, bq)
    delta_rows = delta.reshape(MLA_HEADS, nq, 1, bq)

    strip = _div_tile(bk, ATT_STRIP, 16)

    def body(q_ref, k_ref, v_ref, do_ref, lse_ref, dl_ref, dqn_ref, dqr_ref, dkv_ref, dkr_ref,
             dq_acc, dk_acc, dv_acc, s_scr, dp_scr, p_scr, ds_scr):
        h = pl.program_id(0)
        dq_acc[...] = jnp.zeros(dq_acc.shape, F32)

        def k_block(j, _):
            k0 = pl.multiple_of(j * bk, bk)
            kb = k_ref[pl.ds(k0, bk), :]
            vb = v_ref[pl.ds(k0, bk), :]
            dk_acc[...] = jnp.zeros(dk_acc.shape, F32)
            dv_acc[...] = jnp.zeros(dv_acc.shape, F32)

            def q_step(masked, i, _):
                q0 = pl.multiple_of(i * bq, bq)
                qb = q_ref[pl.ds(q0, bq), :]
                dob = do_ref[pl.ds(q0, bq), :].astype(CDT)
                s_scr[...] = _dot_nt(kb, qb)
                dp_scr[...] = _dot_nt(vb, dob)
                lse_row = lse_ref[i]
                delta_row = dl_ref[i]
                for r0 in range(0, bk, strip):
                    rows = slice(r0, r0 + strip)
                    s_t = s_scr[rows, :] * scale
                    if masked:
                        s_t = jnp.where(_chunk_mask(q0, k0 + r0, s_t.shape, True), s_t, NEG_BIG)
                    p_t = jnp.exp(s_t - lse_row)
                    p_scr[rows, :] = p_t.astype(CDT)
                    ds_scr[rows, :] = (p_t * (dp_scr[rows, :] - delta_row) * scale).astype(CDT)
                ds_t = ds_scr[...]
                dv_acc[...] += _dot(p_scr[...], dob)
                dk_acc[...] += _dot(ds_t, qb)
                dq_acc[pl.ds(q0, bq), :] += _dot_tn(ds_t, kb)
                return 0

            i_full = _first_full_query_block(j, bk, bq, nq)
            lax.fori_loop(_first_query_block(j, bk, bq), i_full, lambda i, c: q_step(True, i, c), 0)
            lax.fori_loop(i_full, nq, lambda i, c: q_step(False, i, c), 0)
            dkv_ref[pl.ds(k0, bk), :QK_NOPE] = dk_acc[:, :QK_NOPE].astype(CDT)
            dkv_ref[pl.ds(k0, bk), QK_NOPE:] = dv_acc[...].astype(CDT)
            dkr_ref[pl.ds(k0, bk), :] = dk_acc[:, QK_NOPE:]
            return 0

        lax.fori_loop(0, nk, k_block, 0)
        dqn_ref[...] = dq_acc[:, :QK_NOPE].astype(CDT)
        dqr_ref[...] = _head_half(dq_acc[:, QK_NOPE:], h)

    per_head = pl.BlockSpec((None, tp, depth), lambda h: (h, 0, 0))
    stat = pl.BlockSpec((None, nq, 1, bq), lambda h: (h, 0, 0, 0))
    lanes_out = pl.BlockSpec((None, tp, LANES), lambda h: (h, 0, 0))
    return pl.pallas_call(
        body,
        grid=(MLA_HEADS,),
        in_specs=[per_head, per_head, pl.BlockSpec((tp, V_HEAD), lambda h: (0, 2 * h + 1)),
                  pl.BlockSpec((tp, V_HEAD), lambda h: (0, h)), stat, stat],
        out_specs=[pl.BlockSpec((tp, QK_NOPE), lambda h: (0, h)), lanes_out,
                   pl.BlockSpec((tp, QK_NOPE + V_HEAD), lambda h: (0, h)), lanes_out],
        out_shape=[jax.ShapeDtypeStruct((tp, MLA_HEADS * QK_NOPE), CDT), jax.ShapeDtypeStruct((MLA_HEADS, tp, LANES), F32),
                   jax.ShapeDtypeStruct((tp, MLA_HEADS * (QK_NOPE + V_HEAD)), CDT),
                   jax.ShapeDtypeStruct((MLA_HEADS, tp, LANES), F32)],
        scratch_shapes=[pltpu.VMEM((tp, depth), F32), pltpu.VMEM((bk, depth), F32), pltpu.VMEM((bk, V_HEAD), F32),
                        pltpu.VMEM((bk, bq), F32), pltpu.VMEM((bk, bq), F32), pltpu.VMEM((bk, bq), CDT),
                        pltpu.VMEM((bk, bq), CDT)],
        compiler_params=_params("parallel"),
        name="attn_bwd",
    )(qp, kp, kv, d_out, lse_rows, delta_rows)


def _mla_unprep(dqr_h, dkr_h, cos2, sin2):
    tp = dqr_h.shape[1]
    tr = _div_tile(tp, ROW_TILE, 8)
    wr = MLA_HEADS * QK_ROPE

    def body(dq_ref, dk_ref, c_ref, s_ref, dqr_out, dkr_out):
        c = c_ref[...]
        s = s_ref[...]
        for p in range(MLA_HEADS // 2):
            x = dq_ref[2 * p] + dq_ref[2 * p + 1]
            dqr_out[:, p * LANES:(p + 1) * LANES] = (x * c - _swap_halves(x) * s).astype(CDT)
        t = dk_ref[0]
        for h in range(1, MLA_HEADS):
            t = t + dk_ref[h]
        t = t * c - _swap_halves(t) * s
        t = t + pltpu.roll(t, QK_ROPE, 1)
        lane = lax.broadcasted_iota(jnp.int32, t.shape, 1)
        dkr_out[...] = jnp.where(lane < QK_ROPE, t, 0.0)

    per_head = pl.BlockSpec((MLA_HEADS, tr, LANES), lambda i: (0, i, 0))
    tab = pl.BlockSpec((tr, LANES), lambda i: (i, 0))
    return pl.pallas_call(
        body,
        grid=(tp // tr,),
        in_specs=[per_head, per_head, tab, tab],
        out_specs=[pl.BlockSpec((tr, wr), lambda i: (i, 0)), tab],
        out_shape=[jax.ShapeDtypeStruct((tp, wr), CDT), jax.ShapeDtypeStruct((tp, LANES), F32)],
        compiler_params=_params("parallel"),
        name="mla_unprep",
    )(dqr_h, dkr_h, cos2, sin2)


HALO = 8


def _softplus(x):
    return jnp.maximum(x, 0.0) + jnp.log1p(jnp.exp(-jnp.abs(x)))


def _one_minus_sq(log_a, a):
    return -jnp.tanh(log_a) * (a * a + 1.0)


def _gelu(y):
    k = math.sqrt(2.0 / math.pi)
    return 0.5 * y * (1.0 + jnp.tanh(k * (y + 0.044715 * (y * y * y))))


def _gelu_grad(y):
    k = math.sqrt(2.0 / math.pi)
    th = jnp.tanh(k * (y + 0.044715 * (y * y * y)))
    return 0.5 * (1.0 + th) + 0.5 * y * (1.0 - th * th) * (k * (1.0 + 3.0 * 0.044715 * (y * y)))


def _lru_gates_fwd(xy, conv_w, conv_b, w_ga, b_ga, w_gx, b_gx, lam):
    tp = xy.shape[0]
    dr = xy.shape[1] // 2
    bw = dr // RNN_BLOCKS
    tr = _div_tile(tp, ROW_TILE, 8)

    def body(x_ref, halo_ref, cw_ref, cb_ref, wa_ref, ba_ref, wx_ref, bx_ref, lam_ref,
             xc_ref, r_ref, i_ref, a_ref, b_ref, xs):
        i = pl.program_id(0)
        xs[0:HALO, :] = jnp.where(i == 0, 0.0, halo_ref[...])
        xs[HALO:, :] = x_ref[...]
        xc = cb_ref[...] + cw_ref[0:1, :] * xs[pl.ds(HALO - CONV_W + 1, tr), :]
        for j in range(1, CONV_W):
            xc = xc + cw_ref[j:j + 1, :] * xs[pl.ds(HALO - CONV_W + 1 + j, tr), :]
        xcb = xc.astype(CDT)
        r = _sigmoid(_dot(xcb, wa_ref[...]) + ba_ref[...])
        ig = _sigmoid(_dot(xcb, wx_ref[...]) + bx_ref[...])
        log_a = (-LRU_C * r) * _softplus(-lam_ref[...])
        a = jnp.exp(log_a)
        xc_ref[...] = xc
        r_ref[...] = r
        i_ref[...] = ig
        a_ref[...] = a
        b_ref[...] = jnp.sqrt(_one_minus_sq(log_a, a)) * (ig * xc)

    blk = pl.BlockSpec((tr, bw), lambda i, n: (i, n))
    vec = pl.BlockSpec((1, bw), lambda i, n: (0, n))
    mat = pl.BlockSpec((None, bw, bw), lambda i, n: (n, 0, 0))
    bias = pl.BlockSpec((None, 1, bw), lambda i, n: (n, 0, 0))
    out = jax.ShapeDtypeStruct((tp, dr), F32)
    return pl.pallas_call(
        body,
        grid=(tp // tr, RNN_BLOCKS),
        in_specs=[blk, pl.BlockSpec((HALO, bw), lambda i, n: (jnp.maximum(i * (tr // HALO) - 1, 0), n)),
                  pl.BlockSpec((CONV_W, bw), lambda i, n: (0, n)), vec, mat, bias, mat, bias, vec],
        out_specs=[blk] * 5,
        out_shape=[out] * 5,
        scratch_shapes=[pltpu.VMEM((tr + HALO, bw), F32)],
        compiler_params=_params("parallel", "parallel"),
        name="lru_gates_fwd",
    )(xy, xy, conv_w, conv_b.reshape(1, dr), w_ga.astype(CDT), b_ga.reshape(RNN_BLOCKS, 1, bw),
      w_gx.astype(CDT), b_gx.reshape(RNN_BLOCKS, 1, bw), lam.reshape(1, dr))


def _stack_rows(rows):
    idx = lax.broadcasted_iota(jnp.int32, (len(rows), rows[0].shape[1]), 0)
    out = jnp.broadcast_to(rows[0], idx.shape)
    for j in range(1, len(rows)):
        out = jnp.where(idx == j, jnp.broadcast_to(rows[j], idx.shape), out)
    return out


def _lru_scan_fwd(a, b, xy):
    tp, dr = a.shape
    cw = min(2 * SCAN_COLS, dr)
    ycol0 = dr // cw
    ch = _div_tile(tp, ROW_TILE, 16)

    def body(a_ref, b_ref, y_ref, hs_ref, m_ref):
        def group(g, h):
            base = pl.multiple_of(g * 8, 8)
            at = a_ref[pl.ds(base, 8), :]
            bt = b_ref[pl.ds(base, 8), :]
            rows = []
            for j in range(8):
                h = at[j:j + 1, :] * h + bt[j:j + 1, :]
                rows.append(h)
            hs_ref[pl.ds(base, 8), :] = _stack_rows(rows)
            return h

        lax.fori_loop(0, tp // 8, group, jnp.zeros((1, cw), F32))

        def gate(c, _):
            r0 = pl.multiple_of(c * ch, ch)
            m_ref[pl.ds(r0, ch), :] = (hs_ref[pl.ds(r0, ch), :] * _gelu(y_ref[pl.ds(r0, ch), :])).astype(CDT)
            return 0

        lax.fori_loop(0, tp // ch, gate, 0)

    col = pl.BlockSpec((tp, cw), lambda n: (0, n))
    return pl.pallas_call(
        body,
        grid=(dr // cw,),
        in_specs=[col, col, pl.BlockSpec((tp, cw), lambda n: (0, ycol0 + n))],
        out_specs=[col, col],
        out_shape=[jax.ShapeDtypeStruct((tp, dr), F32), jax.ShapeDtypeStruct((tp, dr), CDT)],
        compiler_params=_params("parallel"),
        name="lru_scan_fwd",
    )(a, b, xy)


def _lru_scan_bwd(a, hs, dm, xy):
    tp, dr = a.shape
    cw = min(2 * SCAN_COLS, dr)
    ycol0 = dr // cw
    ng = tp // 8
    ch = _div_tile(tp, ROW_TILE, 16)

    def body(a_ref, hs_ref, dm_ref, y_ref, db_ref, da_ref, dy_ref):
        def ungate(c, _):
            rows = pl.ds(pl.multiple_of(c * ch, ch), ch)
            y = y_ref[rows, :]
            dm = dm_ref[rows, :]
            db_ref[rows, :] = dm * _gelu(y)
            dy_ref[rows, :] = (dm * hs_ref[rows, :] * _gelu_grad(y)).astype(CDT)
            return 0

        lax.fori_loop(0, tp // ch, ungate, 0)

        def group(k, carry):
            g_next, a_next = carry
            g = ng - 1 - k
            base = pl.multiple_of(g * 8, 8)
            prev = pl.multiple_of(jnp.maximum(g - 1, 0) * 8, 8)
            dt = db_ref[pl.ds(base, 8), :]
            at = a_ref[pl.ds(base, 8), :]
            ht = hs_ref[pl.ds(base, 8), :]
            h_before = jnp.where(g == 0, 0.0, hs_ref[pl.ds(prev, 8), :][7:8, :])
            g_rows = [None] * 8
            da_rows = [None] * 8
            for j in range(7, -1, -1):
                g_cur = dt[j:j + 1, :] + a_next * g_next
                g_rows[j] = g_cur
                da_rows[j] = g_cur * (ht[j - 1:j, :] if j > 0 else h_before)
                g_next = g_cur
                a_next = at[j:j + 1, :]
            db_ref[pl.ds(base, 8), :] = _stack_rows(g_rows)
            da_ref[pl.ds(base, 8), :] = _stack_rows(da_rows)
            return g_next, a_next

        zero = jnp.zeros((1, cw), F32)
        lax.fori_loop(0, ng, group, (zero, zero))

    col = pl.BlockSpec((tp, cw), lambda n: (0, n))
    col_in = pl.BlockSpec((tp, cw), lambda n: (0, n), pipeline_mode=pl.Buffered(1))
    y_in = pl.BlockSpec((tp, cw), lambda n: (0, ycol0 + n), pipeline_mode=pl.Buffered(1))
    return pl.pallas_call(
        body,
        grid=(dr // cw,),
        in_specs=[col_in, col_in, col_in, y_in],
        out_specs=[col, col, col],
        out_shape=[jax.ShapeDtypeStruct((tp, dr), F32), jax.ShapeDtypeStruct((tp, dr), F32),
                   jax.ShapeDtypeStruct((tp, dr), CDT)],
        compiler_params=_params("parallel"),
        name="lru_scan_bwd",
    )(a, hs, dm, xy)


def _lru_gates_bwd(db, da, xc, r, ig, a, lam, w_ga, w_gx):
    tp, dr = xc.shape
    bw = dr // RNN_BLOCKS
    tr = _div_tile(tp, ROW_TILE, 8)
    nr = tp // tr

    def body(db_ref, da_ref, xc_ref, r_ref, i_ref, a_ref, lam_ref, wa_ref, wx_ref,
             dxc_ref, dwa_ref, dba_ref, dwx_ref, dbx_ref, dlam_ref):
        i = pl.program_id(1)
        xc = xc_ref[...]
        r = r_ref[...]
        ig = i_ref[...]
        a = a_ref[...]
        dbv = db_ref[...]
        sp = _softplus(-lam_ref[...])
        log_a = (-LRU_C * r) * sp
        s = jnp.sqrt(_one_minus_sq(log_a, a))
        d_ix = dbv * s
        d_s = dbv * (ig * xc)
        d_log_a = da_ref[...] * a - d_s * (a * a) / s
        d_r = d_log_a * (-LRU_C * sp)
        d_sp = jnp.sum(d_log_a * (-LRU_C * r), axis=0, keepdims=True)
        dzr = d_r * r * (1.0 - r)
        dzi = (d_ix * xc) * ig * (1.0 - ig)
        dzr_b = dzr.astype(CDT)
        dzi_b = dzi.astype(CDT)
        xcb = xc.astype(CDT)
        dxc_ref[...] = d_ix * ig + _dot_nt(dzr_b, wa_ref[...]) + _dot_nt(dzi_b, wx_ref[...])
        dwa = _dot_tn(xcb, dzr_b)
        dwx = _dot_tn(xcb, dzi_b)
        dba = jnp.sum(dzr, axis=0, keepdims=True)
        dbx = jnp.sum(dzi, axis=0, keepdims=True)

        @pl.when(i == 0)
        def _():
            dwa_ref[...] = dwa
            dwx_ref[...] = dwx
            dba_ref[...] = dba
            dbx_ref[...] = dbx
            dlam_ref[...] = d_sp

        @pl.when(i > 0)
        def _():
            dwa_ref[...] += dwa
            dwx_ref[...] += dwx
            dba_ref[...] += dba
            dbx_ref[...] += dbx
            dlam_ref[...] += d_sp

        @pl.when(i == nr - 1)
        def _():
            dlam_ref[...] = dlam_ref[...] * (-_sigmoid(-lam_ref[...]))

    blk = pl.BlockSpec((tr, bw), lambda n, i: (i, n))
    vec = pl.BlockSpec((1, bw), lambda n, i: (0, n))
    mat = pl.BlockSpec((None, bw, bw), lambda n, i: (n, 0, 0))
    bias = pl.BlockSpec((None, 1, bw), lambda n, i: (n, 0, 0))
    return pl.pallas_call(
        body,
        grid=(RNN_BLOCKS, nr),
        in_specs=[blk] * 6 + [vec, mat, mat],
        out_specs=[blk, mat, bias, mat, bias, vec],
        out_shape=[jax.ShapeDtypeStruct((tp, dr), F32),
                   jax.ShapeDtypeStruct((RNN_BLOCKS, bw, bw), F32), jax.ShapeDtypeStruct((RNN_BLOCKS, 1, bw), F32),
                   jax.ShapeDtypeStruct((RNN_BLOCKS, bw, bw), F32), jax.ShapeDtypeStruct((RNN_BLOCKS, 1, bw), F32),
                   jax.ShapeDtypeStruct((1, dr), F32)],
        compiler_params=_params("parallel", "arbitrary"),
        name="lru_gates_bwd",
    )(db, da, xc, r, ig, a, lam.reshape(1, dr), w_ga.astype(CDT), w_gx.astype(CDT))


def _lru_conv_bwd(dxc, xy, conv_w):
    tp, dr = dxc.shape
    bw = dr // RNN_BLOCKS
    tr = _div_tile(tp, ROW_TILE, 8)
    nr = tp // tr
    per = tr // HALO

    def body(d_ref, dnext_ref, x_ref, xprev_ref, cw_ref, dxb_ref, dcw_ref, dcb_ref, ds, xs):
        i = pl.program_id(1)
        d = d_ref[...]
        ds[0:tr, :] = d
        ds[tr:, :] = jnp.where(i == nr - 1, 0.0, dnext_ref[...])
        xs[0:HALO, :] = jnp.where(i == 0, 0.0, xprev_ref[...])
        xs[HALO:, :] = x_ref[...]
        dxb = cw_ref[0:1, :] * ds[pl.ds(CONV_W - 1, tr), :]
        for j in range(1, CONV_W):
            dxb = dxb + cw_ref[j:j + 1, :] * ds[pl.ds(CONV_W - 1 - j, tr), :]
        dxb_ref[...] = dxb.astype(CDT)
        dcb = jnp.sum(d, axis=0, keepdims=True)
        dcw = [jnp.sum(d * xs[pl.ds(HALO - CONV_W + 1 + j, tr), :], axis=0, keepdims=True) for j in range(CONV_W)]

        @pl.when(i == 0)
        def _():
            dcb_ref[...] = dcb
            for j in range(CONV_W):
                dcw_ref[j] = dcw[j]

        @pl.when(i > 0)
        def _():
            dcb_ref[...] += dcb
            for j in range(CONV_W):
                dcw_ref[j] += dcw[j]

    blk = pl.BlockSpec((tr, bw), lambda n, i: (i, n))
    return pl.pallas_call(
        body,
        grid=(RNN_BLOCKS, nr),
        in_specs=[blk, pl.BlockSpec((HALO, bw), lambda n, i: (jnp.minimum((i + 1) * per, tp // HALO - 1), n)),
                  blk, pl.BlockSpec((HALO, bw), lambda n, i: (jnp.maximum(i * per - 1, 0), n)),
                  pl.BlockSpec((CONV_W, bw), lambda n, i: (0, n))],
        out_specs=[blk, pl.BlockSpec((CONV_W, 1, bw), lambda n, i: (0, 0, n)), pl.BlockSpec((1, bw), lambda n, i: (0, n))],
        out_shape=[jax.ShapeDtypeStruct((tp, dr), CDT), jax.ShapeDtypeStruct((CONV_W, 1, dr), F32),
                   jax.ShapeDtypeStruct((1, dr), F32)],
        scratch_shapes=[pltpu.VMEM((tr + HALO, bw), F32), pltpu.VMEM((tr + HALO, bw), F32)],
        compiler_params=_params("parallel", "arbitrary"),
        name="lru_conv_bwd",
    )(dxc, dxc, xy, xy, conv_w)


def _me():
    return lax.axis_index("x"), lax.axis_index("y"), lax.axis_index("c")


def _peer(rel):
    x, y, c = _me()
    return (1 - x if rel & 4 else x, 1 - y if rel & 2 else y, 1 - c if rel & 1 else c)


def _chip_of(dev):
    return 2 * dev[0] + dev[1]


def _linear(dev):
    return 4 * dev[0] + 2 * dev[1] + dev[2]


CHIP_RELS = (4, 2, 6)
ALL_RELS = (1, 2, 3, 4, 5, 6, 7)
PAIR_RELS = (1,)


def _scatter_send(pieces, rels, piece_of, name):
    n = len(rels)

    def body(src_ref, recv_ref, send_sems, recv_sems):
        copies = []
        for k, rel in enumerate(rels):
            peer = _peer(rel)
            cp = pltpu.make_async_remote_copy(
                src_ref=src_ref.at[piece_of(peer)], dst_ref=recv_ref.at[k],
                send_sem=send_sems.at[k], recv_sem=recv_sems.at[k], device_id=peer, device_id_type=MESH)
            cp.start()
            copies.append(cp)
        for cp in copies:
            cp.wait()

    return pl.pallas_call(
        body,
        in_specs=[pl.BlockSpec(memory_space=pl.ANY)],
        out_specs=pl.BlockSpec(memory_space=pl.ANY),
        out_shape=jax.ShapeDtypeStruct((n,) + pieces.shape[1:], pieces.dtype),
        scratch_shapes=[pltpu.SemaphoreType.DMA((n,)), pltpu.SemaphoreType.DMA((n,))],
        name=name,
    )(pieces)


def _gather_send(piece, rels, n_slots, slot_of, name, n_chunks=1):
    n = len(rels)
    rows = piece.shape[0]
    if rows % (8 * n_chunks):
        n_chunks = 1
    rc = rows // n_chunks

    def body(src_ref, out_ref, send_sems, recv_sems, local_sems):
        me = _me()

        def part(ref, q):
            return ref.at[pl.ds(q * rc, rc)]

        def remote(k, q, slot_dev, to):
            return pltpu.make_async_remote_copy(
                src_ref=part(src_ref, q), dst_ref=part(out_ref.at[slot_of(slot_dev)], q),
                send_sem=send_sems.at[k * n_chunks + q], recv_sem=recv_sems.at[k * n_chunks + q],
                device_id=to, device_id_type=MESH)

        mine = [pltpu.make_async_copy(part(src_ref, q), part(out_ref.at[slot_of(me)], q), local_sems.at[q])
                for q in range(n_chunks)]
        for cp in mine:
            cp.start()
        sends = [remote(k, q, me, _peer(rel)) for k, rel in enumerate(rels) for q in range(n_chunks)]
        for cp in sends:
            cp.start()
        for k, rel in enumerate(rels):
            for q in range(n_chunks):
                remote(k, q, _peer(rel), _peer(rel)).wait_recv()
        for cp in sends:
            cp.wait_send()
        for cp in mine:
            cp.wait()

    return pl.pallas_call(
        body,
        in_specs=[pl.BlockSpec(memory_space=pl.ANY)],
        out_specs=pl.BlockSpec(memory_space=pl.ANY),
        out_shape=jax.ShapeDtypeStruct((n_slots,) + piece.shape, piece.dtype),
        scratch_shapes=[pltpu.SemaphoreType.DMA((n * n_chunks,)), pltpu.SemaphoreType.DMA((n * n_chunks,)),
                        pltpu.SemaphoreType.DMA((n_chunks,))],
        name=name,
    )(piece)


def _gather_chips(shard, name):
    return _gather_send(shard, CHIP_RELS, N_CHIPS, _chip_of, name)


HBM_SPEC = pl.BlockSpec(memory_space=pltpu.HBM)
SEM_SPEC = pl.BlockSpec(memory_space=pltpu.SEMAPHORE)
DATAFLOW = pltpu.SideEffectType.DATAFLOW_SIDE_EFFECTING


def _split_start(src, land, copies, name):
    def body(src_ref, land_ref, send_sem, recv_sem, src_thru, land_thru, token):
        for s_ref, d_ref, peer in copies(src_ref, land_ref):
            pltpu.make_async_remote_copy(src_ref=s_ref, dst_ref=d_ref, send_sem=send_sem, recv_sem=recv_sem,
                                         device_id=peer, device_id_type=MESH).start()
        token[...] = jnp.zeros(token.shape, token.dtype)

    return pl.pallas_call(
        body,
        name=name,
        out_shape=(pltpu.SemaphoreType.DMA(()), pltpu.SemaphoreType.DMA(()), pltpu.HBM(src.shape, src.dtype),
                   pltpu.HBM(land.shape, land.dtype), jax.ShapeDtypeStruct((8, LANES), F32)),
        in_specs=(HBM_SPEC, HBM_SPEC),
        out_specs=(SEM_SPEC, SEM_SPEC, HBM_SPEC, HBM_SPEC, pl.BlockSpec(memory_space=pltpu.VMEM)),
        input_output_aliases={0: 2, 1: 3},
        compiler_params=pltpu.CompilerParams(has_side_effects=DATAFLOW),
    )(pltpu.with_memory_space_constraint(src, pltpu.HBM), pltpu.with_memory_space_constraint(land, pltpu.HBM))


def _split_gather_start(piece, rels, n_slots, slot_of, name):
    land = jnp.broadcast_to(piece[None], (n_slots,) + piece.shape)
    return _split_start(piece, land, lambda s, l: [(s, l.at[slot_of(_me())], _peer(rel)) for rel in rels], name)


def _gather_chips_start(shard, name):
    return _split_gather_start(shard, CHIP_RELS, N_CHIPS, _chip_of, name)


def _split_scatter_start(pieces, name):
    land = lax.empty((len(ALL_RELS),) + pieces.shape[1:], pieces.dtype)
    return _split_start(pieces, land,
                        lambda s, l: [(s.at[_linear(_peer(rel))], l.at[k], _peer(rel)) for k, rel in enumerate(ALL_RELS)], name)


def _split_wait(started, n, after, name, with_src=False):
    send_sem, recv_sem, src_thru, land_thru, _ = started

    def body(src_ref, land_ref, send_sem, recv_sem, after_ref, src_dead, got_ref):
        all_n = land_ref.at[pl.ds(0, n)]
        arrivals = pltpu.make_async_remote_copy(
            src_ref=all_n, dst_ref=all_n, send_sem=send_sem, recv_sem=recv_sem, device_id=_me(), device_id_type=MESH)
        arrivals.wait_send()
        arrivals.wait_recv()

    out = pl.pallas_call(
        body,
        name=name,
        out_shape=(pltpu.HBM(src_thru.shape, src_thru.dtype), pltpu.HBM(land_thru.shape, land_thru.dtype)),
        in_specs=(HBM_SPEC, HBM_SPEC, SEM_SPEC, SEM_SPEC, pl.BlockSpec(memory_space=pl.ANY)),
        out_specs=(HBM_SPEC, HBM_SPEC),
        input_output_aliases={0: 0, 1: 1},
        compiler_params=pltpu.CompilerParams(has_side_effects=DATAFLOW),
    )(src_thru, land_thru, send_sem, recv_sem, after)
    return (out[1], out[0]) if with_src else out[1]


def _sum_pieces(pieces, recv, name):
    _, rr, cc = pieces.shape
    n = recv.shape[0]
    tr = _div_tile(rr, max(8, (1 << 17) // cc // 8 * 8), 8)

    def body(own_ref, recv_ref, o_ref):
        acc = own_ref[...].astype(F32)
        for k in range(n):
            acc = acc + recv_ref[k].astype(F32)
        o_ref[...] = acc

    return pl.pallas_call(
        body,
        grid=(rr // tr,),
        in_specs=[pl.BlockSpec((None, tr, cc), lambda i: (_linear(_me()), i, 0)),
                  pl.BlockSpec((n, tr, cc), lambda i: (0, i, 0))],
        out_specs=pl.BlockSpec((tr, cc), lambda i: (i, 0)),
        out_shape=jax.ShapeDtypeStruct((rr, cc), F32),
        compiler_params=_params("parallel"),
        name=name,
    )(pieces, recv)


def _reduce_to_owner(g8, payload_dtype, name):
    recv = _scatter_send(g8.astype(payload_dtype), ALL_RELS, _linear, name + "_scatter")
    return _sum_pieces(g8, recv, name + "_sum")


def _adamw_layer(w, g, m, v, outs, layer, name):
    nl, rr, cc = w.shape
    _, gr, gc = g.shape
    tr = _div_tile(gr, max(8, (1 << 17) // gc // 8 * 8), 8)
    steps = gr // tr
    c1 = 1.0 - ADAM_B1 ** ADAM_STEP
    c2 = 1.0 - ADAM_B2 ** ADAM_STEP
    if gc == cc:
        assert 2 * gr == rr, (name, g.shape, w.shape)
        slab = pl.BlockSpec((None, tr, gc), lambda h, i: (layer, h * steps + i, 0))
    else:
        assert gr == rr and 2 * gc == cc, (name, g.shape, w.shape)
        slab = pl.BlockSpec((None, tr, gc), lambda h, i: (layer, i, h))

    def body(w_ref, g_ref, m_ref, v_ref, *rest):
        go_ref, d_ref, mo_ref, vo_ref = rest[-4:]
        g_ = g_ref[...]
        m_ = ADAM_B1 * m_ref[...] + (1.0 - ADAM_B1) * g_
        v_ = ADAM_B2 * v_ref[...] + (1.0 - ADAM_B2) * (g_ * g_)
        go_ref[...] = g_
        d_ref[...] = -ADAM_LR * ((m_ / c1) / (jnp.sqrt(v_ / c2) + ADAM_EPS) + ADAM_WD * w_ref[...])
        mo_ref[...] = m_
        vo_ref[...] = v_

    out = jax.ShapeDtypeStruct((nl, rr, cc), F32)
    in_specs = [slab, pl.BlockSpec((None, tr, gc), lambda h, i: (h, i, 0)), slab, slab]
    args = [w, g, m, v]
    aliases = {}
    if outs is not None:
        in_specs += [pl.BlockSpec(memory_space=pl.ANY)] * 4
        args += list(outs)
        aliases = {4 + k: k for k in range(4)}
    return pl.pallas_call(
        body,
        grid=(2, steps),
        in_specs=in_specs,
        out_specs=[slab] * 4,
        out_shape=[out] * 4,
        input_output_aliases=aliases,
        compiler_params=_params("parallel", "parallel"),
        name=name,
    )(*args)


def _adamw(w, g, m, v, name):
    rr, cc = w.shape
    tr = _div_tile(rr, max(8, (1 << 17) // cc // 8 * 8), 8)
    c1 = 1.0 - ADAM_B1 ** ADAM_STEP
    c2 = 1.0 - ADAM_B2 ** ADAM_STEP

    def body(w_ref, g_ref, m_ref, v_ref, d_ref, mo_ref, vo_ref):
        g_ = g_ref[...]
        m_ = ADAM_B1 * m_ref[...] + (1.0 - ADAM_B1) * g_
        v_ = ADAM_B2 * v_ref[...] + (1.0 - ADAM_B2) * (g_ * g_)
        d_ref[...] = -ADAM_LR * ((m_ / c1) / (jnp.sqrt(v_ / c2) + ADAM_EPS) + ADAM_WD * w_ref[...])
        mo_ref[...] = m_
        vo_ref[...] = v_

    blk = pl.BlockSpec((tr, cc), lambda i: (i, 0))
    out = jax.ShapeDtypeStruct((rr, cc), F32)
    return pl.pallas_call(
        body,
        grid=(rr // tr,),
        in_specs=[blk] * 4,
        out_specs=[blk] * 3,
        out_shape=[out] * 3,
        compiler_params=_params("parallel"),
        name=name,
    )(w, g, m, v)


def _pack(arrays, cols, row_mult):
    flat = jnp.concatenate([a.reshape(-1) for a in arrays])
    rows = -(-flat.shape[0] // cols)
    rows = -(-rows // row_mult) * row_mult
    return jnp.pad(flat, (0, rows * cols - flat.shape[0])).reshape(rows, cols)


def _unpack(buf, shapes):
    flat = buf.reshape(-1)
    out, off = [], 0
    for s in shapes:
        n = math.prod(s)
        out.append(flat[off:off + n].reshape(s))
        off += n
    return out


def kernel(x, meta_tokens, norm_mix, norm_ffn, norm_final, mla_w_in, mla_q_norm, mla_kv_norm, mla_w_uq, mla_w_ukv, mla_w_o, lru_w_in, lru_conv_w, lru_conv_b, lru_w_gate_a, lru_b_gate_a, lru_w_gate_x, lru_b_gate_x, lru_lambda, lru_w_o, ffn_w_gu, ffn_w_down, loss_target, m_meta_tokens, m_norm_mix, m_norm_ffn, m_norm_final, m_mla_w_in, m_mla_q_norm, m_mla_kv_norm, m_mla_w_uq, m_mla_w_ukv, m_mla_w_o, m_lru_w_in, m_lru_conv_w, m_lru_conv_b, m_lru_w_gate_a, m_lru_b_gate_a, m_lru_w_gate_x, m_lru_b_gate_x, m_lru_lambda, m_lru_w_o, m_ffn_w_gu, m_ffn_w_down, v_meta_tokens, v_norm_mix, v_norm_ffn, v_norm_final, v_mla_w_in, v_mla_q_norm, v_mla_kv_norm, v_mla_w_uq, v_mla_w_ukv, v_mla_w_o, v_lru_w_in, v_lru_conv_w, v_lru_conv_b, v_lru_w_gate_a, v_lru_b_gate_a, v_lru_w_gate_x, v_lru_b_gate_x, v_lru_lambda, v_lru_w_o, v_ffn_w_gu, v_ffn_w_down):
    d = D_MODEL
    t_real = N_META + SEQ
    tp = _t_pad()
    n_mla = mla_w_in.shape[0]
    n_lru = lru_w_in.shape[0]
    h_dim = MLA_HEADS * V_HEAD
    w_in_cols = Q_LORA + KV_LORA + QK_ROPE
    w_in_pad = Q_LORA + KV_LORA + LANES
    q_cols = MLA_HEADS * (QK_NOPE + QK_ROPE)
    tmm = _div_tile(tp, MM_ROW_TILE, 16)
    tkt = _div_tile(tp, 1408, 16)

    def tile(n, pref):
        return _div_tile(n, pref, LANES)

    small_shapes = [meta_tokens.shape, lru_conv_w.shape, lru_conv_b.shape, lru_lambda.shape]
    csh = meta_tokens.shape[1]
    small4 = _gather_chips(_pack([meta_tokens, lru_conv_w, lru_conv_b, lru_lambda], csh, 16), "gather_small")
    small4, mla_w_in = lax.optimization_barrier((small4, mla_w_in))
    started = {}

    def start(key, shard):
        prev = list(started.values())[-1][4][0, 0] if started else 0.0
        started[key] = _gather_chips_start((shard + prev).astype(CDT), "gather_" + key + "_start")

    def arrived(key, after):
        return _split_wait(started[key], len(CHIP_RELS), after, "gather_" + key + "_wait")

    def start_ffn(layer):
        start(f"w_gu{layer}", ffn_w_gu[layer:layer + 1])
        start(f"w_down{layer}", ffn_w_down[layer:layer + 1])

    start("w_in", jnp.pad(mla_w_in, ((0, 0), (0, 0), (0, w_in_pad - w_in_cols))))
    start("w_uq", mla_w_uq)
    start("w_ukv", mla_w_ukv)
    start("w_o", mla_w_o)
    start_ffn(0)
    start("lw_in", lru_w_in)
    start("lw_o", lru_w_o)
    for layer in range(1, DEPTH):
        start_ffn(layer)
    all_started = list(started.values())[-1][4][0, 0]
    n_gu = ffn_w_gu.shape[2]
    w_gu4, w_down4 = [None] * DEPTH, [None] * DEPTH
    small_full = [jnp.concatenate(parts, axis=-1) for parts in zip(*[_unpack(small4[k], small_shapes) for k in range(N_CHIPS)])]
    meta_full, conv_w_full, conv_b_full, lam_full = small_full

    cos2, sin2 = _rope_tables(tp)

    h = jnp.concatenate([meta_full, x[0], jnp.zeros((tp - t_real, d), F32)], axis=0) + all_started
    saved = []
    for layer in range(DEPTH):
        j = layer // 2
        s = {"h_in": h}
        hn = _rms_fwd(h, norm_mix[layer], width=d, col_block=0, name="norm_mix_fwd")
        s["hn"] = hn
        if layer == 0:
            w_in4, w_uq4, w_ukv4, w_o4 = (arrived(k, hn) for k in ("w_in", "w_uq", "w_ukv", "w_o"))
            w_uq_full = jnp.moveaxis(w_uq4, 0, 2).reshape(n_mla, Q_LORA, MLA_HEADS, QK_NOPE + QK_ROPE)
            w_uq_perm = jnp.concatenate([w_uq_full[..., :QK_NOPE].reshape(n_mla, Q_LORA, -1),
                                         w_uq_full[..., QK_NOPE:].reshape(n_mla, Q_LORA, -1)], axis=-1)
        if layer == 1:
            lw_in4, lw_o4 = arrived("lw_in", hn), arrived("lw_o", hn)
        if layer % 2 == 0:
            proj = _mm("nn", hn, w_in4, kind="row", layer=j, tm=tmm, tn=tile(w_in_pad, 1152), tk=tile(d // N_CHIPS, 512), name="mla_in")
            c_q = _rms_fwd(proj, mla_q_norm[j], width=Q_LORA, col_block=0, name="q_norm_fwd")
            c_kv = _rms_fwd(proj, mla_kv_norm[j], width=KV_LORA, col_block=Q_LORA // KV_LORA, name="kv_norm_fwd")
            q = _mm("nn", c_q, w_uq_perm[j], tm=tmm, tn=tile(q_cols, 1024), tk=Q_LORA, name="mla_uq")
            kv = _mm("nn", c_kv, w_ukv4, kind="col", layer=j, out_dtype=CDT, tm=tmm, tn=tile(w_ukv4.shape[3], 1024), tk=KV_LORA, name="mla_ukv")
            qp, kp = _mla_prep(q, kv, proj, cos2, sin2)
            att, lse = _attn_fwd(qp, kp, kv)
            h = _mm("nn", att, w_o4, kind="row", layer=j, resid=h, tm=tmm, tn=tile(d, 1024), tk=tile(h_dim // N_CHIPS, 512), name="mla_out")
            s.update(proj=proj, c_q=c_q, c_kv=c_kv, qp=qp, kp=kp, kv=kv, att=att, lse=lse)
        else:
            xy = _mm("nn", hn, lw_in4, kind="col", layer=j, tm=tmm, tn=tile(lw_in4.shape[3], 1024), tk=d, name="lru_in")
            xc, r, ig, a, b = _lru_gates_fwd(xy, conv_w_full[j], conv_b_full[j], lru_w_gate_a[j], lru_b_gate_a[j],
                                             lru_w_gate_x[j], lru_b_gate_x[j], lam_full[j])
            hs, mixed = _lru_scan_fwd(a, b, xy)
            h = _mm("nn", mixed, lw_o4, kind="row", layer=j, resid=h, tm=tmm, tn=tile(d, 1024), tk=tile(d // N_CHIPS, 512), name="lru_out")
            s.update(xy=xy, xc=xc, r=r, ig=ig, a=a, hs=hs, mixed=mixed)
        s["h_mid"] = h
        hn2 = _rms_fwd(h, norm_ffn[layer], width=d, col_block=0, name="norm_ffn_fwd")
        w_gu4[layer], w_down4[layer] = arrived(f"w_gu{layer}", hn2), arrived(f"w_down{layer}", hn2)
        gu = _mm("nn", hn2, w_gu4[layer], kind="col", out_dtype=CDT, tm=tmm, tn=tile(n_gu, 1408), tk=d, name="ffn_gu")
        act = _swiglu_fwd(gu)
        h = _mm("nn", act, w_down4[layer], kind="row", resid=h, tm=tmm, tn=tile(d, 1024), tk=tile(D_FF // N_CHIPS, 1408), name="ffn_down")
        s.update(hn2=hn2, gu=gu, act=act)
        saved.append(s)

    target = jnp.concatenate([jnp.zeros((N_META, d), F32), loss_target[0], jnp.zeros((tp - t_real, d), F32)], axis=0)
    dh, dhb, g_norm_final, loss_part = _final_loss(h, norm_final, target)
    loss = lax.psum(loss_part[0, 0], ("x", "y", "c"))

    g_norm_mix, g_norm_ffn = [None] * DEPTH, [None] * DEPTH
    g_q_norm, g_kv_norm = [None] * n_mla, [None] * n_mla
    g_w_uq = [None] * n_mla
    g_gate = {k: [None] * n_lru for k in ("wa", "ba", "wx", "bx", "lam", "cw", "cb")}
    weights = {"w_in": (mla_w_in, m_mla_w_in, v_mla_w_in), "w_uq": (mla_w_uq, m_mla_w_uq, v_mla_w_uq),
               "w_ukv": (mla_w_ukv, m_mla_w_ukv, v_mla_w_ukv), "w_o": (mla_w_o, m_mla_w_o, v_mla_w_o),
               "lw_in": (lru_w_in, m_lru_w_in, v_lru_w_in), "lw_o": (lru_w_o, m_lru_w_o, v_lru_w_o),
               "w_gu": (ffn_w_gu, m_ffn_w_gu, v_ffn_w_gu), "w_down": (ffn_w_down, m_ffn_w_down, v_ffn_w_down)}
    res = {key: None for key in weights}
    units = []

    def reduce_start(key, lyr, pieces):
        tag = f"{key}{lyr}"
        units.append({"key": key, "layer": lyr, "tag": tag, "pieces": pieces, "stage": 0, "age": 0,
                      "copy": _split_scatter_start(pieces, "reduce_" + tag + "_scatter_start")})
        return units[-1]["copy"][4]

    def reduce_advance(after, everything=False):
        tokens = []
        for u in units:
            key, lyr, tag = u["key"], u["layer"], u["tag"]
            if u["stage"] == 1:
                both = _split_wait(u["copy"], 1, after, "reduce_" + tag + "_pair_wait")
                g = both[:, :, :w_in_cols] if key == "w_in" else both
                w, m, v = weights[key]
                res[key] = _adamw_layer(w, g, m, v, res[key], lyr, "adamw_" + key)
                u["stage"] = 2
            elif u["stage"] == 0 and (u["age"] > 0 or everything):
                recv, pieces = _split_wait(u["copy"], len(ALL_RELS), after, "reduce_" + tag + "_scatter_wait", with_src=True)
                red = _sum_pieces(pieces, recv, "reduce_" + key + "_sum")
                u["copy"] = _split_gather_start(red, PAIR_RELS, 2, lambda dev: dev[2], "reduce_" + tag + "_pair_start")
                tokens.append(u["copy"][4])
                u["stage"] = 1
            u["age"] += 1
        return tokens

    def grad_w(key, a_op, b_op, kind, lyr, tm, tn):
        return reduce_start(key, lyr, _mm_tn(a_op, b_op, kind=kind, tm=tm, tn=tn, tk=tkt, name="grad_" + key))

    nope_w = MLA_HEADS * QK_NOPE
    for layer in reversed(range(DEPTH)):
        tokens = []
        j = layer // 2
        s = saved[layer]
        t_down = grad_w("w_down", s["act"], dhb, "row_colhalves", layer, tile(D_FF // N_CHIPS, 1408), tile(d // 2, 1024))
        d_act = _mm("nt", dhb, w_down4[layer], kind="row", out_dtype=CDT, tm=tmm, tn=tile(D_FF // N_CHIPS, 1408), tk=d, name="ffn_down_bwd")
        dgu = _swiglu_bwd(s["gu"], d_act)
        t_gu = grad_w("w_gu", s["hn2"], dgu, "col", layer, tile(d // 2, 1024), tile(n_gu, 1408))
        dhn2 = _mm("nt", dgu, w_gu4[layer], kind="col", tm=tmm, tn=tile(d, 1024), tk=tile(n_gu, 2816), name="ffn_gu_bwd")
        dh, dhb, g_norm_ffn[layer] = _rms_bwd(s["h_mid"], norm_ffn[layer] + (t_down[0, 0] + t_gu[0, 0]), dhn2, dh,
                                              width=d, col_block=0, name="norm_ffn_bwd")
        if layer % 2 == 0:
            t_mix = grad_w("w_o", s["att"], dhb, "row", j, tile(h_dim // N_DEV, 256), tile(d, 1024))[0, 0]
            d_att = _mm("nt", dhb, w_o4, kind="row", layer=j, tm=tmm, tn=tile(h_dim // N_CHIPS, 512), tk=d, name="mla_out_bwd")
            delta = _attn_delta(d_att, s["att"])
            dqn, dqr_h, dkv, dkr_h = _attn_bwd(s["qp"], s["kp"], s["kv"], d_att, s["lse"], delta)
            dqr, dkr = _mla_unprep(dqr_h, dkr_h, cos2, sin2)
            dq = jnp.concatenate([dqn, dqr], axis=-1)
            g_uq = _mm_tn(s["c_q"], dq, tm=Q_LORA, tn=tile(q_cols, 1024), tk=tkt, name="grad_w_uq")
            g_uq = jnp.concatenate([g_uq[:, :nope_w].reshape(Q_LORA, MLA_HEADS, QK_NOPE),
                                    g_uq[:, nope_w:].reshape(Q_LORA, MLA_HEADS, QK_ROPE)], axis=-1)
            g_uq = g_uq.reshape(2, Q_LORA // 2, N_CHIPS, q_cols // N_CHIPS).transpose(2, 0, 1, 3)
            t_mix = t_mix + reduce_start("w_uq", j, g_uq.reshape(N_DEV, Q_LORA // 2, q_cols // N_CHIPS).astype(CDT))[0, 0]
            dc_q = _mm("nt", dq, w_uq_perm[j], tm=tmm, tn=Q_LORA, tk=tile(q_cols, 1024), name="mla_uq_bwd")
            t_mix = t_mix + grad_w("w_ukv", s["c_kv"], dkv, "col", j, tile(KV_LORA // 2, 256), tile(w_ukv4.shape[3], 1024))[0, 0]
            dc_kv = _mm("nt", dkv, w_ukv4, kind="col", layer=j, tm=tmm, tn=KV_LORA, tk=tile(w_ukv4.shape[3], 1024), name="mla_ukv_bwd")
            dpq, _, g_q_norm[j] = _rms_bwd(s["proj"], mla_q_norm[j] + t_mix, dc_q, None, width=Q_LORA, col_block=0, name="q_norm_bwd")
            dpkv, _, g_kv_norm[j] = _rms_bwd(s["proj"], mla_kv_norm[j], dc_kv, None, width=KV_LORA, col_block=Q_LORA // KV_LORA, name="kv_norm_bwd")
            dproj = jnp.concatenate([dpq, dpkv, dkr], axis=-1).astype(CDT)
            tokens.append(grad_w("w_in", s["hn"], dproj, "row", j, tile(d // N_DEV, 256), tile(w_in_pad, 1152)))
            dhn = _mm("nt", dproj, w_in4, kind="row", layer=j, tm=tmm, tn=tile(d // N_CHIPS, 512), tk=tile(w_in_pad, 1152), name="mla_in_bwd")
        else:
            t_mix = grad_w("lw_o", s["mixed"], dhb, "row", j, tile(d // N_DEV, 256), tile(d, 1024))[0, 0]
            dm = _mm("nt", dhb, lw_o4, kind="row", layer=j, tm=tmm, tn=tile(d // N_CHIPS, 512), tk=d, name="lru_out_bwd")
            db, da, dy = _lru_scan_bwd(s["a"], s["hs"], dm, s["xy"])
            dxc, g_gate["wa"][j], g_gate["ba"][j], g_gate["wx"][j], g_gate["bx"][j], g_gate["lam"][j] = _lru_gates_bwd(
                db, da, s["xc"], s["r"], s["ig"], s["a"], lam_full[j] + t_mix, lru_w_gate_a[j], lru_w_gate_x[j])
            dxb, g_gate["cw"][j], g_gate["cb"][j] = _lru_conv_bwd(dxc, s["xy"], conv_w_full[j])
            dxy = jnp.concatenate([dxb, dy], axis=-1)
            tokens.append(grad_w("lw_in", s["hn"], dxy, "col", j, tile(d // 2, 1024), tile(lw_in4.shape[3], 1024)))
            dhn = _mm("nt", dxy, lw_in4, kind="col", layer=j, tm=tmm, tn=tile(d, 1024), tk=tile(lw_in4.shape[3], 1024), name="lru_in_bwd")
        dh, dhb, g_norm_mix[layer] = _rms_bwd(s["h_in"], norm_mix[layer] + t_mix, dhn, dh, width=d, col_block=0, name="norm_mix_bwd")
        tokens += reduce_advance(dh)
        if layer > 0:
            dhb = dhb + sum(tok[0, 0] for tok in tokens).astype(CDT)
        else:
            dh = dh + sum(tok[0, 0] for tok in tokens)

    grad_x = dh[N_META:t_real][None]
    g_meta_full = dh[:N_META]

    for _ in range(3):
        reduce_advance(dh, everything=True)

    g_small_full = [g_meta_full, jnp.stack(g_gate["cw"]).reshape(n_lru, CONV_W, d), jnp.stack(g_gate["cb"]).reshape(n_lru, d),
                    jnp.stack(g_gate["lam"]).reshape(n_lru, d)]
    g_small4 = jnp.stack([_pack([a[..., k * csh:(k + 1) * csh] for a in g_small_full], csh, 16) for k in range(N_CHIPS)])
    rows_s = g_small4.shape[1]
    red = _reduce_to_owner(g_small4.reshape(N_DEV, rows_s // 2, csh), F32, "reduce_small")
    g_small = _gather_send(red, PAIR_RELS, 2, lambda dev: dev[2], "reduce_small_pair").reshape(rows_s, csh)
    small_w = [meta_tokens, lru_conv_w, lru_conv_b, lru_lambda]
    small_m = [m_meta_tokens, m_lru_conv_w, m_lru_conv_b, m_lru_lambda]
    small_v = [v_meta_tokens, v_lru_conv_w, v_lru_conv_b, v_lru_lambda]
    sd, sm, sv = _adamw(_pack(small_w, csh, 16), g_small, _pack(small_m, csh, 16), _pack(small_v, csh, 16), "adamw_small")
    small_out = [_unpack(buf, small_shapes) for buf in (g_small, sd, sm, sv)]

    rep_w = [norm_mix, norm_ffn, norm_final, mla_q_norm, mla_kv_norm, lru_w_gate_a, lru_b_gate_a, lru_w_gate_x, lru_b_gate_x]
    rep_m = [m_norm_mix, m_norm_ffn, m_norm_final, m_mla_q_norm, m_mla_kv_norm, m_lru_w_gate_a, m_lru_b_gate_a, m_lru_w_gate_x, m_lru_b_gate_x]
    rep_v = [v_norm_mix, v_norm_ffn, v_norm_final, v_mla_q_norm, v_mla_kv_norm, v_lru_w_gate_a, v_lru_b_gate_a, v_lru_w_gate_x, v_lru_b_gate_x]
    rep_g = [jnp.stack(g_norm_mix), jnp.stack(g_norm_ffn), g_norm_final, jnp.stack(g_q_norm), jnp.stack(g_kv_norm),
             jnp.stack(g_gate["wa"]), jnp.stack(g_gate["ba"]), jnp.stack(g_gate["wx"]), jnp.stack(g_gate["bx"])]
    rep_shapes = [w.shape for w in rep_w]
    g_rep = _pack(rep_g, LANES, 8 * N_DEV)
    rows_r = g_rep.shape[0]
    red = _reduce_to_owner(g_rep.reshape(N_DEV, rows_r // N_DEV, LANES), F32, "reduce_rep")
    g_rep = _gather_send(red, ALL_RELS, N_DEV, _linear, "reduce_rep_all").reshape(rows_r, LANES)
    rd, rm, rv = _adamw(_pack(rep_w, LANES, 8 * N_DEV), g_rep, _pack(rep_m, LANES, 8 * N_DEV), _pack(rep_v, LANES, 8 * N_DEV), "adamw_rep")
    rep_out = [_unpack(buf, rep_shapes) for buf in (g_rep, rd, rm, rv)]

    def leaf(kind):
        s_, r_ = small_out[kind], rep_out[kind]
        return [s_[0], r_[0], r_[1], r_[2], res["w_in"][kind], r_[3], r_[4], res["w_uq"][kind], res["w_ukv"][kind],
                res["w_o"][kind], res["lw_in"][kind], s_[1], s_[2], r_[5], r_[6], r_[7], r_[8], s_[3],
                res["lw_o"][kind], res["w_gu"][kind], res["w_down"][kind]]

    return (loss, grad_x, *leaf(0), *leaf(1), *leaf(2), *leaf(3))
```

```python
import math

import jax
import jax.numpy as jnp
from jax import lax
from jax.experimental import pallas as pl
from jax.experimental.pallas import tpu as pltpu

F32 = jnp.float32
CDT = jnp.bfloat16
MESH = pl.DeviceIdType.MESH

D_MODEL = 2048
SEQ = 4096
DEPTH = 4
CHUNK = 64
N_META = 16
MLA_HEADS = 16
Q_LORA = 512
KV_LORA = 512
QK_NOPE = 128
QK_ROPE = 64
V_HEAD = 128
ROPE_THETA = 10000.0
RNN_BLOCKS = 16
CONV_W = 4
LRU_C = 8.0
D_FF = 5632
RMS_EPS = 1e-6
NEG_BIG = -1e30
ADAM_LR = 0.001
ADAM_B1 = 0.9
ADAM_B2 = 0.999
ADAM_EPS = 1e-08
ADAM_WD = 0.01
ADAM_STEP = 10

N_CHIPS = 4
N_DEV = 8
LANES = 128
VMEM_LIMIT = 52 * 1024 * 1024
ROW_TILE = 384
MM_ROW_TILE = 704
ATT_TILE = 384
SCAN_COLS = 128
ATT_STRIP = 32


def _div_tile(n, pref, mult):
    if n <= pref:
        return n
    d = (pref // mult) * mult
    while d >= mult:
        if n % d == 0:
            return d
        d -= mult
    raise ValueError(f"no tile for {n} <= {pref} (multiple of {mult})")


def _t_pad():
    t = N_META + SEQ
    step = math.lcm(_row_tile_unit(), 8)
    return -(-t // step) * step


def _row_tile_unit():
    return math.lcm(math.lcm(ROW_TILE, MM_ROW_TILE), ATT_TILE)


def _params(*sem):
    return pltpu.CompilerParams(dimension_semantics=sem, vmem_limit_bytes=VMEM_LIMIT)


def _b_spec(form, b, kind, layer, t_out, t_con):
    if kind == "plain":
        if form == "nn":
            return pl.BlockSpec((t_con, t_out), lambda i, j, k: (k, j))
        return pl.BlockSpec((t_out, t_con), lambda i, j, k: (j, k))
    rows, cols = b.shape[2], b.shape[3]
    if form == "nn":
        blk = (None, None, t_con, t_out)
        if kind == "row":
            per = rows // t_con
            return pl.BlockSpec(blk, lambda i, j, k: (k // per, layer, k % per, j))
        per = cols // t_out
        return pl.BlockSpec(blk, lambda i, j, k: (j // per, layer, k, j % per))
    blk = (None, None, t_out, t_con)
    if kind == "row":
        per = rows // t_out
        return pl.BlockSpec(blk, lambda i, j, k: (j // per, layer, j % per, k))
    per = cols // t_con
    return pl.BlockSpec(blk, lambda i, j, k: (k // per, layer, j, k % per))


def _mm_body(nk, dims, has_resid):
    def body(*refs):
        if has_resid:
            a_ref, b_ref, r_ref, o_ref = refs[:4]
        else:
            a_ref, b_ref, o_ref = refs[:3]
        prod = lax.dot_general(a_ref[...].astype(CDT), b_ref[...].astype(CDT), (dims, ((), ())),
                               preferred_element_type=F32)

        def finish(acc):
            if has_resid:
                acc = acc + r_ref[...]
            o_ref[...] = acc.astype(o_ref.dtype)

        if nk == 1:
            finish(prod)
            return
        acc_ref = refs[-1]
        k = pl.program_id(2)

        @pl.when(k == 0)
        def _():
            acc_ref[...] = prod

        @pl.when(k > 0)
        def _():
            acc_ref[...] += prod

        @pl.when(k == nk - 1)
        def _():
            finish(acc_ref[...])

    return body


def _mm(form, a, b, *, kind="plain", layer=0, out_dtype=F32, resid=None, tm, tn, tk, name):
    m, con = a.shape
    if kind == "plain":
        w_rows, w_cols = b.shape
    elif kind == "row":
        w_rows, w_cols = b.shape[0] * b.shape[2], b.shape[3]
    else:
        w_rows, w_cols = b.shape[2], b.shape[0] * b.shape[3]
    n_out = w_cols if form == "nn" else w_rows
    assert con == (w_rows if form == "nn" else w_cols), (name, a.shape, b.shape)
    nk = con // tk
    assert m % tm == 0 and n_out % tn == 0 and con % tk == 0, (name, m, n_out, con, tm, tn, tk)
    dims = ((1,), (0,)) if form == "nn" else ((1,), (1,))
    in_specs = [pl.BlockSpec((tm, tk), lambda i, j, k: (i, k)), _b_spec(form, b, kind, layer, tn, tk)]
    args = [a, b]
    if resid is not None:
        in_specs.append(pl.BlockSpec((tm, tn), lambda i, j, k: (i, j)))
        args.append(resid)
    return pl.pallas_call(
        _mm_body(nk, dims, resid is not None),
        grid=(m // tm, n_out // tn, nk),
        in_specs=in_specs,
        out_specs=pl.BlockSpec((tm, tn), lambda i, j, k: (i, j)),
        out_shape=jax.ShapeDtypeStruct((m, n_out), out_dtype),
        scratch_shapes=[pltpu.VMEM((tm, tn), F32)] if nk > 1 else [],
        compiler_params=_params("parallel", "parallel", "arbitrary"),
        name=name,
    )(*args)


def _mm_tn(a, b, *, kind="plain", tm, tn, tk, name):
    t, m = a.shape
    n = b.shape[1]
    nk = t // tk
    assert t % tk == 0 and m % tm == 0 and n % tn == 0, (name, t, m, n, tm, tn, tk)
    if kind == "plain":
        out_shape = jax.ShapeDtypeStruct((m, n), F32)
        out_spec = pl.BlockSpec((tm, tn), lambda i, j, k: (i, j))
    elif kind == "row":
        per = (m // N_CHIPS) // tm
        assert per >= 2 and per % 2 == 0, (name, per)
        out_shape = jax.ShapeDtypeStruct((N_DEV, m // N_DEV, n), CDT)
        out_spec = pl.BlockSpec((None, tm, tn), lambda i, j, k: (2 * (i // per) + (i % per) // (per // 2), (i % per) % (per // 2), j))
    elif kind == "row_colhalves":
        per = (m // N_CHIPS) // tm
        nt = n // tn
        assert per >= 1 and nt % 2 == 0, (name, per, nt)
        out_shape = jax.ShapeDtypeStruct((N_DEV, m // N_CHIPS, n // 2), CDT)
        out_spec = pl.BlockSpec((None, tm, tn), lambda i, j, k: (2 * (i // per) + j // (nt // 2), i % per, j % (nt // 2)))
    else:
        per = (n // N_CHIPS) // tn
        mt = m // tm
        assert per >= 1 and mt % 2 == 0, (name, per, mt)
        out_shape = jax.ShapeDtypeStruct((N_DEV, m // 2, n // N_CHIPS), CDT)
        out_spec = pl.BlockSpec((None, tm, tn), lambda i, j, k: (2 * (j // per) + i // (mt // 2), i % (mt // 2), j % per))
    return pl.pallas_call(
        _mm_body(nk, ((0,), (0,)), False),
        grid=(m // tm, n // tn, nk),
        in_specs=[pl.BlockSpec((tk, tm), lambda i, j, k: (k, i)), pl.BlockSpec((tk, tn), lambda i, j, k: (k, j))],
        out_specs=out_spec,
        out_shape=out_shape,
        scratch_shapes=[pltpu.VMEM((tm, tn), F32)] if nk > 1 else [],
        compiler_params=_params("parallel", "parallel", "arbitrary"),
        name=name,
    )(a, b)


def _rms_fwd(x, g, *, width, col_block, name):
    tp = x.shape[0]
    tr = _div_tile(tp, ROW_TILE, 8)

    def body(x_ref, g_ref, o_ref):
        xf = x_ref[...]
        r = lax.rsqrt(jnp.mean(xf * xf, axis=-1, keepdims=True) + RMS_EPS)
        o_ref[...] = ((xf * r) * g_ref[...]).astype(o_ref.dtype)

    return pl.pallas_call(
        body,
        grid=(tp // tr,),
        in_specs=[pl.BlockSpec((tr, width), lambda i: (i, col_block)), pl.BlockSpec((1, width), lambda i: (0, 0))],
        out_specs=pl.BlockSpec((tr, width), lambda i: (i, 0)),
        out_shape=jax.ShapeDtypeStruct((tp, width), CDT),
        compiler_params=_params("parallel"),
        name=name,
    )(x, g.reshape(1, width))


def _rms_bwd(x, g, dy, resid, *, width, col_block, name):
    tp = x.shape[0]
    tr = _div_tile(tp, ROW_TILE, 8)
    has_resid = resid is not None

    def body(*refs):
        if has_resid:
            x_ref, g_ref, dy_ref, res_ref, dx_ref, dxb_ref, dg_ref = refs
        else:
            x_ref, g_ref, dy_ref, dx_ref, dxb_ref, dg_ref = refs
        i = pl.program_id(0)
        xf = x_ref[...]
        r = lax.rsqrt(jnp.mean(xf * xf, axis=-1, keepdims=True) + RMS_EPS)
        xh = xf * r
        dy = dy_ref[...].astype(F32)
        dg = jnp.sum(dy * xh, axis=0, keepdims=True)
        dxh = dy * g_ref[...]
        dx = r * (dxh - xh * jnp.mean(dxh * xh, axis=-1, keepdims=True))
        if has_resid:
            dx = dx + res_ref[...]
        dx_ref[...] = dx
        dxb_ref[...] = dx.astype(CDT)

        @pl.when(i == 0)
        def _():
            dg_ref[...] = dg

        @pl.when(i > 0)
        def _():
            dg_ref[...] += dg

    row = pl.BlockSpec((tr, width), lambda i: (i, 0))
    in_specs = [pl.BlockSpec((tr, width), lambda i: (i, col_block)), pl.BlockSpec((1, width), lambda i: (0, 0)), row]
    args = [x, g.reshape(1, width), dy]
    if has_resid:
        in_specs.append(row)
        args.append(resid)
    return pl.pallas_call(
        body,
        grid=(tp // tr,),
        in_specs=in_specs,
        out_specs=[row, row, pl.BlockSpec((1, width), lambda i: (0, 0))],
        out_shape=[jax.ShapeDtypeStruct((tp, width), F32), jax.ShapeDtypeStruct((tp, width), CDT),
                   jax.ShapeDtypeStruct((1, width), F32)],
        compiler_params=_params("arbitrary"),
        name=name,
    )(*args)


def _final_loss(h, g, target):
    tp, d = h.shape
    tr = _div_tile(tp, ROW_TILE, 8)

    def body(h_ref, g_ref, t_ref, dh_ref, dhb_ref, dg_ref, loss_ref):
        i = pl.program_id(0)
        xf = h_ref[...]
        r = lax.rsqrt(jnp.mean(xf * xf, axis=-1, keepdims=True) + RMS_EPS)
        xh = xf * r
        gain = g_ref[...]
        y = xh * gain
        rows = i * tr + lax.broadcasted_iota(jnp.int32, (tr, 1), 0)
        valid = jnp.logical_and(rows >= N_META, rows < N_META + SEQ)
        err = jnp.where(valid, y - t_ref[...], 0.0)
        part = 0.5 * jnp.sum(jnp.mean(err * err, axis=-1, keepdims=True), axis=0, keepdims=True)
        dy = err * (1.0 / d)
        dg = jnp.sum(dy * xh, axis=0, keepdims=True)
        dxh = dy * gain
        dx = r * (dxh - xh * jnp.mean(dxh * xh, axis=-1, keepdims=True))
        dh_ref[...] = dx
        dhb_ref[...] = dx.astype(CDT)

        @pl.when(i == 0)
        def _():
            dg_ref[...] = dg
            loss_ref[...] = part

        @pl.when(i > 0)
        def _():
            dg_ref[...] += dg
            loss_ref[...] += part

    row = pl.BlockSpec((tr, d), lambda i: (i, 0))
    vec = pl.BlockSpec((1, d), lambda i: (0, 0))
    return pl.pallas_call(
        body,
        grid=(tp // tr,),
        in_specs=[row, vec, row],
        out_specs=[row, row, vec, pl.BlockSpec((1, 1), lambda i: (0, 0))],
        out_shape=[jax.ShapeDtypeStruct((tp, d), F32), jax.ShapeDtypeStruct((tp, d), CDT),
                   jax.ShapeDtypeStruct((1, d), F32), jax.ShapeDtypeStruct((1, 1), F32)],
        compiler_params=_params("arbitrary"),
        name="final_loss",
    )(h, g.reshape(1, d), target)


def _sigmoid(x):
    return 1.0 / (1.0 + jnp.exp(-x))


def _ffn_gate_up(x, w4, *, tm, tn):
    t, d = x.shape
    n_chip = w4.shape[3]
    f = N_CHIPS * n_chip // 2
    per = n_chip // tn
    assert t % tm == 0 and n_chip % tn == 0, (t, tm, n_chip, tn)

    def body(x_ref, wg_ref, wu_ref, g_ref, u_ref, a_ref):
        xb = x_ref[...]
        g = _dot(xb, wg_ref[...])
        u = _dot(xb, wu_ref[...])
        g_ref[...] = g.astype(CDT)
        u_ref[...] = u.astype(CDT)
        a_ref[...] = ((g * _sigmoid(g)) * u).astype(CDT)

    w_blk = (None, None, d, tn)
    out_blk = pl.BlockSpec((tm, tn), lambda j, i: (i, j))
    out = jax.ShapeDtypeStruct((t, f), CDT)
    return pl.pallas_call(
        body,
        grid=(f // tn, t // tm),
        in_specs=[pl.BlockSpec((tm, d), lambda j, i: (i, 0)),
                  pl.BlockSpec(w_blk, lambda j, i: (j // per, 0, 0, j % per)),
                  pl.BlockSpec(w_blk, lambda j, i: (N_CHIPS // 2 + j // per, 0, 0, j % per))],
        out_specs=[out_blk, out_blk, out_blk],
        out_shape=[out, out, out],
        compiler_params=_params("parallel", "parallel"),
        name="ffn_gate_up",
    )(x, w4, w4)


def _swiglu_bwd(g, u, da):
    tp, f = g.shape
    f2 = 2 * f
    tr = _div_tile(tp, 64, 16)

    def body(g_ref, u_ref, da_ref, o_ref):
        g = g_ref[...].astype(F32)
        da = da_ref[...].astype(F32)
        sg = _sigmoid(g)
        o_ref[:, :f] = (da * u_ref[...].astype(F32) * (sg * (1.0 + g * (1.0 - sg)))).astype(o_ref.dtype)
        o_ref[:, f:] = (da * (g * sg)).astype(o_ref.dtype)

    return pl.pallas_call(
        body,
        grid=(tp // tr,),
        in_specs=[pl.BlockSpec((tr, f), lambda i: (i, 0))] * 3,
        out_specs=pl.BlockSpec((tr, f2), lambda i: (i, 0)),
        out_shape=jax.ShapeDtypeStruct((tp, f2), CDT),
        compiler_params=_params("parallel"),
        name="swiglu_bwd",
    )(g, u, da)


def _swap_halves(x):
    lane = lax.broadcasted_iota(jnp.int32, x.shape, x.ndim - 1)
    first = (lane % QK_ROPE) < (QK_ROPE // 2)
    return jnp.where(first, pltpu.roll(x, LANES - QK_ROPE // 2, x.ndim - 1), pltpu.roll(x, QK_ROPE // 2, x.ndim - 1))


def _rope_tables(tp):
    pos = jnp.arange(tp, dtype=F32)
    inv_freq = ROPE_THETA ** (-jnp.arange(0, QK_ROPE, 2, dtype=F32) / QK_ROPE)
    ang = pos[:, None] * inv_freq[None, :]
    cos, sin = jnp.cos(ang), jnp.sin(ang)
    reps = LANES // QK_ROPE
    return jnp.tile(jnp.concatenate([cos, cos], -1), (1, reps)), jnp.tile(jnp.concatenate([-sin, sin], -1), (1, reps))


def _chunk_of(pos):
    shift = CHUNK.bit_length() - 1
    assert CHUNK == 1 << shift
    return jnp.where(pos < N_META, 0, 1 + lax.shift_right_arithmetic(pos - N_META, shift))


def _head_half(x, h):
    lane = lax.broadcasted_iota(jnp.int32, x.shape, x.ndim - 1)
    return jnp.where((lane // QK_ROPE) == (h % 2), x, jnp.zeros_like(x))


def _mla_prep(q, kv, proj, cos2, sin2):
    tp = q.shape[0]
    tr = _div_tile(tp, ROW_TILE, 8)
    nope_w = MLA_HEADS * QK_NOPE
    kr_block = (Q_LORA + KV_LORA) // LANES
    depth = QK_NOPE + LANES

    def body(q_ref, kv_ref, kr_ref, c_ref, s_ref, qp_out, kp_out):
        c = c_ref[...]
        s = s_ref[...]
        k = kr_ref[...]
        k = k + pltpu.roll(k, QK_ROPE, 1)
        k = (k * c + _swap_halves(k) * s).astype(CDT)
        for p in range(MLA_HEADS // 2):
            x = q_ref[:, nope_w + p * LANES:nope_w + (p + 1) * LANES]
            pair = (x * c + _swap_halves(x) * s).astype(CDT)
            for h in (2 * p, 2 * p + 1):
                qp_out[h, :, :QK_NOPE] = q_ref[:, h * QK_NOPE:(h + 1) * QK_NOPE].astype(CDT)
                qp_out[h, :, QK_NOPE:] = _head_half(pair, h)
                kp_out[h, :, :QK_NOPE] = kv_ref[:, 2 * h * QK_NOPE:(2 * h + 1) * QK_NOPE]
                kp_out[h, :, QK_NOPE:] = k

    tab = pl.BlockSpec((tr, LANES), lambda i: (i, 0))
    per_head = pl.BlockSpec((MLA_HEADS, tr, depth), lambda i: (0, i, 0))
    out = jax.ShapeDtypeStruct((MLA_HEADS, tp, depth), CDT)
    return pl.pallas_call(
        body,
        grid=(tp // tr,),
        in_specs=[pl.BlockSpec((tr, q.shape[1]), lambda i: (i, 0)), pl.BlockSpec((tr, kv.shape[1]), lambda i: (i, 0)),
                  pl.BlockSpec((tr, LANES), lambda i: (i, kr_block)), tab, tab],
        out_specs=[per_head, per_head],
        out_shape=[out, out],
        compiler_params=_params("parallel"),
        name="mla_prep",
    )(q, kv, proj, cos2, sin2)


def _dot_nt(a, b):
    return lax.dot_general(a, b, (((1,), (1,)), ((), ())), preferred_element_type=F32)


def _dot_tn(a, b):
    return lax.dot_general(a, b, (((0,), (0,)), ((), ())), preferred_element_type=F32)


def _dot(a, b):
    return jnp.dot(a, b, preferred_element_type=F32)


def _chunk_scalar(p):
    return jnp.where(p < N_META, 0, 1 + jnp.maximum(p - N_META, 0) // CHUNK)


def _last_key_block(i, bq, bk, nk):
    cq = _chunk_scalar(i * bq + bq - 1)
    return jnp.minimum((N_META + CHUNK * cq - 1) // bk, nk - 1)


def _full_key_blocks(i, bq, bk):
    return (N_META + CHUNK * _chunk_scalar(i * bq)) // bk


def _first_query_block(j, bk, bq):
    p0 = N_META + CHUNK * (jnp.maximum(j * bk - N_META, 0) // CHUNK)
    return p0 // bq


def _first_full_query_block(j, bk, bq, nq):
    ck = _chunk_scalar(j * bk + bk - 1)
    p0 = jnp.where(ck == 0, 0, N_META + CHUNK * (ck - 1))
    return jnp.minimum((p0 + bq - 1) // bq, nq)


def _chunk_mask(q0, k0, shape, keys_on_rows):
    if keys_on_rows:
        kc = _chunk_of(k0 + lax.broadcasted_iota(jnp.int32, (shape[0], 1), 0))
        qc = _chunk_of(q0 + lax.broadcasted_iota(jnp.int32, (1, shape[1]), 1))
    else:
        qc = _chunk_of(q0 + lax.broadcasted_iota(jnp.int32, (shape[0], 1), 0))
        kc = _chunk_of(k0 + lax.broadcasted_iota(jnp.int32, (1, shape[1]), 1))
    return kc <= qc


def _attn_fwd(qp, kp, kv):
    tp = qp.shape[1]
    depth = qp.shape[2]
    bq = bk = _div_tile(tp, ATT_TILE, LANES)
    nq, nk = tp // bq, tp // bk
    scale = (QK_NOPE + QK_ROPE) ** -0.5

    def body(q_ref, k_ref, v_ref, o_ref, lse_ref):
        def q_block(i, _):
            q0 = pl.multiple_of(i * bq, bq)
            qb = q_ref[pl.ds(q0, bq), :]

            def k_step(masked, j, carry):
                m_old, l_old, acc = carry
                k0 = pl.multiple_of(j * bk, bk)
                s = _dot_nt(qb, k_ref[pl.ds(k0, bk), :]) * scale
                if masked:
                    s = jnp.where(_chunk_mask(q0, k0, s.shape, False), s, NEG_BIG)
                m_new = jnp.maximum(m_old, jnp.max(s, axis=-1, keepdims=True))
                alpha = jnp.exp(m_old - m_new)
                p = jnp.exp(s - m_new)
                l_new = alpha * l_old + jnp.sum(p, axis=-1, keepdims=True)
                acc = alpha * acc + _dot(p.astype(CDT), v_ref[pl.ds(k0, bk), :])
                return m_new, l_new, acc

            n_full = _full_key_blocks(i, bq, bk)
            carry = (jnp.full((bq, 1), NEG_BIG, F32), jnp.zeros((bq, 1), F32), jnp.zeros((bq, V_HEAD), F32))
            carry = lax.fori_loop(0, n_full, lambda j, c: k_step(False, j, c), carry)
            m_fin, l_fin, acc = lax.fori_loop(n_full, _last_key_block(i, bq, bk, nk) + 1,
                                              lambda j, c: k_step(True, j, c), carry)
            o_ref[pl.ds(q0, bq), :] = acc / l_fin
            lse_ref[pl.ds(q0, bq), :] = m_fin + jnp.log(l_fin)
            return 0

        lax.fori_loop(0, nq, q_block, 0)

    per_head = pl.BlockSpec((None, tp, depth), lambda h: (h, 0, 0))
    return pl.pallas_call(
        body,
        grid=(MLA_HEADS,),
        in_specs=[per_head, per_head, pl.BlockSpec((tp, V_HEAD), lambda h: (0, 2 * h + 1))],
        out_specs=[pl.BlockSpec((tp, V_HEAD), lambda h: (0, h)), pl.BlockSpec((None, tp, 1), lambda h: (h, 0, 0))],
        out_shape=[jax.ShapeDtypeStruct((tp, MLA_HEADS * V_HEAD), F32), jax.ShapeDtypeStruct((MLA_HEADS, tp, 1), F32)],
        compiler_params=_params("parallel"),
        name="attn_fwd",
    )(qp, kp, kv)


def _attn_delta(d_out, out):
    tp = out.shape[0]
    tr = _div_tile(tp, ROW_TILE, 8)

    def body(do_ref, o_ref, d_ref):
        for h in range(MLA_HEADS):
            cols = slice(h * V_HEAD, (h + 1) * V_HEAD)
            d_ref[h] = jnp.sum(do_ref[:, cols] * o_ref[:, cols], axis=-1, keepdims=True)

    row = pl.BlockSpec((tr, MLA_HEADS * V_HEAD), lambda i: (i, 0))
    return pl.pallas_call(
        body,
        grid=(tp // tr,),
        in_specs=[row, row],
        out_specs=pl.BlockSpec((MLA_HEADS, tr, 1), lambda i: (0, i, 0)),
        out_shape=jax.ShapeDtypeStruct((MLA_HEADS, tp, 1), F32),
        compiler_params=_params("parallel"),
        name="attn_delta",
    )(d_out, out)


def _attn_bwd(qp, kp, kv, d_out, lse, delta):
    tp = qp.shape[1]
    depth = qp.shape[2]
    bq = bk = _div_tile(tp, ATT_TILE, LANES)
    nq, nk = tp // bq, tp // bk
    scale = (QK_NOPE + QK_ROPE) ** -0.5
    lse_rows = lse.reshape(MLA_HEADS, nq, 1, bq)
    delta_rows = delta.reshape(MLA_HEADS, nq, 1, bq)

    strip = _div_tile(bk, ATT_STRIP, 16)

    def body(q_ref, k_ref, v_ref, do_ref, lse_ref, dl_ref, dqn_ref, dqr_ref, dkv_ref, dkr_ref,
             dq_acc, dk_acc, dv_acc, s_scr, dp_scr, p_scr, ds_scr):
        h = pl.program_id(0)
        dq_acc[...] = jnp.zeros(dq_acc.shape, F32)

        def k_block(j, _):
            k0 = pl.multiple_of(j * bk, bk)
            kb = k_ref[pl.ds(k0, bk), :]
            vb = v_ref[pl.ds(k0, bk), :]
            dk_acc[...] = jnp.zeros(dk_acc.shape, F32)
            dv_acc[...] = jnp.zeros(dv_acc.shape, F32)

            def q_step(masked, i, _):
                q0 = pl.multiple_of(i * bq, bq)
                qb = q_ref[pl.ds(q0, bq), :]
                dob = do_ref[pl.ds(q0, bq), :].astype(CDT)
                s_scr[...] = _dot_nt(kb, qb)
                dp_scr[...] = _dot_nt(vb, dob)
                lse_row = lse_ref[i]
                delta_row = dl_ref[i]
                for r0 in range(0, bk, strip):
                    rows = slice(r0, r0 + strip)
                    s_t = s_scr[rows, :] * scale
                    if masked:
                        s_t = jnp.where(_chunk_mask(q0, k0 + r0, s_t.shape, True), s_t, NEG_BIG)
                    p_t = jnp.exp(s_t - lse_row)
                    p_scr[rows, :] = p_t.astype(CDT)
                    ds_scr[rows, :] = (p_t * (dp_scr[rows, :] - delta_row) * scale).astype(CDT)
                ds_t = ds_scr[...]
                dv_acc[...] += _dot(p_scr[...], dob)
                dk_acc[...] += _dot(ds_t, qb)
                dq_acc[pl.ds(q0, bq), :] += _dot_tn(ds_t, kb)
                return 0

            i_full = _first_full_query_block(j, bk, bq, nq)
            lax.fori_loop(_first_query_block(j, bk, bq), i_full, lambda i, c: q_step(True, i, c), 0)
            lax.fori_loop(i_full, nq, lambda i, c: q_step(False, i, c), 0)
            dkv_ref[pl.ds(k0, bk), :QK_NOPE] = dk_acc[:, :QK_NOPE].astype(CDT)
            dkv_ref[pl.ds(k0, bk), QK_NOPE:] = dv_acc[...].astype(CDT)
            dkr_ref[pl.ds(k0, bk), :] = dk_acc[:, QK_NOPE:]
            return 0

        lax.fori_loop(0, nk, k_block, 0)
        dqn_ref[...] = dq_acc[:, :QK_NOPE].astype(CDT)
        dqr_ref[...] = _head_half(dq_acc[:, QK_NOPE:], h)

    per_head = pl.BlockSpec((None, tp, depth), lambda h: (h, 0, 0))
    stat = pl.BlockSpec((None, nq, 1, bq), lambda h: (h, 0, 0, 0))
    lanes_out = pl.BlockSpec((None, tp, LANES), lambda h: (h, 0, 0))
    return pl.pallas_call(
        body,
        grid=(MLA_HEADS,),
        in_specs=[per_head, per_head, pl.BlockSpec((tp, V_HEAD), lambda h: (0, 2 * h + 1)),
                  pl.BlockSpec((tp, V_HEAD), lambda h: (0, h)), stat, stat],
        out_specs=[pl.BlockSpec((tp, QK_NOPE), lambda h: (0, h)), lanes_out,
                   pl.BlockSpec((tp, QK_NOPE + V_HEAD), lambda h: (0, h)), lanes_out],
        out_shape=[jax.ShapeDtypeStruct((tp, MLA_HEADS * QK_NOPE), CDT), jax.ShapeDtypeStruct((MLA_HEADS, tp, LANES), F32),
                   jax.ShapeDtypeStruct((tp, MLA_HEADS * (QK_NOPE + V_HEAD)), CDT),
                   jax.ShapeDtypeStruct((MLA_HEADS, tp, LANES), F32)],
        scratch_shapes=[pltpu.VMEM((tp, depth), F32), pltpu.VMEM((bk, depth), F32), pltpu.VMEM((bk, V_HEAD), F32),
                        pltpu.VMEM((bk, bq), F32), pltpu.VMEM((bk, bq), F32), pltpu.VMEM((bk, bq), CDT),
                        pltpu.VMEM((bk, bq), CDT)],
        compiler_params=_params("parallel"),
        name="attn_bwd",
    )(qp, kp, kv, d_out, lse_rows, delta_rows)


def _mla_unprep(dqr_h, dkr_h, cos2, sin2):
    tp = dqr_h.shape[1]
    tr = _div_tile(tp, ROW_TILE, 8)
    wr = MLA_HEADS * QK_ROPE

    def body(dq_ref, dk_ref, c_ref, s_ref, dqr_out, dkr_out):
        c = c_ref[...]
        s = s_ref[...]
        for p in range(MLA_HEADS // 2):
            x = dq_ref[2 * p] + dq_ref[2 * p + 1]
            dqr_out[:, p * LANES:(p + 1) * LANES] = (x * c - _swap_halves(x) * s).astype(CDT)
        t = dk_ref[0]
        for h in range(1, MLA_HEADS):
            t = t + dk_ref[h]
        t = t * c - _swap_halves(t) * s
        t = t + pltpu.roll(t, QK_ROPE, 1)
        lane = lax.broadcasted_iota(jnp.int32, t.shape, 1)
        dkr_out[...] = jnp.where(lane < QK_ROPE, t, 0.0)

    per_head = pl.BlockSpec((MLA_HEADS, tr, LANES), lambda i: (0, i, 0))
    tab = pl.BlockSpec((tr, LANES), lambda i: (i, 0))
    return pl.pallas_call(
        body,
        grid=(tp // tr,),
        in_specs=[per_head, per_head, tab, tab],
        out_specs=[pl.BlockSpec((tr, wr), lambda i: (i, 0)), tab],
        out_shape=[jax.ShapeDtypeStruct((tp, wr), CDT), jax.ShapeDtypeStruct((tp, LANES), F32)],
        compiler_params=_params("parallel"),
        name="mla_unprep",
    )(dqr_h, dkr_h, cos2, sin2)


HALO = 8


def _softplus(x):
    return jnp.maximum(x, 0.0) + jnp.log1p(jnp.exp(-jnp.abs(x)))


def _one_minus_sq(log_a, a):
    return -jnp.tanh(log_a) * (a * a + 1.0)


def _gelu(y):
    k = math.sqrt(2.0 / math.pi)
    return 0.5 * y * (1.0 + jnp.tanh(k * (y + 0.044715 * (y * y * y))))


def _gelu_grad(y):
    k = math.sqrt(2.0 / math.pi)
    th = jnp.tanh(k * (y + 0.044715 * (y * y * y)))
    return 0.5 * (1.0 + th) + 0.5 * y * (1.0 - th * th) * (k * (1.0 + 3.0 * 0.044715 * (y * y)))


def _lru_gates_fwd(xy, conv_w, conv_b, w_ga, b_ga, w_gx, b_gx, lam):
    tp = xy.shape[0]
    dr = xy.shape[1] // 2
    bw = dr // RNN_BLOCKS
    tr = _div_tile(tp, ROW_TILE, 8)

    def body(x_ref, halo_ref, cw_ref, cb_ref, wa_ref, ba_ref, wx_ref, bx_ref, lam_ref,
             xc_ref, r_ref, i_ref, a_ref, b_ref, xs):
        i = pl.program_id(0)
        xs[0:HALO, :] = jnp.where(i == 0, 0.0, halo_ref[...])
        xs[HALO:, :] = x_ref[...]
        xc = cb_ref[...] + cw_ref[0:1, :] * xs[pl.ds(HALO - CONV_W + 1, tr), :]
        for j in range(1, CONV_W):
            xc = xc + cw_ref[j:j + 1, :] * xs[pl.ds(HALO - CONV_W + 1 + j, tr), :]
        xcb = xc.astype(CDT)
        r = _sigmoid(_dot(xcb, wa_ref[...]) + ba_ref[...])
        ig = _sigmoid(_dot(xcb, wx_ref[...]) + bx_ref[...])
        log_a = (-LRU_C * r) * _softplus(-lam_ref[...])
        a = jnp.exp(log_a)
        xc_ref[...] = xc
        r_ref[...] = r
        i_ref[...] = ig
        a_ref[...] = a
        b_ref[...] = jnp.sqrt(_one_minus_sq(log_a, a)) * (ig * xc)

    blk = pl.BlockSpec((tr, bw), lambda i, n: (i, n))
    vec = pl.BlockSpec((1, bw), lambda i, n: (0, n))
    mat = pl.BlockSpec((None, bw, bw), lambda i, n: (n, 0, 0))
    bias = pl.BlockSpec((None, 1, bw), lambda i, n: (n, 0, 0))
    out = jax.ShapeDtypeStruct((tp, dr), F32)
    return pl.pallas_call(
        body,
        grid=(tp // tr, RNN_BLOCKS),
        in_specs=[blk, pl.BlockSpec((HALO, bw), lambda i, n: (jnp.maximum(i * (tr // HALO) - 1, 0), n)),
                  pl.BlockSpec((CONV_W, bw), lambda i, n: (0, n)), vec, mat, bias, mat, bias, vec],
        out_specs=[blk] * 5,
        out_shape=[out] * 5,
        scratch_shapes=[pltpu.VMEM((tr + HALO, bw), F32)],
        compiler_params=_params("parallel", "parallel"),
        name="lru_gates_fwd",
    )(xy, xy, conv_w, conv_b.reshape(1, dr), w_ga.astype(CDT), b_ga.reshape(RNN_BLOCKS, 1, bw),
      w_gx.astype(CDT), b_gx.reshape(RNN_BLOCKS, 1, bw), lam.reshape(1, dr))


def _stack_rows(rows):
    idx = lax.broadcasted_iota(jnp.int32, (len(rows), rows[0].shape[1]), 0)
    out = jnp.broadcast_to(rows[0], idx.shape)
    for j in range(1, len(rows)):
        out = jnp.where(idx == j, jnp.broadcast_to(rows[j], idx.shape), out)
    return out


def _lru_scan_fwd(a, b, xy):
    tp, dr = a.shape
    cw = min(2 * SCAN_COLS, dr)
    ycol0 = dr // cw
    ch = _div_tile(tp, ROW_TILE, 16)

    def body(a_ref, b_ref, y_ref, hs_ref, m_ref):
        def group(g, h):
            base = pl.multiple_of(g * 8, 8)
            at = a_ref[pl.ds(base, 8), :]
            bt = b_ref[pl.ds(base, 8), :]
            rows = []
            for j in range(8):
                h = at[j:j + 1, :] * h + bt[j:j + 1, :]
                rows.append(h)
            hs_ref[pl.ds(base, 8), :] = _stack_rows(rows)
            return h

        lax.fori_loop(0, tp // 8, group, jnp.zeros((1, cw), F32))

        def gate(c, _):
            r0 = pl.multiple_of(c * ch, ch)
            m_ref[pl.ds(r0, ch), :] = (hs_ref[pl.ds(r0, ch), :] * _gelu(y_ref[pl.ds(r0, ch), :])).astype(CDT)
            return 0

        lax.fori_loop(0, tp // ch, gate, 0)

    col = pl.BlockSpec((tp, cw), lambda n: (0, n))
    return pl.pallas_call(
        body,
        grid=(dr // cw,),
        in_specs=[col, col, pl.BlockSpec((tp, cw), lambda n: (0, ycol0 + n))],
        out_specs=[col, col],
        out_shape=[jax.ShapeDtypeStruct((tp, dr), F32), jax.ShapeDtypeStruct((tp, dr), CDT)],
        compiler_params=_params("parallel"),
        name="lru_scan_fwd",
    )(a, b, xy)


def _lru_scan_bwd(a, hs, dm, xy):
    tp, dr = a.shape
    cw = min(2 * SCAN_COLS, dr)
    ycol0 = dr // cw
    ng = tp // 8
    ch = _div_tile(tp, ROW_TILE, 16)

    def body(a_ref, hs_ref, dm_ref, y_ref, db_ref, da_ref, dy_ref):
        def ungate(c, _):
            rows = pl.ds(pl.multiple_of(c * ch, ch), ch)
            y = y_ref[rows, :]
            dm = dm_ref[rows, :]
            db_ref[rows, :] = dm * _gelu(y)
            dy_ref[rows, :] = (dm * hs_ref[rows, :] * _gelu_grad(y)).astype(CDT)
            return 0

        lax.fori_loop(0, tp // ch, ungate, 0)

        def group(k, carry):
            g_next, a_next = carry
            g = ng - 1 - k
            base = pl.multiple_of(g * 8, 8)
            prev = pl.multiple_of(jnp.maximum(g - 1, 0) * 8, 8)
            dt = db_ref[pl.ds(base, 8), :]
            at = a_ref[pl.ds(base, 8), :]
            ht = hs_ref[pl.ds(base, 8), :]
            h_before = jnp.where(g == 0, 0.0, hs_ref[pl.ds(prev, 8), :][7:8, :])
            g_rows = [None] * 8
            da_rows = [None] * 8
            for j in range(7, -1, -1):
                g_cur = dt[j:j + 1, :] + a_next * g_next
                g_rows[j] = g_cur
                da_rows[j] = g_cur * (ht[j - 1:j, :] if j > 0 else h_before)
                g_next = g_cur
                a_next = at[j:j + 1, :]
            db_ref[pl.ds(base, 8), :] = _stack_rows(g_rows)
            da_ref[pl.ds(base, 8), :] = _stack_rows(da_rows)
            return g_next, a_next

        zero = jnp.zeros((1, cw), F32)
        lax.fori_loop(0, ng, group, (zero, zero))

    col = pl.BlockSpec((tp, cw), lambda n: (0, n))
    col_in = pl.BlockSpec((tp, cw), lambda n: (0, n), pipeline_mode=pl.Buffered(1))
    y_in = pl.BlockSpec((tp, cw), lambda n: (0, ycol0 + n), pipeline_mode=pl.Buffered(1))
    return pl.pallas_call(
        body,
        grid=(dr // cw,),
        in_specs=[col_in, col_in, col_in, y_in],
        out_specs=[col, col, col],
        out_shape=[jax.ShapeDtypeStruct((tp, dr), F32), jax.ShapeDtypeStruct((tp, dr), F32),
                   jax.ShapeDtypeStruct((tp, dr), CDT)],
        compiler_params=_params("parallel"),
        name="lru_scan_bwd",
    )(a, hs, dm, xy)


def _lru_gates_bwd(db, da, xc, r, ig, a, lam, w_ga, w_gx):
    tp, dr = xc.shape
    bw = dr // RNN_BLOCKS
    tr = _div_tile(tp, ROW_TILE, 8)
    nr = tp // tr

    def body(db_ref, da_ref, xc_ref, r_ref, i_ref, a_ref, lam_ref, wa_ref, wx_ref,
             dxc_ref, dwa_ref, dba_ref, dwx_ref, dbx_ref, dlam_ref):
        i = pl.program_id(1)
        xc = xc_ref[...]
        r = r_ref[...]
        ig = i_ref[...]
        a = a_ref[...]
        dbv = db_ref[...]
        sp = _softplus(-lam_ref[...])
        log_a = (-LRU_C * r) * sp
        s = jnp.sqrt(_one_minus_sq(log_a, a))
        d_ix = dbv * s
        d_s = dbv * (ig * xc)
        d_log_a = da_ref[...] * a - d_s * (a * a) / s
        d_r = d_log_a * (-LRU_C * sp)
        d_sp = jnp.sum(d_log_a * (-LRU_C * r), axis=0, keepdims=True)
        dzr = d_r * r * (1.0 - r)
        dzi = (d_ix * xc) * ig * (1.0 - ig)
        dzr_b = dzr.astype(CDT)
        dzi_b = dzi.astype(CDT)
        xcb = xc.astype(CDT)
        dxc_ref[...] = d_ix * ig + _dot_nt(dzr_b, wa_ref[...]) + _dot_nt(dzi_b, wx_ref[...])
        dwa = _dot_tn(xcb, dzr_b)
        dwx = _dot_tn(xcb, dzi_b)
        dba = jnp.sum(dzr, axis=0, keepdims=True)
        dbx = jnp.sum(dzi, axis=0, keepdims=True)

        @pl.when(i == 0)
        def _():
            dwa_ref[...] = dwa
            dwx_ref[...] = dwx
            dba_ref[...] = dba
            dbx_ref[...] = dbx
            dlam_ref[...] = d_sp

        @pl.when(i > 0)
        def _():
            dwa_ref[...] += dwa
            dwx_ref[...] += dwx
            dba_ref[...] += dba
            dbx_ref[...] += dbx
            dlam_ref[...] += d_sp

        @pl.when(i == nr - 1)
        def _():
            dlam_ref[...] = dlam_ref[...] * (-_sigmoid(-lam_ref[...]))

    blk = pl.BlockSpec((tr, bw), lambda n, i: (i, n))
    vec = pl.BlockSpec((1, bw), lambda n, i: (0, n))
    mat = pl.BlockSpec((None, bw, bw), lambda n, i: (n, 0, 0))
    bias = pl.BlockSpec((None, 1, bw), lambda n, i: (n, 0, 0))
    return pl.pallas_call(
        body,
        grid=(RNN_BLOCKS, nr),
        in_specs=[blk] * 6 + [vec, mat, mat],
        out_specs=[blk, mat, bias, mat, bias, vec],
        out_shape=[jax.ShapeDtypeStruct((tp, dr), F32),
                   jax.ShapeDtypeStruct((RNN_BLOCKS, bw, bw), F32), jax.ShapeDtypeStruct((RNN_BLOCKS, 1, bw), F32),
                   jax.ShapeDtypeStruct((RNN_BLOCKS, bw, bw), F32), jax.ShapeDtypeStruct((RNN_BLOCKS, 1, bw), F32),
                   jax.ShapeDtypeStruct((1, dr), F32)],
        compiler_params=_params("parallel", "arbitrary"),
        name="lru_gates_bwd",
    )(db, da, xc, r, ig, a, lam.reshape(1, dr), w_ga.astype(CDT), w_gx.astype(CDT))


def _lru_conv_bwd(dxc, xy, conv_w):
    tp, dr = dxc.shape
    bw = dr // RNN_BLOCKS
    tr = _div_tile(tp, ROW_TILE, 8)
    nr = tp // tr
    per = tr // HALO

    def body(d_ref, dnext_ref, x_ref, xprev_ref, cw_ref, dxb_ref, dcw_ref, dcb_ref, ds, xs):
        i = pl.program_id(1)
        d = d_ref[...]
        ds[0:tr, :] = d
        ds[tr:, :] = jnp.where(i == nr - 1, 0.0, dnext_ref[...])
        xs[0:HALO, :] = jnp.where(i == 0, 0.0, xprev_ref[...])
        xs[HALO:, :] = x_ref[...]
        dxb = cw_ref[0:1, :] * ds[pl.ds(CONV_W - 1, tr), :]
        for j in range(1, CONV_W):
            dxb = dxb + cw_ref[j:j + 1, :] * ds[pl.ds(CONV_W - 1 - j, tr), :]
        dxb_ref[...] = dxb.astype(CDT)
        dcb = jnp.sum(d, axis=0, keepdims=True)
        dcw = [jnp.sum(d * xs[pl.ds(HALO - CONV_W + 1 + j, tr), :], axis=0, keepdims=True) for j in range(CONV_W)]

        @pl.when(i == 0)
        def _():
            dcb_ref[...] = dcb
            for j in range(CONV_W):
                dcw_ref[j] = dcw[j]

        @pl.when(i > 0)
        def _():
            dcb_ref[...] += dcb
            for j in range(CONV_W):
                dcw_ref[j] += dcw[j]

    blk = pl.BlockSpec((tr, bw), lambda n, i: (i, n))
    return pl.pallas_call(
        body,
        grid=(RNN_BLOCKS, nr),
        in_specs=[blk, pl.BlockSpec((HALO, bw), lambda n, i: (jnp.minimum((i + 1) * per, tp // HALO - 1), n)),
                  blk, pl.BlockSpec((HALO, bw), lambda n, i: (jnp.maximum(i * per - 1, 0), n)),
                  pl.BlockSpec((CONV_W, bw), lambda n, i: (0, n))],
        out_specs=[blk, pl.BlockSpec((CONV_W, 1, bw), lambda n, i: (0, 0, n)), pl.BlockSpec((1, bw), lambda n, i: (0, n))],
        out_shape=[jax.ShapeDtypeStruct((tp, dr), CDT), jax.ShapeDtypeStruct((CONV_W, 1, dr), F32),
                   jax.ShapeDtypeStruct((1, dr), F32)],
        scratch_shapes=[pltpu.VMEM((tr + HALO, bw), F32), pltpu.VMEM((tr + HALO, bw), F32)],
        compiler_params=_params("parallel", "arbitrary"),
        name="lru_conv_bwd",
    )(dxc, dxc, xy, xy, conv_w)


def _me():
    return lax.axis_index("x"), lax.axis_index("y"), lax.axis_index("c")


def _peer(rel):
    x, y, c = _me()
    return (1 - x if rel & 4 else x, 1 - y if rel & 2 else y, 1 - c if rel & 1 else c)


def _chip_of(dev):
    return 2 * dev[0] + dev[1]


def _linear(dev):
    return 4 * dev[0] + 2 * dev[1] + dev[2]


CHIP_RELS = (4, 2, 6)
ALL_RELS = (1, 2, 3, 4, 5, 6, 7)
PAIR_RELS = (1,)


def _scatter_send(pieces, rels, piece_of, name):
    n = len(rels)

    def body(src_ref, recv_ref, send_sems, recv_sems):
        copies = []
        for k, rel in enumerate(rels):
            peer = _peer(rel)
            cp = pltpu.make_async_remote_copy(
                src_ref=src_ref.at[piece_of(peer)], dst_ref=recv_ref.at[k],
                send_sem=send_sems.at[k], recv_sem=recv_sems.at[k], device_id=peer, device_id_type=MESH)
            cp.start()
            copies.append(cp)
        for cp in copies:
            cp.wait()

    return pl.pallas_call(
        body,
        in_specs=[pl.BlockSpec(memory_space=pl.ANY)],
        out_specs=pl.BlockSpec(memory_space=pl.ANY),
        out_shape=jax.ShapeDtypeStruct((n,) + pieces.shape[1:], pieces.dtype),
        scratch_shapes=[pltpu.SemaphoreType.DMA((n,)), pltpu.SemaphoreType.DMA((n,))],
        name=name,
    )(pieces)


def _gather_send(piece, rels, n_slots, slot_of, name, n_chunks=1):
    n = len(rels)
    rows = piece.shape[0]
    if rows % (8 * n_chunks):
        n_chunks = 1
    rc = rows // n_chunks

    def body(src_ref, out_ref, send_sems, recv_sems, local_sems):
        me = _me()

        def part(ref, q):
            return ref.at[pl.ds(q * rc, rc)]

        def remote(k, q, slot_dev, to):
            return pltpu.make_async_remote_copy(
                src_ref=part(src_ref, q), dst_ref=part(out_ref.at[slot_of(slot_dev)], q),
                send_sem=send_sems.at[k * n_chunks + q], recv_sem=recv_sems.at[k * n_chunks + q],
                device_id=to, device_id_type=MESH)

        mine = [pltpu.make_async_copy(part(src_ref, q), part(out_ref.at[slot_of(me)], q), local_sems.at[q])
                for q in range(n_chunks)]
        for cp in mine:
            cp.start()
        sends = [remote(k, q, me, _peer(rel)) for k, rel in enumerate(rels) for q in range(n_chunks)]
        for cp in sends:
            cp.start()
        for k, rel in enumerate(rels):
            for q in range(n_chunks):
                remote(k, q, _peer(rel), _peer(rel)).wait_recv()
        for cp in sends:
            cp.wait_send()
        for cp in mine:
            cp.wait()

    return pl.pallas_call(
        body,
        in_specs=[pl.BlockSpec(memory_space=pl.ANY)],
        out_specs=pl.BlockSpec(memory_space=pl.ANY),
        out_shape=jax.ShapeDtypeStruct((n_slots,) + piece.shape, piece.dtype),
        scratch_shapes=[pltpu.SemaphoreType.DMA((n * n_chunks,)), pltpu.SemaphoreType.DMA((n * n_chunks,)),
                        pltpu.SemaphoreType.DMA((n_chunks,))],
        name=name,
    )(piece)


def _gather_chips(shard, name):
    return _gather_send(shard, CHIP_RELS, N_CHIPS, _chip_of, name)


HBM_SPEC = pl.BlockSpec(memory_space=pltpu.HBM)
SEM_SPEC = pl.BlockSpec(memory_space=pltpu.SEMAPHORE)
DATAFLOW = pltpu.SideEffectType.DATAFLOW_SIDE_EFFECTING


def _split_start(src, land, copies, name):
    def body(src_ref, land_ref, send_sem, recv_sem, src_thru, land_thru, token):
        for s_ref, d_ref, peer in copies(src_ref, land_ref):
            pltpu.make_async_remote_copy(src_ref=s_ref, dst_ref=d_ref, send_sem=send_sem, recv_sem=recv_sem,
                                         device_id=peer, device_id_type=MESH).start()
        token[...] = jnp.zeros(token.shape, token.dtype)

    return pl.pallas_call(
        body,
        name=name,
        out_shape=(pltpu.SemaphoreType.DMA(()), pltpu.SemaphoreType.DMA(()), pltpu.HBM(src.shape, src.dtype),
                   pltpu.HBM(land.shape, land.dtype), jax.ShapeDtypeStruct((8, LANES), F32)),
        in_specs=(HBM_SPEC, HBM_SPEC),
        out_specs=(SEM_SPEC, SEM_SPEC, HBM_SPEC, HBM_SPEC, pl.BlockSpec(memory_space=pltpu.VMEM)),
        input_output_aliases={0: 2, 1: 3},
        compiler_params=pltpu.CompilerParams(has_side_effects=DATAFLOW),
    )(pltpu.with_memory_space_constraint(src, pltpu.HBM), pltpu.with_memory_space_constraint(land, pltpu.HBM))


def _split_gather_start(piece, rels, n_slots, slot_of, name):
    land = jnp.broadcast_to(piece[None], (n_slots,) + piece.shape)
    return _split_start(piece, land, lambda s, l: [(s, l.at[slot_of(_me())], _peer(rel)) for rel in rels], name)


def _gather_chips_start(shard, name):
    return _split_gather_start(shard, CHIP_RELS, N_CHIPS, _chip_of, name)


def _split_scatter_start(pieces, name):
    land = lax.empty((len(ALL_RELS),) + pieces.shape[1:], pieces.dtype)
    return _split_start(pieces, land,
                        lambda s, l: [(s.at[_linear(_peer(rel))], l.at[k], _peer(rel)) for k, rel in enumerate(ALL_RELS)], name)


def _split_wait(started, n, after, name, with_src=False):
    send_sem, recv_sem, src_thru, land_thru, _ = started

    def body(src_ref, land_ref, send_sem, recv_sem, after_ref, src_dead, got_ref):
        all_n = land_ref.at[pl.ds(0, n)]
        arrivals = pltpu.make_async_remote_copy(
            src_ref=all_n, dst_ref=all_n, send_sem=send_sem, recv_sem=recv_sem, device_id=_me(), device_id_type=MESH)
        arrivals.wait_send()
        arrivals.wait_recv()

    out = pl.pallas_call(
        body,
        name=name,
        out_shape=(pltpu.HBM(src_thru.shape, src_thru.dtype), pltpu.HBM(land_thru.shape, land_thru.dtype)),
        in_specs=(HBM_SPEC, HBM_SPEC, SEM_SPEC, SEM_SPEC, pl.BlockSpec(memory_space=pl.ANY)),
        out_specs=(HBM_SPEC, HBM_SPEC),
        input_output_aliases={0: 0, 1: 1},
        compiler_params=pltpu.CompilerParams(has_side_effects=DATAFLOW),
    )(src_thru, land_thru, send_sem, recv_sem, after)
    return (out[1], out[0]) if with_src else out[1]


def _sum_pieces(pieces, recv, name):
    _, rr, cc = pieces.shape
    n = recv.shape[0]
    tr = _div_tile(rr, max(8, (1 << 17) // cc // 8 * 8), 8)

    def body(own_ref, recv_ref, o_ref):
        acc = own_ref[...].astype(F32)
        for k in range(n):
            acc = acc + recv_ref[k].astype(F32)
        o_ref[...] = acc

    return pl.pallas_call(
        body,
        grid=(rr // tr,),
        in_specs=[pl.BlockSpec((None, tr, cc), lambda i: (_linear(_me()), i, 0)),
                  pl.BlockSpec((n, tr, cc), lambda i: (0, i, 0))],
        out_specs=pl.BlockSpec((tr, cc), lambda i: (i, 0)),
        out_shape=jax.ShapeDtypeStruct((rr, cc), F32),
        compiler_params=_params("parallel"),
        name=name,
    )(pieces, recv)


def _reduce_to_owner(g8, payload_dtype, name):
    recv = _scatter_send(g8.astype(payload_dtype), ALL_RELS, _linear, name + "_scatter")
    return _sum_pieces(g8, recv, name + "_sum")


def _adamw_layer(w, g, m, v, outs, layer, name):
    nl, rr, cc = w.shape
    _, gr, gc = g.shape
    tr = _div_tile(gr, max(8, (1 << 17) // gc // 8 * 8), 8)
    steps = gr // tr
    c1 = 1.0 - ADAM_B1 ** ADAM_STEP
    c2 = 1.0 - ADAM_B2 ** ADAM_STEP
    if gc == cc:
        assert 2 * gr == rr, (name, g.shape, w.shape)
        slab = pl.BlockSpec((None, tr, gc), lambda h, i: (layer, h * steps + i, 0))
    else:
        assert gr == rr and 2 * gc == cc, (name, g.shape, w.shape)
        slab = pl.BlockSpec((None, tr, gc), lambda h, i: (layer, i, h))

    def body(w_ref, g_ref, m_ref, v_ref, *rest):
        go_ref, d_ref, mo_ref, vo_ref = rest[-4:]
        g_ = g_ref[...]
        m_ = ADAM_B1 * m_ref[...] + (1.0 - ADAM_B1) * g_
        v_ = ADAM_B2 * v_ref[...] + (1.0 - ADAM_B2) * (g_ * g_)
        go_ref[...] = g_
        d_ref[...] = -ADAM_LR * ((m_ / c1) / (jnp.sqrt(v_ / c2) + ADAM_EPS) + ADAM_WD * w_ref[...])
        mo_ref[...] = m_
        vo_ref[...] = v_

    out = jax.ShapeDtypeStruct((nl, rr, cc), F32)
    in_specs = [slab, pl.BlockSpec((None, tr, gc), lambda h, i: (h, i, 0)), slab, slab]
    args = [w, g, m, v]
    aliases = {}
    if outs is not None:
        in_specs += [pl.BlockSpec(memory_space=pl.ANY)] * 4
        args += list(outs)
        aliases = {4 + k: k for k in range(4)}
    return pl.pallas_call(
        body,
        grid=(2, steps),
        in_specs=in_specs,
        out_specs=[slab] * 4,
        out_shape=[out] * 4,
        input_output_aliases=aliases,
        compiler_params=_params("parallel", "parallel"),
        name=name,
    )(*args)


def _adamw(w, g, m, v, name):
    rr, cc = w.shape
    tr = _div_tile(rr, max(8, (1 << 17) // cc // 8 * 8), 8)
    c1 = 1.0 - ADAM_B1 ** ADAM_STEP
    c2 = 1.0 - ADAM_B2 ** ADAM_STEP

    def body(w_ref, g_ref, m_ref, v_ref, d_ref, mo_ref, vo_ref):
        g_ = g_ref[...]
        m_ = ADAM_B1 * m_ref[...] + (1.0 - ADAM_B1) * g_
        v_ = ADAM_B2 * v_ref[...] + (1.0 - ADAM_B2) * (g_ * g_)
        d_ref[...] = -ADAM_LR * ((m_ / c1) / (jnp.sqrt(v_ / c2) + ADAM_EPS) + ADAM_WD * w_ref[...])
        mo_ref[...] = m_
        vo_ref[...] = v_

    blk = pl.BlockSpec((tr, cc), lambda i: (i, 0))
    out = jax.ShapeDtypeStruct((rr, cc), F32)
    return pl.pallas_call(
        body,
        grid=(rr // tr,),
        in_specs=[blk] * 4,
        out_specs=[blk] * 3,
        out_shape=[out] * 3,
        compiler_params=_params("parallel"),
        name=name,
    )(w, g, m, v)


def _pack(arrays, cols, row_mult):
    flat = jnp.concatenate([a.reshape(-1) for a in arrays])
    rows = -(-flat.shape[0] // cols)
    rows = -(-rows // row_mult) * row_mult
    return jnp.pad(flat, (0, rows * cols - flat.shape[0])).reshape(rows, cols)


def _unpack(buf, shapes):
    flat = buf.reshape(-1)
    out, off = [], 0
    for s in shapes:
        n = math.prod(s)
        out.append(flat[off:off + n].reshape(s))
        off += n
    return out


def kernel(x, meta_tokens, norm_mix, norm_ffn, norm_final, mla_w_in, mla_q_norm, mla_kv_norm, mla_w_uq, mla_w_ukv, mla_w_o, lru_w_in, lru_conv_w, lru_conv_b, lru_w_gate_a, lru_b_gate_a, lru_w_gate_x, lru_b_gate_x, lru_lambda, lru_w_o, ffn_w_gu, ffn_w_down, loss_target, m_meta_tokens, m_norm_mix, m_norm_ffn, m_norm_final, m_mla_w_in, m_mla_q_norm, m_mla_kv_norm, m_mla_w_uq, m_mla_w_ukv, m_mla_w_o, m_lru_w_in, m_lru_conv_w, m_lru_conv_b, m_lru_w_gate_a, m_lru_b_gate_a, m_lru_w_gate_x, m_lru_b_gate_x, m_lru_lambda, m_lru_w_o, m_ffn_w_gu, m_ffn_w_down, v_meta_tokens, v_norm_mix, v_norm_ffn, v_norm_final, v_mla_w_in, v_mla_q_norm, v_mla_kv_norm, v_mla_w_uq, v_mla_w_ukv, v_mla_w_o, v_lru_w_in, v_lru_conv_w, v_lru_conv_b, v_lru_w_gate_a, v_lru_b_gate_a, v_lru_w_gate_x, v_lru_b_gate_x, v_lru_lambda, v_lru_w_o, v_ffn_w_gu, v_ffn_w_down):
    d = D_MODEL
    t_real = N_META + SEQ
    tp = _t_pad()
    n_mla = mla_w_in.shape[0]
    n_lru = lru_w_in.shape[0]
    h_dim = MLA_HEADS * V_HEAD
    w_in_cols = Q_LORA + KV_LORA + QK_ROPE
    w_in_pad = Q_LORA + KV_LORA + LANES
    q_cols = MLA_HEADS * (QK_NOPE + QK_ROPE)
    tmm = _div_tile(tp, MM_ROW_TILE, 16)
    tkt = _div_tile(tp, 1408, 16)

    def tile(n, pref):
        return _div_tile(n, pref, LANES)

    small_shapes = [meta_tokens.shape, lru_conv_w.shape, lru_conv_b.shape, lru_lambda.shape]
    csh = meta_tokens.shape[1]
    small4 = _gather_chips(_pack([meta_tokens, lru_conv_w, lru_conv_b, lru_lambda], csh, 16), "gather_small")
    small4, mla_w_in = lax.optimization_barrier((small4, mla_w_in))
    started = {}

    def start(key, shard):
        prev = list(started.values())[-1][4][0, 0] if started else 0.0
        started[key] = _gather_chips_start((shard + prev).astype(CDT), "gather_" + key + "_start")

    def arrived(key, after):
        return _split_wait(started[key], len(CHIP_RELS), after, "gather_" + key + "_wait")

    def start_ffn(layer):
        start(f"w_gu{layer}", ffn_w_gu[layer:layer + 1])
        start(f"w_down{layer}", ffn_w_down[layer:layer + 1])

    start("w_in", jnp.pad(mla_w_in, ((0, 0), (0, 0), (0, w_in_pad - w_in_cols))))
    start("w_uq", mla_w_uq)
    start("w_ukv", mla_w_ukv)
    start("w_o", mla_w_o)
    start_ffn(0)
    start("lw_in", lru_w_in)
    start("lw_o", lru_w_o)
    for layer in range(1, DEPTH):
        start_ffn(layer)
    all_started = list(started.values())[-1][4][0, 0]
    n_gu = ffn_w_gu.shape[2]
    w_gu4, w_down4 = [None] * DEPTH, [None] * DEPTH
    small_full = [jnp.concatenate(parts, axis=-1) for parts in zip(*[_unpack(small4[k], small_shapes) for k in range(N_CHIPS)])]
    meta_full, conv_w_full, conv_b_full, lam_full = small_full

    cos2, sin2 = _rope_tables(tp)

    h = jnp.concatenate([meta_full, x[0], jnp.zeros((tp - t_real, d), F32)], axis=0) + all_started
    saved = []
    for layer in range(DEPTH):
        j = layer // 2
        s = {"h_in": h}
        hn = _rms_fwd(h, norm_mix[layer], width=d, col_block=0, name="norm_mix_fwd")
        s["hn"] = hn
        if layer == 0:
            w_in4, w_uq4, w_ukv4, w_o4 = (arrived(k, hn) for k in ("w_in", "w_uq", "w_ukv", "w_o"))
            w_uq_full = jnp.moveaxis(w_uq4, 0, 2).reshape(n_mla, Q_LORA, MLA_HEADS, QK_NOPE + QK_ROPE)
            w_uq_perm = jnp.concatenate([w_uq_full[..., :QK_NOPE].reshape(n_mla, Q_LORA, -1),
                                         w_uq_full[..., QK_NOPE:].reshape(n_mla, Q_LORA, -1)], axis=-1)
        if layer == 1:
            lw_in4, lw_o4 = arrived("lw_in", hn), arrived("lw_o", hn)
        if layer % 2 == 0:
            proj = _mm("nn", hn, w_in4, kind="row", layer=j, tm=tmm, tn=tile(w_in_pad, 1152), tk=tile(d // N_CHIPS, 512), name="mla_in")
            c_q = _rms_fwd(proj, mla_q_norm[j], width=Q_LORA, col_block=0, name="q_norm_fwd")
            c_kv = _rms_fwd(proj, mla_kv_norm[j], width=KV_LORA, col_block=Q_LORA // KV_LORA, name="kv_norm_fwd")
            q = _mm("nn", c_q, w_uq_perm[j], tm=tmm, tn=tile(q_cols, 1024), tk=Q_LORA, name="mla_uq")
            kv = _mm("nn", c_kv, w_ukv4, kind="col", layer=j, out_dtype=CDT, tm=tmm, tn=tile(w_ukv4.shape[3], 1024), tk=KV_LORA, name="mla_ukv")
            qp, kp = _mla_prep(q, kv, proj, cos2, sin2)
            att, lse = _attn_fwd(qp, kp, kv)
            h = _mm("nn", att, w_o4, kind="row", layer=j, resid=h, tm=tmm, tn=tile(d, 1024), tk=tile(h_dim // N_CHIPS, 512), name="mla_out")
            s.update(proj=proj, c_q=c_q, c_kv=c_kv, qp=qp, kp=kp, kv=kv, att=att, lse=lse)
        else:
            xy = _mm("nn", hn, lw_in4, kind="col", layer=j, tm=tmm, tn=tile(lw_in4.shape[3], 1024), tk=d, name="lru_in")
            xc, r, ig, a, b = _lru_gates_fwd(xy, conv_w_full[j], conv_b_full[j], lru_w_gate_a[j], lru_b_gate_a[j],
                                             lru_w_gate_x[j], lru_b_gate_x[j], lam_full[j])
            hs, mixed = _lru_scan_fwd(a, b, xy)
            h = _mm("nn", mixed, lw_o4, kind="row", layer=j, resid=h, tm=tmm, tn=tile(d, 1024), tk=tile(d // N_CHIPS, 512), name="lru_out")
            s.update(xy=xy, xc=xc, r=r, ig=ig, a=a, hs=hs, mixed=mixed)
        s["h_mid"] = h
        hn2 = _rms_fwd(h, norm_ffn[layer], width=d, col_block=0, name="norm_ffn_fwd")
        w_gu4[layer], w_down4[layer] = arrived(f"w_gu{layer}", hn2), arrived(f"w_down{layer}", hn2)
        gate, up, act = _ffn_gate_up(hn2, w_gu4[layer], tm=_div_tile(tp, MM_ROW_TILE // 2, 16), tn=tile(n_gu, 1408))
        h = _mm("nn", act, w_down4[layer], kind="row", resid=h, tm=tmm, tn=tile(d, 1024), tk=tile(D_FF // N_CHIPS, 1408), name="ffn_down")
        s.update(hn2=hn2, gate=gate, up=up, act=act)
        saved.append(s)

    target = jnp.concatenate([jnp.zeros((N_META, d), F32), loss_target[0], jnp.zeros((tp - t_real, d), F32)], axis=0)
    dh, dhb, g_norm_final, loss_part = _final_loss(h, norm_final, target)
    loss = lax.psum(loss_part[0, 0], ("x", "y", "c"))

    g_norm_mix, g_norm_ffn = [None] * DEPTH, [None] * DEPTH
    g_q_norm, g_kv_norm = [None] * n_mla, [None] * n_mla
    g_w_uq = [None] * n_mla
    g_gate = {k: [None] * n_lru for k in ("wa", "ba", "wx", "bx", "lam", "cw", "cb")}
    weights = {"w_in": (mla_w_in, m_mla_w_in, v_mla_w_in), "w_uq": (mla_w_uq, m_mla_w_uq, v_mla_w_uq),
               "w_ukv": (mla_w_ukv, m_mla_w_ukv, v_mla_w_ukv), "w_o": (mla_w_o, m_mla_w_o, v_mla_w_o),
               "lw_in": (lru_w_in, m_lru_w_in, v_lru_w_in), "lw_o": (lru_w_o, m_lru_w_o, v_lru_w_o),
               "w_gu": (ffn_w_gu, m_ffn_w_gu, v_ffn_w_gu), "w_down": (ffn_w_down, m_ffn_w_down, v_ffn_w_down)}
    res = {key: None for key in weights}
    units = []

    def reduce_start(key, lyr, pieces):
        tag = f"{key}{lyr}"
        units.append({"key": key, "layer": lyr, "tag": tag, "pieces": pieces, "stage": 0, "age": 0,
                      "copy": _split_scatter_start(pieces, "reduce_" + tag + "_scatter_start")})
        return units[-1]["copy"][4]

    def reduce_advance(after, everything=False):
        tokens = []
        for u in units:
            key, lyr, tag = u["key"], u["layer"], u["tag"]
            if u["stage"] == 1:
                both = _split_wait(u["copy"], 1, after, "reduce_" + tag + "_pair_wait")
                g = both[:, :, :w_in_cols] if key == "w_in" else both
                w, m, v = weights[key]
                res[key] = _adamw_layer(w, g, m, v, res[key], lyr, "adamw_" + key)
                u["stage"] = 2
            elif u["stage"] == 0 and (u["age"] > 0 or everything):
                recv, pieces = _split_wait(u["copy"], len(ALL_RELS), after, "reduce_" + tag + "_scatter_wait", with_src=True)
                red = _sum_pieces(pieces, recv, "reduce_" + key + "_sum")
                u["copy"] = _split_gather_start(red, PAIR_RELS, 2, lambda dev: dev[2], "reduce_" + tag + "_pair_start")
                tokens.append(u["copy"][4])
                u["stage"] = 1
            u["age"] += 1
        return tokens

    def grad_w(key, a_op, b_op, kind, lyr, tm, tn, tk=tkt):
        return reduce_start(key, lyr, _mm_tn(a_op, b_op, kind=kind, tm=tm, tn=tn, tk=tk, name="grad_" + key))

    nope_w = MLA_HEADS * QK_NOPE
    for layer in reversed(range(DEPTH)):
        tokens = []
        j = layer // 2
        s = saved[layer]
        t_down = grad_w("w_down", s["act"], dhb, "row_colhalves", layer, tile(D_FF // N_CHIPS, 1408), tile(d // 2, 1024))
        d_act = _mm("nt", dhb, w_down4[layer], kind="row", out_dtype=CDT, tm=tmm, tn=tile(D_FF // N_CHIPS, 1408), tk=d, name="ffn_down_bwd")
        dgu = _swiglu_bwd(s["gate"], s["up"], d_act)
        t_gu = grad_w("w_gu", s["hn2"], dgu, "col", layer, tile(d // 2, 1024), tile(n_gu, 1408), tk=_div_tile(tp, 2112, 16))
        dhn2 = _mm("nt", dgu, w_gu4[layer], kind="col", tm=tmm, tn=tile(d, 1024), tk=tile(n_gu, 2816), name="ffn_gu_bwd")
        dh, dhb, g_norm_ffn[layer] = _rms_bwd(s["h_mid"], norm_ffn[layer] + (t_down[0, 0] + t_gu[0, 0]), dhn2, dh,
                                              width=d, col_block=0, name="norm_ffn_bwd")
        if layer % 2 == 0:
            t_mix = grad_w("w_o", s["att"], dhb, "row", j, tile(h_dim // N_DEV, 256), tile(d, 1024))[0, 0]
            d_att = _mm("nt", dhb, w_o4, kind="row", layer=j, tm=tmm, tn=tile(h_dim // N_CHIPS, 512), tk=d, name="mla_out_bwd")
            delta = _attn_delta(d_att, s["att"])
            dqn, dqr_h, dkv, dkr_h = _attn_bwd(s["qp"], s["kp"], s["kv"], d_att, s["lse"], delta)
            dqr, dkr = _mla_unprep(dqr_h, dkr_h, cos2, sin2)
            dq = jnp.concatenate([dqn, dqr], axis=-1)
            g_uq = _mm_tn(s["c_q"], dq, tm=Q_LORA, tn=tile(q_cols, 1024), tk=tkt, name="grad_w_uq")
            g_uq = jnp.concatenate([g_uq[:, :nope_w].reshape(Q_LORA, MLA_HEADS, QK_NOPE),
                                    g_uq[:, nope_w:].reshape(Q_LORA, MLA_HEADS, QK_ROPE)], axis=-1)
            g_uq = g_uq.reshape(2, Q_LORA // 2, N_CHIPS, q_cols // N_CHIPS).transpose(2, 0, 1, 3)
            t_mix = t_mix + reduce_start("w_uq", j, g_uq.reshape(N_DEV, Q_LORA // 2, q_cols // N_CHIPS).astype(CDT))[0, 0]
            dc_q = _mm("nt", dq, w_uq_perm[j], tm=tmm, tn=Q_LORA, tk=tile(q_cols, 1024), name="mla_uq_bwd")
            t_mix = t_mix + grad_w("w_ukv", s["c_kv"], dkv, "col", j, tile(KV_LORA // 2, 256), tile(w_ukv4.shape[3], 1024))[0, 0]
            dc_kv = _mm("nt", dkv, w_ukv4, kind="col", layer=j, tm=tmm, tn=KV_LORA, tk=tile(w_ukv4.shape[3], 1024), name="mla_ukv_bwd")
            dpq, _, g_q_norm[j] = _rms_bwd(s["proj"], mla_q_norm[j] + t_mix, dc_q, None, width=Q_LORA, col_block=0, name="q_norm_bwd")
            dpkv, _, g_kv_norm[j] = _rms_bwd(s["proj"], mla_kv_norm[j], dc_kv, None, width=KV_LORA, col_block=Q_LORA // KV_LORA, name="kv_norm_bwd")
            dproj = jnp.concatenate([dpq, dpkv, dkr], axis=-1).astype(CDT)
            tokens.append(grad_w("w_in", s["hn"], dproj, "row", j, tile(d // N_DEV, 256), tile(w_in_pad, 1152)))
            dhn = _mm("nt", dproj, w_in4, kind="row", layer=j, tm=tmm, tn=tile(d // N_CHIPS, 512), tk=tile(w_in_pad, 1152), name="mla_in_bwd")
        else:
            t_mix = grad_w("lw_o", s["mixed"], dhb, "row", j, tile(d // N_DEV, 256), tile(d, 1024))[0, 0]
            dm = _mm("nt", dhb, lw_o4, kind="row", layer=j, tm=tmm, tn=tile(d // N_CHIPS, 512), tk=d, name="lru_out_bwd")
            db, da, dy = _lru_scan_bwd(s["a"], s["hs"], dm, s["xy"])
            dxc, g_gate["wa"][j], g_gate["ba"][j], g_gate["wx"][j], g_gate["bx"][j], g_gate["lam"][j] = _lru_gates_bwd(
                db, da, s["xc"], s["r"], s["ig"], s["a"], lam_full[j] + t_mix, lru_w_gate_a[j], lru_w_gate_x[j])
            dxb, g_gate["cw"][j], g_gate["cb"][j] = _lru_conv_bwd(dxc, s["xy"], conv_w_full[j])
            dxy = jnp.concatenate([dxb, dy], axis=-1)
            tokens.append(grad_w("lw_in", s["hn"], dxy, "col", j, tile(d // 2, 1024), tile(lw_in4.shape[3], 1024)))
            dhn = _mm("nt", dxy, lw_in4, kind="col", layer=j, tm=tmm, tn=tile(d, 1024), tk=tile(lw_in4.shape[3], 1024), name="lru_in_bwd")
        dh, dhb, g_norm_mix[layer] = _rms_bwd(s["h_in"], norm_mix[layer] + t_mix, dhn, dh, width=d, col_block=0, name="norm_mix_bwd")
        tokens += reduce_advance(dh)
        if layer > 0:
            dhb = dhb + sum(tok[0, 0] for tok in tokens).astype(CDT)
        else:
            dh = dh + sum(tok[0, 0] for tok in tokens)

    grad_x = dh[N_META:t_real][None]
    g_meta_full = dh[:N_META]

    for _ in range(3):
        reduce_advance(dh, everything=True)

    g_small_full = [g_meta_full, jnp.stack(g_gate["cw"]).reshape(n_lru, CONV_W, d), jnp.stack(g_gate["cb"]).reshape(n_lru, d),
                    jnp.stack(g_gate["lam"]).reshape(n_lru, d)]
    g_small4 = jnp.stack([_pack([a[..., k * csh:(k + 1) * csh] for a in g_small_full], csh, 16) for k in range(N_CHIPS)])
    rows_s = g_small4.shape[1]
    red = _reduce_to_owner(g_small4.reshape(N_DEV, rows_s // 2, csh), F32, "reduce_small")
    g_small = _gather_send(red, PAIR_RELS, 2, lambda dev: dev[2], "reduce_small_pair").reshape(rows_s, csh)
    small_w = [meta_tokens, lru_conv_w, lru_conv_b, lru_lambda]
    small_m = [m_meta_tokens, m_lru_conv_w, m_lru_conv_b, m_lru_lambda]
    small_v = [v_meta_tokens, v_lru_conv_w, v_lru_conv_b, v_lru_lambda]
    sd, sm, sv = _adamw(_pack(small_w, csh, 16), g_small, _pack(small_m, csh, 16), _pack(small_v, csh, 16), "adamw_small")
    small_out = [_unpack(buf, small_shapes) for buf in (g_small, sd, sm, sv)]

    rep_w = [norm_mix, norm_ffn, norm_final, mla_q_norm, mla_kv_norm, lru_w_gate_a, lru_b_gate_a, lru_w_gate_x, lru_b_gate_x]
    rep_m = [m_norm_mix, m_norm_ffn, m_norm_final, m_mla_q_norm, m_mla_kv_norm, m_lru_w_gate_a, m_lru_b_gate_a, m_lru_w_gate_x, m_lru_b_gate_x]
    rep_v = [v_norm_mix, v_norm_ffn, v_norm_final, v_mla_q_norm, v_mla_kv_norm, v_lru_w_gate_a, v_lru_b_gate_a, v_lru_w_gate_x, v_lru_b_gate_x]
    rep_g = [jnp.stack(g_norm_mix), jnp.stack(g_norm_ffn), g_norm_final, jnp.stack(g_q_norm), jnp.stack(g_kv_norm),
             jnp.stack(g_gate["wa"]), jnp.stack(g_gate["ba"]), jnp.stack(g_gate["wx"]), jnp.stack(g_gate["bx"])]
    rep_shapes = [w.shape for w in rep_w]
    g_rep = _pack(rep_g, LANES, 8 * N_DEV)
    rows_r = g_rep.shape[0]
    red = _reduce_to_owner(g_rep.reshape(N_DEV, rows_r // N_DEV, LANES), F32, "reduce_rep")
    g_rep = _gather_send(red, ALL_RELS, N_DEV, _linear, "reduce_rep_all").reshape(rows_r, LANES)
    rd, rm, rv = _adamw(_pack(rep_w, LANES, 8 * N_DEV), g_rep, _pack(rep_m, LANES, 8 * N_DEV), _pack(rep_v, LANES, 8 * N_DEV), "adamw_rep")
    rep_out = [_unpack(buf, rep_shapes) for buf in (g_rep, rd, rm, rv)]

    def leaf(kind):
        s_, r_ = small_out[kind], rep_out[kind]
        return [s_[0], r_[0], r_[1], r_[2], res["w_in"][kind], r_[3], r_[4], res["w_uq"][kind], res["w_ukv"][kind],
                res["w_o"][kind], res["lw_in"][kind], s_[1], s_[2], r_[5], r_[6], r_[7], r_[8], s_[3],
                res["lw_o"][kind], res["w_gu"][kind], res["w_down"][kind]]

    return (loss, grad_x, *leaf(0), *leaf(1), *leaf(2), *leaf(3))
```

```python
import math

import jax
import jax.numpy as jnp
from jax import lax
from jax.experimental import pallas as pl
from jax.experimental.pallas import tpu as pltpu

F32 = jnp.float32
CDT = jnp.bfloat16
MESH = pl.DeviceIdType.MESH

D_MODEL = 2048
SEQ = 4096
DEPTH = 4
CHUNK = 64
N_META = 16
MLA_HEADS = 16
Q_LORA = 512
KV_LORA = 512
QK_NOPE = 128
QK_ROPE = 64
V_HEAD = 128
ROPE_THETA = 10000.0
RNN_BLOCKS = 16
CONV_W = 4
LRU_C = 8.0
ATT_SCALE = (QK_NOPE + QK_ROPE) ** -0.5
D_FF = 5632
RMS_EPS = 1e-6
NEG_BIG = -1e30
ADAM_LR = 0.001
ADAM_B1 = 0.9
ADAM_B2 = 0.999
ADAM_EPS = 1e-08
ADAM_WD = 0.01
ADAM_STEP = 10

N_CHIPS = 4
N_DEV = 8
LANES = 128
VMEM_LIMIT = 52 * 1024 * 1024
ROW_TILE = 384
MM_ROW_TILE = 704
ATT_TILE = 384
SCAN_COLS = 128
ATT_STRIP = 32


def _div_tile(n, pref, mult):
    if n <= pref:
        return n
    d = (pref // mult) * mult
    while d >= mult:
        if n % d == 0:
            return d
        d -= mult
    raise ValueError(f"no tile for {n} <= {pref} (multiple of {mult})")


def _t_pad():
    t = N_META + SEQ
    step = math.lcm(_row_tile_unit(), 8)
    return -(-t // step) * step


def _row_tile_unit():
    return math.lcm(math.lcm(ROW_TILE, MM_ROW_TILE), ATT_TILE)


def _params(*sem):
    return pltpu.CompilerParams(dimension_semantics=sem, vmem_limit_bytes=VMEM_LIMIT)


def _b_spec(form, b, kind, layer, t_out, t_con):
    if kind == "plain":
        if form == "nn":
            return pl.BlockSpec((t_con, t_out), lambda i, j, k: (k, j))
        return pl.BlockSpec((t_out, t_con), lambda i, j, k: (j, k))
    rows, cols = b.shape[2], b.shape[3]
    if form == "nn":
        blk = (None, None, t_con, t_out)
        if kind == "row":
            per = rows // t_con
            return pl.BlockSpec(blk, lambda i, j, k: (k // per, layer, k % per, j))
        per = cols // t_out
        return pl.BlockSpec(blk, lambda i, j, k: (j // per, layer, k, j % per))
    blk = (None, None, t_out, t_con)
    if kind == "row":
        per = rows // t_out
        return pl.BlockSpec(blk, lambda i, j, k: (j // per, layer, j % per, k))
    per = cols // t_con
    return pl.BlockSpec(blk, lambda i, j, k: (k // per, layer, j, k % per))


def _mm_body(nk, dims, has_resid):
    def body(*refs):
        if has_resid:
            a_ref, b_ref, r_ref, o_ref = refs[:4]
        else:
            a_ref, b_ref, o_ref = refs[:3]
        prod = lax.dot_general(a_ref[...].astype(CDT), b_ref[...].astype(CDT), (dims, ((), ())),
                               preferred_element_type=F32)

        def finish(acc):
            if has_resid:
                acc = acc + r_ref[...]
            o_ref[...] = acc.astype(o_ref.dtype)

        if nk == 1:
            finish(prod)
            return
        acc_ref = refs[-1]
        k = pl.program_id(2)

        @pl.when(k == 0)
        def _():
            acc_ref[...] = prod

        @pl.when(k > 0)
        def _():
            acc_ref[...] += prod

        @pl.when(k == nk - 1)
        def _():
            finish(acc_ref[...])

    return body


def _mm(form, a, b, *, kind="plain", layer=0, out_dtype=F32, resid=None, tm, tn, tk, name):
    m, con = a.shape
    if kind == "plain":
        w_rows, w_cols = b.shape
    elif kind == "row":
        w_rows, w_cols = b.shape[0] * b.shape[2], b.shape[3]
    else:
        w_rows, w_cols = b.shape[2], b.shape[0] * b.shape[3]
    n_out = w_cols if form == "nn" else w_rows
    assert con == (w_rows if form == "nn" else w_cols), (name, a.shape, b.shape)
    nk = con // tk
    assert m % tm == 0 and n_out % tn == 0 and con % tk == 0, (name, m, n_out, con, tm, tn, tk)
    dims = ((1,), (0,)) if form == "nn" else ((1,), (1,))
    in_specs = [pl.BlockSpec((tm, tk), lambda i, j, k: (i, k)), _b_spec(form, b, kind, layer, tn, tk)]
    args = [a, b]
    if resid is not None:
        in_specs.append(pl.BlockSpec((tm, tn), lambda i, j, k: (i, j)))
        args.append(resid)
    return pl.pallas_call(
        _mm_body(nk, dims, resid is not None),
        grid=(m // tm, n_out // tn, nk),
        in_specs=in_specs,
        out_specs=pl.BlockSpec((tm, tn), lambda i, j, k: (i, j)),
        out_shape=jax.ShapeDtypeStruct((m, n_out), out_dtype),
        scratch_shapes=[pltpu.VMEM((tm, tn), F32)] if nk > 1 else [],
        compiler_params=_params("parallel", "parallel", "arbitrary"),
        name=name,
    )(*args)


def _mm_tn(a, b, *, kind="plain", tm, tn, tk, name):
    t, m = a.shape
    n = b.shape[1]
    nk = t // tk
    assert t % tk == 0 and m % tm == 0 and n % tn == 0, (name, t, m, n, tm, tn, tk)
    if kind == "plain":
        out_shape = jax.ShapeDtypeStruct((m, n), F32)
        out_spec = pl.BlockSpec((tm, tn), lambda i, j, k: (i, j))
    elif kind == "row":
        per = (m // N_CHIPS) // tm
        assert per >= 2 and per % 2 == 0, (name, per)
        out_shape = jax.ShapeDtypeStruct((N_DEV, m // N_DEV, n), CDT)
        out_spec = pl.BlockSpec((None, tm, tn), lambda i, j, k: (2 * (i // per) + (i % per) // (per // 2), (i % per) % (per // 2), j))
    elif kind == "row_colhalves":
        per = (m // N_CHIPS) // tm
        nt = n // tn
        assert per >= 1 and nt % 2 == 0, (name, per, nt)
        out_shape = jax.ShapeDtypeStruct((N_DEV, m // N_CHIPS, n // 2), CDT)
        out_spec = pl.BlockSpec((None, tm, tn), lambda i, j, k: (2 * (i // per) + j // (nt // 2), i % per, j % (nt // 2)))
    else:
        per = (n // N_CHIPS) // tn
        mt = m // tm
        assert per >= 1 and mt % 2 == 0, (name, per, mt)
        out_shape = jax.ShapeDtypeStruct((N_DEV, m // 2, n // N_CHIPS), CDT)
        out_spec = pl.BlockSpec((None, tm, tn), lambda i, j, k: (2 * (j // per) + i // (mt // 2), i % (mt // 2), j % per))
    return pl.pallas_call(
        _mm_body(nk, ((0,), (0,)), False),
        grid=(m // tm, n // tn, nk),
        in_specs=[pl.BlockSpec((tk, tm), lambda i, j, k: (k, i)), pl.BlockSpec((tk, tn), lambda i, j, k: (k, j))],
        out_specs=out_spec,
        out_shape=out_shape,
        scratch_shapes=[pltpu.VMEM((tm, tn), F32)] if nk > 1 else [],
        compiler_params=_params("parallel", "parallel", "arbitrary"),
        name=name,
    )(a, b)


def _rms_fwd(x, g, *, width, col_block, name):
    tp = x.shape[0]
    tr = _div_tile(tp, ROW_TILE, 8)

    def body(x_ref, g_ref, o_ref):
        xf = x_ref[...]
        r = lax.rsqrt(jnp.mean(xf * xf, axis=-1, keepdims=True) + RMS_EPS)
        o_ref[...] = ((xf * r) * g_ref[...]).astype(o_ref.dtype)

    return pl.pallas_call(
        body,
        grid=(tp // tr,),
        in_specs=[pl.BlockSpec((tr, width), lambda i: (i, col_block)), pl.BlockSpec((1, width), lambda i: (0, 0))],
        out_specs=pl.BlockSpec((tr, width), lambda i: (i, 0)),
        out_shape=jax.ShapeDtypeStruct((tp, width), CDT),
        compiler_params=_params("parallel"),
        name=name,
    )(x, g.reshape(1, width))


def _rms_bwd(x, g, dy, resid, *, width, col_block, name):
    tp = x.shape[0]
    tr = _div_tile(tp, ROW_TILE, 8)
    has_resid = resid is not None

    def body(*refs):
        if has_resid:
            x_ref, g_ref, dy_ref, res_ref, dx_ref, dxb_ref, dg_ref = refs
        else:
            x_ref, g_ref, dy_ref, dx_ref, dxb_ref, dg_ref = refs
        i = pl.program_id(0)
        xf = x_ref[...]
        r = lax.rsqrt(jnp.mean(xf * xf, axis=-1, keepdims=True) + RMS_EPS)
        xh = xf * r
        dy = dy_ref[...].astype(F32)
        dg = jnp.sum(dy * xh, axis=0, keepdims=True)
        dxh = dy * g_ref[...]
        dx = r * (dxh - xh * jnp.mean(dxh * xh, axis=-1, keepdims=True))
        if has_resid:
            dx = dx + res_ref[...]
        dx_ref[...] = dx
        dxb_ref[...] = dx.astype(CDT)

        @pl.when(i == 0)
        def _():
            dg_ref[...] = dg

        @pl.when(i > 0)
        def _():
            dg_ref[...] += dg

    row = pl.BlockSpec((tr, width), lambda i: (i, 0))
    in_specs = [pl.BlockSpec((tr, width), lambda i: (i, col_block)), pl.BlockSpec((1, width), lambda i: (0, 0)), row]
    args = [x, g.reshape(1, width), dy]
    if has_resid:
        in_specs.append(row)
        args.append(resid)
    return pl.pallas_call(
        body,
        grid=(tp // tr,),
        in_specs=in_specs,
        out_specs=[row, row, pl.BlockSpec((1, width), lambda i: (0, 0))],
        out_shape=[jax.ShapeDtypeStruct((tp, width), F32), jax.ShapeDtypeStruct((tp, width), CDT),
                   jax.ShapeDtypeStruct((1, width), F32)],
        compiler_params=_params("arbitrary"),
        name=name,
    )(*args)


def _final_loss(h, g, target):
    tp, d = h.shape
    tr = _div_tile(tp, ROW_TILE, 8)

    def body(h_ref, g_ref, t_ref, dh_ref, dhb_ref, dg_ref, loss_ref):
        i = pl.program_id(0)
        xf = h_ref[...]
        r = lax.rsqrt(jnp.mean(xf * xf, axis=-1, keepdims=True) + RMS_EPS)
        xh = xf * r
        gain = g_ref[...]
        y = xh * gain
        rows = i * tr + lax.broadcasted_iota(jnp.int32, (tr, 1), 0)
        valid = jnp.logical_and(rows >= N_META, rows < N_META + SEQ)
        err = jnp.where(valid, y - t_ref[...], 0.0)
        part = 0.5 * jnp.sum(jnp.mean(err * err, axis=-1, keepdims=True), axis=0, keepdims=True)
        dy = err * (1.0 / d)
        dg = jnp.sum(dy * xh, axis=0, keepdims=True)
        dxh = dy * gain
        dx = r * (dxh - xh * jnp.mean(dxh * xh, axis=-1, keepdims=True))
        dh_ref[...] = dx
        dhb_ref[...] = dx.astype(CDT)

        @pl.when(i == 0)
        def _():
            dg_ref[...] = dg
            loss_ref[...] = part

        @pl.when(i > 0)
        def _():
            dg_ref[...] += dg
            loss_ref[...] += part

    row = pl.BlockSpec((tr, d), lambda i: (i, 0))
    vec = pl.BlockSpec((1, d), lambda i: (0, 0))
    return pl.pallas_call(
        body,
        grid=(tp // tr,),
        in_specs=[row, vec, row],
        out_specs=[row, row, vec, pl.BlockSpec((1, 1), lambda i: (0, 0))],
        out_shape=[jax.ShapeDtypeStruct((tp, d), F32), jax.ShapeDtypeStruct((tp, d), CDT),
                   jax.ShapeDtypeStruct((1, d), F32), jax.ShapeDtypeStruct((1, 1), F32)],
        compiler_params=_params("arbitrary"),
        name="final_loss",
    )(h, g.reshape(1, d), target)


def _sigmoid(x):
    return 1.0 / (1.0 + jnp.exp(-x))


def _ffn_gate_up(x, w4, *, tm, tn):
    t, d = x.shape
    n_chip = w4.shape[3]
    f = N_CHIPS * n_chip // 2
    per = n_chip // tn
    assert t % tm == 0 and n_chip % tn == 0, (t, tm, n_chip, tn)

    def body(x_ref, wg_ref, wu_ref, g_ref, u_ref, a_ref):
        xb = x_ref[...]
        g = _dot(xb, wg_ref[...])
        u = _dot(xb, wu_ref[...])
        g_ref[...] = g.astype(CDT)
        u_ref[...] = u.astype(CDT)
        a_ref[...] = ((g * _sigmoid(g)) * u).astype(CDT)

    w_blk = (None, None, d, tn)
    out_blk = pl.BlockSpec((tm, tn), lambda j, i: (i, j))
    out = jax.ShapeDtypeStruct((t, f), CDT)
    return pl.pallas_call(
        body,
        grid=(f // tn, t // tm),
        in_specs=[pl.BlockSpec((tm, d), lambda j, i: (i, 0)),
                  pl.BlockSpec(w_blk, lambda j, i: (j // per, 0, 0, j % per)),
                  pl.BlockSpec(w_blk, lambda j, i: (N_CHIPS // 2 + j // per, 0, 0, j % per))],
        out_specs=[out_blk, out_blk, out_blk],
        out_shape=[out, out, out],
        compiler_params=_params("parallel", "parallel"),
        name="ffn_gate_up",
    )(x, w4, w4)


def _swiglu_bwd(g, u, da):
    tp, f = g.shape
    f2 = 2 * f
    tr = _div_tile(tp, 128, 16)

    def body(g_ref, u_ref, da_ref, o_ref):
        g = g_ref[...].astype(F32)
        da = da_ref[...].astype(F32)
        sg = _sigmoid(g)
        o_ref[:, :f] = (da * u_ref[...].astype(F32) * (sg * (1.0 + g * (1.0 - sg)))).astype(o_ref.dtype)
        o_ref[:, f:] = (da * (g * sg)).astype(o_ref.dtype)

    return pl.pallas_call(
        body,
        grid=(tp // tr,),
        in_specs=[pl.BlockSpec((tr, f), lambda i: (i, 0))] * 3,
        out_specs=pl.BlockSpec((tr, f2), lambda i: (i, 0)),
        out_shape=jax.ShapeDtypeStruct((tp, f2), CDT),
        compiler_params=_params("parallel"),
        name="swiglu_bwd",
    )(g, u, da)


def _swap_halves(x):
    lane = lax.broadcasted_iota(jnp.int32, x.shape, x.ndim - 1)
    first = (lane % QK_ROPE) < (QK_ROPE // 2)
    return jnp.where(first, pltpu.roll(x, LANES - QK_ROPE // 2, x.ndim - 1), pltpu.roll(x, QK_ROPE // 2, x.ndim - 1))


def _rope_tables(tp):
    pos = jnp.arange(tp, dtype=F32)
    inv_freq = ROPE_THETA ** (-jnp.arange(0, QK_ROPE, 2, dtype=F32) / QK_ROPE)
    ang = pos[:, None] * inv_freq[None, :]
    cos, sin = jnp.cos(ang), jnp.sin(ang)
    reps = LANES // QK_ROPE
    return jnp.tile(jnp.concatenate([cos, cos], -1), (1, reps)), jnp.tile(jnp.concatenate([-sin, sin], -1), (1, reps))


def _chunk_of(pos):
    shift = CHUNK.bit_length() - 1
    assert CHUNK == 1 << shift
    return jnp.where(pos < N_META, 0, 1 + lax.shift_right_arithmetic(pos - N_META, shift))


def _head_half(x, h):
    lane = lax.broadcasted_iota(jnp.int32, x.shape, x.ndim - 1)
    return jnp.where((lane // QK_ROPE) == (h % 2), x, jnp.zeros_like(x))


def _mla_prep(q, kv, proj, cos2, sin2):
    tp = q.shape[0]
    tr = _div_tile(tp, ROW_TILE, 8)
    nope_w = MLA_HEADS * QK_NOPE
    kr_block = (Q_LORA + KV_LORA) // LANES
    depth = QK_NOPE + LANES
    scale = ATT_SCALE

    def body(q_ref, kv_ref, kr_ref, c_ref, s_ref, qp_out, kp_out):
        c = c_ref[...]
        s = s_ref[...]
        k = kr_ref[...]
        k = k + pltpu.roll(k, QK_ROPE, 1)
        k = (k * c + _swap_halves(k) * s).astype(CDT)
        for p in range(MLA_HEADS // 2):
            x = q_ref[:, nope_w + p * LANES:nope_w + (p + 1) * LANES]
            pair = ((x * c + _swap_halves(x) * s) * scale).astype(CDT)
            for h in (2 * p, 2 * p + 1):
                qp_out[h, :, :QK_NOPE] = (q_ref[:, h * QK_NOPE:(h + 1) * QK_NOPE] * scale).astype(CDT)
                qp_out[h, :, QK_NOPE:] = _head_half(pair, h)
                kp_out[h, :, :QK_NOPE] = kv_ref[:, 2 * h * QK_NOPE:(2 * h + 1) * QK_NOPE]
                kp_out[h, :, QK_NOPE:] = k

    tab = pl.BlockSpec((tr, LANES), lambda i: (i, 0))
    per_head = pl.BlockSpec((MLA_HEADS, tr, depth), lambda i: (0, i, 0))
    out = jax.ShapeDtypeStruct((MLA_HEADS, tp, depth), CDT)
    return pl.pallas_call(
        body,
        grid=(tp // tr,),
        in_specs=[pl.BlockSpec((tr, q.shape[1]), lambda i: (i, 0)), pl.BlockSpec((tr, kv.shape[1]), lambda i: (i, 0)),
                  pl.BlockSpec((tr, LANES), lambda i: (i, kr_block)), tab, tab],
        out_specs=[per_head, per_head],
        out_shape=[out, out],
        compiler_params=_params("parallel"),
        name="mla_prep",
    )(q, kv, proj, cos2, sin2)


def _dot_nt(a, b):
    return lax.dot_general(a, b, (((1,), (1,)), ((), ())), preferred_element_type=F32)


def _dot_tn(a, b):
    return lax.dot_general(a, b, (((0,), (0,)), ((), ())), preferred_element_type=F32)


def _dot(a, b):
    return jnp.dot(a, b, preferred_element_type=F32)


def _chunk_scalar(p):
    return jnp.where(p < N_META, 0, 1 + jnp.maximum(p - N_META, 0) // CHUNK)


def _last_key_block(i, bq, bk, nk):
    cq = _chunk_scalar(i * bq + bq - 1)
    return jnp.minimum((N_META + CHUNK * cq - 1) // bk, nk - 1)


def _full_key_blocks(i, bq, bk):
    return (N_META + CHUNK * _chunk_scalar(i * bq)) // bk


def _first_query_block(j, bk, bq):
    p0 = N_META + CHUNK * (jnp.maximum(j * bk - N_META, 0) // CHUNK)
    return p0 // bq


def _first_full_query_block(j, bk, bq, nq):
    ck = _chunk_scalar(j * bk + bk - 1)
    p0 = jnp.where(ck == 0, 0, N_META + CHUNK * (ck - 1))
    return jnp.minimum((p0 + bq - 1) // bq, nq)


def _chunk_mask(q0, k0, shape, keys_on_rows):
    if keys_on_rows:
        kc = _chunk_of(k0 + lax.broadcasted_iota(jnp.int32, (shape[0], 1), 0))
        qc = _chunk_of(q0 + lax.broadcasted_iota(jnp.int32, (1, shape[1]), 1))
    else:
        qc = _chunk_of(q0 + lax.broadcasted_iota(jnp.int32, (shape[0], 1), 0))
        kc = _chunk_of(k0 + lax.broadcasted_iota(jnp.int32, (1, shape[1]), 1))
    return kc <= qc


def _attn_fwd(qp, kp, kv):
    tp = qp.shape[1]
    depth = qp.shape[2]
    bq = bk = _div_tile(tp, ATT_TILE, LANES)
    nq, nk = tp // bq, tp // bk
    scale = (QK_NOPE + QK_ROPE) ** -0.5

    def body(q_ref, k_ref, v_ref, o_ref, lse_ref):
        def q_block(i, _):
            q0 = pl.multiple_of(i * bq, bq)
            qb = q_ref[pl.ds(q0, bq), :]

            def k_step(masked, j, carry):
                m_old, l_old, acc = carry
                k0 = pl.multiple_of(j * bk, bk)
                s = _dot_nt(qb, k_ref[pl.ds(k0, bk), :])
                if masked:
                    s = jnp.where(_chunk_mask(q0, k0, s.shape, False), s, NEG_BIG)
                m_new = jnp.maximum(m_old, jnp.max(s, axis=-1, keepdims=True))
                alpha = jnp.exp(m_old - m_new)
                p = jnp.exp(s - m_new)
                l_new = alpha * l_old + jnp.sum(p, axis=-1, keepdims=True)
                acc = alpha * acc + _dot(p.astype(CDT), v_ref[pl.ds(k0, bk), :])
                return m_new, l_new, acc

            n_full = _full_key_blocks(i, bq, bk)
            carry = (jnp.full((bq, 1), NEG_BIG, F32), jnp.zeros((bq, 1), F32), jnp.zeros((bq, V_HEAD), F32))
            carry = lax.fori_loop(0, n_full, lambda j, c: k_step(False, j, c), carry)
            m_fin, l_fin, acc = lax.fori_loop(n_full, _last_key_block(i, bq, bk, nk) + 1,
                                              lambda j, c: k_step(True, j, c), carry)
            o_ref[pl.ds(q0, bq), :] = acc / l_fin
            lse_ref[pl.ds(q0, bq), :] = m_fin + jnp.log(l_fin)
            return 0

        lax.fori_loop(0, nq, q_block, 0)

    per_head = pl.BlockSpec((None, tp, depth), lambda h: (h, 0, 0))
    return pl.pallas_call(
        body,
        grid=(MLA_HEADS,),
        in_specs=[per_head, per_head, pl.BlockSpec((tp, V_HEAD), lambda h: (0, 2 * h + 1))],
        out_specs=[pl.BlockSpec((tp, V_HEAD), lambda h: (0, h)), pl.BlockSpec((None, tp, 1), lambda h: (h, 0, 0))],
        out_shape=[jax.ShapeDtypeStruct((tp, MLA_HEADS * V_HEAD), F32), jax.ShapeDtypeStruct((MLA_HEADS, tp, 1), F32)],
        compiler_params=_params("parallel"),
        name="attn_fwd",
    )(qp, kp, kv)


def _attn_delta(d_out, out):
    tp = out.shape[0]
    tr = _div_tile(tp, ROW_TILE, 8)

    def body(do_ref, o_ref, d_ref):
        for h in range(MLA_HEADS):
            cols = slice(h * V_HEAD, (h + 1) * V_HEAD)
            d_ref[h] = jnp.sum(do_ref[:, cols] * o_ref[:, cols], axis=-1, keepdims=True)

    row = pl.BlockSpec((tr, MLA_HEADS * V_HEAD), lambda i: (i, 0))
    return pl.pallas_call(
        body,
        grid=(tp // tr,),
        in_specs=[row, row],
        out_specs=pl.BlockSpec((MLA_HEADS, tr, 1), lambda i: (0, i, 0)),
        out_shape=jax.ShapeDtypeStruct((MLA_HEADS, tp, 1), F32),
        compiler_params=_params("parallel"),
        name="attn_delta",
    )(d_out, out)


def _attn_bwd(qp, kp, kv, d_out, lse, delta):
    tp = qp.shape[1]
    depth = qp.shape[2]
    bq = bk = _div_tile(tp, ATT_TILE, LANES)
    nq, nk = tp // bq, tp // bk
    scale = (QK_NOPE + QK_ROPE) ** -0.5
    lse_rows = lse.reshape(MLA_HEADS, nq, 1, bq)
    delta_rows = delta.reshape(MLA_HEADS, nq, 1, bq)

    strip = _div_tile(bk, ATT_STRIP, 16)

    def body(q_ref, k_ref, v_ref, do_ref, lse_ref, dl_ref, dqn_ref, dqr_ref, dkv_ref, dkr_ref,
             dq_acc, dk_acc, dv_acc, s_scr, dp_scr, p_scr, ds_scr):
        h = pl.program_id(0)
        dq_acc[...] = jnp.zeros(dq_acc.shape, F32)

        def k_block(j, _):
            k0 = pl.multiple_of(j * bk, bk)
            kb = k_ref[pl.ds(k0, bk), :]
            vb = v_ref[pl.ds(k0, bk), :]
            dk_acc[...] = jnp.zeros(dk_acc.shape, F32)
            dv_acc[...] = jnp.zeros(dv_acc.shape, F32)

            def q_step(masked, i, _):
                q0 = pl.multiple_of(i * bq, bq)
                qb = q_ref[pl.ds(q0, bq), :]
                dob = do_ref[pl.ds(q0, bq), :].astype(CDT)
                s_scr[...] = _dot_nt(kb, qb)
                dp_scr[...] = _dot_nt(vb, dob)
                lse_row = lse_ref[i]
                delta_row = dl_ref[i]
                for r0 in range(0, bk, strip):
                    rows = slice(r0, r0 + strip)
                    s_t = s_scr[rows, :]
                    if masked:
                        s_t = jnp.where(_chunk_mask(q0, k0 + r0, s_t.shape, True), s_t, NEG_BIG)
                    p_t = jnp.exp(s_t - lse_row)
                    p_scr[rows, :] = p_t.astype(CDT)
                    ds_scr[rows, :] = (p_t * (dp_scr[rows, :] - delta_row)).astype(CDT)
                ds_t = ds_scr[...]
                dv_acc[...] += _dot(p_scr[...], dob)
                dk_acc[...] += _dot(ds_t, qb)
                dq_acc[pl.ds(q0, bq), :] += _dot_tn(ds_t, kb)
                return 0

            i_full = _first_full_query_block(j, bk, bq, nq)
            lax.fori_loop(_first_query_block(j, bk, bq), i_full, lambda i, c: q_step(True, i, c), 0)
            lax.fori_loop(i_full, nq, lambda i, c: q_step(False, i, c), 0)
            dkv_ref[pl.ds(k0, bk), :QK_NOPE] = dk_acc[:, :QK_NOPE].astype(CDT)
            dkv_ref[pl.ds(k0, bk), QK_NOPE:] = dv_acc[...].astype(CDT)
            dkr_ref[pl.ds(k0, bk), :] = dk_acc[:, QK_NOPE:]
            return 0

        lax.fori_loop(0, nk, k_block, 0)
        dqn_ref[...] = (dq_acc[:, :QK_NOPE] * scale).astype(CDT)
        dqr_ref[...] = _head_half(dq_acc[:, QK_NOPE:] * scale, h)

    per_head = pl.BlockSpec((None, tp, depth), lambda h: (h, 0, 0))
    stat = pl.BlockSpec((None, nq, 1, bq), lambda h: (h, 0, 0, 0))
    lanes_out = pl.BlockSpec((None, tp, LANES), lambda h: (h, 0, 0))
    return pl.pallas_call(
        body,
        grid=(MLA_HEADS,),
        in_specs=[per_head, per_head, pl.BlockSpec((tp, V_HEAD), lambda h: (0, 2 * h + 1)),
                  pl.BlockSpec((tp, V_HEAD), lambda h: (0, h)), stat, stat],
        out_specs=[pl.BlockSpec((tp, QK_NOPE), lambda h: (0, h)), lanes_out,
                   pl.BlockSpec((tp, QK_NOPE + V_HEAD), lambda h: (0, h)), lanes_out],
        out_shape=[jax.ShapeDtypeStruct((tp, MLA_HEADS * QK_NOPE), CDT), jax.ShapeDtypeStruct((MLA_HEADS, tp, LANES), F32),
                   jax.ShapeDtypeStruct((tp, MLA_HEADS * (QK_NOPE + V_HEAD)), CDT),
                   jax.ShapeDtypeStruct((MLA_HEADS, tp, LANES), F32)],
        scratch_shapes=[pltpu.VMEM((tp, depth), F32), pltpu.VMEM((bk, depth), F32), pltpu.VMEM((bk, V_HEAD), F32),
                        pltpu.VMEM((bk, bq), F32), pltpu.VMEM((bk, bq), F32), pltpu.VMEM((bk, bq), CDT),
                        pltpu.VMEM((bk, bq), CDT)],
        compiler_params=_params("parallel"),
        name="attn_bwd",
    )(qp, kp, kv, d_out, lse_rows, delta_rows)


def _mla_unprep(dqr_h, dkr_h, cos2, sin2):
    tp = dqr_h.shape[1]
    tr = _div_tile(tp, ROW_TILE, 8)
    wr = MLA_HEADS * QK_ROPE

    def body(dq_ref, dk_ref, c_ref, s_ref, dqr_out, dkr_out):
        c = c_ref[...]
        s = s_ref[...]
        for p in range(MLA_HEADS // 2):
            x = dq_ref[2 * p] + dq_ref[2 * p + 1]
            dqr_out[:, p * LANES:(p + 1) * LANES] = (x * c - _swap_halves(x) * s).astype(CDT)
        t = dk_ref[0]
        for h in range(1, MLA_HEADS):
            t = t + dk_ref[h]
        t = t * c - _swap_halves(t) * s
        t = t + pltpu.roll(t, QK_ROPE, 1)
        lane = lax.broadcasted_iota(jnp.int32, t.shape, 1)
        dkr_out[...] = jnp.where(lane < QK_ROPE, t, 0.0)

    per_head = pl.BlockSpec((MLA_HEADS, tr, LANES), lambda i: (0, i, 0))
    tab = pl.BlockSpec((tr, LANES), lambda i: (i, 0))
    return pl.pallas_call(
        body,
        grid=(tp // tr,),
        in_specs=[per_head, per_head, tab, tab],
        out_specs=[pl.BlockSpec((tr, wr), lambda i: (i, 0)), tab],
        out_shape=[jax.ShapeDtypeStruct((tp, wr), CDT), jax.ShapeDtypeStruct((tp, LANES), F32)],
        compiler_params=_params("parallel"),
        name="mla_unprep",
    )(dqr_h, dkr_h, cos2, sin2)


HALO = 8


def _softplus(x):
    return jnp.maximum(x, 0.0) + jnp.log1p(jnp.exp(-jnp.abs(x)))


def _one_minus_sq(log_a, a):
    return -jnp.tanh(log_a) * (a * a + 1.0)


def _gelu(y):
    k = math.sqrt(2.0 / math.pi)
    return 0.5 * y * (1.0 + jnp.tanh(k * (y + 0.044715 * (y * y * y))))


def _gelu_grad(y):
    k = math.sqrt(2.0 / math.pi)
    th = jnp.tanh(k * (y + 0.044715 * (y * y * y)))
    return 0.5 * (1.0 + th) + 0.5 * y * (1.0 - th * th) * (k * (1.0 + 3.0 * 0.044715 * (y * y)))


def _lru_gates_fwd(xy, conv_w, conv_b, w_ga, b_ga, w_gx, b_gx, lam):
    tp = xy.shape[0]
    dr = xy.shape[1] // 2
    bw = dr // RNN_BLOCKS
    tr = _div_tile(tp, ROW_TILE, 8)

    def body(x_ref, halo_ref, cw_ref, cb_ref, wa_ref, ba_ref, wx_ref, bx_ref, lam_ref,
             xc_ref, r_ref, i_ref, a_ref, b_ref, xs):
        i = pl.program_id(0)
        xs[0:HALO, :] = jnp.where(i == 0, 0.0, halo_ref[...])
        xs[HALO:, :] = x_ref[...]
        xc = cb_ref[...] + cw_ref[0:1, :] * xs[pl.ds(HALO - CONV_W + 1, tr), :]
        for j in range(1, CONV_W):
            xc = xc + cw_ref[j:j + 1, :] * xs[pl.ds(HALO - CONV_W + 1 + j, tr), :]
        xcb = xc.astype(CDT)
        r = _sigmoid(_dot(xcb, wa_ref[...]) + ba_ref[...])
        ig = _sigmoid(_dot(xcb, wx_ref[...]) + bx_ref[...])
        log_a = (-LRU_C * r) * _softplus(-lam_ref[...])
        a = jnp.exp(log_a)
        xc_ref[...] = xc
        r_ref[...] = r
        i_ref[...] = ig
        a_ref[...] = a
        b_ref[...] = jnp.sqrt(_one_minus_sq(log_a, a)) * (ig * xc)

    blk = pl.BlockSpec((tr, bw), lambda i, n: (i, n))
    vec = pl.BlockSpec((1, bw), lambda i, n: (0, n))
    mat = pl.BlockSpec((None, bw, bw), lambda i, n: (n, 0, 0))
    bias = pl.BlockSpec((None, 1, bw), lambda i, n: (n, 0, 0))
    out = jax.ShapeDtypeStruct((tp, dr), F32)
    return pl.pallas_call(
        body,
        grid=(tp // tr, RNN_BLOCKS),
        in_specs=[blk, pl.BlockSpec((HALO, bw), lambda i, n: (jnp.maximum(i * (tr // HALO) - 1, 0), n)),
                  pl.BlockSpec((CONV_W, bw), lambda i, n: (0, n)), vec, mat, bias, mat, bias, vec],
        out_specs=[blk] * 5,
        out_shape=[out] * 5,
        scratch_shapes=[pltpu.VMEM((tr + HALO, bw), F32)],
        compiler_params=_params("parallel", "parallel"),
        name="lru_gates_fwd",
    )(xy, xy, conv_w, conv_b.reshape(1, dr), w_ga.astype(CDT), b_ga.reshape(RNN_BLOCKS, 1, bw),
      w_gx.astype(CDT), b_gx.reshape(RNN_BLOCKS, 1, bw), lam.reshape(1, dr))


def _stack_rows(rows):
    idx = lax.broadcasted_iota(jnp.int32, (len(rows), rows[0].shape[1]), 0)
    out = jnp.broadcast_to(rows[0], idx.shape)
    for j in range(1, len(rows)):
        out = jnp.where(idx == j, jnp.broadcast_to(rows[j], idx.shape), out)
    return out


def _lru_scan_fwd(a, b, xy):
    tp, dr = a.shape
    cw = min(2 * SCAN_COLS, dr)
    ycol0 = dr // cw
    ch = _div_tile(tp, ROW_TILE, 16)

    def body(a_ref, b_ref, y_ref, hs_ref, m_ref):
        def group(g, h):
            base = pl.multiple_of(g * 8, 8)
            at = a_ref[pl.ds(base, 8), :]
            bt = b_ref[pl.ds(base, 8), :]
            rows = []
            for j in range(8):
                h = at[j:j + 1, :] * h + bt[j:j + 1, :]
                rows.append(h)
            hs_ref[pl.ds(base, 8), :] = _stack_rows(rows)
            return h

        lax.fori_loop(0, tp // 8, group, jnp.zeros((1, cw), F32))

        def gate(c, _):
            r0 = pl.multiple_of(c * ch, ch)
            m_ref[pl.ds(r0, ch), :] = (hs_ref[pl.ds(r0, ch), :] * _gelu(y_ref[pl.ds(r0, ch), :])).astype(CDT)
            return 0

        lax.fori_loop(0, tp // ch, gate, 0)

    col = pl.BlockSpec((tp, cw), lambda n: (0, n))
    return pl.pallas_call(
        body,
        grid=(dr // cw,),
        in_specs=[col, col, pl.BlockSpec((tp, cw), lambda n: (0, ycol0 + n))],
        out_specs=[col, col],
        out_shape=[jax.ShapeDtypeStruct((tp, dr), F32), jax.ShapeDtypeStruct((tp, dr), CDT)],
        compiler_params=_params("parallel"),
        name="lru_scan_fwd",
    )(a, b, xy)


def _lru_scan_bwd(a, hs, dm, xy):
    tp, dr = a.shape
    cw = min(2 * SCAN_COLS, dr)
    ycol0 = dr // cw
    ng = tp // 8
    ch = _div_tile(tp, ROW_TILE, 16)

    def body(a_ref, hs_ref, dm_ref, y_ref, db_ref, da_ref, dy_ref):
        def ungate(c, _):
            rows = pl.ds(pl.multiple_of(c * ch, ch), ch)
            y = y_ref[rows, :]
            dm = dm_ref[rows, :]
            db_ref[rows, :] = dm * _gelu(y)
            dy_ref[rows, :] = (dm * hs_ref[rows, :] * _gelu_grad(y)).astype(CDT)
            return 0

        lax.fori_loop(0, tp // ch, ungate, 0)

        def group(k, carry):
            g_next, a_next = carry
            g = ng - 1 - k
            base = pl.multiple_of(g * 8, 8)
            prev = pl.multiple_of(jnp.maximum(g - 1, 0) * 8, 8)
            dt = db_ref[pl.ds(base, 8), :]
            at = a_ref[pl.ds(base, 8), :]
            ht = hs_ref[pl.ds(base, 8), :]
            h_before = jnp.where(g == 0, 0.0, hs_ref[pl.ds(prev, 8), :][7:8, :])
            g_rows = [None] * 8
            da_rows = [None] * 8
            for j in range(7, -1, -1):
                g_cur = dt[j:j + 1, :] + a_next * g_next
                g_rows[j] = g_cur
                da_rows[j] = g_cur * (ht[j - 1:j, :] if j > 0 else h_before)
                g_next = g_cur
                a_next = at[j:j + 1, :]
            db_ref[pl.ds(base, 8), :] = _stack_rows(g_rows)
            da_ref[pl.ds(base, 8), :] = _stack_rows(da_rows)
            return g_next, a_next

        zero = jnp.zeros((1, cw), F32)
        lax.fori_loop(0, ng, group, (zero, zero))

    col = pl.BlockSpec((tp, cw), lambda n: (0, n))
    col_in = pl.BlockSpec((tp, cw), lambda n: (0, n), pipeline_mode=pl.Buffered(1))
    y_in = pl.BlockSpec((tp, cw), lambda n: (0, ycol0 + n), pipeline_mode=pl.Buffered(1))
    return pl.pallas_call(
        body,
        grid=(dr // cw,),
        in_specs=[col_in, col_in, col_in, y_in],
        out_specs=[col, col, col],
        out_shape=[jax.ShapeDtypeStruct((tp, dr), F32), jax.ShapeDtypeStruct((tp, dr), F32),
                   jax.ShapeDtypeStruct((tp, dr), CDT)],
        compiler_params=_params("parallel"),
        name="lru_scan_bwd",
    )(a, hs, dm, xy)


def _lru_gates_bwd(db, da, xc, r, ig, a, lam, w_ga, w_gx):
    tp, dr = xc.shape
    bw = dr // RNN_BLOCKS
    tr = _div_tile(tp, ROW_TILE, 8)
    nr = tp // tr

    def body(db_ref, da_ref, xc_ref, r_ref, i_ref, a_ref, lam_ref, wa_ref, wx_ref,
             dxc_ref, dwa_ref, dba_ref, dwx_ref, dbx_ref, dlam_ref):
        i = pl.program_id(1)
        xc = xc_ref[...]
        r = r_ref[...]
        ig = i_ref[...]
        a = a_ref[...]
        dbv = db_ref[...]
        sp = _softplus(-lam_ref[...])
        log_a = (-LRU_C * r) * sp
        s = jnp.sqrt(_one_minus_sq(log_a, a))
        d_ix = dbv * s
        d_s = dbv * (ig * xc)
        d_log_a = da_ref[...] * a - d_s * (a * a) / s
        d_r = d_log_a * (-LRU_C * sp)
        d_sp = jnp.sum(d_log_a * (-LRU_C * r), axis=0, keepdims=True)
        dzr = d_r * r * (1.0 - r)
        dzi = (d_ix * xc) * ig * (1.0 - ig)
        dzr_b = dzr.astype(CDT)
        dzi_b = dzi.astype(CDT)
        xcb = xc.astype(CDT)
        dxc_ref[...] = d_ix * ig + _dot_nt(dzr_b, wa_ref[...]) + _dot_nt(dzi_b, wx_ref[...])
        dwa = _dot_tn(xcb, dzr_b)
        dwx = _dot_tn(xcb, dzi_b)
        dba = jnp.sum(dzr, axis=0, keepdims=True)
        dbx = jnp.sum(dzi, axis=0, keepdims=True)

        @pl.when(i == 0)
        def _():
            dwa_ref[...] = dwa
            dwx_ref[...] = dwx
            dba_ref[...] = dba
            dbx_ref[...] = dbx
            dlam_ref[...] = d_sp

        @pl.when(i > 0)
        def _():
            dwa_ref[...] += dwa
            dwx_ref[...] += dwx
            dba_ref[...] += dba
            dbx_ref[...] += dbx
            dlam_ref[...] += d_sp

        @pl.when(i == nr - 1)
        def _():
            dlam_ref[...] = dlam_ref[...] * (-_sigmoid(-lam_ref[...]))

    blk = pl.BlockSpec((tr, bw), lambda n, i: (i, n))
    vec = pl.BlockSpec((1, bw), lambda n, i: (0, n))
    mat = pl.BlockSpec((None, bw, bw), lambda n, i: (n, 0, 0))
    bias = pl.BlockSpec((None, 1, bw), lambda n, i: (n, 0, 0))
    return pl.pallas_call(
        body,
        grid=(RNN_BLOCKS, nr),
        in_specs=[blk] * 6 + [vec, mat, mat],
        out_specs=[blk, mat, bias, mat, bias, vec],
        out_shape=[jax.ShapeDtypeStruct((tp, dr), F32),
                   jax.ShapeDtypeStruct((RNN_BLOCKS, bw, bw), F32), jax.ShapeDtypeStruct((RNN_BLOCKS, 1, bw), F32),
                   jax.ShapeDtypeStruct((RNN_BLOCKS, bw, bw), F32), jax.ShapeDtypeStruct((RNN_BLOCKS, 1, bw), F32),
                   jax.ShapeDtypeStruct((1, dr), F32)],
        compiler_params=_params("parallel", "arbitrary"),
        name="lru_gates_bwd",
    )(db, da, xc, r, ig, a, lam.reshape(1, dr), w_ga.astype(CDT), w_gx.astype(CDT))


def _lru_conv_bwd(dxc, xy, conv_w):
    tp, dr = dxc.shape
    bw = dr // RNN_BLOCKS
    tr = _div_tile(tp, ROW_TILE, 8)
    nr = tp // tr
    per = tr // HALO

    def body(d_ref, dnext_ref, x_ref, xprev_ref, cw_ref, dxb_ref, dcw_ref, dcb_ref, ds, xs):
        i = pl.program_id(1)
        d = d_ref[...]
        ds[0:tr, :] = d
        ds[tr:, :] = jnp.where(i == nr - 1, 0.0, dnext_ref[...])
        xs[0:HALO, :] = jnp.where(i == 0, 0.0, xprev_ref[...])
        xs[HALO:, :] = x_ref[...]
        dxb = cw_ref[0:1, :] * ds[pl.ds(CONV_W - 1, tr), :]
        for j in range(1, CONV_W):
            dxb = dxb + cw_ref[j:j + 1, :] * ds[pl.ds(CONV_W - 1 - j, tr), :]
        dxb_ref[...] = dxb.astype(CDT)
        dcb = jnp.sum(d, axis=0, keepdims=True)
        dcw = [jnp.sum(d * xs[pl.ds(HALO - CONV_W + 1 + j, tr), :], axis=0, keepdims=True) for j in range(CONV_W)]

        @pl.when(i == 0)
        def _():
            dcb_ref[...] = dcb
            for j in range(CONV_W):
                dcw_ref[j] = dcw[j]

        @pl.when(i > 0)
        def _():
            dcb_ref[...] += dcb
            for j in range(CONV_W):
                dcw_ref[j] += dcw[j]

    blk = pl.BlockSpec((tr, bw), lambda n, i: (i, n))
    return pl.pallas_call(
        body,
        grid=(RNN_BLOCKS, nr),
        in_specs=[blk, pl.BlockSpec((HALO, bw), lambda n, i: (jnp.minimum((i + 1) * per, tp // HALO - 1), n)),
                  blk, pl.BlockSpec((HALO, bw), lambda n, i: (jnp.maximum(i * per - 1, 0), n)),
                  pl.BlockSpec((CONV_W, bw), lambda n, i: (0, n))],
        out_specs=[blk, pl.BlockSpec((CONV_W, 1, bw), lambda n, i: (0, 0, n)), pl.BlockSpec((1, bw), lambda n, i: (0, n))],
        out_shape=[jax.ShapeDtypeStruct((tp, dr), CDT), jax.ShapeDtypeStruct((CONV_W, 1, dr), F32),
                   jax.ShapeDtypeStruct((1, dr), F32)],
        scratch_shapes=[pltpu.VMEM((tr + HALO, bw), F32), pltpu.VMEM((tr + HALO, bw), F32)],
        compiler_params=_params("parallel", "arbitrary"),
        name="lru_conv_bwd",
    )(dxc, dxc, xy, xy, conv_w)


def _me():
    return lax.axis_index("x"), lax.axis_index("y"), lax.axis_index("c")


def _peer(rel):
    x, y, c = _me()
    return (1 - x if rel & 4 else x, 1 - y if rel & 2 else y, 1 - c if rel & 1 else c)


def _chip_of(dev):
    return 2 * dev[0] + dev[1]


def _linear(dev):
    return 4 * dev[0] + 2 * dev[1] + dev[2]


CHIP_RELS = (4, 2, 6)
ALL_RELS = (1, 2, 3, 4, 5, 6, 7)
PAIR_RELS = (1,)


def _scatter_send(pieces, rels, piece_of, name):
    n = len(rels)

    def body(src_ref, recv_ref, send_sems, recv_sems):
        copies = []
        for k, rel in enumerate(rels):
            peer = _peer(rel)
            cp = pltpu.make_async_remote_copy(
                src_ref=src_ref.at[piece_of(peer)], dst_ref=recv_ref.at[k],
                send_sem=send_sems.at[k], recv_sem=recv_sems.at[k], device_id=peer, device_id_type=MESH)
            cp.start()
            copies.append(cp)
        for cp in copies:
            cp.wait()

    return pl.pallas_call(
        body,
        in_specs=[pl.BlockSpec(memory_space=pl.ANY)],
        out_specs=pl.BlockSpec(memory_space=pl.ANY),
        out_shape=jax.ShapeDtypeStruct((n,) + pieces.shape[1:], pieces.dtype),
        scratch_shapes=[pltpu.SemaphoreType.DMA((n,)), pltpu.SemaphoreType.DMA((n,))],
        name=name,
    )(pieces)


def _gather_send(piece, rels, n_slots, slot_of, name, n_chunks=1):
    n = len(rels)
    rows = piece.shape[0]
    if rows % (8 * n_chunks):
        n_chunks = 1
    rc = rows // n_chunks

    def body(src_ref, out_ref, send_sems, recv_sems, local_sems):
        me = _me()

        def part(ref, q):
            return ref.at[pl.ds(q * rc, rc)]

        def remote(k, q, slot_dev, to):
            return pltpu.make_async_remote_copy(
                src_ref=part(src_ref, q), dst_ref=part(out_ref.at[slot_of(slot_dev)], q),
                send_sem=send_sems.at[k * n_chunks + q], recv_sem=recv_sems.at[k * n_chunks + q],
                device_id=to, device_id_type=MESH)

        mine = [pltpu.make_async_copy(part(src_ref, q), part(out_ref.at[slot_of(me)], q), local_sems.at[q])
                for q in range(n_chunks)]
        for cp in mine:
            cp.start()
        sends = [remote(k, q, me, _peer(rel)) for k, rel in enumerate(rels) for q in range(n_chunks)]
        for cp in sends:
            cp.start()
        for k, rel in enumerate(rels):
            for q in range(n_chunks):
                remote(k, q, _peer(rel), _peer(rel)).wait_recv()
        for cp in sends:
            cp.wait_send()
        for cp in mine:
            cp.wait()

    return pl.pallas_call(
        body,
        in_specs=[pl.BlockSpec(memory_space=pl.ANY)],
        out_specs=pl.BlockSpec(memory_space=pl.ANY),
        out_shape=jax.ShapeDtypeStruct((n_slots,) + piece.shape, piece.dtype),
        scratch_shapes=[pltpu.SemaphoreType.DMA((n * n_chunks,)), pltpu.SemaphoreType.DMA((n * n_chunks,)),
                        pltpu.SemaphoreType.DMA((n_chunks,))],
        name=name,
    )(piece)


def _gather_chips(shard, name):
    return _gather_send(shard, CHIP_RELS, N_CHIPS, _chip_of, name)


HBM_SPEC = pl.BlockSpec(memory_space=pltpu.HBM)
SEM_SPEC = pl.BlockSpec(memory_space=pltpu.SEMAPHORE)
DATAFLOW = pltpu.SideEffectType.DATAFLOW_SIDE_EFFECTING


def _split_start(src, land, copies, name):
    def body(src_ref, land_ref, send_sem, recv_sem, src_thru, land_thru, token):
        for s_ref, d_ref, peer in copies(src_ref, land_ref):
            pltpu.make_async_remote_copy(src_ref=s_ref, dst_ref=d_ref, send_sem=send_sem, recv_sem=recv_sem,
                                         device_id=peer, device_id_type=MESH).start()
        token[...] = jnp.zeros(token.shape, token.dtype)

    return pl.pallas_call(
        body,
        name=name,
        out_shape=(pltpu.SemaphoreType.DMA(()), pltpu.SemaphoreType.DMA(()), pltpu.HBM(src.shape, src.dtype),
                   pltpu.HBM(land.shape, land.dtype), jax.ShapeDtypeStruct((8, LANES), F32)),
        in_specs=(HBM_SPEC, HBM_SPEC),
        out_specs=(SEM_SPEC, SEM_SPEC, HBM_SPEC, HBM_SPEC, pl.BlockSpec(memory_space=pltpu.VMEM)),
        input_output_aliases={0: 2, 1: 3},
        compiler_params=pltpu.CompilerParams(has_side_effects=DATAFLOW),
    )(pltpu.with_memory_space_constraint(src, pltpu.HBM), pltpu.with_memory_space_constraint(land, pltpu.HBM))


def _split_gather_start(piece, rels, n_slots, slot_of, name):
    land = jnp.broadcast_to(piece[None], (n_slots,) + piece.shape)
    return _split_start(piece, land, lambda s, l: [(s, l.at[slot_of(_me())], _peer(rel)) for rel in rels], name)


def _gather_chips_start(shard, name):
    return _split_gather_start(shard, CHIP_RELS, N_CHIPS, _chip_of, name)


def _split_scatter_start(pieces, name):
    land = lax.empty((len(ALL_RELS),) + pieces.shape[1:], pieces.dtype)
    return _split_start(pieces, land,
                        lambda s, l: [(s.at[_linear(_peer(rel))], l.at[k], _peer(rel)) for k, rel in enumerate(ALL_RELS)], name)


def _split_wait(started, n, after, name, with_src=False):
    send_sem, recv_sem, src_thru, land_thru, _ = started

    def body(src_ref, land_ref, send_sem, recv_sem, after_ref, src_dead, got_ref):
        all_n = land_ref.at[pl.ds(0, n)]
        arrivals = pltpu.make_async_remote_copy(
            src_ref=all_n, dst_ref=all_n, send_sem=send_sem, recv_sem=recv_sem, device_id=_me(), device_id_type=MESH)
        arrivals.wait_send()
        arrivals.wait_recv()

    out = pl.pallas_call(
        body,
        name=name,
        out_shape=(pltpu.HBM(src_thru.shape, src_thru.dtype), pltpu.HBM(land_thru.shape, land_thru.dtype)),
        in_specs=(HBM_SPEC, HBM_SPEC, SEM_SPEC, SEM_SPEC, pl.BlockSpec(memory_space=pl.ANY)),
        out_specs=(HBM_SPEC, HBM_SPEC),
        input_output_aliases={0: 0, 1: 1},
        compiler_params=pltpu.CompilerParams(has_side_effects=DATAFLOW),
    )(src_thru, land_thru, send_sem, recv_sem, after)
    return (out[1], out[0]) if with_src else out[1]


def _sum_pieces(pieces, recv, name):
    _, rr, cc = pieces.shape
    n = recv.shape[0]
    tr = _div_tile(rr, max(8, (1 << 17) // cc // 8 * 8), 8)

    def body(own_ref, recv_ref, o_ref):
        acc = own_ref[...].astype(F32)
        for k in range(n):
            acc = acc + recv_ref[k].astype(F32)
        o_ref[...] = acc

    return pl.pallas_call(
        body,
        grid=(rr // tr,),
        in_specs=[pl.BlockSpec((None, tr, cc), lambda i: (_linear(_me()), i, 0)),
                  pl.BlockSpec((n, tr, cc), lambda i: (0, i, 0))],
        out_specs=pl.BlockSpec((tr, cc), lambda i: (i, 0)),
        out_shape=jax.ShapeDtypeStruct((rr, cc), F32),
        compiler_params=_params("parallel"),
        name=name,
    )(pieces, recv)


def _reduce_to_owner(g8, payload_dtype, name):
    recv = _scatter_send(g8.astype(payload_dtype), ALL_RELS, _linear, name + "_scatter")
    return _sum_pieces(g8, recv, name + "_sum")


def _adamw_layer(w, g, m, v, outs, layer, name):
    nl, rr, cc = w.shape
    _, gr, gc = g.shape
    tr = _div_tile(gr, max(8, (1 << 17) // gc // 8 * 8), 8)
    steps = gr // tr
    c1 = 1.0 - ADAM_B1 ** ADAM_STEP
    c2 = 1.0 - ADAM_B2 ** ADAM_STEP
    if gc == cc:
        assert 2 * gr == rr, (name, g.shape, w.shape)
        slab = pl.BlockSpec((None, tr, gc), lambda h, i: (layer, h * steps + i, 0))
    else:
        assert gr == rr and 2 * gc == cc, (name, g.shape, w.shape)
        slab = pl.BlockSpec((None, tr, gc), lambda h, i: (layer, i, h))

    def body(w_ref, g_ref, m_ref, v_ref, *rest):
        go_ref, d_ref, mo_ref, vo_ref = rest[-4:]
        g_ = g_ref[...]
        m_ = ADAM_B1 * m_ref[...] + (1.0 - ADAM_B1) * g_
        v_ = ADAM_B2 * v_ref[...] + (1.0 - ADAM_B2) * (g_ * g_)
        go_ref[...] = g_
        d_ref[...] = -ADAM_LR * ((m_ / c1) / (jnp.sqrt(v_ / c2) + ADAM_EPS) + ADAM_WD * w_ref[...])
        mo_ref[...] = m_
        vo_ref[...] = v_

    out = jax.ShapeDtypeStruct((nl, rr, cc), F32)
    in_specs = [slab, pl.BlockSpec((None, tr, gc), lambda h, i: (h, i, 0)), slab, slab]
    args = [w, g, m, v]
    aliases = {}
    if outs is not None:
        in_specs += [pl.BlockSpec(memory_space=pl.ANY)] * 4
        args += list(outs)
        aliases = {4 + k: k for k in range(4)}
    return pl.pallas_call(
        body,
        grid=(2, steps),
        in_specs=in_specs,
        out_specs=[slab] * 4,
        out_shape=[out] * 4,
        input_output_aliases=aliases,
        compiler_params=_params("parallel", "parallel"),
        name=name,
    )(*args)


def _adamw(w, g, m, v, name):
    rr, cc = w.shape
    tr = _div_tile(rr, max(8, (1 << 17) // cc // 8 * 8), 8)
    c1 = 1.0 - ADAM_B1 ** ADAM_STEP
    c2 = 1.0 - ADAM_B2 ** ADAM_STEP

    def body(w_ref, g_ref, m_ref, v_ref, d_ref, mo_ref, vo_ref):
        g_ = g_ref[...]
        m_ = ADAM_B1 * m_ref[...] + (1.0 - ADAM_B1) * g_
        v_ = ADAM_B2 * v_ref[...] + (1.0 - ADAM_B2) * (g_ * g_)
        d_ref[...] = -ADAM_LR * ((m_ / c1) / (jnp.sqrt(v_ / c2) + ADAM_EPS) + ADAM_WD * w_ref[...])
        mo_ref[...] = m_
        vo_ref[...] = v_

    blk = pl.BlockSpec((tr, cc), lambda i: (i, 0))
    out = jax.ShapeDtypeStruct((rr, cc), F32)
    return pl.pallas_call(
        body,
        grid=(rr // tr,),
        in_specs=[blk] * 4,
        out_specs=[blk] * 3,
        out_shape=[out] * 3,
        compiler_params=_params("parallel"),
        name=name,
    )(w, g, m, v)


def _pack(arrays, cols, row_mult):
    flat = jnp.concatenate([a.reshape(-1) for a in arrays])
    rows = -(-flat.shape[0] // cols)
    rows = -(-rows // row_mult) * row_mult
    return jnp.pad(flat, (0, rows * cols - flat.shape[0])).reshape(rows, cols)


def _unpack(buf, shapes):
    flat = buf.reshape(-1)
    out, off = [], 0
    for s in shapes:
        n = math.prod(s)
        out.append(flat[off:off + n].reshape(s))
        off += n
    return out


def kernel(x, meta_tokens, norm_mix, norm_ffn, norm_final, mla_w_in, mla_q_norm, mla_kv_norm, mla_w_uq, mla_w_ukv, mla_w_o, lru_w_in, lru_conv_w, lru_conv_b, lru_w_gate_a, lru_b_gate_a, lru_w_gate_x, lru_b_gate_x, lru_lambda, lru_w_o, ffn_w_gu, ffn_w_down, loss_target, m_meta_tokens, m_norm_mix, m_norm_ffn, m_norm_final, m_mla_w_in, m_mla_q_norm, m_mla_kv_norm, m_mla_w_uq, m_mla_w_ukv, m_mla_w_o, m_lru_w_in, m_lru_conv_w, m_lru_conv_b, m_lru_w_gate_a, m_lru_b_gate_a, m_lru_w_gate_x, m_lru_b_gate_x, m_lru_lambda, m_lru_w_o, m_ffn_w_gu, m_ffn_w_down, v_meta_tokens, v_norm_mix, v_norm_ffn, v_norm_final, v_mla_w_in, v_mla_q_norm, v_mla_kv_norm, v_mla_w_uq, v_mla_w_ukv, v_mla_w_o, v_lru_w_in, v_lru_conv_w, v_lru_conv_b, v_lru_w_gate_a, v_lru_b_gate_a, v_lru_w_gate_x, v_lru_b_gate_x, v_lru_lambda, v_lru_w_o, v_ffn_w_gu, v_ffn_w_down):
    d = D_MODEL
    t_real = N_META + SEQ
    tp = _t_pad()
    n_mla = mla_w_in.shape[0]
    n_lru = lru_w_in.shape[0]
    h_dim = MLA_HEADS * V_HEAD
    w_in_cols = Q_LORA + KV_LORA + QK_ROPE
    w_in_pad = Q_LORA + KV_LORA + LANES
    q_cols = MLA_HEADS * (QK_NOPE + QK_ROPE)
    tmm = _div_tile(tp, MM_ROW_TILE, 16)
    tkt = _div_tile(tp, 1408, 16)

    def tile(n, pref):
        return _div_tile(n, pref, LANES)

    small_shapes = [meta_tokens.shape, lru_conv_w.shape, lru_conv_b.shape, lru_lambda.shape]
    csh = meta_tokens.shape[1]
    small4 = _gather_chips(_pack([meta_tokens, lru_conv_w, lru_conv_b, lru_lambda], csh, 16), "gather_small")
    small4, mla_w_in = lax.optimization_barrier((small4, mla_w_in))
    started = {}

    def start(key, shard):
        prev = list(started.values())[-1][4][0, 0] if started else 0.0
        started[key] = _gather_chips_start((shard + prev).astype(CDT), "gather_" + key + "_start")

    def arrived(key, after):
        return _split_wait(started[key], len(CHIP_RELS), after, "gather_" + key + "_wait")

    def start_ffn(layer):
        start(f"w_gu{layer}", ffn_w_gu[layer:layer + 1])
        start(f"w_down{layer}", ffn_w_down[layer:layer + 1])

    start("w_in", jnp.pad(mla_w_in, ((0, 0), (0, 0), (0, w_in_pad - w_in_cols))))
    start("w_uq", mla_w_uq)
    start("w_ukv", mla_w_ukv)
    start("w_o", mla_w_o)
    start_ffn(0)
    start("lw_in", lru_w_in)
    start("lw_o", lru_w_o)
    for layer in range(1, DEPTH):
        start_ffn(layer)
    all_started = list(started.values())[-1][4][0, 0]
    n_gu = ffn_w_gu.shape[2]
    w_gu4, w_down4 = [None] * DEPTH, [None] * DEPTH
    small_full = [jnp.concatenate(parts, axis=-1) for parts in zip(*[_unpack(small4[k], small_shapes) for k in range(N_CHIPS)])]
    meta_full, conv_w_full, conv_b_full, lam_full = small_full

    cos2, sin2 = _rope_tables(tp)

    h = jnp.concatenate([meta_full, x[0], jnp.zeros((tp - t_real, d), F32)], axis=0) + all_started
    saved = []
    for layer in range(DEPTH):
        j = layer // 2
        s = {"h_in": h}
        hn = _rms_fwd(h, norm_mix[layer], width=d, col_block=0, name="norm_mix_fwd")
        s["hn"] = hn
        if layer == 0:
            w_in4, w_uq4, w_ukv4, w_o4 = (arrived(k, hn) for k in ("w_in", "w_uq", "w_ukv", "w_o"))
            w_uq_full = jnp.moveaxis(w_uq4, 0, 2).reshape(n_mla, Q_LORA, MLA_HEADS, QK_NOPE + QK_ROPE)
            w_uq_perm = jnp.concatenate([w_uq_full[..., :QK_NOPE].reshape(n_mla, Q_LORA, -1),
                                         w_uq_full[..., QK_NOPE:].reshape(n_mla, Q_LORA, -1)], axis=-1)
        if layer == 1:
            lw_in4, lw_o4 = arrived("lw_in", hn), arrived("lw_o", hn)
        if layer % 2 == 0:
            proj = _mm("nn", hn, w_in4, kind="row", layer=j, tm=tmm, tn=tile(w_in_pad, 1152), tk=tile(d // N_CHIPS, 512), name="mla_in")
            c_q = _rms_fwd(proj, mla_q_norm[j], width=Q_LORA, col_block=0, name="q_norm_fwd")
            c_kv = _rms_fwd(proj, mla_kv_norm[j], width=KV_LORA, col_block=Q_LORA // KV_LORA, name="kv_norm_fwd")
            q = _mm("nn", c_q, w_uq_perm[j], tm=tmm, tn=tile(q_cols, 1024), tk=Q_LORA, name="mla_uq")
            kv = _mm("nn", c_kv, w_ukv4, kind="col", layer=j, out_dtype=CDT, tm=tmm, tn=tile(w_ukv4.shape[3], 1024), tk=KV_LORA, name="mla_ukv")
            qp, kp = _mla_prep(q, kv, proj, cos2, sin2)
            att, lse = _attn_fwd(qp, kp, kv)
            h = _mm("nn", att, w_o4, kind="row", layer=j, resid=h, tm=tmm, tn=tile(d, 1024), tk=tile(h_dim // N_CHIPS, 512), name="mla_out")
            s.update(proj=proj, c_q=c_q, c_kv=c_kv, qp=qp, kp=kp, kv=kv, att=att, lse=lse)
        else:
            xy = _mm("nn", hn, lw_in4, kind="col", layer=j, tm=tmm, tn=tile(lw_in4.shape[3], 1024), tk=d, name="lru_in")
            xc, r, ig, a, b = _lru_gates_fwd(xy, conv_w_full[j], conv_b_full[j], lru_w_gate_a[j], lru_b_gate_a[j],
                                             lru_w_gate_x[j], lru_b_gate_x[j], lam_full[j])
            hs, mixed = _lru_scan_fwd(a, b, xy)
            h = _mm("nn", mixed, lw_o4, kind="row", layer=j, resid=h, tm=tmm, tn=tile(d, 1024), tk=tile(d // N_CHIPS, 512), name="lru_out")
            s.update(xy=xy, xc=xc, r=r, ig=ig, a=a, hs=hs, mixed=mixed)
        s["h_mid"] = h
        hn2 = _rms_fwd(h, norm_ffn[layer], width=d, col_block=0, name="norm_ffn_fwd")
        w_gu4[layer], w_down4[layer] = arrived(f"w_gu{layer}", hn2), arrived(f"w_down{layer}", hn2)
        gate, up, act = _ffn_gate_up(hn2, w_gu4[layer], tm=_div_tile(tp, MM_ROW_TILE // 2, 16), tn=tile(n_gu, 1408))
        h = _mm("nn", act, w_down4[layer], kind="row", resid=h, tm=tmm, tn=tile(d, 1024), tk=tile(D_FF // N_CHIPS, 1408), name="ffn_down")
        s.update(hn2=hn2, gate=gate, up=up, act=act)
        saved.append(s)

    target = jnp.concatenate([jnp.zeros((N_META, d), F32), loss_target[0], jnp.zeros((tp - t_real, d), F32)], axis=0)
    dh, dhb, g_norm_final, loss_part = _final_loss(h, norm_final, target)
    loss = lax.psum(loss_part[0, 0], ("x", "y", "c"))

    g_norm_mix, g_norm_ffn = [None] * DEPTH, [None] * DEPTH
    g_q_norm, g_kv_norm = [None] * n_mla, [None] * n_mla
    g_w_uq = [None] * n_mla
    g_gate = {k: [None] * n_lru for k in ("wa", "ba", "wx", "bx", "lam", "cw", "cb")}
    weights = {"w_in": (mla_w_in, m_mla_w_in, v_mla_w_in), "w_uq": (mla_w_uq, m_mla_w_uq, v_mla_w_uq),
               "w_ukv": (mla_w_ukv, m_mla_w_ukv, v_mla_w_ukv), "w_o": (mla_w_o, m_mla_w_o, v_mla_w_o),
               "lw_in": (lru_w_in, m_lru_w_in, v_lru_w_in), "lw_o": (lru_w_o, m_lru_w_o, v_lru_w_o),
               "w_gu": (ffn_w_gu, m_ffn_w_gu, v_ffn_w_gu), "w_down": (ffn_w_down, m_ffn_w_down, v_ffn_w_down)}
    res = {key: None for key in weights}
    units = []

    def reduce_start(key, lyr, pieces):
        tag = f"{key}{lyr}"
        units.append({"key": key, "layer": lyr, "tag": tag, "pieces": pieces, "stage": 0, "age": 0,
                      "copy": _split_scatter_start(pieces, "reduce_" + tag + "_scatter_start")})
        return units[-1]["copy"][4]

    def reduce_advance(after, everything=False):
        tokens = []
        for u in units:
            key, lyr, tag = u["key"], u["layer"], u["tag"]
            if u["stage"] == 1:
                both = _split_wait(u["copy"], 1, after, "reduce_" + tag + "_pair_wait")
                g = both[:, :, :w_in_cols] if key == "w_in" else both
                w, m, v = weights[key]
                res[key] = _adamw_layer(w, g, m, v, res[key], lyr, "adamw_" + key)
                u["stage"] = 2
            elif u["stage"] == 0 and (u["age"] > 0 or everything):
                recv, pieces = _split_wait(u["copy"], len(ALL_RELS), after, "reduce_" + tag + "_scatter_wait", with_src=True)
                red = _sum_pieces(pieces, recv, "reduce_" + key + "_sum")
                u["copy"] = _split_gather_start(red, PAIR_RELS, 2, lambda dev: dev[2], "reduce_" + tag + "_pair_start")
                tokens.append(u["copy"][4])
                u["stage"] = 1
            u["age"] += 1
        return tokens

    def grad_w(key, a_op, b_op, kind, lyr, tm, tn, tk=tkt):
        return reduce_start(key, lyr, _mm_tn(a_op, b_op, kind=kind, tm=tm, tn=tn, tk=tk, name="grad_" + key))

    nope_w = MLA_HEADS * QK_NOPE
    for layer in reversed(range(DEPTH)):
        tokens = []
        j = layer // 2
        s = saved[layer]
        t_down = grad_w("w_down", s["act"], dhb, "row_colhalves", layer, tile(D_FF // N_CHIPS, 1408), tile(d // 2, 1024),
                        tk=_div_tile(tp, 2112, 16))
        d_act = _mm("nt", dhb, w_down4[layer], kind="row", out_dtype=CDT, tm=tmm, tn=tile(D_FF // N_CHIPS, 1408), tk=d, name="ffn_down_bwd")
        dgu = _swiglu_bwd(s["gate"], s["up"], d_act)
        t_gu = grad_w("w_gu", s["hn2"], dgu, "col", layer, tile(d // 2, 1024), tile(n_gu, 1408), tk=_div_tile(tp, 2112, 16))
        dhn2 = _mm("nt", dgu, w_gu4[layer], kind="col", tm=tmm, tn=tile(d, 1024), tk=tile(n_gu, 2816), name="ffn_gu_bwd")
        dh, dhb, g_norm_ffn[layer] = _rms_bwd(s["h_mid"], norm_ffn[layer] + (t_down[0, 0] + t_gu[0, 0]), dhn2, dh,
                                              width=d, col_block=0, name="norm_ffn_bwd")
        if layer % 2 == 0:
            t_mix = grad_w("w_o", s["att"], dhb, "row", j, tile(h_dim // N_DEV, 256), tile(d, 1024))[0, 0]
            d_att = _mm("nt", dhb, w_o4, kind="row", layer=j, tm=tmm, tn=tile(h_dim // N_CHIPS, 512), tk=d, name="mla_out_bwd")
            delta = _attn_delta(d_att, s["att"])
            dqn, dqr_h, dkv, dkr_h = _attn_bwd(s["qp"], s["kp"], s["kv"], d_att, s["lse"], delta)
            dqr, dkr = _mla_unprep(dqr_h, dkr_h, cos2, sin2)
            dq = jnp.concatenate([dqn, dqr], axis=-1)
            g_uq = _mm_tn(s["c_q"], dq, tm=Q_LORA, tn=tile(q_cols, 1024), tk=tkt, name="grad_w_uq")
            g_uq = jnp.concatenate([g_uq[:, :nope_w].reshape(Q_LORA, MLA_HEADS, QK_NOPE),
                                    g_uq[:, nope_w:].reshape(Q_LORA, MLA_HEADS, QK_ROPE)], axis=-1)
            g_uq = g_uq.reshape(2, Q_LORA // 2, N_CHIPS, q_cols // N_CHIPS).transpose(2, 0, 1, 3)
            t_mix = t_mix + reduce_start("w_uq", j, g_uq.reshape(N_DEV, Q_LORA // 2, q_cols // N_CHIPS).astype(CDT))[0, 0]
            dc_q = _mm("nt", dq, w_uq_perm[j], tm=tmm, tn=Q_LORA, tk=tile(q_cols, 1024), name="mla_uq_bwd")
            t_mix = t_mix + grad_w("w_ukv", s["c_kv"], dkv, "col", j, tile(KV_LORA // 2, 256), tile(w_ukv4.shape[3], 1024))[0, 0]
            dc_kv = _mm("nt", dkv, w_ukv4, kind="col", layer=j, tm=tmm, tn=KV_LORA, tk=tile(w_ukv4.shape[3], 1024), name="mla_ukv_bwd")
            dpq, _, g_q_norm[j] = _rms_bwd(s["proj"], mla_q_norm[j] + t_mix, dc_q, None, width=Q_LORA, col_block=0, name="q_norm_bwd")
            dpkv, _, g_kv_norm[j] = _rms_bwd(s["proj"], mla_kv_norm[j], dc_kv, None, width=KV_LORA, col_block=Q_LORA // KV_LORA, name="kv_norm_bwd")
            dproj = jnp.concatenate([dpq, dpkv, dkr], axis=-1).astype(CDT)
            tokens.append(grad_w("w_in", s["hn"], dproj, "row", j, tile(d // N_DEV, 256), tile(w_in_pad, 1152)))
            dhn = _mm("nt", dproj, w_in4, kind="row", layer=j, tm=tmm, tn=tile(d // N_CHIPS, 512), tk=tile(w_in_pad, 1152), name="mla_in_bwd")
        else:
            t_mix = grad_w("lw_o", s["mixed"], dhb, "row", j, tile(d // N_DEV, 256), tile(d, 1024))[0, 0]
            dm = _mm("nt", dhb, lw_o4, kind="row", layer=j, tm=tmm, tn=tile(d // N_CHIPS, 512), tk=d, name="lru_out_bwd")
            db, da, dy = _lru_scan_bwd(s["a"], s["hs"], dm, s["xy"])
            dxc, g_gate["wa"][j], g_gate["ba"][j], g_gate["wx"][j], g_gate["bx"][j], g_gate["lam"][j] = _lru_gates_bwd(
                db, da, s["xc"], s["r"], s["ig"], s["a"], lam_full[j] + t_mix, lru_w_gate_a[j], lru_w_gate_x[j])
            dxb, g_gate["cw"][j], g_gate["cb"][j] = _lru_conv_bwd(dxc, s["xy"], conv_w_full[j])
            dxy = jnp.concatenate([dxb, dy], axis=-1)
            tokens.append(grad_w("lw_in", s["hn"], dxy, "col", j, tile(d // 2, 1024), tile(lw_in4.shape[3], 1024)))
            dhn = _mm("nt", dxy, lw_in4, kind="col", layer=j, tm=tmm, tn=tile(d, 1024), tk=tile(lw_in4.shape[3], 1024), name="lru_in_bwd")
        dh, dhb, g_norm_mix[layer] = _rms_bwd(s["h_in"], norm_mix[layer] + t_mix, dhn, dh, width=d, col_block=0, name="norm_mix_bwd")
        tokens += reduce_advance(dh)
        if layer > 0:
            dhb = dhb + sum(tok[0, 0] for tok in tokens).astype(CDT)
        else:
            dh = dh + sum(tok[0, 0] for tok in tokens)

    grad_x = dh[N_META:t_real][None]
    g_meta_full = dh[:N_META]

    for _ in range(3):
        reduce_advance(dh, everything=True)

    g_small_full = [g_meta_full, jnp.stack(g_gate["cw"]).reshape(n_lru, CONV_W, d), jnp.stack(g_gate["cb"]).reshape(n_lru, d),
                    jnp.stack(g_gate["lam"]).reshape(n_lru, d)]
    g_small4 = jnp.stack([_pack([a[..., k * csh:(k + 1) * csh] for a in g_small_full], csh, 16) for k in range(N_CHIPS)])
    rows_s = g_small4.shape[1]
    red = _reduce_to_owner(g_small4.reshape(N_DEV, rows_s // 2, csh), F32, "reduce_small")
    g_small = _gather_send(red, PAIR_RELS, 2, lambda dev: dev[2], "reduce_small_pair").reshape(rows_s, csh)
    small_w = [meta_tokens, lru_conv_w, lru_conv_b, lru_lambda]
    small_m = [m_meta_tokens, m_lru_conv_w, m_lru_conv_b, m_lru_lambda]
    small_v = [v_meta_tokens, v_lru_conv_w, v_lru_conv_b, v_lru_lambda]
    sd, sm, sv = _adamw(_pack(small_w, csh, 16), g_small, _pack(small_m, csh, 16), _pack(small_v, csh, 16), "adamw_small")
    small_out = [_unpack(buf, small_shapes) for buf in (g_small, sd, sm, sv)]

    rep_w = [norm_mix, norm_ffn, norm_final, mla_q_norm, mla_kv_norm, lru_w_gate_a, lru_b_gate_a, lru_w_gate_x, lru_b_gate_x]
    rep_m = [m_norm_mix, m_norm_ffn, m_norm_final, m_mla_q_norm, m_mla_kv_norm, m_lru_w_gate_a, m_lru_b_gate_a, m_lru_w_gate_x, m_lru_b_gate_x]
    rep_v = [v_norm_mix, v_norm_ffn, v_norm_final, v_mla_q_norm, v_mla_kv_norm, v_lru_w_gate_a, v_lru_b_gate_a, v_lru_w_gate_x, v_lru_b_gate_x]
    rep_g = [jnp.stack(g_norm_mix), jnp.stack(g_norm_ffn), g_norm_final, jnp.stack(g_q_norm), jnp.stack(g_kv_norm),
             jnp.stack(g_gate["wa"]), jnp.stack(g_gate["ba"]), jnp.stack(g_gate["wx"]), jnp.stack(g_gate["bx"])]
    rep_shapes = [w.shape for w in rep_w]
    g_rep = _pack(rep_g, LANES, 8 * N_DEV)
    rows_r = g_rep.shape[0]
    red = _reduce_to_owner(g_rep.reshape(N_DEV, rows_r // N_DEV, LANES), F32, "reduce_rep")
    g_rep = _gather_send(red, ALL_RELS, N_DEV, _linear, "reduce_rep_all").reshape(rows_r, LANES)
    rd, rm, rv = _adamw(_pack(rep_w, LANES, 8 * N_DEV), g_rep, _pack(rep_m, LANES, 8 * N_DEV), _pack(rep_v, LANES, 8 * N_DEV), "adamw_rep")
    rep_out = [_unpack(buf, rep_shapes) for buf in (g_rep, rd, rm, rv)]

    def leaf(kind):
        s_, r_ = small_out[kind], rep_out[kind]
        return [s_[0], r_[0], r_[1], r_[2], res["w_in"][kind], r_[3], r_[4], res["w_uq"][kind], res["w_ukv"][kind],
                res["w_o"][kind], res["lw_in"][kind], s_[1], s_[2], r_[5], r_[6], r_[7], r_[8], s_[3],
                res["lw_o"][kind], res["w_gu"][kind], res["w_down"][kind]]

    return (loss, grad_x, *leaf(0), *leaf(1), *leaf(2), *leaf(3))
```

```python
import math

import jax
import jax.numpy as jnp
from jax import lax
from jax.experimental import pallas as pl
from jax.experimental.pallas import tpu as pltpu

F32 = jnp.float32
CDT = jnp.bfloat16
MESH = pl.DeviceIdType.MESH

D_MODEL = 2048
SEQ = 4096
DEPTH = 4
CHUNK = 64
N_META = 16
MLA_HEADS = 16
Q_LORA = 512
KV_LORA = 512
QK_NOPE = 128
QK_ROPE = 64
V_HEAD = 128
ROPE_THETA = 10000.0
RNN_BLOCKS = 16
CONV_W = 4
LRU_C = 8.0
ATT_SCALE = (QK_NOPE + QK_ROPE) ** -0.5
D_FF = 5632
RMS_EPS = 1e-6
NEG_BIG = -1e30
ADAM_LR = 0.001
ADAM_B1 = 0.9
ADAM_B2 = 0.999
ADAM_EPS = 1e-08
ADAM_WD = 0.01
ADAM_STEP = 10

N_CHIPS = 4
N_DEV = 8
LANES = 128
VMEM_LIMIT = 52 * 1024 * 1024
ROW_TILE = 384
MM_ROW_TILE = 704
ATT_TILE = 384
SCAN_COLS = 128
ATT_STRIP = 32
LRU_ROW_TILE = 1408


def _div_tile(n, pref, mult):
    if n <= pref:
        return n
    d = (pref // mult) * mult
    while d >= mult:
        if n % d == 0:
            return d
        d -= mult
    raise ValueError(f"no tile for {n} <= {pref} (multiple of {mult})")


def _t_pad():
    t = N_META + SEQ
    step = math.lcm(_row_tile_unit(), 8)
    return -(-t // step) * step


def _row_tile_unit():
    return math.lcm(math.lcm(ROW_TILE, MM_ROW_TILE), ATT_TILE)


def _params(*sem):
    return pltpu.CompilerParams(dimension_semantics=sem, vmem_limit_bytes=VMEM_LIMIT)


def _b_spec(form, b, kind, layer, t_out, t_con):
    if kind == "plain":
        if form == "nn":
            return pl.BlockSpec((t_con, t_out), lambda i, j, k: (k, j))
        return pl.BlockSpec((t_out, t_con), lambda i, j, k: (j, k))
    rows, cols = b.shape[2], b.shape[3]
    if form == "nn":
        blk = (None, None, t_con, t_out)
        if kind == "row":
            per = rows // t_con
            return pl.BlockSpec(blk, lambda i, j, k: (k // per, layer, k % per, j))
        per = cols // t_out
        return pl.BlockSpec(blk, lambda i, j, k: (j // per, layer, k, j % per))
    blk = (None, None, t_out, t_con)
    if kind == "row":
        per = rows // t_out
        return pl.BlockSpec(blk, lambda i, j, k: (j // per, layer, j % per, k))
    per = cols // t_con
    return pl.BlockSpec(blk, lambda i, j, k: (k // per, layer, j, k % per))


def _mm_body(nk, dims, has_resid):
    def body(*refs):
        if has_resid:
            a_ref, b_ref, r_ref, o_ref = refs[:4]
        else:
            a_ref, b_ref, o_ref = refs[:3]
        prod = lax.dot_general(a_ref[...].astype(CDT), b_ref[...].astype(CDT), (dims, ((), ())),
                               preferred_element_type=F32)

        def finish(acc):
            if has_resid:
                acc = acc + r_ref[...]
            o_ref[...] = acc.astype(o_ref.dtype)

        if nk == 1:
            finish(prod)
            return
        acc_ref = refs[-1]
        k = pl.program_id(2)

        @pl.when(k == 0)
        def _():
            acc_ref[...] = prod

        @pl.when(k > 0)
        def _():
            acc_ref[...] += prod

        @pl.when(k == nk - 1)
        def _():
            finish(acc_ref[...])

    return body


def _mm(form, a, b, *, kind="plain", layer=0, out_dtype=F32, resid=None, tm, tn, tk, name):
    m, con = a.shape
    if kind == "plain":
        w_rows, w_cols = b.shape
    elif kind == "row":
        w_rows, w_cols = b.shape[0] * b.shape[2], b.shape[3]
    else:
        w_rows, w_cols = b.shape[2], b.shape[0] * b.shape[3]
    n_out = w_cols if form == "nn" else w_rows
    assert con == (w_rows if form == "nn" else w_cols), (name, a.shape, b.shape)
    nk = con // tk
    assert m % tm == 0 and n_out % tn == 0 and con % tk == 0, (name, m, n_out, con, tm, tn, tk)
    dims = ((1,), (0,)) if form == "nn" else ((1,), (1,))
    in_specs = [pl.BlockSpec((tm, tk), lambda i, j, k: (i, k)), _b_spec(form, b, kind, layer, tn, tk)]
    args = [a, b]
    if resid is not None:
        in_specs.append(pl.BlockSpec((tm, tn), lambda i, j, k: (i, j)))
        args.append(resid)
    return pl.pallas_call(
        _mm_body(nk, dims, resid is not None),
        grid=(m // tm, n_out // tn, nk),
        in_specs=in_specs,
        out_specs=pl.BlockSpec((tm, tn), lambda i, j, k: (i, j)),
        out_shape=jax.ShapeDtypeStruct((m, n_out), out_dtype),
        scratch_shapes=[pltpu.VMEM((tm, tn), F32)] if nk > 1 else [],
        compiler_params=_params("parallel", "parallel", "arbitrary"),
        name=name,
    )(*args)


def _mm_tn(a, b, *, kind="plain", tm, tn, tk, name):
    t, m = a.shape
    n = b.shape[1]
    nk = t // tk
    assert t % tk == 0 and m % tm == 0 and n % tn == 0, (name, t, m, n, tm, tn, tk)
    if kind == "plain":
        out_shape = jax.ShapeDtypeStruct((m, n), F32)
        out_spec = pl.BlockSpec((tm, tn), lambda i, j, k: (i, j))
    elif kind == "row":
        per = (m // N_CHIPS) // tm
        assert per >= 2 and per % 2 == 0, (name, per)
        out_shape = jax.ShapeDtypeStruct((N_DEV, m // N_DEV, n), CDT)
        out_spec = pl.BlockSpec((None, tm, tn), lambda i, j, k: (2 * (i // per) + (i % per) // (per // 2), (i % per) % (per // 2), j))
    elif kind == "row_colhalves":
        per = (m // N_CHIPS) // tm
        nt = n // tn
        assert per >= 1 and nt % 2 == 0, (name, per, nt)
        out_shape = jax.ShapeDtypeStruct((N_DEV, m // N_CHIPS, n // 2), CDT)
        out_spec = pl.BlockSpec((None, tm, tn), lambda i, j, k: (2 * (i // per) + j // (nt // 2), i % per, j % (nt // 2)))
    else:
        per = (n // N_CHIPS) // tn
        mt = m // tm
        assert per >= 1 and mt % 2 == 0, (name, per, mt)
        out_shape = jax.ShapeDtypeStruct((N_DEV, m // 2, n // N_CHIPS), CDT)
        out_spec = pl.BlockSpec((None, tm, tn), lambda i, j, k: (2 * (j // per) + i // (mt // 2), i % (mt // 2), j % per))
    return pl.pallas_call(
        _mm_body(nk, ((0,), (0,)), False),
        grid=(m // tm, n // tn, nk),
        in_specs=[pl.BlockSpec((tk, tm), lambda i, j, k: (k, i)), pl.BlockSpec((tk, tn), lambda i, j, k: (k, j))],
        out_specs=out_spec,
        out_shape=out_shape,
        scratch_shapes=[pltpu.VMEM((tm, tn), F32)] if nk > 1 else [],
        compiler_params=_params("parallel", "parallel", "arbitrary"),
        name=name,
    )(a, b)


def _rms_fwd(x, g, *, width, col_block, name):
    tp = x.shape[0]
    tr = _div_tile(tp, ROW_TILE, 8)

    def body(x_ref, g_ref, o_ref):
        xf = x_ref[...]
        r = lax.rsqrt(jnp.mean(xf * xf, axis=-1, keepdims=True) + RMS_EPS)
        o_ref[...] = ((xf * r) * g_ref[...]).astype(o_ref.dtype)

    return pl.pallas_call(
        body,
        grid=(tp // tr,),
        in_specs=[pl.BlockSpec((tr, width), lambda i: (i, col_block)), pl.BlockSpec((1, width), lambda i: (0, 0))],
        out_specs=pl.BlockSpec((tr, width), lambda i: (i, 0)),
        out_shape=jax.ShapeDtypeStruct((tp, width), CDT),
        compiler_params=_params("parallel"),
        name=name,
    )(x, g.reshape(1, width))


def _rms_bwd(x, g, dy, resid, *, width, col_block, name):
    tp = x.shape[0]
    tr = _div_tile(tp, ROW_TILE, 8)
    has_resid = resid is not None

    def body(*refs):
        if has_resid:
            x_ref, g_ref, dy_ref, res_ref, dx_ref, dxb_ref, dg_ref = refs
        else:
            x_ref, g_ref, dy_ref, dx_ref, dxb_ref, dg_ref = refs
        i = pl.program_id(0)
        xf = x_ref[...]
        r = lax.rsqrt(jnp.mean(xf * xf, axis=-1, keepdims=True) + RMS_EPS)
        xh = xf * r
        dy = dy_ref[...].astype(F32)
        dg = jnp.sum(dy * xh, axis=0, keepdims=True)
        dxh = dy * g_ref[...]
        dx = r * (dxh - xh * jnp.mean(dxh * xh, axis=-1, keepdims=True))
        if has_resid:
            dx = dx + res_ref[...]
        dx_ref[...] = dx
        dxb_ref[...] = dx.astype(CDT)

        @pl.when(i == 0)
        def _():
            dg_ref[...] = dg

        @pl.when(i > 0)
        def _():
            dg_ref[...] += dg

    row = pl.BlockSpec((tr, width), lambda i: (i, 0))
    in_specs = [pl.BlockSpec((tr, width), lambda i: (i, col_block)), pl.BlockSpec((1, width), lambda i: (0, 0)), row]
    args = [x, g.reshape(1, width), dy]
    if has_resid:
        in_specs.append(row)
        args.append(resid)
    return pl.pallas_call(
        body,
        grid=(tp // tr,),
        in_specs=in_specs,
        out_specs=[row, row, pl.BlockSpec((1, width), lambda i: (0, 0))],
        out_shape=[jax.ShapeDtypeStruct((tp, width), F32), jax.ShapeDtypeStruct((tp, width), CDT),
                   jax.ShapeDtypeStruct((1, width), F32)],
        compiler_params=_params("arbitrary"),
        name=name,
    )(*args)


def _final_loss(h, g, target):
    tp, d = h.shape
    tr = _div_tile(tp, ROW_TILE, 8)

    def body(h_ref, g_ref, t_ref, dh_ref, dhb_ref, dg_ref, loss_ref):
        i = pl.program_id(0)
        xf = h_ref[...]
        r = lax.rsqrt(jnp.mean(xf * xf, axis=-1, keepdims=True) + RMS_EPS)
        xh = xf * r
        gain = g_ref[...]
        y = xh * gain
        rows = i * tr + lax.broadcasted_iota(jnp.int32, (tr, 1), 0)
        valid = jnp.logical_and(rows >= N_META, rows < N_META + SEQ)
        err = jnp.where(valid, y - t_ref[...], 0.0)
        part = 0.5 * jnp.sum(jnp.mean(err * err, axis=-1, keepdims=True), axis=0, keepdims=True)
        dy = err * (1.0 / d)
        dg = jnp.sum(dy * xh, axis=0, keepdims=True)
        dxh = dy * gain
        dx = r * (dxh - xh * jnp.mean(dxh * xh, axis=-1, keepdims=True))
        dh_ref[...] = dx
        dhb_ref[...] = dx.astype(CDT)

        @pl.when(i == 0)
        def _():
            dg_ref[...] = dg
            loss_ref[...] = part

        @pl.when(i > 0)
        def _():
            dg_ref[...] += dg
            loss_ref[...] += part

    row = pl.BlockSpec((tr, d), lambda i: (i, 0))
    vec = pl.BlockSpec((1, d), lambda i: (0, 0))
    return pl.pallas_call(
        body,
        grid=(tp // tr,),
        in_specs=[row, vec, row],
        out_specs=[row, row, vec, pl.BlockSpec((1, 1), lambda i: (0, 0))],
        out_shape=[jax.ShapeDtypeStruct((tp, d), F32), jax.ShapeDtypeStruct((tp, d), CDT),
                   jax.ShapeDtypeStruct((1, d), F32), jax.ShapeDtypeStruct((1, 1), F32)],
        compiler_params=_params("arbitrary"),
        name="final_loss",
    )(h, g.reshape(1, d), target)


def _sigmoid(x):
    return 1.0 / (1.0 + jnp.exp(-x))


def _ffn_gate_up(x, w4, *, tm, tn):
    t, d = x.shape
    n_chip = w4.shape[3]
    f = N_CHIPS * n_chip // 2
    per = n_chip // tn
    assert t % tm == 0 and n_chip % tn == 0, (t, tm, n_chip, tn)

    def body(x_ref, wg_ref, wu_ref, g_ref, u_ref, a_ref):
        xb = x_ref[...]
        g = _dot(xb, wg_ref[...])
        u = _dot(xb, wu_ref[...])
        g_ref[...] = g.astype(CDT)
        u_ref[...] = u.astype(CDT)
        a_ref[...] = ((g * _sigmoid(g)) * u).astype(CDT)

    w_blk = (None, None, d, tn)
    out_blk = pl.BlockSpec((tm, tn), lambda j, i: (i, j))
    out = jax.ShapeDtypeStruct((t, f), CDT)
    return pl.pallas_call(
        body,
        grid=(f // tn, t // tm),
        in_specs=[pl.BlockSpec((tm, d), lambda j, i: (i, 0)),
                  pl.BlockSpec(w_blk, lambda j, i: (j // per, 0, 0, j % per)),
                  pl.BlockSpec(w_blk, lambda j, i: (N_CHIPS // 2 + j // per, 0, 0, j % per))],
        out_specs=[out_blk, out_blk, out_blk],
        out_shape=[out, out, out],
        compiler_params=_params("parallel", "parallel"),
        name="ffn_gate_up",
    )(x, w4, w4)


def _swiglu_bwd(g, u, da):
    tp, f = g.shape
    f2 = 2 * f
    tr = _div_tile(tp, 128, 16)

    def body(g_ref, u_ref, da_ref, o_ref):
        g = g_ref[...].astype(F32)
        da = da_ref[...].astype(F32)
        sg = _sigmoid(g)
        o_ref[:, :f] = (da * u_ref[...].astype(F32) * (sg * (1.0 + g * (1.0 - sg)))).astype(o_ref.dtype)
        o_ref[:, f:] = (da * (g * sg)).astype(o_ref.dtype)

    return pl.pallas_call(
        body,
        grid=(tp // tr,),
        in_specs=[pl.BlockSpec((tr, f), lambda i: (i, 0))] * 3,
        out_specs=pl.BlockSpec((tr, f2), lambda i: (i, 0)),
        out_shape=jax.ShapeDtypeStruct((tp, f2), CDT),
        compiler_params=_params("parallel"),
        name="swiglu_bwd",
    )(g, u, da)


def _swap_halves(x):
    lane = lax.broadcasted_iota(jnp.int32, x.shape, x.ndim - 1)
    first = (lane % QK_ROPE) < (QK_ROPE // 2)
    return jnp.where(first, pltpu.roll(x, LANES - QK_ROPE // 2, x.ndim - 1), pltpu.roll(x, QK_ROPE // 2, x.ndim - 1))


def _rope_tables(tp):
    pos = jnp.arange(tp, dtype=F32)
    inv_freq = ROPE_THETA ** (-jnp.arange(0, QK_ROPE, 2, dtype=F32) / QK_ROPE)
    ang = pos[:, None] * inv_freq[None, :]
    cos, sin = jnp.cos(ang), jnp.sin(ang)
    reps = LANES // QK_ROPE
    return jnp.tile(jnp.concatenate([cos, cos], -1), (1, reps)), jnp.tile(jnp.concatenate([-sin, sin], -1), (1, reps))


def _chunk_of(pos):
    shift = CHUNK.bit_length() - 1
    assert CHUNK == 1 << shift
    return jnp.where(pos < N_META, 0, 1 + lax.shift_right_arithmetic(pos - N_META, shift))


def _head_half(x, h):
    lane = lax.broadcasted_iota(jnp.int32, x.shape, x.ndim - 1)
    return jnp.where((lane // QK_ROPE) == (h % 2), x, jnp.zeros_like(x))


def _mla_prep(q, kv, proj, cos2, sin2):
    tp = q.shape[0]
    tr = _div_tile(tp, ROW_TILE, 8)
    nope_w = MLA_HEADS * QK_NOPE
    kr_block = (Q_LORA + KV_LORA) // LANES
    depth = QK_NOPE + LANES
    scale = ATT_SCALE

    def body(q_ref, kv_ref, kr_ref, c_ref, s_ref, qp_out, kp_out):
        c = c_ref[...]
        s = s_ref[...]
        k = kr_ref[...]
        k = k + pltpu.roll(k, QK_ROPE, 1)
        k = (k * c + _swap_halves(k) * s).astype(CDT)
        for p in range(MLA_HEADS // 2):
            x = q_ref[:, nope_w + p * LANES:nope_w + (p + 1) * LANES]
            pair = ((x * c + _swap_halves(x) * s) * scale).astype(CDT)
            for h in (2 * p, 2 * p + 1):
                qp_out[h, :, :QK_NOPE] = (q_ref[:, h * QK_NOPE:(h + 1) * QK_NOPE] * scale).astype(CDT)
                qp_out[h, :, QK_NOPE:] = _head_half(pair, h)
                kp_out[h, :, :QK_NOPE] = kv_ref[:, 2 * h * QK_NOPE:(2 * h + 1) * QK_NOPE]
                kp_out[h, :, QK_NOPE:] = k

    tab = pl.BlockSpec((tr, LANES), lambda i: (i, 0))
    per_head = pl.BlockSpec((MLA_HEADS, tr, depth), lambda i: (0, i, 0))
    out = jax.ShapeDtypeStruct((MLA_HEADS, tp, depth), CDT)
    return pl.pallas_call(
        body,
        grid=(tp // tr,),
        in_specs=[pl.BlockSpec((tr, q.shape[1]), lambda i: (i, 0)), pl.BlockSpec((tr, kv.shape[1]), lambda i: (i, 0)),
                  pl.BlockSpec((tr, LANES), lambda i: (i, kr_block)), tab, tab],
        out_specs=[per_head, per_head],
        out_shape=[out, out],
        compiler_params=_params("parallel"),
        name="mla_prep",
    )(q, kv, proj, cos2, sin2)


def _dot_nt(a, b):
    return lax.dot_general(a, b, (((1,), (1,)), ((), ())), preferred_element_type=F32)


def _dot_tn(a, b):
    return lax.dot_general(a, b, (((0,), (0,)), ((), ())), preferred_element_type=F32)


def _dot(a, b):
    return jnp.dot(a, b, preferred_element_type=F32)


def _chunk_scalar(p):
    return jnp.where(p < N_META, 0, 1 + jnp.maximum(p - N_META, 0) // CHUNK)


def _last_key_block(i, bq, bk, nk):
    cq = _chunk_scalar(i * bq + bq - 1)
    return jnp.minimum((N_META + CHUNK * cq - 1) // bk, nk - 1)


def _full_key_blocks(i, bq, bk):
    return (N_META + CHUNK * _chunk_scalar(i * bq)) // bk


def _first_query_block(j, bk, bq):
    p0 = N_META + CHUNK * (jnp.maximum(j * bk - N_META, 0) // CHUNK)
    return p0 // bq


def _first_full_query_block(j, bk, bq, nq):
    ck = _chunk_scalar(j * bk + bk - 1)
    p0 = jnp.where(ck == 0, 0, N_META + CHUNK * (ck - 1))
    return jnp.minimum((p0 + bq - 1) // bq, nq)


def _chunk_mask(q0, k0, shape, keys_on_rows):
    if keys_on_rows:
        kc = _chunk_of(k0 + lax.broadcasted_iota(jnp.int32, (shape[0], 1), 0))
        qc = _chunk_of(q0 + lax.broadcasted_iota(jnp.int32, (1, shape[1]), 1))
    else:
        qc = _chunk_of(q0 + lax.broadcasted_iota(jnp.int32, (shape[0], 1), 0))
        kc = _chunk_of(k0 + lax.broadcasted_iota(jnp.int32, (1, shape[1]), 1))
    return kc <= qc


def _attn_fwd(qp, kp, kv):
    tp = qp.shape[1]
    depth = qp.shape[2]
    bq = bk = _div_tile(tp, ATT_TILE, LANES)
    nq, nk = tp // bq, tp // bk
    scale = (QK_NOPE + QK_ROPE) ** -0.5

    def body(q_ref, k_ref, v_ref, o_ref, lse_ref):
        def q_block(i, _):
            q0 = pl.multiple_of(i * bq, bq)
            qb = q_ref[pl.ds(q0, bq), :]

            def k_step(masked, j, carry):
                m_old, l_old, acc = carry
                k0 = pl.multiple_of(j * bk, bk)
                s = _dot_nt(qb, k_ref[pl.ds(k0, bk), :])
                if masked:
                    s = jnp.where(_chunk_mask(q0, k0, s.shape, False), s, NEG_BIG)
                m_new = jnp.maximum(m_old, jnp.max(s, axis=-1, keepdims=True))
                alpha = jnp.exp(m_old - m_new)
                p = jnp.exp(s - m_new)
                l_new = alpha * l_old + jnp.sum(p, axis=-1, keepdims=True)
                acc = alpha * acc + _dot(p.astype(CDT), v_ref[pl.ds(k0, bk), :])
                return m_new, l_new, acc

            n_full = _full_key_blocks(i, bq, bk)
            carry = (jnp.full((bq, 1), NEG_BIG, F32), jnp.zeros((bq, 1), F32), jnp.zeros((bq, V_HEAD), F32))
            carry = lax.fori_loop(0, n_full, lambda j, c: k_step(False, j, c), carry)
            m_fin, l_fin, acc = lax.fori_loop(n_full, _last_key_block(i, bq, bk, nk) + 1,
                                              lambda j, c: k_step(True, j, c), carry)
            o_ref[pl.ds(q0, bq), :] = acc / l_fin
            lse_ref[pl.ds(q0, bq), :] = m_fin + jnp.log(l_fin)
            return 0

        lax.fori_loop(0, nq, q_block, 0)

    per_head = pl.BlockSpec((None, tp, depth), lambda h: (h, 0, 0))
    return pl.pallas_call(
        body,
        grid=(MLA_HEADS,),
        in_specs=[per_head, per_head, pl.BlockSpec((tp, V_HEAD), lambda h: (0, 2 * h + 1))],
        out_specs=[pl.BlockSpec((tp, V_HEAD), lambda h: (0, h)), pl.BlockSpec((None, tp, 1), lambda h: (h, 0, 0))],
        out_shape=[jax.ShapeDtypeStruct((tp, MLA_HEADS * V_HEAD), F32), jax.ShapeDtypeStruct((MLA_HEADS, tp, 1), F32)],
        compiler_params=_params("parallel"),
        name="attn_fwd",
    )(qp, kp, kv)


def _attn_delta(d_out, out):
    tp = out.shape[0]
    tr = _div_tile(tp, ROW_TILE, 8)

    def body(do_ref, o_ref, d_ref):
        for h in range(MLA_HEADS):
            cols = slice(h * V_HEAD, (h + 1) * V_HEAD)
            d_ref[h] = jnp.sum(do_ref[:, cols] * o_ref[:, cols], axis=-1, keepdims=True)

    row = pl.BlockSpec((tr, MLA_HEADS * V_HEAD), lambda i: (i, 0))
    return pl.pallas_call(
        body,
        grid=(tp // tr,),
        in_specs=[row, row],
        out_specs=pl.BlockSpec((MLA_HEADS, tr, 1), lambda i: (0, i, 0)),
        out_shape=jax.ShapeDtypeStruct((MLA_HEADS, tp, 1), F32),
        compiler_params=_params("parallel"),
        name="attn_delta",
    )(d_out, out)


def _attn_bwd(qp, kp, kv, d_out, lse, delta):
    tp = qp.shape[1]
    depth = qp.shape[2]
    bq = bk = _div_tile(tp, ATT_TILE, LANES)
    nq, nk = tp // bq, tp // bk
    scale = (QK_NOPE + QK_ROPE) ** -0.5
    lse_rows = lse.reshape(MLA_HEADS, nq, 1, bq)
    delta_rows = delta.reshape(MLA_HEADS, nq, 1, bq)

    strip = _div_tile(bk, ATT_STRIP, 16)

    def body(q_ref, k_ref, v_ref, do_ref, lse_ref, dl_ref, dqn_ref, dqr_ref, dkv_ref, dkr_ref,
             dq_acc, dk_acc, dv_acc, s_scr, dp_scr, p_scr, ds_scr):
        h = pl.program_id(0)
        dq_acc[...] = jnp.zeros(dq_acc.shape, F32)

        def k_block(j, _):
            k0 = pl.multiple_of(j * bk, bk)
            kb = k_ref[pl.ds(k0, bk), :]
            vb = v_ref[pl.ds(k0, bk), :]
            dk_acc[...] = jnp.zeros(dk_acc.shape, F32)
            dv_acc[...] = jnp.zeros(dv_acc.shape, F32)

            def q_step(masked, i, _):
                q0 = pl.multiple_of(i * bq, bq)
                qb = q_ref[pl.ds(q0, bq), :]
                dob = do_ref[pl.ds(q0, bq), :].astype(CDT)
                s_scr[...] = _dot_nt(kb, qb)
                dp_scr[...] = _dot_nt(vb, dob)
                lse_row = lse_ref[i]
                delta_row = dl_ref[i]
                for r0 in range(0, bk, strip):
                    rows = slice(r0, r0 + strip)
                    s_t = s_scr[rows, :]
                    if masked:
                        s_t = jnp.where(_chunk_mask(q0, k0 + r0, s_t.shape, True), s_t, NEG_BIG)
                    p_t = jnp.exp(s_t - lse_row)
                    p_scr[rows, :] = p_t.astype(CDT)
                    ds_scr[rows, :] = (p_t * (dp_scr[rows, :] - delta_row)).astype(CDT)
                ds_t = ds_scr[...]
                dv_acc[...] += _dot(p_scr[...], dob)
                dk_acc[...] += _dot(ds_t, qb)
                dq_acc[pl.ds(q0, bq), :] += _dot_tn(ds_t, kb)
                return 0

            i_full = _first_full_query_block(j, bk, bq, nq)
            lax.fori_loop(_first_query_block(j, bk, bq), i_full, lambda i, c: q_step(True, i, c), 0)
            lax.fori_loop(i_full, nq, lambda i, c: q_step(False, i, c), 0)
            dkv_ref[pl.ds(k0, bk), :QK_NOPE] = dk_acc[:, :QK_NOPE].astype(CDT)
            dkv_ref[pl.ds(k0, bk), QK_NOPE:] = dv_acc[...].astype(CDT)
            dkr_ref[pl.ds(k0, bk), :] = dk_acc[:, QK_NOPE:]
            return 0

        lax.fori_loop(0, nk, k_block, 0)
        dqn_ref[...] = (dq_acc[:, :QK_NOPE] * scale).astype(CDT)
        dqr_ref[...] = _head_half(dq_acc[:, QK_NOPE:] * scale, h)

    per_head = pl.BlockSpec((None, tp, depth), lambda h: (h, 0, 0))
    stat = pl.BlockSpec((None, nq, 1, bq), lambda h: (h, 0, 0, 0))
    lanes_out = pl.BlockSpec((None, tp, LANES), lambda h: (h, 0, 0))
    return pl.pallas_call(
        body,
        grid=(MLA_HEADS,),
        in_specs=[per_head, per_head, pl.BlockSpec((tp, V_HEAD), lambda h: (0, 2 * h + 1)),
                  pl.BlockSpec((tp, V_HEAD), lambda h: (0, h)), stat, stat],
        out_specs=[pl.BlockSpec((tp, QK_NOPE), lambda h: (0, h)), lanes_out,
                   pl.BlockSpec((tp, QK_NOPE + V_HEAD), lambda h: (0, h)), lanes_out],
        out_shape=[jax.ShapeDtypeStruct((tp, MLA_HEADS * QK_NOPE), CDT), jax.ShapeDtypeStruct((MLA_HEADS, tp, LANES), F32),
                   jax.ShapeDtypeStruct((tp, MLA_HEADS * (QK_NOPE + V_HEAD)), CDT),
                   jax.ShapeDtypeStruct((MLA_HEADS, tp, LANES), F32)],
        scratch_shapes=[pltpu.VMEM((tp, depth), F32), pltpu.VMEM((bk, depth), F32), pltpu.VMEM((bk, V_HEAD), F32),
                        pltpu.VMEM((bk, bq), F32), pltpu.VMEM((bk, bq), F32), pltpu.VMEM((bk, bq), CDT),
                        pltpu.VMEM((bk, bq), CDT)],
        compiler_params=_params("parallel"),
        name="attn_bwd",
    )(qp, kp, kv, d_out, lse_rows, delta_rows)


def _mla_unprep(dqr_h, dkr_h, cos2, sin2):
    tp = dqr_h.shape[1]
    tr = _div_tile(tp, ROW_TILE, 8)
    wr = MLA_HEADS * QK_ROPE

    def body(dq_ref, dk_ref, c_ref, s_ref, dqr_out, dkr_out):
        c = c_ref[...]
        s = s_ref[...]
        for p in range(MLA_HEADS // 2):
            x = dq_ref[2 * p] + dq_ref[2 * p + 1]
            dqr_out[:, p * LANES:(p + 1) * LANES] = (x * c - _swap_halves(x) * s).astype(CDT)
        t = dk_ref[0]
        for h in range(1, MLA_HEADS):
            t = t + dk_ref[h]
        t = t * c - _swap_halves(t) * s
        t = t + pltpu.roll(t, QK_ROPE, 1)
        lane = lax.broadcasted_iota(jnp.int32, t.shape, 1)
        dkr_out[...] = jnp.where(lane < QK_ROPE, t, 0.0)

    per_head = pl.BlockSpec((MLA_HEADS, tr, LANES), lambda i: (0, i, 0))
    tab = pl.BlockSpec((tr, LANES), lambda i: (i, 0))
    return pl.pallas_call(
        body,
        grid=(tp // tr,),
        in_specs=[per_head, per_head, tab, tab],
        out_specs=[pl.BlockSpec((tr, wr), lambda i: (i, 0)), tab],
        out_shape=[jax.ShapeDtypeStruct((tp, wr), CDT), jax.ShapeDtypeStruct((tp, LANES), F32)],
        compiler_params=_params("parallel"),
        name="mla_unprep",
    )(dqr_h, dkr_h, cos2, sin2)


HALO = 8


def _softplus(x):
    return jnp.maximum(x, 0.0) + jnp.log1p(jnp.exp(-jnp.abs(x)))


def _one_minus_sq(log_a, a):
    return -jnp.tanh(log_a) * (a * a + 1.0)


def _gelu(y):
    k = math.sqrt(2.0 / math.pi)
    return 0.5 * y * (1.0 + jnp.tanh(k * (y + 0.044715 * (y * y * y))))


def _gelu_grad(y):
    k = math.sqrt(2.0 / math.pi)
    th = jnp.tanh(k * (y + 0.044715 * (y * y * y)))
    return 0.5 * (1.0 + th) + 0.5 * y * (1.0 - th * th) * (k * (1.0 + 3.0 * 0.044715 * (y * y)))


def _lru_gates_fwd(xy, conv_w, conv_b, w_ga, b_ga, w_gx, b_gx, lam):
    tp = xy.shape[0]
    dr = xy.shape[1] // 2
    bw = dr // RNN_BLOCKS
    tr = _div_tile(tp, LRU_ROW_TILE, 8)

    def body(x_ref, halo_ref, cw_ref, cb_ref, wa_ref, ba_ref, wx_ref, bx_ref, lam_ref,
             xc_ref, r_ref, i_ref, a_ref, b_ref, xs):
        i = pl.program_id(0)
        xs[0:HALO, :] = jnp.where(i == 0, 0.0, halo_ref[...])
        xs[HALO:, :] = x_ref[...]
        xc = cb_ref[...] + cw_ref[0:1, :] * xs[pl.ds(HALO - CONV_W + 1, tr), :]
        for j in range(1, CONV_W):
            xc = xc + cw_ref[j:j + 1, :] * xs[pl.ds(HALO - CONV_W + 1 + j, tr), :]
        xcb = xc.astype(CDT)
        r = _sigmoid(_dot(xcb, wa_ref[...]) + ba_ref[...])
        ig = _sigmoid(_dot(xcb, wx_ref[...]) + bx_ref[...])
        log_a = (-LRU_C * r) * _softplus(-lam_ref[...])
        a = jnp.exp(log_a)
        xc_ref[...] = xc
        r_ref[...] = r
        i_ref[...] = ig
        a_ref[...] = a
        b_ref[...] = jnp.sqrt(_one_minus_sq(log_a, a)) * (ig * xc)

    blk = pl.BlockSpec((tr, bw), lambda i, n: (i, n))
    vec = pl.BlockSpec((1, bw), lambda i, n: (0, n))
    mat = pl.BlockSpec((None, bw, bw), lambda i, n: (n, 0, 0))
    bias = pl.BlockSpec((None, 1, bw), lambda i, n: (n, 0, 0))
    out = jax.ShapeDtypeStruct((tp, dr), F32)
    return pl.pallas_call(
        body,
        grid=(tp // tr, RNN_BLOCKS),
        in_specs=[blk, pl.BlockSpec((HALO, bw), lambda i, n: (jnp.maximum(i * (tr // HALO) - 1, 0), n)),
                  pl.BlockSpec((CONV_W, bw), lambda i, n: (0, n)), vec, mat, bias, mat, bias, vec],
        out_specs=[blk] * 5,
        out_shape=[out] * 5,
        scratch_shapes=[pltpu.VMEM((tr + HALO, bw), F32)],
        compiler_params=_params("parallel", "parallel"),
        name="lru_gates_fwd",
    )(xy, xy, conv_w, conv_b.reshape(1, dr), w_ga.astype(CDT), b_ga.reshape(RNN_BLOCKS, 1, bw),
      w_gx.astype(CDT), b_gx.reshape(RNN_BLOCKS, 1, bw), lam.reshape(1, dr))


def _stack_rows(rows):
    idx = lax.broadcasted_iota(jnp.int32, (len(rows), rows[0].shape[1]), 0)
    out = jnp.broadcast_to(rows[0], idx.shape)
    for j in range(1, len(rows)):
        out = jnp.where(idx == j, jnp.broadcast_to(rows[j], idx.shape), out)
    return out


def _lru_scan_fwd(a, b, xy):
    tp, dr = a.shape
    cw = min(2 * SCAN_COLS, dr)
    ycol0 = dr // cw
    ch = _div_tile(tp, ROW_TILE, 16)

    def body(a_ref, b_ref, y_ref, hs_ref, m_ref):
        def group(g, h):
            base = pl.multiple_of(g * 8, 8)
            at = a_ref[pl.ds(base, 8), :]
            bt = b_ref[pl.ds(base, 8), :]
            rows = []
            for j in range(8):
                h = at[j:j + 1, :] * h + bt[j:j + 1, :]
                rows.append(h)
            hs_ref[pl.ds(base, 8), :] = _stack_rows(rows)
            return h

        lax.fori_loop(0, tp // 8, group, jnp.zeros((1, cw), F32))

        def gate(c, _):
            r0 = pl.multiple_of(c * ch, ch)
            m_ref[pl.ds(r0, ch), :] = (hs_ref[pl.ds(r0, ch), :] * _gelu(y_ref[pl.ds(r0, ch), :])).astype(CDT)
            return 0

        lax.fori_loop(0, tp // ch, gate, 0)

    col = pl.BlockSpec((tp, cw), lambda n: (0, n))
    return pl.pallas_call(
        body,
        grid=(dr // cw,),
        in_specs=[col, col, pl.BlockSpec((tp, cw), lambda n: (0, ycol0 + n))],
        out_specs=[col, col],
        out_shape=[jax.ShapeDtypeStruct((tp, dr), F32), jax.ShapeDtypeStruct((tp, dr), CDT)],
        compiler_params=_params("parallel"),
        name="lru_scan_fwd",
    )(a, b, xy)


def _lru_scan_bwd(a, hs, dm, xy):
    tp, dr = a.shape
    cw = min(2 * SCAN_COLS, dr)
    ycol0 = dr // cw
    ng = tp // 8
    ch = _div_tile(tp, ROW_TILE, 16)

    def body(a_ref, hs_ref, dm_ref, y_ref, db_ref, da_ref, dy_ref):
        def ungate(c, _):
            rows = pl.ds(pl.multiple_of(c * ch, ch), ch)
            y = y_ref[rows, :]
            dm = dm_ref[rows, :]
            db_ref[rows, :] = dm * _gelu(y)
            dy_ref[rows, :] = (dm * hs_ref[rows, :] * _gelu_grad(y)).astype(CDT)
            return 0

        lax.fori_loop(0, tp // ch, ungate, 0)

        def group(k, carry):
            g_next, a_next = carry
            g = ng - 1 - k
            base = pl.multiple_of(g * 8, 8)
            prev = pl.multiple_of(jnp.maximum(g - 1, 0) * 8, 8)
            dt = db_ref[pl.ds(base, 8), :]
            at = a_ref[pl.ds(base, 8), :]
            ht = hs_ref[pl.ds(base, 8), :]
            h_before = jnp.where(g == 0, 0.0, hs_ref[pl.ds(prev, 8), :][7:8, :])
            g_rows = [None] * 8
            da_rows = [None] * 8
            for j in range(7, -1, -1):
                g_cur = dt[j:j + 1, :] + a_next * g_next
                g_rows[j] = g_cur
                da_rows[j] = g_cur * (ht[j - 1:j, :] if j > 0 else h_before)
                g_next = g_cur
                a_next = at[j:j + 1, :]
            db_ref[pl.ds(base, 8), :] = _stack_rows(g_rows)
            da_ref[pl.ds(base, 8), :] = _stack_rows(da_rows)
            return g_next, a_next

        zero = jnp.zeros((1, cw), F32)
        lax.fori_loop(0, ng, group, (zero, zero))

    col = pl.BlockSpec((tp, cw), lambda n: (0, n))
    col_in = pl.BlockSpec((tp, cw), lambda n: (0, n), pipeline_mode=pl.Buffered(1))
    y_in = pl.BlockSpec((tp, cw), lambda n: (0, ycol0 + n), pipeline_mode=pl.Buffered(1))
    return pl.pallas_call(
        body,
        grid=(dr // cw,),
        in_specs=[col_in, col_in, col_in, y_in],
        out_specs=[col, col, col],
        out_shape=[jax.ShapeDtypeStruct((tp, dr), F32), jax.ShapeDtypeStruct((tp, dr), F32),
                   jax.ShapeDtypeStruct((tp, dr), CDT)],
        compiler_params=_params("parallel"),
        name="lru_scan_bwd",
    )(a, hs, dm, xy)


def _lru_gates_bwd(db, da, xc, r, ig, a, lam, w_ga, w_gx):
    tp, dr = xc.shape
    bw = dr // RNN_BLOCKS
    tr = _div_tile(tp, LRU_ROW_TILE, 8)
    nr = tp // tr

    def body(db_ref, da_ref, xc_ref, r_ref, i_ref, a_ref, lam_ref, wa_ref, wx_ref,
             dxc_ref, dwa_ref, dba_ref, dwx_ref, dbx_ref, dlam_ref):
        i = pl.program_id(1)
        xc = xc_ref[...]
        r = r_ref[...]
        ig = i_ref[...]
        a = a_ref[...]
        dbv = db_ref[...]
        sp = _softplus(-lam_ref[...])
        log_a = (-LRU_C * r) * sp
        s = jnp.sqrt(_one_minus_sq(log_a, a))
        d_ix = dbv * s
        d_s = dbv * (ig * xc)
        d_log_a = da_ref[...] * a - d_s * (a * a) / s
        d_r = d_log_a * (-LRU_C * sp)
        d_sp = jnp.sum(d_log_a * (-LRU_C * r), axis=0, keepdims=True)
        dzr = d_r * r * (1.0 - r)
        dzi = (d_ix * xc) * ig * (1.0 - ig)
        dzr_b = dzr.astype(CDT)
        dzi_b = dzi.astype(CDT)
        xcb = xc.astype(CDT)
        dxc_ref[...] = d_ix * ig + _dot_nt(dzr_b, wa_ref[...]) + _dot_nt(dzi_b, wx_ref[...])
        dwa = _dot_tn(xcb, dzr_b)
        dwx = _dot_tn(xcb, dzi_b)
        dba = jnp.sum(dzr, axis=0, keepdims=True)
        dbx = jnp.sum(dzi, axis=0, keepdims=True)

        @pl.when(i == 0)
        def _():
            dwa_ref[...] = dwa
            dwx_ref[...] = dwx
            dba_ref[...] = dba
            dbx_ref[...] = dbx
            dlam_ref[...] = d_sp

        @pl.when(i > 0)
        def _():
            dwa_ref[...] += dwa
            dwx_ref[...] += dwx
            dba_ref[...] += dba
            dbx_ref[...] += dbx
            dlam_ref[...] += d_sp

        @pl.when(i == nr - 1)
        def _():
            dlam_ref[...] = dlam_ref[...] * (-_sigmoid(-lam_ref[...]))

    blk = pl.BlockSpec((tr, bw), lambda n, i: (i, n))
    vec = pl.BlockSpec((1, bw), lambda n, i: (0, n))
    mat = pl.BlockSpec((None, bw, bw), lambda n, i: (n, 0, 0))
    bias = pl.BlockSpec((None, 1, bw), lambda n, i: (n, 0, 0))
    return pl.pallas_call(
        body,
        grid=(RNN_BLOCKS, nr),
        in_specs=[blk] * 6 + [vec, mat, mat],
        out_specs=[blk, mat, bias, mat, bias, vec],
        out_shape=[jax.ShapeDtypeStruct((tp, dr), F32),
                   jax.ShapeDtypeStruct((RNN_BLOCKS, bw, bw), F32), jax.ShapeDtypeStruct((RNN_BLOCKS, 1, bw), F32),
                   jax.ShapeDtypeStruct((RNN_BLOCKS, bw, bw), F32), jax.ShapeDtypeStruct((RNN_BLOCKS, 1, bw), F32),
                   jax.ShapeDtypeStruct((1, dr), F32)],
        compiler_params=_params("parallel", "arbitrary"),
        name="lru_gates_bwd",
    )(db, da, xc, r, ig, a, lam.reshape(1, dr), w_ga.astype(CDT), w_gx.astype(CDT))


def _lru_conv_bwd(dxc, xy, conv_w):
    tp, dr = dxc.shape
    bw = dr // RNN_BLOCKS
    tr = _div_tile(tp, LRU_ROW_TILE, 8)
    nr = tp // tr
    per = tr // HALO

    def body(d_ref, dnext_ref, x_ref, xprev_ref, cw_ref, dxb_ref, dcw_ref, dcb_ref, ds, xs):
        i = pl.program_id(1)
        d = d_ref[...]
        ds[0:tr, :] = d
        ds[tr:, :] = jnp.where(i == nr - 1, 0.0, dnext_ref[...])
        xs[0:HALO, :] = jnp.where(i == 0, 0.0, xprev_ref[...])
        xs[HALO:, :] = x_ref[...]
        dxb = cw_ref[0:1, :] * ds[pl.ds(CONV_W - 1, tr), :]
        for j in range(1, CONV_W):
            dxb = dxb + cw_ref[j:j + 1, :] * ds[pl.ds(CONV_W - 1 - j, tr), :]
        dxb_ref[...] = dxb.astype(CDT)
        dcb = jnp.sum(d, axis=0, keepdims=True)
        dcw = [jnp.sum(d * xs[pl.ds(HALO - CONV_W + 1 + j, tr), :], axis=0, keepdims=True) for j in range(CONV_W)]

        @pl.when(i == 0)
        def _():
            dcb_ref[...] = dcb
            for j in range(CONV_W):
                dcw_ref[j] = dcw[j]

        @pl.when(i > 0)
        def _():
            dcb_ref[...] += dcb
            for j in range(CONV_W):
                dcw_ref[j] += dcw[j]

    blk = pl.BlockSpec((tr, bw), lambda n, i: (i, n))
    return pl.pallas_call(
        body,
        grid=(RNN_BLOCKS, nr),
        in_specs=[blk, pl.BlockSpec((HALO, bw), lambda n, i: (jnp.minimum((i + 1) * per, tp // HALO - 1), n)),
                  blk, pl.BlockSpec((HALO, bw), lambda n, i: (jnp.maximum(i * per - 1, 0), n)),
                  pl.BlockSpec((CONV_W, bw), lambda n, i: (0, n))],
        out_specs=[blk, pl.BlockSpec((CONV_W, 1, bw), lambda n, i: (0, 0, n)), pl.BlockSpec((1, bw), lambda n, i: (0, n))],
        out_shape=[jax.ShapeDtypeStruct((tp, dr), CDT), jax.ShapeDtypeStruct((CONV_W, 1, dr), F32),
                   jax.ShapeDtypeStruct((1, dr), F32)],
        scratch_shapes=[pltpu.VMEM((tr + HALO, bw), F32), pltpu.VMEM((tr + HALO, bw), F32)],
        compiler_params=_params("parallel", "arbitrary"),
        name="lru_conv_bwd",
    )(dxc, dxc, xy, xy, conv_w)


def _me():
    return lax.axis_index("x"), lax.axis_index("y"), lax.axis_index("c")


def _peer(rel):
    x, y, c = _me()
    return (1 - x if rel & 4 else x, 1 - y if rel & 2 else y, 1 - c if rel & 1 else c)


def _chip_of(dev):
    return 2 * dev[0] + dev[1]


def _linear(dev):
    return 4 * dev[0] + 2 * dev[1] + dev[2]


CHIP_RELS = (4, 2, 6)
ALL_RELS = (1, 2, 3, 4, 5, 6, 7)
PAIR_RELS = (1,)


def _scatter_send(pieces, rels, piece_of, name):
    n = len(rels)

    def body(src_ref, recv_ref, send_sems, recv_sems):
        copies = []
        for k, rel in enumerate(rels):
            peer = _peer(rel)
            cp = pltpu.make_async_remote_copy(
                src_ref=src_ref.at[piece_of(peer)], dst_ref=recv_ref.at[k],
                send_sem=send_sems.at[k], recv_sem=recv_sems.at[k], device_id=peer, device_id_type=MESH)
            cp.start()
            copies.append(cp)
        for cp in copies:
            cp.wait()

    return pl.pallas_call(
        body,
        in_specs=[pl.BlockSpec(memory_space=pl.ANY)],
        out_specs=pl.BlockSpec(memory_space=pl.ANY),
        out_shape=jax.ShapeDtypeStruct((n,) + pieces.shape[1:], pieces.dtype),
        scratch_shapes=[pltpu.SemaphoreType.DMA((n,)), pltpu.SemaphoreType.DMA((n,))],
        name=name,
    )(pieces)


def _gather_send(piece, rels, n_slots, slot_of, name, n_chunks=1):
    n = len(rels)
    rows = piece.shape[0]
    if rows % (8 * n_chunks):
        n_chunks = 1
    rc = rows // n_chunks

    def body(src_ref, out_ref, send_sems, recv_sems, local_sems):
        me = _me()

        def part(ref, q):
            return ref.at[pl.ds(q * rc, rc)]

        def remote(k, q, slot_dev, to):
            return pltpu.make_async_remote_copy(
                src_ref=part(src_ref, q), dst_ref=part(out_ref.at[slot_of(slot_dev)], q),
                send_sem=send_sems.at[k * n_chunks + q], recv_sem=recv_sems.at[k * n_chunks + q],
                device_id=to, device_id_type=MESH)

        mine = [pltpu.make_async_copy(part(src_ref, q), part(out_ref.at[slot_of(me)], q), local_sems.at[q])
                for q in range(n_chunks)]
        for cp in mine:
            cp.start()
        sends = [remote(k, q, me, _peer(rel)) for k, rel in enumerate(rels) for q in range(n_chunks)]
        for cp in sends:
            cp.start()
        for k, rel in enumerate(rels):
            for q in range(n_chunks):
                remote(k, q, _peer(rel), _peer(rel)).wait_recv()
        for cp in sends:
            cp.wait_send()
        for cp in mine:
            cp.wait()

    return pl.pallas_call(
        body,
        in_specs=[pl.BlockSpec(memory_space=pl.ANY)],
        out_specs=pl.BlockSpec(memory_space=pl.ANY),
        out_shape=jax.ShapeDtypeStruct((n_slots,) + piece.shape, piece.dtype),
        scratch_shapes=[pltpu.SemaphoreType.DMA((n * n_chunks,)), pltpu.SemaphoreType.DMA((n * n_chunks,)),
                        pltpu.SemaphoreType.DMA((n_chunks,))],
        name=name,
    )(piece)


def _gather_chips(shard, name):
    return _gather_send(shard, CHIP_RELS, N_CHIPS, _chip_of, name)


HBM_SPEC = pl.BlockSpec(memory_space=pltpu.HBM)
SEM_SPEC = pl.BlockSpec(memory_space=pltpu.SEMAPHORE)
DATAFLOW = pltpu.SideEffectType.DATAFLOW_SIDE_EFFECTING


def _split_start(src, land, copies, name):
    def body(src_ref, land_ref, send_sem, recv_sem, src_thru, land_thru, token):
        for s_ref, d_ref, peer in copies(src_ref, land_ref):
            pltpu.make_async_remote_copy(src_ref=s_ref, dst_ref=d_ref, send_sem=send_sem, recv_sem=recv_sem,
                                         device_id=peer, device_id_type=MESH).start()
        token[...] = jnp.zeros(token.shape, token.dtype)

    return pl.pallas_call(
        body,
        name=name,
        out_shape=(pltpu.SemaphoreType.DMA(()), pltpu.SemaphoreType.DMA(()), pltpu.HBM(src.shape, src.dtype),
                   pltpu.HBM(land.shape, land.dtype), jax.ShapeDtypeStruct((8, LANES), F32)),
        in_specs=(HBM_SPEC, HBM_SPEC),
        out_specs=(SEM_SPEC, SEM_SPEC, HBM_SPEC, HBM_SPEC, pl.BlockSpec(memory_space=pltpu.VMEM)),
        input_output_aliases={0: 2, 1: 3},
        compiler_params=pltpu.CompilerParams(has_side_effects=DATAFLOW),
    )(pltpu.with_memory_space_constraint(src, pltpu.HBM), pltpu.with_memory_space_constraint(land, pltpu.HBM))


def _split_gather_start(piece, rels, n_slots, slot_of, name):
    land = jnp.broadcast_to(piece[None], (n_slots,) + piece.shape)
    return _split_start(piece, land, lambda s, l: [(s, l.at[slot_of(_me())], _peer(rel)) for rel in rels], name)


def _gather_chips_start(shard, name):
    return _split_gather_start(shard, CHIP_RELS, N_CHIPS, _chip_of, name)


def _split_scatter_start(pieces, name):
    land = lax.empty((len(ALL_RELS),) + pieces.shape[1:], pieces.dtype)
    return _split_start(pieces, land,
                        lambda s, l: [(s.at[_linear(_peer(rel))], l.at[k], _peer(rel)) for k, rel in enumerate(ALL_RELS)], name)


def _split_wait(started, n, after, name, with_src=False):
    send_sem, recv_sem, src_thru, land_thru, _ = started

    def body(src_ref, land_ref, send_sem, recv_sem, after_ref, src_dead, got_ref):
        all_n = land_ref.at[pl.ds(0, n)]
        arrivals = pltpu.make_async_remote_copy(
            src_ref=all_n, dst_ref=all_n, send_sem=send_sem, recv_sem=recv_sem, device_id=_me(), device_id_type=MESH)
        arrivals.wait_send()
        arrivals.wait_recv()

    out = pl.pallas_call(
        body,
        name=name,
        out_shape=(pltpu.HBM(src_thru.shape, src_thru.dtype), pltpu.HBM(land_thru.shape, land_thru.dtype)),
        in_specs=(HBM_SPEC, HBM_SPEC, SEM_SPEC, SEM_SPEC, pl.BlockSpec(memory_space=pl.ANY)),
        out_specs=(HBM_SPEC, HBM_SPEC),
        input_output_aliases={0: 0, 1: 1},
        compiler_params=pltpu.CompilerParams(has_side_effects=DATAFLOW),
    )(src_thru, land_thru, send_sem, recv_sem, after)
    return (out[1], out[0]) if with_src else out[1]


def _sum_pieces(pieces, recv, name):
    _, rr, cc = pieces.shape
    n = recv.shape[0]
    tr = _div_tile(rr, max(8, (1 << 17) // cc // 8 * 8), 8)

    def body(own_ref, recv_ref, o_ref):
        acc = own_ref[...].astype(F32)
        for k in range(n):
            acc = acc + recv_ref[k].astype(F32)
        o_ref[...] = acc

    return pl.pallas_call(
        body,
        grid=(rr // tr,),
        in_specs=[pl.BlockSpec((None, tr, cc), lambda i: (_linear(_me()), i, 0)),
                  pl.BlockSpec((n, tr, cc), lambda i: (0, i, 0))],
        out_specs=pl.BlockSpec((tr, cc), lambda i: (i, 0)),
        out_shape=jax.ShapeDtypeStruct((rr, cc), F32),
        compiler_params=_params("parallel"),
        name=name,
    )(pieces, recv)


def _reduce_to_owner(g8, payload_dtype, name):
    recv = _scatter_send(g8.astype(payload_dtype), ALL_RELS, _linear, name + "_scatter")
    return _sum_pieces(g8, recv, name + "_sum")


def _adamw_layer(w, g, m, v, outs, layer, name):
    nl, rr, cc = w.shape
    _, gr, gc = g.shape
    tr = _div_tile(gr, max(8, (1 << 17) // gc // 8 * 8), 8)
    steps = gr // tr
    c1 = 1.0 - ADAM_B1 ** ADAM_STEP
    c2 = 1.0 - ADAM_B2 ** ADAM_STEP
    if gc == cc:
        assert 2 * gr == rr, (name, g.shape, w.shape)
        slab = pl.BlockSpec((None, tr, gc), lambda h, i: (layer, h * steps + i, 0))
    else:
        assert gr == rr and 2 * gc == cc, (name, g.shape, w.shape)
        slab = pl.BlockSpec((None, tr, gc), lambda h, i: (layer, i, h))

    def body(w_ref, g_ref, m_ref, v_ref, *rest):
        go_ref, d_ref, mo_ref, vo_ref = rest[-4:]
        g_ = g_ref[...]
        m_ = ADAM_B1 * m_ref[...] + (1.0 - ADAM_B1) * g_
        v_ = ADAM_B2 * v_ref[...] + (1.0 - ADAM_B2) * (g_ * g_)
        go_ref[...] = g_
        d_ref[...] = -ADAM_LR * ((m_ / c1) / (jnp.sqrt(v_ / c2) + ADAM_EPS) + ADAM_WD * w_ref[...])
        mo_ref[...] = m_
        vo_ref[...] = v_

    out = jax.ShapeDtypeStruct((nl, rr, cc), F32)
    in_specs = [slab, pl.BlockSpec((None, tr, gc), lambda h, i: (h, i, 0)), slab, slab]
    args = [w, g, m, v]
    aliases = {}
    if outs is not None:
        in_specs += [pl.BlockSpec(memory_space=pl.ANY)] * 4
        args += list(outs)
        aliases = {4 + k: k for k in range(4)}
    return pl.pallas_call(
        body,
        grid=(2, steps),
        in_specs=in_specs,
        out_specs=[slab] * 4,
        out_shape=[out] * 4,
        input_output_aliases=aliases,
        compiler_params=_params("parallel", "parallel"),
        name=name,
    )(*args)


def _adamw(w, g, m, v, name):
    rr, cc = w.shape
    tr = _div_tile(rr, max(8, (1 << 17) // cc // 8 * 8), 8)
    c1 = 1.0 - ADAM_B1 ** ADAM_STEP
    c2 = 1.0 - ADAM_B2 ** ADAM_STEP

    def body(w_ref, g_ref, m_ref, v_ref, d_ref, mo_ref, vo_ref):
        g_ = g_ref[...]
        m_ = ADAM_B1 * m_ref[...] + (1.0 - ADAM_B1) * g_
        v_ = ADAM_B2 * v_ref[...] + (1.0 - ADAM_B2) * (g_ * g_)
        d_ref[...] = -ADAM_LR * ((m_ / c1) / (jnp.sqrt(v_ / c2) + ADAM_EPS) + ADAM_WD * w_ref[...])
        mo_ref[...] = m_
        vo_ref[...] = v_

    blk = pl.BlockSpec((tr, cc), lambda i: (i, 0))
    out = jax.ShapeDtypeStruct((rr, cc), F32)
    return pl.pallas_call(
        body,
        grid=(rr // tr,),
        in_specs=[blk] * 4,
        out_specs=[blk] * 3,
        out_shape=[out] * 3,
        compiler_params=_params("parallel"),
        name=name,
    )(w, g, m, v)


def _pack(arrays, cols, row_mult):
    flat = jnp.concatenate([a.reshape(-1) for a in arrays])
    rows = -(-flat.shape[0] // cols)
    rows = -(-rows // row_mult) * row_mult
    return jnp.pad(flat, (0, rows * cols - flat.shape[0])).reshape(rows, cols)


def _unpack(buf, shapes):
    flat = buf.reshape(-1)
    out, off = [], 0
    for s in shapes:
        n = math.prod(s)
        out.append(flat[off:off + n].reshape(s))
        off += n
    return out


def kernel(x, meta_tokens, norm_mix, norm_ffn, norm_final, mla_w_in, mla_q_norm, mla_kv_norm, mla_w_uq, mla_w_ukv, mla_w_o, lru_w_in, lru_conv_w, lru_conv_b, lru_w_gate_a, lru_b_gate_a, lru_w_gate_x, lru_b_gate_x, lru_lambda, lru_w_o, ffn_w_gu, ffn_w_down, loss_target, m_meta_tokens, m_norm_mix, m_norm_ffn, m_norm_final, m_mla_w_in, m_mla_q_norm, m_mla_kv_norm, m_mla_w_uq, m_mla_w_ukv, m_mla_w_o, m_lru_w_in, m_lru_conv_w, m_lru_conv_b, m_lru_w_gate_a, m_lru_b_gate_a, m_lru_w_gate_x, m_lru_b_gate_x, m_lru_lambda, m_lru_w_o, m_ffn_w_gu, m_ffn_w_down, v_meta_tokens, v_norm_mix, v_norm_ffn, v_norm_final, v_mla_w_in, v_mla_q_norm, v_mla_kv_norm, v_mla_w_uq, v_mla_w_ukv, v_mla_w_o, v_lru_w_in, v_lru_conv_w, v_lru_conv_b, v_lru_w_gate_a, v_lru_b_gate_a, v_lru_w_gate_x, v_lru_b_gate_x, v_lru_lambda, v_lru_w_o, v_ffn_w_gu, v_ffn_w_down):
    d = D_MODEL
    t_real = N_META + SEQ
    tp = _t_pad()
    n_mla = mla_w_in.shape[0]
    n_lru = lru_w_in.shape[0]
    h_dim = MLA_HEADS * V_HEAD
    w_in_cols = Q_LORA + KV_LORA + QK_ROPE
    w_in_pad = Q_LORA + KV_LORA + LANES
    q_cols = MLA_HEADS * (QK_NOPE + QK_ROPE)
    tmm = _div_tile(tp, MM_ROW_TILE, 16)
    tkt = _div_tile(tp, 1408, 16)

    def tile(n, pref):
        return _div_tile(n, pref, LANES)

    small_shapes = [meta_tokens.shape, lru_conv_w.shape, lru_conv_b.shape, lru_lambda.shape]
    csh = meta_tokens.shape[1]
    small4 = _gather_chips(_pack([meta_tokens, lru_conv_w, lru_conv_b, lru_lambda], csh, 16), "gather_small")
    small4, mla_w_in = lax.optimization_barrier((small4, mla_w_in))
    started = {}

    def start(key, shard):
        prev = list(started.values())[-1][4][0, 0] if started else 0.0
        started[key] = _gather_chips_start((shard + prev).astype(CDT), "gather_" + key + "_start")

    def arrived(key, after):
        return _split_wait(started[key], len(CHIP_RELS), after, "gather_" + key + "_wait")

    def start_ffn(layer):
        start(f"w_gu{layer}", ffn_w_gu[layer:layer + 1])
        start(f"w_down{layer}", ffn_w_down[layer:layer + 1])

    start("w_in", jnp.pad(mla_w_in, ((0, 0), (0, 0), (0, w_in_pad - w_in_cols))))
    start("w_uq", mla_w_uq)
    start("w_ukv", mla_w_ukv)
    start("w_o", mla_w_o)
    start_ffn(0)
    start("lw_in", lru_w_in)
    start("lw_o", lru_w_o)
    for layer in range(1, DEPTH):
        start_ffn(layer)
    all_started = list(started.values())[-1][4][0, 0]
    n_gu = ffn_w_gu.shape[2]
    w_gu4, w_down4 = [None] * DEPTH, [None] * DEPTH
    small_full = [jnp.concatenate(parts, axis=-1) for parts in zip(*[_unpack(small4[k], small_shapes) for k in range(N_CHIPS)])]
    meta_full, conv_w_full, conv_b_full, lam_full = small_full

    cos2, sin2 = _rope_tables(tp)

    h = jnp.concatenate([meta_full, x[0], jnp.zeros((tp - t_real, d), F32)], axis=0) + all_started
    saved = []
    for layer in range(DEPTH):
        j = layer // 2
        s = {"h_in": h}
        hn = _rms_fwd(h, norm_mix[layer], width=d, col_block=0, name="norm_mix_fwd")
        s["hn"] = hn
        if layer == 0:
            w_in4, w_uq4, w_ukv4, w_o4 = (arrived(k, hn) for k in ("w_in", "w_uq", "w_ukv", "w_o"))
            w_uq_full = jnp.moveaxis(w_uq4, 0, 2).reshape(n_mla, Q_LORA, MLA_HEADS, QK_NOPE + QK_ROPE)
            w_uq_perm = jnp.concatenate([w_uq_full[..., :QK_NOPE].reshape(n_mla, Q_LORA, -1),
                                         w_uq_full[..., QK_NOPE:].reshape(n_mla, Q_LORA, -1)], axis=-1)
        if layer == 1:
            lw_in4, lw_o4 = arrived("lw_in", hn), arrived("lw_o", hn)
        if layer % 2 == 0:
            proj = _mm("nn", hn, w_in4, kind="row", layer=j, tm=tmm, tn=tile(w_in_pad, 1152), tk=tile(d // N_CHIPS, 512), name="mla_in")
            c_q = _rms_fwd(proj, mla_q_norm[j], width=Q_LORA, col_block=0, name="q_norm_fwd")
            c_kv = _rms_fwd(proj, mla_kv_norm[j], width=KV_LORA, col_block=Q_LORA // KV_LORA, name="kv_norm_fwd")
            q = _mm("nn", c_q, w_uq_perm[j], tm=tmm, tn=tile(q_cols, 1024), tk=Q_LORA, name="mla_uq")
            kv = _mm("nn", c_kv, w_ukv4, kind="col", layer=j, out_dtype=CDT, tm=tmm, tn=tile(w_ukv4.shape[3], 1024), tk=KV_LORA, name="mla_ukv")
            qp, kp = _mla_prep(q, kv, proj, cos2, sin2)
            att, lse = _attn_fwd(qp, kp, kv)
            h = _mm("nn", att, w_o4, kind="row", layer=j, resid=h, tm=tmm, tn=tile(d, 1024), tk=tile(h_dim // N_CHIPS, 512), name="mla_out")
            s.update(proj=proj, c_q=c_q, c_kv=c_kv, qp=qp, kp=kp, kv=kv, att=att, lse=lse)
        else:
            xy = _mm("nn", hn, lw_in4, kind="col", layer=j, tm=tmm, tn=tile(lw_in4.shape[3], 1024), tk=d, name="lru_in")
            xc, r, ig, a, b = _lru_gates_fwd(xy, conv_w_full[j], conv_b_full[j], lru_w_gate_a[j], lru_b_gate_a[j],
                                             lru_w_gate_x[j], lru_b_gate_x[j], lam_full[j])
            hs, mixed = _lru_scan_fwd(a, b, xy)
            h = _mm("nn", mixed, lw_o4, kind="row", layer=j, resid=h, tm=tmm, tn=tile(d, 1024), tk=tile(d // N_CHIPS, 512), name="lru_out")
            s.update(xy=xy, xc=xc, r=r, ig=ig, a=a, hs=hs, mixed=mixed)
        s["h_mid"] = h
        hn2 = _rms_fwd(h, norm_ffn[layer], width=d, col_block=0, name="norm_ffn_fwd")
        w_gu4[layer], w_down4[layer] = arrived(f"w_gu{layer}", hn2), arrived(f"w_down{layer}", hn2)
        gate, up, act = _ffn_gate_up(hn2, w_gu4[layer], tm=_div_tile(tp, MM_ROW_TILE // 2, 16), tn=tile(n_gu, 1408))
        h = _mm("nn", act, w_down4[layer], kind="row", resid=h, tm=tmm, tn=tile(d, 1024), tk=tile(D_FF // N_CHIPS, 1408), name="ffn_down")
        s.update(hn2=hn2, gate=gate, up=up, act=act)
        saved.append(s)

    target = jnp.concatenate([jnp.zeros((N_META, d), F32), loss_target[0], jnp.zeros((tp - t_real, d), F32)], axis=0)
    dh, dhb, g_norm_final, loss_part = _final_loss(h, norm_final, target)
    loss = lax.psum(loss_part[0, 0], ("x", "y", "c"))

    g_norm_mix, g_norm_ffn = [None] * DEPTH, [None] * DEPTH
    g_q_norm, g_kv_norm = [None] * n_mla, [None] * n_mla
    g_w_uq = [None] * n_mla
    g_gate = {k: [None] * n_lru for k in ("wa", "ba", "wx", "bx", "lam", "cw", "cb")}
    weights = {"w_in": (mla_w_in, m_mla_w_in, v_mla_w_in), "w_uq": (mla_w_uq, m_mla_w_uq, v_mla_w_uq),
               "w_ukv": (mla_w_ukv, m_mla_w_ukv, v_mla_w_ukv), "w_o": (mla_w_o, m_mla_w_o, v_mla_w_o),
               "lw_in": (lru_w_in, m_lru_w_in, v_lru_w_in), "lw_o": (lru_w_o, m_lru_w_o, v_lru_w_o),
               "w_gu": (ffn_w_gu, m_ffn_w_gu, v_ffn_w_gu), "w_down": (ffn_w_down, m_ffn_w_down, v_ffn_w_down)}
    res = {key: None for key in weights}
    units = []

    def reduce_start(key, lyr, pieces):
        tag = f"{key}{lyr}"
        units.append({"key": key, "layer": lyr, "tag": tag, "pieces": pieces, "stage": 0, "age": 0,
                      "copy": _split_scatter_start(pieces, "reduce_" + tag + "_scatter_start")})
        return units[-1]["copy"][4]

    def reduce_advance(after, everything=False):
        tokens = []
        for u in units:
            key, lyr, tag = u["key"], u["layer"], u["tag"]
            if u["stage"] == 1:
                both = _split_wait(u["copy"], 1, after, "reduce_" + tag + "_pair_wait")
                g = both[:, :, :w_in_cols] if key == "w_in" else both
                w, m, v = weights[key]
                res[key] = _adamw_layer(w, g, m, v, res[key], lyr, "adamw_" + key)
                u["stage"] = 2
            elif u["stage"] == 0 and (u["age"] > 0 or everything):
                recv, pieces = _split_wait(u["copy"], len(ALL_RELS), after, "reduce_" + tag + "_scatter_wait", with_src=True)
                red = _sum_pieces(pieces, recv, "reduce_" + key + "_sum")
                u["copy"] = _split_gather_start(red, PAIR_RELS, 2, lambda dev: dev[2], "reduce_" + tag + "_pair_start")
                tokens.append(u["copy"][4])
                u["stage"] = 1
            u["age"] += 1
        return tokens

    def grad_w(key, a_op, b_op, kind, lyr, tm, tn, tk=tkt):
        return reduce_start(key, lyr, _mm_tn(a_op, b_op, kind=kind, tm=tm, tn=tn, tk=tk, name="grad_" + key))

    nope_w = MLA_HEADS * QK_NOPE
    for layer in reversed(range(DEPTH)):
        tokens = []
        j = layer // 2
        s = saved[layer]
        t_down = grad_w("w_down", s["act"], dhb, "row_colhalves", layer, tile(D_FF // N_CHIPS, 1408), tile(d // 2, 1024),
                        tk=_div_tile(tp, 2112, 16))
        d_act = _mm("nt", dhb, w_down4[layer], kind="row", out_dtype=CDT, tm=tmm, tn=tile(D_FF // N_CHIPS, 1408), tk=d, name="ffn_down_bwd")
        dgu = _swiglu_bwd(s["gate"], s["up"], d_act)
        t_gu = grad_w("w_gu", s["hn2"], dgu, "col", layer, tile(d // 2, 1024), tile(n_gu, 1408), tk=_div_tile(tp, 2112, 16))
        dhn2 = _mm("nt", dgu, w_gu4[layer], kind="col", tm=tmm, tn=tile(d, 1024), tk=tile(n_gu, 2816), name="ffn_gu_bwd")
        dh, dhb, g_norm_ffn[layer] = _rms_bwd(s["h_mid"], norm_ffn[layer] + (t_down[0, 0] + t_gu[0, 0]), dhn2, dh,
                                              width=d, col_block=0, name="norm_ffn_bwd")
        if layer % 2 == 0:
            t_mix = grad_w("w_o", s["att"], dhb, "row", j, tile(h_dim // N_DEV, 256), tile(d, 1024))[0, 0]
            d_att = _mm("nt", dhb, w_o4, kind="row", layer=j, tm=tmm, tn=tile(h_dim // N_CHIPS, 512), tk=d, name="mla_out_bwd")
            delta = _attn_delta(d_att, s["att"])
            dqn, dqr_h, dkv, dkr_h = _attn_bwd(s["qp"], s["kp"], s["kv"], d_att, s["lse"], delta)
            dqr, dkr = _mla_unprep(dqr_h, dkr_h, cos2, sin2)
            dq = jnp.concatenate([dqn, dqr], axis=-1)
            g_uq = _mm_tn(s["c_q"], dq, tm=Q_LORA, tn=tile(q_cols, 1024), tk=tkt, name="grad_w_uq")
            g_uq = jnp.concatenate([g_uq[:, :nope_w].reshape(Q_LORA, MLA_HEADS, QK_NOPE),
                                    g_uq[:, nope_w:].reshape(Q_LORA, MLA_HEADS, QK_ROPE)], axis=-1)
            g_uq = g_uq.reshape(2, Q_LORA // 2, N_CHIPS, q_cols // N_CHIPS).transpose(2, 0, 1, 3)
            t_mix = t_mix + reduce_start("w_uq", j, g_uq.reshape(N_DEV, Q_LORA // 2, q_cols // N_CHIPS).astype(CDT))[0, 0]
            dc_q = _mm("nt", dq, w_uq_perm[j], tm=tmm, tn=Q_LORA, tk=tile(q_cols, 1024), name="mla_uq_bwd")
            t_mix = t_mix + grad_w("w_ukv", s["c_kv"], dkv, "col", j, tile(KV_LORA // 2, 256), tile(w_ukv4.shape[3], 1024))[0, 0]
            dc_kv = _mm("nt", dkv, w_ukv4, kind="col", layer=j, tm=tmm, tn=KV_LORA, tk=tile(w_ukv4.shape[3], 1024), name="mla_ukv_bwd")
            dpq, _, g_q_norm[j] = _rms_bwd(s["proj"], mla_q_norm[j] + t_mix, dc_q, None, width=Q_LORA, col_block=0, name="q_norm_bwd")
            dpkv, _, g_kv_norm[j] = _rms_bwd(s["proj"], mla_kv_norm[j], dc_kv, None, width=KV_LORA, col_block=Q_LORA // KV_LORA, name="kv_norm_bwd")
            dproj = jnp.concatenate([dpq, dpkv, dkr], axis=-1).astype(CDT)
            tokens.append(grad_w("w_in", s["hn"], dproj, "row", j, tile(d // N_DEV, 256), tile(w_in_pad, 1152)))
            dhn = _mm("nt", dproj, w_in4, kind="row", layer=j, tm=tmm, tn=tile(d // N_CHIPS, 512), tk=tile(w_in_pad, 1152), name="mla_in_bwd")
        else:
            t_mix = grad_w("lw_o", s["mixed"], dhb, "row", j, tile(d // N_DEV, 256), tile(d, 1024))[0, 0]
            dm = _mm("nt", dhb, lw_o4, kind="row", layer=j, tm=tmm, tn=tile(d // N_CHIPS, 512), tk=d, name="lru_out_bwd")
            db, da, dy = _lru_scan_bwd(s["a"], s["hs"], dm, s["xy"])
            dxc, g_gate["wa"][j], g_gate["ba"][j], g_gate["wx"][j], g_gate["bx"][j], g_gate["lam"][j] = _lru_gates_bwd(
                db, da, s["xc"], s["r"], s["ig"], s["a"], lam_full[j] + t_mix, lru_w_gate_a[j], lru_w_gate_x[j])
            dxb, g_gate["cw"][j], g_gate["cb"][j] = _lru_conv_bwd(dxc, s["xy"], conv_w_full[j])
            dxy = jnp.concatenate([dxb, dy], axis=-1)
            tokens.append(grad_w("lw_in", s["hn"], dxy, "col", j, tile(d // 2, 1024), tile(lw_in4.shape[3], 1024)))
            dhn = _mm("nt", dxy, lw_in4, kind="col", layer=j, tm=tmm, tn=tile(d, 1024), tk=tile(lw_in4.shape[3], 1024), name="lru_in_bwd")
        dh, dhb, g_norm_mix[layer] = _rms_bwd(s["h_in"], norm_mix[layer] + t_mix, dhn, dh, width=d, col_block=0, name="norm_mix_bwd")
        tokens += reduce_advance(dh)
        if layer > 0:
            dhb = dhb + sum(tok[0, 0] for tok in tokens).astype(CDT)
        else:
            dh = dh + sum(tok[0, 0] for tok in tokens)

    grad_x = dh[N_META:t_real][None]
    g_meta_full = dh[:N_META]

    for _ in range(3):
        reduce_advance(dh, everything=True)

    g_small_full = [g_meta_full, jnp.stack(g_gate["cw"]).reshape(n_lru, CONV_W, d), jnp.stack(g_gate["cb"]).reshape(n_lru, d),
                    jnp.stack(g_gate["lam"]).reshape(n_lru, d)]
    g_small4 = jnp.stack([_pack([a[..., k * csh:(k + 1) * csh] for a in g_small_full], csh, 16) for k in range(N_CHIPS)])
    rows_s = g_small4.shape[1]
    red = _reduce_to_owner(g_small4.reshape(N_DEV, rows_s // 2, csh), F32, "reduce_small")
    g_small = _gather_send(red, PAIR_RELS, 2, lambda dev: dev[2], "reduce_small_pair").reshape(rows_s, csh)
    small_w = [meta_tokens, lru_conv_w, lru_conv_b, lru_lambda]
    small_m = [m_meta_tokens, m_lru_conv_w, m_lru_conv_b, m_lru_lambda]
    small_v = [v_meta_tokens, v_lru_conv_w, v_lru_conv_b, v_lru_lambda]
    sd, sm, sv = _adamw(_pack(small_w, csh, 16), g_small, _pack(small_m, csh, 16), _pack(small_v, csh, 16), "adamw_small")
    small_out = [_unpack(buf, small_shapes) for buf in (g_small, sd, sm, sv)]

    rep_w = [norm_mix, norm_ffn, norm_final, mla_q_norm, mla_kv_norm, lru_w_gate_a, lru_b_gate_a, lru_w_gate_x, lru_b_gate_x]
    rep_m = [m_norm_mix, m_norm_ffn, m_norm_final, m_mla_q_norm, m_mla_kv_norm, m_lru_w_gate_a, m_lru_b_gate_a, m_lru_w_gate_x, m_lru_b_gate_x]
    rep_v = [v_norm_mix, v_norm_ffn, v_norm_final, v_mla_q_norm, v_mla_kv_norm, v_lru_w_gate_a, v_lru_b_gate_a, v_lru_w_gate_x, v_lru_b_gate_x]
    rep_g = [jnp.stack(g_norm_mix), jnp.stack(g_norm_ffn), g_norm_final, jnp.stack(g_q_norm), jnp.stack(g_kv_norm),
             jnp.stack(g_gate["wa"]), jnp.stack(g_gate["ba"]), jnp.stack(g_gate["wx"]), jnp.stack(g_gate["bx"])]
    rep_shapes = [w.shape for w in rep_w]
    g_rep = _pack(rep_g, LANES, 8 * N_DEV)
    rows_r = g_rep.shape[0]
    red = _reduce_to_owner(g_rep.reshape(N_DEV, rows_r // N_DEV, LANES), F32, "reduce_rep")
    g_rep = _gather_send(red, ALL_RELS, N_DEV, _linear, "reduce_rep_all").reshape(rows_r, LANES)
    rd, rm, rv = _adamw(_pack(rep_w, LANES, 8 * N_DEV), g_rep, _pack(rep_m, LANES, 8 * N_DEV), _pack(rep_v, LANES, 8 * N_DEV), "adamw_rep")
    rep_out = [_unpack(buf, rep_shapes) for buf in (g_rep, rd, rm, rv)]

    def leaf(kind):
        s_, r_ = small_out[kind], rep_out[kind]
        return [s_[0], r_[0], r_[1], r_[2], res["w_in"][kind], r_[3], r_[4], res["w_uq"][kind], res["w_ukv"][kind],
                res["w_o"][kind], res["lw_in"][kind], s_[1], s_[2], r_[5], r_[6], r_[7], r_[8], s_[3],
                res["lw_o"][kind], res["w_gu"][kind], res["w_down"][kind]]

    return (loss, grad_x, *leaf(0), *leaf(1), *leaf(2), *leaf(3))
```
